```python
import math
import numpy as np
import jax
import jax.numpy as jnp
from jax import lax

D_MODEL = 1024
BATCH = 16
SEQ = 2048
DEPTH = 1
DEC_BATCH = 2
DEC_SEQ = 8192
PAST_LEN = 128

HEAD_DIM = 64
N_HEADS_A = 8
N_HEADS_B = 8
WIDTH_A = N_HEADS_A * HEAD_DIM
WIDTH_B = N_HEADS_B * HEAD_DIM
MIX_WIDTH = WIDTH_A + WIDTH_B
DILATED_PATTERNS = ((128, 1), (512, 4), (2048, 16))
SW_BLOCK = 64
ROPE_THETA = 10000.0
GRID_W = 64
NA_ROWS = 8
NA_COLS = 16
N_EXPERTS = 32
TOP_K = 4
D_FF = 1024
SWIGLU_ALPHA = 1.702
SWIGLU_LIMIT = 7.0
MOE_BLOCK = 256
DEEPNORM_ALPHA = (2 * DEPTH) ** 0.25
DEEPNORM_BETA = (8 * DEPTH) ** -0.25
LN_EPS = 1e-5
RMS_EPS = 1e-6
NEG_INF = -1e30

kernel_name = "hybrid_dilated_neighbourhood_moe_encoder"


def layer_norm(x, g, b):
    xf = x.astype(jnp.float32)
    mu = jnp.mean(xf, axis=-1, keepdims=True)
    var = jnp.mean(jnp.square(xf - mu), axis=-1, keepdims=True)
    y = (xf - mu) * lax.rsqrt(var + LN_EPS) * g.astype(jnp.float32) + b.astype(jnp.float32)
    return y.astype(x.dtype)


def group_rms_norm(o, g):
    return o * lax.rsqrt(jnp.mean(jnp.square(o), axis=-1, keepdims=True) + RMS_EPS) * g.astype(jnp.float32)


def rope(x, pos):
    half = HEAD_DIM // 2
    inv_freq = 1.0 / (ROPE_THETA ** (jnp.arange(half, dtype=jnp.float32) / half))
    ang = pos[:, None] * inv_freq[None, :]
    cos, sin = jnp.cos(ang), jnp.sin(ang)
    xf = x.astype(jnp.float32)
    x1, x2 = xf[..., :half], xf[..., half:]
    return jnp.concatenate([x1 * cos - x2 * sin, x2 * cos + x1 * sin], axis=-1).astype(x.dtype)


def banded_window_attention(q, k, v, radius):
    *lead, L, dh = q.shape
    nlead = len(lead)
    nb = -(-L // SW_BLOCK)
    Lp = nb * SW_BLOCK
    span = SW_BLOCK + 2 * radius
    qb = jnp.pad(q, [(0, 0)] * nlead + [(0, Lp - L), (0, 0)]).reshape(*lead, nb, SW_BLOCK, dh)
    kv_pad = [(0, 0)] * nlead + [(radius, Lp - L + radius), (0, 0)]
    idx = np.arange(nb)[:, None] * SW_BLOCK + np.arange(span)[None, :]
    kb = jnp.take(jnp.pad(k, kv_pad), idx, axis=-2)
    vb = jnp.take(jnp.pad(v, kv_pad), idx, axis=-2)
    qpos = np.arange(nb)[:, None] * SW_BLOCK + np.arange(SW_BLOCK)[None, :]
    kpos = idx - radius
    valid = ((np.abs(qpos[:, :, None] - kpos[:, None, :]) <= radius)
             & (kpos[:, None, :] >= 0) & (kpos[:, None, :] < L))
    s = jnp.einsum('...nqd,...nkd->...nqk', qb, kb, preferred_element_type=jnp.float32)
    s = jnp.where(valid, s, NEG_INF)
    m = jnp.max(s, axis=-1, keepdims=True)
    p = jnp.exp(s - m)
    l = jnp.sum(p, axis=-1, keepdims=True)
    o = jnp.einsum('...nqk,...nkd->...nqd', p, vb.astype(jnp.float32)) / l
    lse = (m + jnp.log(l))[..., 0]
    o = o.reshape(*lead, Lp, dh)[..., :L, :]
    lse = lse.reshape(*lead, Lp)[..., :L]
    return o, lse


def dilated_attention(q, k, v):
    B, H, T, dh = q.shape
    outs, lses = [], []
    for window, dil in DILATED_PATTERNS:
        radius = window // (2 * dil)
        def to_residue(a):
            return a.reshape(B, H, T // dil, dil, dh).swapaxes(2, 3)
        o, lse = banded_window_attention(to_residue(q), to_residue(k), to_residue(v), radius)
        outs.append(o.swapaxes(2, 3).reshape(B, H, T, dh))
        lses.append(lse.swapaxes(2, 3).reshape(B, H, T))
    w = jax.nn.softmax(jnp.stack(lses, axis=0), axis=0)
    return jnp.einsum('pbht,pbhtd->bhtd', w, jnp.stack(outs, axis=0))


def neighbourhood_attention(q, k, v, rpb):
    B, H, T, dh = q.shape
    rows = T // GRID_W
    kh = min(NA_ROWS, rows)
    r = np.arange(rows)
    row_start = np.clip(r - kh // 2, 0, rows - kh)
    krow = row_start[:, None] + np.arange(kh)[None, :]
    c = np.arange(GRID_W)
    col_start = np.clip(c - NA_COLS // 2, 0, GRID_W - NA_COLS)
    col_mask = (c[None, :] >= col_start[:, None]) & (c[None, :] < col_start[:, None] + NA_COLS)
    dr_idx = (krow - r[:, None]) + NA_ROWS - 1
    dc_idx = np.clip(c[None, :] - c[:, None], -(NA_COLS - 1), NA_COLS - 1) + NA_COLS - 1
    bias = rpb.astype(jnp.float32)[:, dr_idx[:, :, None, None], dc_idx[None, None, :, :]]
    bias = bias.transpose(0, 1, 3, 2, 4)
    bias = jnp.where(col_mask[None, None, :, None, :], bias, NEG_INF).reshape(H, rows, GRID_W, kh * GRID_W)
    qr = q.reshape(B, H, rows, GRID_W, dh)
    kg = k.reshape(B, H, rows, GRID_W, dh)[:, :, krow].reshape(B, H, rows, kh * GRID_W, dh)
    vg = v.reshape(B, H, rows, GRID_W, dh)[:, :, krow].reshape(B, H, rows, kh * GRID_W, dh)
    s = jnp.einsum('bhrqd,bhrkd->bhrqk', qr, kg, preferred_element_type=jnp.float32) + bias[None]
    p = jax.nn.softmax(s, axis=-1)
    o = jnp.einsum('bhrqk,bhrkd->bhrqd', p, vg.astype(jnp.float32))
    return o.reshape(B, H, T, dh)


def token_mixer(u, w_in, rpb, g_out_a, g_out_b, w_out):
    B, T, _ = u.shape
    proj = u @ w_in
    cuts = np.cumsum([WIDTH_A, WIDTH_A, WIDTH_A, WIDTH_B, WIDTH_B])
    qa, ka, va, qb, kb, vb = jnp.split(proj, cuts, axis=-1)
    def heads(a, n):
        return a.reshape(B, T, n, HEAD_DIM).transpose(0, 2, 1, 3)
    pos = jnp.arange(T, dtype=jnp.float32)
    scale = HEAD_DIM ** -0.5
    qa = rope(heads(qa, N_HEADS_A), pos) * scale
    ka = rope(heads(ka, N_HEADS_A), pos)
    oa = dilated_attention(qa, ka, heads(va, N_HEADS_A))
    ob = neighbourhood_attention(heads(qb, N_HEADS_B) * scale, heads(kb, N_HEADS_B),
                                 heads(vb, N_HEADS_B), rpb)
    oa = group_rms_norm(oa.transpose(0, 2, 1, 3).reshape(B, T, WIDTH_A), g_out_a)
    ob = group_rms_norm(ob.transpose(0, 2, 1, 3).reshape(B, T, WIDTH_B), g_out_b)
    return jnp.concatenate([oa, ob], axis=-1).astype(u.dtype) @ w_out


def moe_ffn(u, w_router, b_router, w_up, b_up, w_down, b_down):
    B, T, D = u.shape
    h = u.reshape(B * T, D)
    N = B * T
    logits = jnp.dot(h, w_router, preferred_element_type=jnp.float32) + b_router.astype(jnp.float32)
    top_vals, top_idx = lax.top_k(logits, TOP_K)
    gates = jax.nn.softmax(top_vals, axis=-1)
    A = N * TOP_K
    expert = top_idx.reshape(-1).astype(jnp.int32)
    token = jnp.repeat(jnp.arange(N, dtype=jnp.int32), TOP_K)
    gate = gates.reshape(-1)
    order = jnp.argsort(expert)
    e_s, t_s, g_s = expert[order], token[order], gate[order]
    counts = jnp.bincount(expert, length=N_EXPERTS)
    padded = (counts + MOE_BLOCK - 1) // MOE_BLOCK * MOE_BLOCK
    pad_end = jnp.cumsum(padded)
    pad_start = pad_end - padded
    sorted_start = jnp.cumsum(counts) - counts
    dest = pad_start[e_s] + jnp.arange(A, dtype=jnp.int32) - sorted_start[e_s]
    n_blocks = -(-A // MOE_BLOCK) + N_EXPERTS
    n_rows = n_blocks * MOE_BLOCK
    tok_pad = jnp.zeros((n_rows,), jnp.int32).at[dest].set(t_s)
    gate_pad = jnp.zeros((n_rows,), jnp.float32).at[dest].set(g_s)
    blk_start = jnp.arange(n_blocks, dtype=jnp.int32) * MOE_BLOCK
    blk_expert = jnp.minimum(jnp.searchsorted(pad_end, blk_start, side='right'), N_EXPERTS - 1)

    def expert_block(args):
        tok, g, e = args
        xb = h[tok]
        hu = xb @ w_up[e] + b_up[e]
        x_glu = jnp.minimum(hu[..., ::2], SWIGLU_LIMIT)
        x_lin = jnp.clip(hu[..., 1::2], -SWIGLU_LIMIT, SWIGLU_LIMIT)
        act = x_glu * jax.nn.sigmoid(SWIGLU_ALPHA * x_glu) * (x_lin + 1)
        y = act @ w_down[e] + b_down[e]
        return y.astype(jnp.float32) * g[:, None]

    y_pad = lax.map(expert_block, (tok_pad.reshape(n_blocks, MOE_BLOCK),
                                   gate_pad.reshape(n_blocks, MOE_BLOCK), blk_expert))
    y = jnp.zeros((N, D), jnp.float32).at[tok_pad].add(y_pad.reshape(n_rows, D))
    return y.reshape(B, T, D).astype(u.dtype)


def encoder_layer(x, c, w_mod, b_mod, w_in, rpb, g_out_a, g_out_b, w_out, ln1_g, ln1_b,
                  w_router, b_router, w_up, b_up, w_down, b_down, ln2_g, ln2_b):
    mod = (jax.nn.silu(c) @ w_mod + b_mod)[:, None, :]
    shift_a, scale_a, gate_a, shift_f, scale_f, gate_f = jnp.split(mod, 6, axis=-1)
    u = x * (1 + scale_a) + shift_a
    x = layer_norm(DEEPNORM_ALPHA * x + (1 + gate_a) * token_mixer(u, w_in, rpb, g_out_a, g_out_b, w_out),
                   ln1_g, ln1_b)
    u = x * (1 + scale_f) + shift_f
    x = layer_norm(DEEPNORM_ALPHA * x + (1 + gate_f) * moe_ffn(u, w_router, b_router, w_up, b_up, w_down, b_down),
                   ln2_g, ln2_b)
    return x


def setup_inputs(seed: int = 0) -> dict:
    key = jax.random.key(seed)
    ks = jax.random.split(key, 24)
    f32 = jnp.float32
    nrm = lambda k, shape, s: jax.random.normal(k, shape, f32) * s
    D = D_MODEL
    return {
        "x_prompt": nrm(ks[0], (BATCH, SEQ, D), 1.0),
        "x_sample": nrm(ks[1], (DEC_BATCH, DEC_SEQ, D), 1.0),
        "c_prompt": nrm(ks[2], (BATCH, D), 1.0),
        "c_sample": nrm(ks[3], (DEC_BATCH, D), 1.0),
        "w_mod": nrm(ks[4], (DEPTH, D, 6 * D), 0.2 * D ** -0.5),
        "b_mod": nrm(ks[5], (DEPTH, 6 * D), 0.02),
        "w_in": nrm(ks[6], (DEPTH, D, 3 * MIX_WIDTH), D ** -0.5),
        "rpb": nrm(ks[7], (DEPTH, N_HEADS_B, 2 * NA_ROWS - 1, 2 * NA_COLS - 1), 0.1),
        "g_out_a": 1.0 + nrm(ks[8], (DEPTH, WIDTH_A), 0.02),
        "g_out_b": 1.0 + nrm(ks[9], (DEPTH, WIDTH_B), 0.02),
        "w_out": nrm(ks[10], (DEPTH, MIX_WIDTH, D), MIX_WIDTH ** -0.5 * DEEPNORM_BETA),
        "ln1_g": 1.0 + nrm(ks[11], (DEPTH, D), 0.02),
        "ln1_b": nrm(ks[12], (DEPTH, D), 0.02),
        "w_router": nrm(ks[13], (DEPTH, D, N_EXPERTS), D ** -0.5),
        "b_router": nrm(ks[14], (DEPTH, N_EXPERTS), 0.01),
        "w_up": nrm(ks[15], (DEPTH, N_EXPERTS, D, 2 * D_FF), D ** -0.5),
        "b_up": nrm(ks[16], (DEPTH, N_EXPERTS, 2 * D_FF), 0.02),
        "w_down": nrm(ks[17], (DEPTH, N_EXPERTS, D_FF, D), D_FF ** -0.5 * DEEPNORM_BETA),
        "b_down": nrm(ks[18], (DEPTH, N_EXPERTS, D), 0.02),
        "ln2_g": 1.0 + nrm(ks[19], (DEPTH, D), 0.02),
        "ln2_b": nrm(ks[20], (DEPTH, D), 0.02),
    }


def reference(x_prompt, x_sample, c_prompt, c_sample, w_mod, b_mod, w_in, rpb, g_out_a, g_out_b,
              w_out, ln1_g, ln1_b, w_router, b_router, w_up, b_up, w_down, b_down, ln2_g, ln2_b):
    y_prompt = x_prompt
    y_sample = x_sample
    for l in range(DEPTH):
        y_prompt = encoder_layer(y_prompt, c_prompt, w_mod[l], b_mod[l], w_in[l], rpb[l], g_out_a[l], g_out_b[l],
                                 w_out[l], ln1_g[l], ln1_b[l], w_router[l], b_router[l], w_up[l], b_up[l],
                                 w_down[l], b_down[l], ln2_g[l], ln2_b[l])
        y_sample = encoder_layer(y_sample, c_sample, w_mod[l], b_mod[l], w_in[l], rpb[l], g_out_a[l], g_out_b[l],
                                 w_out[l], ln1_g[l], ln1_b[l], w_router[l], b_router[l], w_up[l], b_up[l],
                                 w_down[l], b_down[l], ln2_g[l], ln2_b[l])
    return (y_prompt, y_sample)
```

```python
import functools

import numpy as np
import jax
import jax.numpy as jnp
from jax import lax
from jax.experimental import pallas as pl
from jax.experimental.pallas import tpu as pltpu

F32 = jnp.float32
BF16 = jnp.bfloat16
I32 = jnp.int32

D_MODEL = 1024
HEAD_DIM = 64
N_HEADS = 8
WIDTH = N_HEADS * HEAD_DIM
N_PAIRS = WIDTH // 128
ROPE_THETA = 10000.0
RADIUS = 64
GRID_W = 64
NA_ROWS = 8
NA_COLS = 16
N_EXPERTS = 32
TOP_K = 4
D_FF = 1024
SWIGLU_ALPHA = 1.702
SWIGLU_LIMIT = 7.0
DEEPNORM_ALPHA = 2.0 ** 0.25
LN_EPS = 1e-5
RMS_EPS = 1e-6
NEG = -1e30

LANES = 128
RES = 16
CHUNK = 128
RES_PER_STEP = 4
MOE_ROWS = 256
ROUTE_TILE = 512
SCATTER_TILE = 256
GATHER_TILE = 128
VMEM_LIMIT = 56 * 1024 * 1024


def _cparams(sem, vmem=VMEM_LIMIT):
    return pltpu.CompilerParams(dimension_semantics=sem, vmem_limit_bytes=vmem)


def _mod_kernel(c_ref, w_ref, b_ref, o_ref):
    c = c_ref[...]
    s = c * (1.0 / (1.0 + jnp.exp(-c)))
    o_ref[...] = jnp.dot(s.astype(BF16), w_ref[...].astype(BF16), preferred_element_type=F32) + b_ref[...]


def _modulation(c, w_mod, b_mod):
    rows = c.shape[0]
    n_out = w_mod.shape[1]
    tn = 1024
    return pl.pallas_call(
        _mod_kernel,
        grid=(n_out // tn,),
        in_specs=[pl.BlockSpec((rows, D_MODEL), lambda j: (0, 0)),
                  pl.BlockSpec((D_MODEL, tn), lambda j: (0, j)),
                  pl.BlockSpec((1, tn), lambda j: (0, j))],
        out_specs=pl.BlockSpec((rows, tn), lambda j: (0, j)),
        out_shape=jax.ShapeDtypeStruct((rows, n_out), F32),
        compiler_params=_cparams(("arbitrary",)),
        name="modulation",
    )(c, w_mod, b_mod.reshape(1, n_out))


def _inproj_kernel(x_ref, mod_ref, w_ref, cos_ref, sin_ref,
                   qa_ref, ka_ref, va_ref, qb_ref, kb_ref, vb_ref, u_scr):
    D = D_MODEL
    shift = mod_ref[:, 0:D]
    scale = mod_ref[:, D:2 * D]
    for a in range(RES_PER_STEP):
        xa = x_ref[:, a * D:(a + 1) * D]
        u_scr[a * CHUNK:(a + 1) * CHUNK, :] = (xa * (1.0 + scale) + shift).astype(BF16)
    u = u_scr[...]

    lane = lax.broadcasted_iota(I32, (1, WIDTH), 1)
    first_half = (lane % HEAD_DIM) < (HEAD_DIM // 2)
    cosf = jnp.concatenate(
        [jnp.tile(cos_ref[:, a * LANES:(a + 1) * LANES], (1, N_PAIRS)) for a in range(RES_PER_STEP)], axis=0)
    sinf = jnp.concatenate(
        [jnp.tile(sin_ref[:, a * LANES:(a + 1) * LANES], (1, N_PAIRS)) for a in range(RES_PER_STEP)], axis=0)

    def proj(i):
        return jnp.dot(u, w_ref[:, i * WIDTH:(i + 1) * WIDTH], preferred_element_type=F32)

    def rope(p):
        rot = jnp.where(first_half, pltpu.roll(p, WIDTH - HEAD_DIM // 2, 1), pltpu.roll(p, HEAD_DIM // 2, 1))
        return p * cosf + rot * sinf

    def emit_res_major(ref, p):
        for a in range(RES_PER_STEP):
            for hp in range(N_PAIRS):
                ref[a, hp] = p[a * CHUNK:(a + 1) * CHUNK, hp * LANES:(hp + 1) * LANES].astype(BF16)

    def emit_natural(ref, p):
        for a in range(RES_PER_STEP):
            for hp in range(N_PAIRS):
                ref[hp, :, a * LANES:(a + 1) * LANES] = (
                    p[a * CHUNK:(a + 1) * CHUNK, hp * LANES:(hp + 1) * LANES].astype(BF16))

    qscale = HEAD_DIM ** -0.5
    emit_res_major(qa_ref, rope(proj(0)) * qscale)
    emit_res_major(ka_ref, rope(proj(1)))
    emit_res_major(va_ref, proj(2))
    emit_natural(qb_ref, proj(3) * qscale)
    emit_natural(kb_ref, proj(4))
    emit_natural(vb_ref, proj(5))


def _input_projection(x, mod3, w_in_bf, cos_v, sin_v, mod_row0):
    B, T, D = x.shape
    L = T // RES
    BL = B * L
    cps = L // CHUNK
    xv = x.reshape(BL, RES * D)
    nq = RES // RES_PER_STEP
    res_shape = jax.ShapeDtypeStruct((RES, N_PAIRS, BL, LANES), BF16)
    nat_shape = jax.ShapeDtypeStruct((N_PAIRS, BL, RES * LANES), BF16)
    res_spec = pl.BlockSpec((RES_PER_STEP, N_PAIRS, CHUNK, LANES), lambda i, q: (q, 0, i, 0))
    nat_spec = pl.BlockSpec((N_PAIRS, CHUNK, RES_PER_STEP * LANES), lambda i, q: (0, i, q))
    outs = pl.pallas_call(
        _inproj_kernel,
        grid=(BL // CHUNK, nq),
        in_specs=[pl.BlockSpec((CHUNK, RES_PER_STEP * D), lambda i, q: (i, q)),
                  pl.BlockSpec((None, 1, 6 * D), lambda i, q: (mod_row0 + i // cps, 0, 0)),
                  pl.BlockSpec((D, 6 * WIDTH), lambda i, q: (0, 0)),
                  pl.BlockSpec((CHUNK, RES_PER_STEP * LANES), lambda i, q: (i % cps, q)),
                  pl.BlockSpec((CHUNK, RES_PER_STEP * LANES), lambda i, q: (i % cps, q))],
        out_specs=[res_spec, res_spec, res_spec, nat_spec, nat_spec, nat_spec],
        out_shape=[res_shape, res_shape, res_shape, nat_shape, nat_shape, nat_shape],
        scratch_shapes=[pltpu.VMEM((RES_PER_STEP * CHUNK, D), BF16)],
        compiler_params=_cparams(("arbitrary", "arbitrary")),
        name="input_projection",
    )(xv, mod3, w_in_bf, cos_v, sin_v)
    qa, ka, va, qb, kb, vb = outs
    N = B * T
    return qa, ka, va, qb.reshape(N_PAIRS, N, LANES), kb.reshape(N_PAIRS, N, LANES), vb.reshape(N_PAIRS, N, LANES)


def _rope_tables(T):
    half = HEAD_DIM // 2
    inv_freq = 1.0 / (ROPE_THETA ** (jnp.arange(half, dtype=F32) / half))
    ang = jnp.arange(T, dtype=F32)[:, None] * inv_freq[None, :]
    cos, sin = jnp.cos(ang), jnp.sin(ang)
    cos_h = jnp.concatenate([cos, cos], axis=-1)
    sin_h = jnp.concatenate([-sin, sin], axis=-1)
    cos2 = jnp.concatenate([cos_h, cos_h], axis=-1)
    sin2 = jnp.concatenate([sin_h, sin_h], axis=-1)
    L = T // RES
    return cos2.reshape(L, RES * LANES), sin2.reshape(L, RES * LANES)


P2_ROWS = 32
P2_KROWS = 64
P1_ROWS = 16
P1_KROWS = 48
HALO = 64


def _band_tables(has_halo):
    def mask(ok):
        return np.where(ok, 0.0, NEG).astype(np.float32)
    mq = np.arange(CHUNK)[:, None]
    if has_halo:
        koff3 = np.arange(2 * CHUNK) - HALO
    else:
        koff3 = np.arange(CHUNK)
    b3 = mask(np.abs(mq - koff3[None, :]) <= RADIUS)
    j = np.repeat(np.arange(4), P2_ROWS)[:, None]
    a = np.tile(np.arange(P2_ROWS), 4)[:, None]
    jk = np.repeat(np.arange(4), P2_KROWS)[None, :]
    bk = np.tile(np.arange(P2_KROWS), 4)[None, :]
    b2 = mask(np.abs(4 * (a - (bk - 16)) + (j - jk)) <= RADIUS)
    koff2 = (bk - 16).reshape(-1)
    r = np.repeat(np.arange(RES), P1_ROWS)[:, None]
    a = np.tile(np.arange(P1_ROWS), RES)[:, None]
    rk = np.repeat(np.arange(RES), P1_KROWS)[None, :]
    bk = np.tile(np.arange(P1_KROWS), RES)[None, :]
    b1 = mask(np.abs(RES * (a - (bk - 16)) + (r - rk)) <= RADIUS)
    koff1 = (bk - 16).reshape(-1)
    return (jnp.asarray(b3), jnp.asarray(koff3.astype(np.int32)[None, :]),
            jnp.asarray(b2), jnp.asarray(koff2.astype(np.int32)[None, :]),
            jnp.asarray(b1), jnp.asarray(koff1.astype(np.int32)[None, :]))


def _dilated_kernel(*refs, has_halo, seq_rows, chunks_per_seq):
    if has_halo:
        (q_ref, k_ref, v_ref, kp_ref, kn_ref, vp_ref, vn_ref,
         b3_ref, o3_ref, b2_ref, o2_ref, b1_ref, o1_ref,
         out_ref, kf, vf, acc, ms, ls) = refs
    else:
        (q_ref, k_ref, v_ref, b3_ref, o3_ref, b2_ref, o2_ref, b1_ref, o1_ref,
         out_ref, kf, vf, acc, ms, ls) = refs

    c = pl.program_id(0) % chunks_per_seq
    row0 = c * CHUNK
    lane = lax.broadcasted_iota(I32, (1, LANES), 1)
    head0 = lane < HEAD_DIM

    for r in range(RES):
        kf[r, HALO:HALO + CHUNK, :] = k_ref[r]
        vf[r, HALO:HALO + CHUNK, :] = v_ref[r]
        if has_halo:
            kf[r, 0:HALO, :] = kp_ref[r]
            vf[r, 0:HALO, :] = vp_ref[r]
            kf[r, HALO + CHUNK:, :] = kn_ref[r]
            vf[r, HALO + CHUNK:, :] = vn_ref[r]
        else:
            zeros = jnp.zeros((HALO, LANES), BF16)
            kf[r, 0:HALO, :] = zeros
            vf[r, 0:HALO, :] = zeros
            kf[r, HALO + CHUNK:, :] = zeros
            vf[r, HALO + CHUNK:, :] = zeros

    def in_seq(base, off_ref):
        kv_row = base + off_ref[...]
        return jnp.where((kv_row >= 0) & (kv_row < seq_rows), 0.0, NEG).astype(F32)

    def attend(q, k, v, bias):
        outs = []
        for h in range(2):
            sel = head0 if h == 0 else jnp.logical_not(head0)
            qh = jnp.where(sel, q, jnp.zeros_like(q))
            s = lax.dot_general(qh, k, (((1,), (1,)), ((), ())), preferred_element_type=F32) + bias
            m = jnp.max(s, axis=-1, keepdims=True)
            p = jnp.exp(s - m)
            l = jnp.sum(p, axis=-1, keepdims=True)
            o = jnp.dot(p.astype(BF16), v, preferred_element_type=F32)
            outs.append((o, m, l))
        (o0, m0, l0), (o1, m1, l1) = outs
        return jnp.where(head0, o0, o1), jnp.where(head0, m0, m1), jnp.where(head0, l0, l1)

    def merge(a_old, m_old, l_old, o, m, l):
        mn = jnp.maximum(m_old, m)
        wa = jnp.exp(m_old - mn)
        wb = jnp.exp(m - mn)
        return a_old * wa + o * wb, mn, l_old * wa + l * wb

    if has_halo:
        dyn3 = in_seq(row0, o3_ref)
        bias3 = b3_ref[...] + dyn3
    else:
        bias3 = b3_ref[...]

    def body3(r, carry):
        q = q_ref[r]
        if has_halo:
            k = kf[r]
            v = vf[r]
        else:
            k = k_ref[r]
            v = v_ref[r]
        o, m, l = attend(q, k, v, bias3)
        acc[r] = o
        ms[r] = m
        ls[r] = l
        return carry

    lax.fori_loop(0, RES, body3, 0)

    b2 = b2_ref[...]

    def body2(t, carry):
        r4 = t // (CHUNK // P2_ROWS)
        g = t % (CHUNK // P2_ROWS)
        qs = pl.multiple_of(g * P2_ROWS, P2_ROWS)
        ks = pl.multiple_of(HALO - 16 + g * P2_ROWS, 16)
        bias = b2 + in_seq(row0 + g * P2_ROWS, o2_ref)
        q = jnp.concatenate([q_ref[4 * j + r4, pl.ds(qs, P2_ROWS), :] for j in range(4)], axis=0)
        k = jnp.concatenate([kf[4 * j + r4, pl.ds(ks, P2_KROWS), :] for j in range(4)], axis=0)
        v = jnp.concatenate([vf[4 * j + r4, pl.ds(ks, P2_KROWS), :] for j in range(4)], axis=0)
        o, m, l = attend(q, k, v, bias)
        a_old = jnp.concatenate([acc[4 * j + r4, pl.ds(qs, P2_ROWS), :] for j in range(4)], axis=0)
        m_old = jnp.concatenate([ms[4 * j + r4, pl.ds(qs, P2_ROWS), :] for j in range(4)], axis=0)
        l_old = jnp.concatenate([ls[4 * j + r4, pl.ds(qs, P2_ROWS), :] for j in range(4)], axis=0)
        a_new, m_new, l_new = merge(a_old, m_old, l_old, o, m, l)
        for j in range(4):
            acc[4 * j + r4, pl.ds(qs, P2_ROWS), :] = a_new[j * P2_ROWS:(j + 1) * P2_ROWS]
            ms[4 * j + r4, pl.ds(qs, P2_ROWS), :] = m_new[j * P2_ROWS:(j + 1) * P2_ROWS]
            ls[4 * j + r4, pl.ds(qs, P2_ROWS), :] = l_new[j * P2_ROWS:(j + 1) * P2_ROWS]
        return carry

    lax.fori_loop(0, 4 * (CHUNK // P2_ROWS), body2, 0)

    b1 = b1_ref[...]

    def body1(g, carry):
        qs = pl.multiple_of(g * P1_ROWS, P1_ROWS)
        ks = pl.multiple_of(HALO - 16 + g * P1_ROWS, 16)
        bias = b1 + in_seq(row0 + g * P1_ROWS, o1_ref)
        q = jnp.concatenate([q_ref[r, pl.ds(qs, P1_ROWS), :] for r in range(RES)], axis=0)
        k = jnp.concatenate([kf[r, pl.ds(ks, P1_KROWS), :] for r in range(RES)], axis=0)
        v = jnp.concatenate([vf[r, pl.ds(ks, P1_KROWS), :] for r in range(RES)], axis=0)
        o, m, l = attend(q, k, v, bias)
        a_old = jnp.concatenate([acc[r, pl.ds(qs, P1_ROWS), :] for r in range(RES)], axis=0)
        m_old = jnp.concatenate([ms[r, pl.ds(qs, P1_ROWS), :] for r in range(RES)], axis=0)
        l_old = jnp.concatenate([ls[r, pl.ds(qs, P1_ROWS), :] for r in range(RES)], axis=0)
        a_new, m_new, l_new = merge(a_old, m_old, l_old, o, m, l)
        for r in range(RES):
            acc[r, pl.ds(qs, P1_ROWS), :] = a_new[r * P1_ROWS:(r + 1) * P1_ROWS]
            ms[r, pl.ds(qs, P1_ROWS), :] = m_new[r * P1_ROWS:(r + 1) * P1_ROWS]
            ls[r, pl.ds(qs, P1_ROWS), :] = l_new[r * P1_ROWS:(r + 1) * P1_ROWS]
        return carry

    lax.fori_loop(0, CHUNK // P1_ROWS, body1, 0)

    def body_out(r, carry):
        out_ref[r] = (acc[r] / ls[r]).astype(BF16)
        return carry

    lax.fori_loop(0, RES, body_out, 0)


def _dilated_attention(qa, ka, va, B, T):
    L = T // RES
    BL = B * L
    cps = L // CHUNK
    has_halo = cps > 1
    tables = _band_tables(has_halo)
    blk = (RES, None, CHUNK, LANES)
    center = pl.BlockSpec(blk, lambda i, hp: (0, hp, i, 0))
    in_specs = [center, center, center]
    args = [qa, ka, va]
    if has_halo:
        hblk = (RES, None, HALO, LANES)
        per = CHUNK // HALO

        def prev_map(i, hp):
            return (0, hp, jnp.maximum(per * i - 1, (i // cps) * cps * per), 0)

        def next_map(i, hp):
            return (0, hp, jnp.minimum(per * i + per, (i // cps + 1) * cps * per - 1), 0)

        in_specs += [pl.BlockSpec(hblk, prev_map), pl.BlockSpec(hblk, next_map),
                     pl.BlockSpec(hblk, prev_map), pl.BlockSpec(hblk, next_map)]
        args += [ka, ka, va, va]
    for t in tables:
        in_specs.append(pl.BlockSpec(t.shape, lambda i, hp: (0, 0)))
        args.append(t)
    kern = functools.partial(_dilated_kernel, has_halo=has_halo, seq_rows=L, chunks_per_seq=cps)
    return pl.pallas_call(
        kern,
        grid=(BL // CHUNK, N_PAIRS),
        in_specs=in_specs,
        out_specs=pl.BlockSpec(blk, lambda i, hp: (0, hp, i, 0)),
        out_shape=jax.ShapeDtypeStruct((RES, N_PAIRS, BL, LANES), BF16),
        scratch_shapes=[pltpu.VMEM((RES, 2 * CHUNK, LANES), BF16),
                        pltpu.VMEM((RES, 2 * CHUNK, LANES), BF16),
                        pltpu.VMEM((RES, CHUNK, LANES), F32),
                        pltpu.VMEM((RES, CHUNK, LANES), F32),
                        pltpu.VMEM((RES, CHUNK, LANES), F32)],
        compiler_params=_cparams(("arbitrary", "arbitrary")),
        name="dilated_attention",
    )(*args)


NA_KEYS = NA_ROWS * GRID_W
NA_BLOCK_ROWS = 32


def _na_bias_table(rpb):
    c = np.arange(GRID_W)
    col_start = np.clip(c - NA_COLS // 2, 0, GRID_W - NA_COLS)
    col_mask = (c[None, :] >= col_start[:, None]) & (c[None, :] < col_start[:, None] + NA_COLS)
    dc_idx = np.clip(c[None, :] - c[:, None], -(NA_COLS - 1), NA_COLS - 1) + NA_COLS - 1
    rel = rpb.astype(F32)[:, :, dc_idx]
    rel = jnp.where(col_mask[None, None], rel, NEG)
    per_off = [rel[:, d0:d0 + NA_ROWS].transpose(0, 2, 1, 3).reshape(N_HEADS, GRID_W, NA_KEYS)
               for d0 in range(NA_ROWS)]
    return jnp.stack(per_off, axis=0).reshape(NA_ROWS, N_PAIRS, 2, GRID_W, NA_KEYS)


def _na_kernel(q_ref, k_ref, v_ref, bias_ref, out_ref, *, grid_rows, block_rows):
    gb = pl.program_id(2)
    lane = lax.broadcasted_iota(I32, (1, LANES), 1)
    head0 = lane < HEAD_DIM

    def body(i, carry):
        g = gb * block_rows + i
        rs = jnp.clip(g - NA_ROWS // 2, 0, grid_rows - NA_ROWS)
        d0 = rs - g + NA_ROWS - 1
        q = q_ref[pl.ds(pl.multiple_of(i * GRID_W, GRID_W), GRID_W), :]
        ks = pl.multiple_of(rs * GRID_W, GRID_W)
        k = k_ref[pl.ds(ks, NA_KEYS), :]
        v = v_ref[pl.ds(ks, NA_KEYS), :]
        outs = []
        for h in range(2):
            sel = head0 if h == 0 else jnp.logical_not(head0)
            qh = jnp.where(sel, q, jnp.zeros_like(q))
            s = lax.dot_general(qh, k, (((1,), (1,)), ((), ())), preferred_element_type=F32) + bias_ref[d0, h]
            m = jnp.max(s, axis=-1, keepdims=True)
            p = jnp.exp(s - m)
            l = jnp.sum(p, axis=-1, keepdims=True)
            o = jnp.dot(p.astype(BF16), v, preferred_element_type=F32)
            outs.append(o / l)
        out_ref[pl.ds(pl.multiple_of(i * GRID_W, GRID_W), GRID_W), :] = (
            jnp.where(head0, outs[0], outs[1]).astype(BF16))
        return carry

    lax.fori_loop(0, block_rows, body, 0)


def _neighbourhood_attention(qb, kb, vb, bias, B, T):
    G = T // GRID_W
    rb = min(NA_BLOCK_ROWS, G)
    nb = G // rb
    kern = functools.partial(_na_kernel, grid_rows=G, block_rows=rb)
    seq = pl.BlockSpec((None, T, LANES), lambda hp, b, gb: (hp, b, 0))
    qblk = pl.BlockSpec((None, rb * GRID_W, LANES), lambda hp, b, gb: (hp, b * nb + gb, 0))
    return pl.pallas_call(
        kern,
        grid=(N_PAIRS, B, nb),
        in_specs=[qblk, seq, seq,
                  pl.BlockSpec((NA_ROWS, None, 2, GRID_W, NA_KEYS), lambda hp, b, gb: (0, hp, 0, 0, 0))],
        out_specs=qblk,
        out_shape=jax.ShapeDtypeStruct((N_PAIRS, B * T, LANES), BF16),
        compiler_params=_cparams(("arbitrary", "arbitrary", "arbitrary")),
        name="neighbourhood_attention",
    )(qb, kb, vb, bias)


def _layer_norm(h, g, b):
    mu = jnp.mean(h, axis=-1, keepdims=True)
    d = h - mu
    var = jnp.mean(d * d, axis=-1, keepdims=True)
    return d * lax.rsqrt(var + LN_EPS) * g + b


def _outproj_kernel(oa_ref, ob_ref, x_ref, mod_ref, ga_ref, gb_ref, w_ref, lg_ref, lb_ref, wr_ref, br_ref,
                    x1_ref, u2_ref, logit_ref, mix_scr):
    D = D_MODEL
    gate_a = mod_ref[:, 2 * D:3 * D]
    shift_f = mod_ref[:, 3 * D:4 * D]
    scale_f = mod_ref[:, 4 * D:5 * D]

    def rms(o, g):
        return o * lax.rsqrt(jnp.mean(o * o, axis=-1, keepdims=True) + RMS_EPS) * g

    for a in range(RES_PER_STEP):
        oa = jnp.concatenate([oa_ref[a, hp] for hp in range(N_PAIRS)], axis=1).astype(F32)
        ob = jnp.concatenate([ob_ref[hp, :, a * LANES:(a + 1) * LANES] for hp in range(N_PAIRS)], axis=1).astype(F32)
        mix = jnp.concatenate([rms(oa, ga_ref[...]), rms(ob, gb_ref[...])], axis=1)
        mix_scr[a * CHUNK:(a + 1) * CHUNK, :] = mix.astype(BF16)
    y = jnp.dot(mix_scr[...], w_ref[...], preferred_element_type=F32)

    wr = wr_ref[...]
    wr_hi = wr.astype(BF16)
    wr_lo = (wr - wr_hi.astype(F32)).astype(BF16)
    for a in range(RES_PER_STEP):
        xa = x_ref[:, a * D:(a + 1) * D]
        h = DEEPNORM_ALPHA * xa + (1.0 + gate_a) * y[a * CHUNK:(a + 1) * CHUNK]
        x1 = _layer_norm(h, lg_ref[...], lb_ref[...])
        x1_ref[:, a * D:(a + 1) * D] = x1
        u2 = x1 * (1.0 + scale_f) + shift_f
        u2_ref[:, a * D:(a + 1) * D] = u2
        u_hi = u2.astype(BF16)
        u_lo = (u2 - u_hi.astype(F32)).astype(BF16)
        logits = (jnp.dot(u_hi, wr_hi, preferred_element_type=F32)
                  + jnp.dot(u_lo, wr_hi, preferred_element_type=F32)
                  + jnp.dot(u_hi, wr_lo, preferred_element_type=F32)) + br_ref[...]
        logit_ref[:, a * LANES:(a + 1) * LANES] = logits


def _output_projection(oa, ob, x, mod3, mod_row0, g_a, g_b, w_out_bf, ln_g, ln_b, w_router_pad, b_router_pad):
    B, T, D = x.shape
    L = T // RES
    BL = B * L
    cps = L // CHUNK
    nq = RES // RES_PER_STEP
    xv = x.reshape(BL, RES * D)
    obv = ob.reshape(N_PAIRS, BL, RES * LANES)
    row_spec = pl.BlockSpec((CHUNK, RES_PER_STEP * D), lambda i, q: (i, q))
    const = lambda shape: pl.BlockSpec(shape, lambda i, q: tuple(0 for _ in shape))
    x1, u2, logits = pl.pallas_call(
        _outproj_kernel,
        grid=(BL // CHUNK, nq),
        in_specs=[pl.BlockSpec((RES_PER_STEP, N_PAIRS, CHUNK, LANES), lambda i, q: (q, 0, i, 0)),
                  pl.BlockSpec((N_PAIRS, CHUNK, RES_PER_STEP * LANES), lambda i, q: (0, i, q)),
                  row_spec,
                  pl.BlockSpec((None, 1, 6 * D), lambda i, q: (mod_row0 + i // cps, 0, 0)),
                  const((1, WIDTH)), const((1, WIDTH)), const((2 * WIDTH, D)),
                  const((1, D)), const((1, D)), const((D, LANES)), const((1, LANES))],
        out_specs=[row_spec, row_spec, pl.BlockSpec((CHUNK, RES_PER_STEP * LANES), lambda i, q: (i, q))],
        out_shape=[jax.ShapeDtypeStruct((BL, RES * D), F32), jax.ShapeDtypeStruct((BL, RES * D), F32),
                   jax.ShapeDtypeStruct((BL, RES * LANES), F32)],
        scratch_shapes=[pltpu.VMEM((RES_PER_STEP * CHUNK, 2 * WIDTH), BF16)],
        compiler_params=_cparams(("arbitrary", "arbitrary")),
        name="output_projection",
    )(oa, obv, xv, mod3, g_a, g_b, w_out_bf, ln_g, ln_b, w_router_pad, b_router_pad)
    N = B * T
    return x1.reshape(N, D), u2.reshape(N, D), logits.reshape(N, LANES)


def _route_kernel(logit_ref, dest_ref, gate_ref, count_ref, cnt_scr, start_scr, *, block_rows):
    phase = pl.program_id(0)
    i = pl.program_id(1)
    tn = logit_ref.shape[0]
    lane = lax.broadcasted_iota(I32, (tn, LANES), 1)

    @pl.when(i == 0)
    def _():
        @pl.when(phase == 1)
        def _():
            cnt = cnt_scr[...]
            count_ref[...] = cnt
            padded = jnp.ceil(cnt * (1.0 / block_rows)) * block_rows
            lane1 = lax.broadcasted_iota(I32, (1, LANES), 1)
            incl = padded
            s = 1
            while s < LANES:
                incl = incl + jnp.where(lane1 >= s, pltpu.roll(incl, s, 1), 0.0)
                s *= 2
            start_scr[...] = incl - padded
        cnt_scr[...] = jnp.zeros_like(cnt_scr)

    logits = jnp.where(lane < N_EXPERTS, logit_ref[...], -3.0e38)
    vals, hots = [], []
    multi = jnp.zeros((tn, LANES), F32)
    for _ in range(TOP_K):
        m = jnp.max(logits, axis=-1, keepdims=True)
        idx = jnp.min(jnp.where(logits == m, lane, LANES), axis=-1, keepdims=True)
        hot = lane == idx
        vals.append(m)
        hots.append(hot)
        multi = multi + hot.astype(F32)
        logits = jnp.where(hot, -3.0e38, logits)

    @pl.when(phase == 1)
    def _():
        row = lax.broadcasted_iota(I32, (tn, tn), 0)
        col = lax.broadcasted_iota(I32, (tn, tn), 1)
        lower = (col < row).astype(BF16)
        before = jnp.dot(lower, multi.astype(BF16), preferred_element_type=F32)
        slot = start_scr[...] + cnt_scr[...] + before
        es = [jnp.exp(v - vals[0]) for v in vals]
        tot = es[0] + es[1] + es[2] + es[3]
        lane4 = lax.broadcasted_iota(I32, (tn, TOP_K), 1)
        dest = jnp.zeros((tn, TOP_K), F32)
        gates = jnp.zeros((tn, TOP_K), F32)
        for k in range(TOP_K):
            d = jnp.sum(jnp.where(hots[k], slot, 0.0), axis=-1, keepdims=True)
            dest = jnp.where(lane4 == k, d, dest)
            gates = jnp.where(lane4 == k, es[k] / tot, gates)
        dest_ref[...] = dest.astype(I32)
        gate_ref[...] = gates

    cnt_scr[...] = cnt_scr[...] + jnp.sum(multi, axis=0, keepdims=True)


def _routing(logits, block_rows):
    N = logits.shape[0]
    tn = ROUTE_TILE
    kern = functools.partial(_route_kernel, block_rows=block_rows)
    return pl.pallas_call(
        kern,
        grid=(2, N // tn),
        in_specs=[pl.BlockSpec((tn, LANES), lambda p, i: (i, 0))],
        out_specs=[pl.BlockSpec((tn, TOP_K), lambda p, i: (i * p, 0)),
                   pl.BlockSpec((tn, TOP_K), lambda p, i: (i * p, 0)),
                   pl.BlockSpec((1, LANES), lambda p, i: (0, 0))],
        out_shape=[jax.ShapeDtypeStruct((N, TOP_K), I32), jax.ShapeDtypeStruct((N, TOP_K), F32),
                   jax.ShapeDtypeStruct((1, LANES), F32)],
        scratch_shapes=[pltpu.VMEM((1, LANES), F32), pltpu.VMEM((1, LANES), F32)],
        compiler_params=_cparams(("arbitrary", "arbitrary")),
        name="moe_routing",
    )(logits)


def _dispatch_kernel(dest_ref, u_ref, xs_in_ref, xs_ref, sem, *, tile):
    del xs_in_ref
    t0 = pl.program_id(0) * tile

    def row_copy(i, k):
        return pltpu.make_async_copy(u_ref.at[pl.ds(t0 + i, 1)], xs_ref.at[pl.ds(dest_ref[0, i * TOP_K + k], 1)], sem)

    def start(i, carry):
        for k in range(TOP_K):
            row_copy(i, k).start()
        return carry

    lax.fori_loop(0, tile, start, 0)

    def wait(i, carry):
        for k in range(TOP_K):
            row_copy(i, k).wait()
        return carry

    lax.fori_loop(0, tile, wait, 0)


def _dispatch(dest, u2, xs, row0, n_rows):
    tile = SCATTER_TILE
    nt = n_rows // tile
    dest_t = lax.slice_in_dim(dest, row0, row0 + n_rows, axis=0).reshape(nt, 1, tile * TOP_K)
    kern = functools.partial(_dispatch_kernel, tile=tile)
    return pl.pallas_call(
        kern,
        grid=(nt,),
        in_specs=[pl.BlockSpec((None, 1, tile * TOP_K), lambda i: (i, 0, 0), memory_space=pltpu.SMEM),
                  pl.BlockSpec(memory_space=pl.ANY),
                  pl.BlockSpec(memory_space=pl.ANY)],
        out_specs=pl.BlockSpec(memory_space=pl.ANY),
        out_shape=jax.ShapeDtypeStruct(xs.shape, xs.dtype),
        scratch_shapes=[pltpu.SemaphoreType.DMA(())],
        input_output_aliases={2: 0},
        compiler_params=_cparams(("arbitrary",)),
        name="moe_dispatch",
    )(dest_t, u2, xs)


def _expert_kernel(be_ref, nu_ref, x_ref, wu_ref, bu_ref, wd_ref, bd_ref, y_ref):
    del be_ref

    @pl.when(pl.program_id(0) < nu_ref[0])
    def _():
        x = x_ref[...].astype(BF16)
        hu = jnp.dot(x, wu_ref[...], preferred_element_type=F32) + bu_ref[...]
        glu = jnp.minimum(hu[:, :D_FF], SWIGLU_LIMIT)
        lin = jnp.clip(hu[:, D_FF:], -SWIGLU_LIMIT, SWIGLU_LIMIT)
        act = glu * (1.0 / (1.0 + jnp.exp(-SWIGLU_ALPHA * glu))) * (lin + 1.0)
        y_ref[...] = jnp.dot(act.astype(BF16), wd_ref[...], preferred_element_type=F32) + bd_ref[...]

    @pl.when(pl.program_id(0) >= nu_ref[0])
    def _():
        y_ref[...] = jnp.zeros_like(y_ref)


def _expert_blocks(xs, blk_expert, n_used, w_up_de, b_up_de, w_down_bf, b_down):
    n_rows, D = xs.shape
    nblk = n_rows // MOE_ROWS

    def xmap(j, be, nu):
        return (jnp.minimum(j, nu[0] - 1), 0)

    grid_spec = pltpu.PrefetchScalarGridSpec(
        num_scalar_prefetch=2,
        grid=(nblk,),
        in_specs=[pl.BlockSpec((MOE_ROWS, D), xmap),
                  pl.BlockSpec((None, D, 2 * D_FF), lambda j, be, nu: (be[j], 0, 0)),
                  pl.BlockSpec((None, 1, 2 * D_FF), lambda j, be, nu: (be[j], 0, 0)),
                  pl.BlockSpec((None, D_FF, D), lambda j, be, nu: (be[j], 0, 0)),
                  pl.BlockSpec((None, 1, D), lambda j, be, nu: (be[j], 0, 0))],
        out_specs=pl.BlockSpec((MOE_ROWS, D), lambda j, be, nu: (j, 0)),
    )
    return pl.pallas_call(
        _expert_kernel,
        grid_spec=grid_spec,
        out_shape=jax.ShapeDtypeStruct((n_rows, D), F32),
        compiler_params=_cparams(("arbitrary",)),
        name="moe_experts",
    )(blk_expert, n_used, xs, w_up_de, b_up_de, w_down_bf, b_down)


def _combine_kernel(dest_ref, y_ref, gate_ref, x1_ref, mod_ref, lg_ref, lb_ref, out_ref, ybuf, sem, *, tile):
    D = D_MODEL

    def row_copy(i, k):
        return pltpu.make_async_copy(y_ref.at[pl.ds(dest_ref[0, i * TOP_K + k], 1)], ybuf.at[k, pl.ds(i, 1)], sem)

    def start(i, carry):
        for k in range(TOP_K):
            row_copy(i, k).start()
        return carry

    lax.fori_loop(0, tile, start, 0)

    def wait(i, carry):
        for k in range(TOP_K):
            row_copy(i, k).wait()
        return carry

    lax.fori_loop(0, tile, wait, 0)

    gates = gate_ref[...]
    y = jnp.zeros((tile, D), F32)
    for k in range(TOP_K):
        y = y + gates[:, k:k + 1] * ybuf[k]
    gate_f = mod_ref[:, 5 * D:6 * D]
    h = DEEPNORM_ALPHA * x1_ref[...] + (1.0 + gate_f) * y
    out_ref[...] = _layer_norm(h, lg_ref[...], lb_ref[...])


def _combine(dest, y_pad, gates, x1, mod3, mod_row0, ln_g, ln_b, row0, B, T):
    D = D_MODEL
    tile = GATHER_TILE
    n_rows = B * T
    nt = n_rows // tile
    tps = T // tile
    dest_t = lax.slice_in_dim(dest, row0, row0 + n_rows, axis=0).reshape(nt, 1, tile * TOP_K)
    kern = functools.partial(_combine_kernel, tile=tile)
    out = pl.pallas_call(
        kern,
        grid=(nt,),
        in_specs=[pl.BlockSpec((None, 1, tile * TOP_K), lambda i: (i, 0, 0), memory_space=pltpu.SMEM),
                  pl.BlockSpec(memory_space=pl.ANY),
                  pl.BlockSpec((tile, TOP_K), lambda i: (i, 0)),
                  pl.BlockSpec((tile, D), lambda i: (i, 0)),
                  pl.BlockSpec((None, 1, 6 * D), lambda i: (mod_row0 + i // tps, 0, 0)),
                  pl.BlockSpec((1, D), lambda i: (0, 0)),
                  pl.BlockSpec((1, D), lambda i: (0, 0))],
        out_specs=pl.BlockSpec((tile, D), lambda i: (i, 0)),
        out_shape=jax.ShapeDtypeStruct((n_rows, D), F32),
        scratch_shapes=[pltpu.VMEM((TOP_K, tile, D), F32), pltpu.SemaphoreType.DMA(())],
        compiler_params=_cparams(("arbitrary",)),
        name="moe_combine",
    )(dest_t, y_pad, gates, x1, mod3, ln_g, ln_b)
    return out.reshape(B, T, D)


def kernel(x_prompt, x_sample, c_prompt, c_sample, w_mod, b_mod, w_in, rpb, g_out_a, g_out_b, w_out, ln1_g, ln1_b,
           w_router, b_router, w_up, b_up, w_down, b_down, ln2_g, ln2_b):
    D = D_MODEL
    groups = [x_prompt, x_sample]
    conds = [c_prompt, c_sample]
    n_cond = sum(c.shape[0] for c in conds)
    pad_rows = -n_cond % 8
    c_all = jnp.concatenate(conds + [jnp.zeros((pad_rows, D), F32)], axis=0)
    mod3 = _modulation(c_all, w_mod[0], b_mod[0]).reshape(n_cond + pad_rows, 1, 6 * D)

    w_in_bf = w_in[0].astype(BF16)
    w_out_bf = w_out[0].astype(BF16)
    na_bias = _na_bias_table(rpb[0])
    g_a = g_out_a[0].reshape(1, WIDTH)
    g_b = g_out_b[0].reshape(1, WIDTH)
    ln1g, ln1b = ln1_g[0].reshape(1, D), ln1_b[0].reshape(1, D)
    ln2g, ln2b = ln2_g[0].reshape(1, D), ln2_b[0].reshape(1, D)
    w_router_pad = jnp.pad(w_router[0], ((0, 0), (0, LANES - N_EXPERTS)))
    b_router_pad = jnp.pad(b_router[0], (0, LANES - N_EXPERTS)).reshape(1, LANES)

    x1s, u2s, logit_list = [], [], []
    mod_row0 = 0
    mod_rows = []
    for x in groups:
        B, T, _ = x.shape
        cos_v, sin_v = _rope_tables(T)
        qa, ka, va, qb, kb, vb = _input_projection(x, mod3, w_in_bf, cos_v, sin_v, mod_row0)
        oa = _dilated_attention(qa, ka, va, B, T)
        ob = _neighbourhood_attention(qb, kb, vb, na_bias, B, T)
        x1, u2, logits = _output_projection(oa, ob, x, mod3, mod_row0, g_a, g_b, w_out_bf, ln1g, ln1b,
                                            w_router_pad, b_router_pad)
        x1s.append(x1)
        u2s.append(u2)
        logit_list.append(logits)
        mod_rows.append(mod_row0)
        mod_row0 += B

    logits_all = jnp.concatenate(logit_list, axis=0)
    N = logits_all.shape[0]
    dest, gates, counts = _routing(logits_all, MOE_ROWS)
    n_blocks = (N * TOP_K) // MOE_ROWS + N_EXPERTS
    n_rows = n_blocks * MOE_ROWS
    cnt = counts[0, :N_EXPERTS].astype(I32)
    blocks_per = (cnt + MOE_ROWS - 1) // MOE_ROWS
    blk_end = jnp.cumsum(blocks_per)
    n_used = blk_end[-1:].astype(I32)
    blk_expert = jnp.minimum(
        jnp.searchsorted(blk_end, jnp.arange(n_blocks, dtype=I32), side='right'), N_EXPERTS - 1).astype(I32)

    xs = jnp.zeros((n_rows, D), F32)
    row0 = 0
    for u2 in u2s:
        xs = _dispatch(dest, u2, xs, row0, u2.shape[0])
        row0 += u2.shape[0]

    ev, od = w_up[0][:, :, 0::2], w_up[0][:, :, 1::2]
    w_up_de = jnp.concatenate([ev, od], axis=-1).astype(BF16)
    b_up_de = jnp.concatenate([b_up[0][:, 0::2], b_up[0][:, 1::2]], axis=-1).reshape(N_EXPERTS, 1, 2 * D_FF)
    w_down_bf = w_down[0].astype(BF16)
    b_down3 = b_down[0].reshape(N_EXPERTS, 1, D)
    y_pad = _expert_blocks(xs, blk_expert, n_used, w_up_de, b_up_de, w_down_bf, b_down3)

    outs = []
    row0 = 0
    for x, x1, mrow in zip(groups, x1s, mod_rows):
        B, T, _ = x.shape
        x1_all = x1
        outs.append(_combine_group(dest, y_pad, gates, x1_all, mod3, mrow, ln2g, ln2b, row0, B, T))
        row0 += B * T
    return tuple(outs)


def _combine_group(dest, y_pad, gates, x1, mod3, mod_row0, ln_g, ln_b, row0, B, T):
    gates_g = lax.slice_in_dim(gates, row0, row0 + B * T, axis=0)
    return _combine(dest, y_pad, gates_g, x1, mod3, mod_row0, ln_g, ln_b, row0, B, T)
```

```python
import functools

import numpy as np
import jax
import jax.numpy as jnp
from jax import lax
from jax.experimental import pallas as pl
from jax.experimental.pallas import tpu as pltpu

F32 = jnp.float32
BF16 = jnp.bfloat16
I32 = jnp.int32

D_MODEL = 1024
HEAD_DIM = 64
N_HEADS = 8
WIDTH = N_HEADS * HEAD_DIM
N_PAIRS = WIDTH // 128
ROPE_THETA = 10000.0
RADIUS = 64
GRID_W = 64
NA_ROWS = 8
NA_COLS = 16
N_EXPERTS = 32
TOP_K = 4
D_FF = 1024
SWIGLU_ALPHA = 1.702
SWIGLU_LIMIT = 7.0
DEEPNORM_ALPHA = 2.0 ** 0.25
LN_EPS = 1e-5
RMS_EPS = 1e-6
NEG = -1e30

LANES = 128
RES = 16
CHUNK = 128
RES_PER_STEP = 4
MOE_ROWS = 256
ROUTE_TILE = 512
SCATTER_TILE = 256
GATHER_TILE = 128
VMEM_LIMIT = 56 * 1024 * 1024


def _cparams(sem, vmem=VMEM_LIMIT):
    return pltpu.CompilerParams(dimension_semantics=sem, vmem_limit_bytes=vmem)


def _mod_kernel(c_ref, w_ref, b_ref, o_ref):
    c = c_ref[...]
    s = c * (1.0 / (1.0 + jnp.exp(-c)))
    o_ref[...] = jnp.dot(s.astype(BF16), w_ref[...].astype(BF16), preferred_element_type=F32) + b_ref[...]


def _modulation(c, w_mod, b_mod):
    rows = c.shape[0]
    n_out = w_mod.shape[1]
    tn = 1024
    return pl.pallas_call(
        _mod_kernel,
        grid=(n_out // tn,),
        in_specs=[pl.BlockSpec((rows, D_MODEL), lambda j: (0, 0)),
                  pl.BlockSpec((D_MODEL, tn), lambda j: (0, j)),
                  pl.BlockSpec((1, tn), lambda j: (0, j))],
        out_specs=pl.BlockSpec((rows, tn), lambda j: (0, j)),
        out_shape=jax.ShapeDtypeStruct((rows, n_out), F32),
        compiler_params=_cparams(("arbitrary",)),
        name="modulation",
    )(c, w_mod, b_mod.reshape(1, n_out))


def _inproj_kernel(x_ref, mod_ref, w_ref, cos_ref, sin_ref,
                   qa_ref, ka_ref, va_ref, qb_ref, kb_ref, vb_ref, u_scr):
    D = D_MODEL
    shift = mod_ref[:, 0:D]
    scale = mod_ref[:, D:2 * D]
    for a in range(RES_PER_STEP):
        xa = x_ref[:, a * D:(a + 1) * D]
        u_scr[a * CHUNK:(a + 1) * CHUNK, :] = (xa * (1.0 + scale) + shift).astype(BF16)
    u = u_scr[...]

    lane = lax.broadcasted_iota(I32, (1, WIDTH), 1)
    first_half = (lane % HEAD_DIM) < (HEAD_DIM // 2)
    cosf = jnp.concatenate(
        [jnp.tile(cos_ref[:, a * LANES:(a + 1) * LANES], (1, N_PAIRS)) for a in range(RES_PER_STEP)], axis=0)
    sinf = jnp.concatenate(
        [jnp.tile(sin_ref[:, a * LANES:(a + 1) * LANES], (1, N_PAIRS)) for a in range(RES_PER_STEP)], axis=0)

    def proj(i):
        return jnp.dot(u, w_ref[:, i * WIDTH:(i + 1) * WIDTH], preferred_element_type=F32)

    def rope(p):
        rot = jnp.where(first_half, pltpu.roll(p, WIDTH - HEAD_DIM // 2, 1), pltpu.roll(p, HEAD_DIM // 2, 1))
        return p * cosf + rot * sinf

    def emit_res_major(ref, p):
        for a in range(RES_PER_STEP):
            for hp in range(N_PAIRS):
                ref[a, hp] = p[a * CHUNK:(a + 1) * CHUNK, hp * LANES:(hp + 1) * LANES].astype(BF16)

    def emit_natural(ref, p):
        for a in range(RES_PER_STEP):
            for hp in range(N_PAIRS):
                ref[hp, :, a * LANES:(a + 1) * LANES] = (
                    p[a * CHUNK:(a + 1) * CHUNK, hp * LANES:(hp + 1) * LANES].astype(BF16))

    qscale = HEAD_DIM ** -0.5
    emit_res_major(qa_ref, rope(proj(0)) * qscale)
    emit_res_major(ka_ref, rope(proj(1)))
    emit_res_major(va_ref, proj(2))
    emit_natural(qb_ref, proj(3) * qscale)
    emit_natural(kb_ref, proj(4))
    emit_natural(vb_ref, proj(5))


def _input_projection(x, mod3, w_in_bf, cos_v, sin_v, mod_row0):
    B, T, D = x.shape
    L = T // RES
    BL = B * L
    cps = L // CHUNK
    xv = x.reshape(BL, RES * D)
    nq = RES // RES_PER_STEP
    res_shape = jax.ShapeDtypeStruct((RES, N_PAIRS, BL, LANES), BF16)
    nat_shape = jax.ShapeDtypeStruct((N_PAIRS, BL, RES * LANES), BF16)
    res_spec = pl.BlockSpec((RES_PER_STEP, N_PAIRS, CHUNK, LANES), lambda i, q: (q, 0, i, 0))
    nat_spec = pl.BlockSpec((N_PAIRS, CHUNK, RES_PER_STEP * LANES), lambda i, q: (0, i, q))
    outs = pl.pallas_call(
        _inproj_kernel,
        grid=(BL // CHUNK, nq),
        in_specs=[pl.BlockSpec((CHUNK, RES_PER_STEP * D), lambda i, q: (i, q)),
                  pl.BlockSpec((None, 1, 6 * D), lambda i, q: (mod_row0 + i // cps, 0, 0)),
                  pl.BlockSpec((D, 6 * WIDTH), lambda i, q: (0, 0)),
                  pl.BlockSpec((CHUNK, RES_PER_STEP * LANES), lambda i, q: (i % cps, q)),
                  pl.BlockSpec((CHUNK, RES_PER_STEP * LANES), lambda i, q: (i % cps, q))],
        out_specs=[res_spec, res_spec, res_spec, nat_spec, nat_spec, nat_spec],
        out_shape=[res_shape, res_shape, res_shape, nat_shape, nat_shape, nat_shape],
        scratch_shapes=[pltpu.VMEM((RES_PER_STEP * CHUNK, D), BF16)],
        compiler_params=_cparams(("arbitrary", "arbitrary")),
        name="input_projection",
    )(xv, mod3, w_in_bf, cos_v, sin_v)
    qa, ka, va, qb, kb, vb = outs
    N = B * T
    return qa, ka, va, qb.reshape(N_PAIRS, N, LANES), kb.reshape(N_PAIRS, N, LANES), vb.reshape(N_PAIRS, N, LANES)


def _rope_tables(T):
    half = HEAD_DIM // 2
    inv_freq = 1.0 / (ROPE_THETA ** (jnp.arange(half, dtype=F32) / half))
    ang = jnp.arange(T, dtype=F32)[:, None] * inv_freq[None, :]
    cos, sin = jnp.cos(ang), jnp.sin(ang)
    cos_h = jnp.concatenate([cos, cos], axis=-1)
    sin_h = jnp.concatenate([-sin, sin], axis=-1)
    cos2 = jnp.concatenate([cos_h, cos_h], axis=-1)
    sin2 = jnp.concatenate([sin_h, sin_h], axis=-1)
    L = T // RES
    return cos2.reshape(L, RES * LANES), sin2.reshape(L, RES * LANES)


P2_ROWS = 32
P2_KROWS = 64
P1_ROWS = 16
P1_KROWS = 48
HALO = 64
P3_BATCH = 4
P1_BATCH = 2


def _band_tables(has_halo):
    def mask(ok):
        return np.where(ok, 0.0, NEG).astype(np.float32)
    mq = np.arange(CHUNK)[:, None]
    koff3 = np.arange(2 * CHUNK) - HALO
    b3 = mask(np.abs(mq - koff3[None, :]) <= RADIUS)
    j = np.repeat(np.arange(4), P2_ROWS)[:, None]
    a = np.tile(np.arange(P2_ROWS), 4)[:, None]
    jk = np.repeat(np.arange(4), P2_KROWS)[None, :]
    bk = np.tile(np.arange(P2_KROWS), 4)[None, :]
    b2 = mask(np.abs(4 * (a - (bk - 16)) + (j - jk)) <= RADIUS)
    koff2 = (bk - 16).reshape(-1)
    r = np.repeat(np.arange(RES), P1_ROWS)[:, None]
    a = np.tile(np.arange(P1_ROWS), RES)[:, None]
    rk = np.repeat(np.arange(RES), P1_KROWS)[None, :]
    bk = np.tile(np.arange(P1_KROWS), RES)[None, :]
    b1 = mask(np.abs(RES * (a - (bk - 16)) + (r - rk)) <= RADIUS)
    koff1 = (bk - 16).reshape(-1)
    return (jnp.asarray(b3), jnp.asarray(koff3.astype(np.int32)[None, :]),
            jnp.asarray(b2), jnp.asarray(koff2.astype(np.int32)[None, :]),
            jnp.asarray(b1), jnp.asarray(koff1.astype(np.int32)[None, :]))


def _attend_pairs(problems, head0):
    scores = []
    for q, k, _, _ in problems:
        for h in range(2):
            sel = head0 if h == 0 else jnp.logical_not(head0)
            qh = jnp.where(sel, q, jnp.zeros_like(q))
            scores.append(lax.dot_general(qh, k, (((1,), (1,)), ((), ())), preferred_element_type=F32))
    probs, stats = [], []
    for i, s in enumerate(scores):
        s = s + problems[i // 2][3]
        m = jnp.max(s, axis=-1, keepdims=True)
        p = jnp.exp(s - m)
        stats.append((m, jnp.sum(p, axis=-1, keepdims=True)))
        probs.append(p.astype(BF16))
    outs = [jnp.dot(p, problems[i // 2][2], preferred_element_type=F32) for i, p in enumerate(probs)]
    results = []
    for i in range(len(problems)):
        (m0, l0), (m1, l1) = stats[2 * i], stats[2 * i + 1]
        results.append((jnp.where(head0, outs[2 * i], outs[2 * i + 1]),
                        jnp.where(head0, m0, m1), jnp.where(head0, l0, l1)))
    return results


def _dilated_kernel(*refs, has_halo, seq_rows, chunks_per_seq):
    if has_halo:
        (q_ref, k_ref, v_ref, kp_ref, kn_ref, vp_ref, vn_ref,
         b3_ref, o3_ref, b2_ref, o2_ref, b1_ref, o1_ref,
         out_ref, kf, vf, acc, ms, ls) = refs
    else:
        (q_ref, k_ref, v_ref, b3_ref, o3_ref, b2_ref, o2_ref, b1_ref, o1_ref,
         out_ref, kf, vf, acc, ms, ls) = refs

    c = pl.program_id(0) % chunks_per_seq
    row0 = c * CHUNK
    lane = lax.broadcasted_iota(I32, (1, LANES), 1)
    head0 = lane < HEAD_DIM

    for r in range(RES):
        kf[r, HALO:HALO + CHUNK, :] = k_ref[r]
        vf[r, HALO:HALO + CHUNK, :] = v_ref[r]
        if has_halo:
            kf[r, 0:HALO, :] = kp_ref[r]
            vf[r, 0:HALO, :] = vp_ref[r]
            kf[r, HALO + CHUNK:, :] = kn_ref[r]
            vf[r, HALO + CHUNK:, :] = vn_ref[r]
        else:
            zeros = jnp.zeros((HALO, LANES), BF16)
            kf[r, 0:HALO, :] = zeros
            vf[r, 0:HALO, :] = zeros
            kf[r, HALO + CHUNK:, :] = zeros
            vf[r, HALO + CHUNK:, :] = zeros

    def in_seq(base, off_ref):
        kv_row = base + off_ref[...]
        return jnp.where((kv_row >= 0) & (kv_row < seq_rows), 0.0, NEG).astype(F32)

    def attend(problems):
        return _attend_pairs(problems, head0)

    def merge(a_old, m_old, l_old, o, m, l):
        mn = jnp.maximum(m_old, m)
        wa = jnp.exp(m_old - mn)
        wb = jnp.exp(m - mn)
        return a_old * wa + o * wb, mn, l_old * wa + l * wb

    bias3 = b3_ref[...] + in_seq(row0, o3_ref)

    def body3(it, carry):
        rs = [it * P3_BATCH + u for u in range(P3_BATCH)]
        res = attend([(q_ref[r], kf[r], vf[r], bias3) for r in rs])
        for r, (o, m, l) in zip(rs, res):
            acc[r] = o
            ms[r] = m
            ls[r] = l
        return carry

    lax.fori_loop(0, RES // P3_BATCH, body3, 0)

    b2 = b2_ref[...]

    def body2(g, carry):
        qs = pl.multiple_of(g * P2_ROWS, P2_ROWS)
        ks = pl.multiple_of(HALO - 16 + g * P2_ROWS, 16)
        bias = b2 + in_seq(row0 + g * P2_ROWS, o2_ref)

        def gather(ref, r4, start, rows):
            return jnp.concatenate([ref[4 * j + r4, pl.ds(start, rows), :] for j in range(4)], axis=0)

        res = attend([(gather(q_ref, r4, qs, P2_ROWS), gather(kf, r4, ks, P2_KROWS),
                       gather(vf, r4, ks, P2_KROWS), bias) for r4 in range(4)])
        for r4, (o, m, l) in enumerate(res):
            a_new, m_new, l_new = merge(gather(acc, r4, qs, P2_ROWS), gather(ms, r4, qs, P2_ROWS),
                                        gather(ls, r4, qs, P2_ROWS), o, m, l)
            for j in range(4):
                acc[4 * j + r4, pl.ds(qs, P2_ROWS), :] = a_new[j * P2_ROWS:(j + 1) * P2_ROWS]
                ms[4 * j + r4, pl.ds(qs, P2_ROWS), :] = m_new[j * P2_ROWS:(j + 1) * P2_ROWS]
                ls[4 * j + r4, pl.ds(qs, P2_ROWS), :] = l_new[j * P2_ROWS:(j + 1) * P2_ROWS]
        return carry

    lax.fori_loop(0, CHUNK // P2_ROWS, body2, 0)

    b1 = b1_ref[...]

    def body1(it, carry):
        def gather(ref, start, rows):
            return jnp.concatenate([ref[r, pl.ds(start, rows), :] for r in range(RES)], axis=0)

        starts, problems = [], []
        for u in range(P1_BATCH):
            g = it * P1_BATCH + u
            qs = pl.multiple_of(g * P1_ROWS, P1_ROWS)
            ks = pl.multiple_of(HALO - 16 + g * P1_ROWS, 16)
            starts.append(qs)
            problems.append((gather(q_ref, qs, P1_ROWS), gather(kf, ks, P1_KROWS), gather(vf, ks, P1_KROWS),
                             b1 + in_seq(row0 + g * P1_ROWS, o1_ref)))
        for qs, (o, m, l) in zip(starts, attend(problems)):
            a_new, m_new, l_new = merge(gather(acc, qs, P1_ROWS), gather(ms, qs, P1_ROWS),
                                        gather(ls, qs, P1_ROWS), o, m, l)
            for r in range(RES):
                acc[r, pl.ds(qs, P1_ROWS), :] = a_new[r * P1_ROWS:(r + 1) * P1_ROWS]
                ms[r, pl.ds(qs, P1_ROWS), :] = m_new[r * P1_ROWS:(r + 1) * P1_ROWS]
                ls[r, pl.ds(qs, P1_ROWS), :] = l_new[r * P1_ROWS:(r + 1) * P1_ROWS]
        return carry

    lax.fori_loop(0, CHUNK // P1_ROWS // P1_BATCH, body1, 0)

    def body_out(r, carry):
        out_ref[r] = (acc[r] / ls[r]).astype(BF16)
        return carry

    lax.fori_loop(0, RES, body_out, 0)


def _dilated_attention(qa, ka, va, B, T):
    L = T // RES
    BL = B * L
    cps = L // CHUNK
    has_halo = cps > 1
    tables = _band_tables(has_halo)
    blk = (RES, None, CHUNK, LANES)
    center = pl.BlockSpec(blk, lambda i, hp: (0, hp, i, 0))
    in_specs = [center, center, center]
    args = [qa, ka, va]
    if has_halo:
        hblk = (RES, None, HALO, LANES)
        per = CHUNK // HALO

        def prev_map(i, hp):
            return (0, hp, jnp.maximum(per * i - 1, (i // cps) * cps * per), 0)

        def next_map(i, hp):
            return (0, hp, jnp.minimum(per * i + per, (i // cps + 1) * cps * per - 1), 0)

        in_specs += [pl.BlockSpec(hblk, prev_map), pl.BlockSpec(hblk, next_map),
                     pl.BlockSpec(hblk, prev_map), pl.BlockSpec(hblk, next_map)]
        args += [ka, ka, va, va]
    for t in tables:
        in_specs.append(pl.BlockSpec(t.shape, lambda i, hp: (0, 0)))
        args.append(t)
    kern = functools.partial(_dilated_kernel, has_halo=has_halo, seq_rows=L, chunks_per_seq=cps)
    return pl.pallas_call(
        kern,
        grid=(BL // CHUNK, N_PAIRS),
        in_specs=in_specs,
        out_specs=pl.BlockSpec(blk, lambda i, hp: (0, hp, i, 0)),
        out_shape=jax.ShapeDtypeStruct((RES, N_PAIRS, BL, LANES), BF16),
        scratch_shapes=[pltpu.VMEM((RES, 2 * CHUNK, LANES), BF16),
                        pltpu.VMEM((RES, 2 * CHUNK, LANES), BF16),
                        pltpu.VMEM((RES, CHUNK, LANES), F32),
                        pltpu.VMEM((RES, CHUNK, LANES), F32),
                        pltpu.VMEM((RES, CHUNK, LANES), F32)],
        compiler_params=_cparams(("arbitrary", "arbitrary")),
        name="dilated_attention",
    )(*args)


NA_KEYS = NA_ROWS * GRID_W
NA_BLOCK_ROWS = 32
NA_BATCH = 8


def _na_bias_table(rpb):
    c = np.arange(GRID_W)
    col_start = np.clip(c - NA_COLS // 2, 0, GRID_W - NA_COLS)
    col_mask = (c[None, :] >= col_start[:, None]) & (c[None, :] < col_start[:, None] + NA_COLS)
    dc_idx = np.clip(c[None, :] - c[:, None], -(NA_COLS - 1), NA_COLS - 1) + NA_COLS - 1
    rel = rpb.astype(F32)[:, :, dc_idx]
    rel = jnp.where(col_mask[None, None], rel, NEG)
    per_off = [rel[:, d0:d0 + NA_ROWS].transpose(0, 2, 1, 3).reshape(N_HEADS, GRID_W, NA_KEYS)
               for d0 in range(NA_ROWS)]
    return jnp.stack(per_off, axis=0).reshape(NA_ROWS, N_PAIRS, 2, GRID_W, NA_KEYS)


def _na_kernel(q_ref, k_ref, v_ref, bias_ref, out_ref, *, grid_rows, block_rows):
    gb = pl.program_id(2)
    lane = lax.broadcasted_iota(I32, (1, LANES), 1)
    head0 = lane < HEAD_DIM

    def body(it, carry):
        rows, scores = [], []
        for u in range(NA_BATCH):
            i = it * NA_BATCH + u
            g = gb * block_rows + i
            rs = jnp.clip(g - NA_ROWS // 2, 0, grid_rows - NA_ROWS)
            d0 = rs - g + NA_ROWS - 1
            qs = pl.multiple_of(i * GRID_W, GRID_W)
            ks = pl.multiple_of(rs * GRID_W, GRID_W)
            q = q_ref[pl.ds(qs, GRID_W), :]
            k = k_ref[pl.ds(ks, NA_KEYS), :]
            rows.append((qs, ks, d0))
            for h in range(2):
                sel = head0 if h == 0 else jnp.logical_not(head0)
                qh = jnp.where(sel, q, jnp.zeros_like(q))
                scores.append(lax.dot_general(qh, k, (((1,), (1,)), ((), ())), preferred_element_type=F32))
        probs, sums = [], []
        for n, s in enumerate(scores):
            s = s + bias_ref[rows[n // 2][2], n % 2]
            p = jnp.exp(s - jnp.max(s, axis=-1, keepdims=True))
            sums.append(jnp.sum(p, axis=-1, keepdims=True))
            probs.append(p.astype(BF16))
        outs = []
        for n, p in enumerate(probs):
            v = v_ref[pl.ds(rows[n // 2][1], NA_KEYS), :]
            outs.append(jnp.dot(p, v, preferred_element_type=F32) / sums[n])
        for u, (qs, _, _) in enumerate(rows):
            out_ref[pl.ds(qs, GRID_W), :] = jnp.where(head0, outs[2 * u], outs[2 * u + 1]).astype(BF16)
        return carry

    lax.fori_loop(0, block_rows // NA_BATCH, body, 0)


def _neighbourhood_attention(qb, kb, vb, bias, B, T):
    G = T // GRID_W
    rb = min(NA_BLOCK_ROWS, G)
    nb = G // rb
    kern = functools.partial(_na_kernel, grid_rows=G, block_rows=rb)
    seq = pl.BlockSpec((None, T, LANES), lambda hp, b, gb: (hp, b, 0))
    qblk = pl.BlockSpec((None, rb * GRID_W, LANES), lambda hp, b, gb: (hp, b * nb + gb, 0))
    return pl.pallas_call(
        kern,
        grid=(N_PAIRS, B, nb),
        in_specs=[qblk, seq, seq,
                  pl.BlockSpec((NA_ROWS, None, 2, GRID_W, NA_KEYS), lambda hp, b, gb: (0, hp, 0, 0, 0))],
        out_specs=qblk,
        out_shape=jax.ShapeDtypeStruct((N_PAIRS, B * T, LANES), BF16),
        compiler_params=_cparams(("arbitrary", "arbitrary", "arbitrary")),
        name="neighbourhood_attention",
    )(qb, kb, vb, bias)


def _layer_norm(h, g, b):
    mu = jnp.mean(h, axis=-1, keepdims=True)
    d = h - mu
    var = jnp.mean(d * d, axis=-1, keepdims=True)
    return d * lax.rsqrt(var + LN_EPS) * g + b


def _outproj_kernel(oa_ref, ob_ref, x_ref, mod_ref, ga_ref, gb_ref, w_ref, lg_ref, lb_ref, wr_ref, br_ref,
                    x1_ref, u2_ref, logit_ref, mix_scr):
    D = D_MODEL
    gate_a = mod_ref[:, 2 * D:3 * D]
    shift_f = mod_ref[:, 3 * D:4 * D]
    scale_f = mod_ref[:, 4 * D:5 * D]

    def rms(o, g):
        return o * lax.rsqrt(jnp.mean(o * o, axis=-1, keepdims=True) + RMS_EPS) * g

    for a in range(RES_PER_STEP):
        oa = jnp.concatenate([oa_ref[a, hp] for hp in range(N_PAIRS)], axis=1).astype(F32)
        ob = jnp.concatenate([ob_ref[hp, :, a * LANES:(a + 1) * LANES] for hp in range(N_PAIRS)], axis=1).astype(F32)
        mix = jnp.concatenate([rms(oa, ga_ref[...]), rms(ob, gb_ref[...])], axis=1)
        mix_scr[a * CHUNK:(a + 1) * CHUNK, :] = mix.astype(BF16)
    y = jnp.dot(mix_scr[...], w_ref[...], preferred_element_type=F32)

    wr = wr_ref[...]
    wr_hi = wr.astype(BF16)
    wr_lo = (wr - wr_hi.astype(F32)).astype(BF16)
    for a in range(RES_PER_STEP):
        xa = x_ref[:, a * D:(a + 1) * D]
        h = DEEPNORM_ALPHA * xa + (1.0 + gate_a) * y[a * CHUNK:(a + 1) * CHUNK]
        x1 = _layer_norm(h, lg_ref[...], lb_ref[...])
        x1_ref[:, a * D:(a + 1) * D] = x1
        u2 = x1 * (1.0 + scale_f) + shift_f
        u2_ref[:, a * D:(a + 1) * D] = u2.astype(BF16)
        u_hi = u2.astype(BF16)
        u_lo = (u2 - u_hi.astype(F32)).astype(BF16)
        logits = (jnp.dot(u_hi, wr_hi, preferred_element_type=F32)
                  + jnp.dot(u_lo, wr_hi, preferred_element_type=F32)
                  + jnp.dot(u_hi, wr_lo, preferred_element_type=F32)) + br_ref[...]
        logit_ref[:, a * LANES:(a + 1) * LANES] = logits


def _output_projection(oa, ob, x, mod3, mod_row0, g_a, g_b, w_out_bf, ln_g, ln_b, w_router_pad, b_router_pad):
    B, T, D = x.shape
    L = T // RES
    BL = B * L
    cps = L // CHUNK
    nq = RES // RES_PER_STEP
    xv = x.reshape(BL, RES * D)
    obv = ob.reshape(N_PAIRS, BL, RES * LANES)
    row_spec = pl.BlockSpec((CHUNK, RES_PER_STEP * D), lambda i, q: (i, q))
    const = lambda shape: pl.BlockSpec(shape, lambda i, q: tuple(0 for _ in shape))
    x1, u2, logits = pl.pallas_call(
        _outproj_kernel,
        grid=(BL // CHUNK, nq),
        in_specs=[pl.BlockSpec((RES_PER_STEP, N_PAIRS, CHUNK, LANES), lambda i, q: (q, 0, i, 0)),
                  pl.BlockSpec((N_PAIRS, CHUNK, RES_PER_STEP * LANES), lambda i, q: (0, i, q)),
                  row_spec,
                  pl.BlockSpec((None, 1, 6 * D), lambda i, q: (mod_row0 + i // cps, 0, 0)),
                  const((1, WIDTH)), const((1, WIDTH)), const((2 * WIDTH, D)),
                  const((1, D)), const((1, D)), const((D, LANES)), const((1, LANES))],
        out_specs=[row_spec, row_spec, pl.BlockSpec((CHUNK, RES_PER_STEP * LANES), lambda i, q: (i, q))],
        out_shape=[jax.ShapeDtypeStruct((BL, RES * D), F32), jax.ShapeDtypeStruct((BL, RES * D), BF16),
                   jax.ShapeDtypeStruct((BL, RES * LANES), F32)],
        scratch_shapes=[pltpu.VMEM((RES_PER_STEP * CHUNK, 2 * WIDTH), BF16)],
        compiler_params=_cparams(("arbitrary", "arbitrary")),
        name="output_projection",
    )(oa, obv, xv, mod3, g_a, g_b, w_out_bf, ln_g, ln_b, w_router_pad, b_router_pad)
    N = B * T
    return x1.reshape(N, D), u2.reshape(N, D), logits.reshape(N, LANES)


def _route_kernel(logit_ref, idx_ref, gate_ref, cnt_ref):
    tn = logit_ref.shape[0]
    lane = lax.broadcasted_iota(I32, (tn, LANES), 1)
    logits = jnp.where(lane < N_EXPERTS, logit_ref[...], -3.0e38)
    vals, idxs = [], []
    multi = jnp.zeros((tn, LANES), F32)
    for _ in range(TOP_K):
        m = jnp.max(logits, axis=-1, keepdims=True)
        idx = jnp.min(jnp.where(logits == m, lane, LANES), axis=-1, keepdims=True)
        hot = lane == idx
        vals.append(m)
        idxs.append(idx)
        multi = multi + hot.astype(F32)
        logits = jnp.where(hot, -3.0e38, logits)
    es = [jnp.exp(v - vals[0]) for v in vals]
    tot = es[0] + es[1] + es[2] + es[3]
    lane4 = lax.broadcasted_iota(I32, (tn, TOP_K), 1)
    sel = jnp.zeros((tn, TOP_K), I32)
    gates = jnp.zeros((tn, TOP_K), F32)
    for k in range(TOP_K):
        sel = jnp.where(lane4 == k, idxs[k], sel)
        gates = jnp.where(lane4 == k, es[k] / tot, gates)
    idx_ref[...] = sel
    gate_ref[...] = gates
    cnt_ref[...] = jnp.sum(multi, axis=0, keepdims=True).astype(I32)


def _routing(logits):
    N = logits.shape[0]
    tn = SCATTER_TILE
    nt = N // tn
    return pl.pallas_call(
        _route_kernel,
        grid=(nt,),
        in_specs=[pl.BlockSpec((tn, LANES), lambda i: (i, 0))],
        out_specs=[pl.BlockSpec((tn, TOP_K), lambda i: (i, 0)),
                   pl.BlockSpec((tn, TOP_K), lambda i: (i, 0)),
                   pl.BlockSpec((None, 1, LANES), lambda i: (i, 0, 0))],
        out_shape=[jax.ShapeDtypeStruct((N, TOP_K), I32), jax.ShapeDtypeStruct((N, TOP_K), F32),
                   jax.ShapeDtypeStruct((nt, 1, LANES), I32)],
        compiler_params=_cparams(("arbitrary",)),
        name="moe_routing",
    )(logits)


SEG = 8
STAGE_ROWS = SCATTER_TILE * TOP_K + N_EXPERTS * SEG


def _lane_prefix_exclusive(v):
    lane = lax.broadcasted_iota(I32, v.shape, 1)
    incl = v
    s = 1
    while s < LANES:
        incl = incl + jnp.where(lane >= s, pltpu.roll(incl, s, 1), 0.0)
        s *= 2
    return incl - v


def _tile_ranks(idx, tn):
    lane = lax.broadcasted_iota(I32, (tn, LANES), 1)
    hots = [lane == idx[:, k:k + 1] for k in range(TOP_K)]
    multi = jnp.zeros((tn, LANES), F32)
    for h in hots:
        multi = multi + h.astype(F32)
    row = lax.broadcasted_iota(I32, (tn, tn), 0)
    col = lax.broadcasted_iota(I32, (tn, tn), 1)
    lower = (col < row).astype(BF16)
    before = jnp.dot(lower, multi.astype(BF16), preferred_element_type=F32)
    return hots, before


def _dispatch_kernel(*refs, first):
    if first:
        cnt_s, start_s, tail_s, idx_ref, u_ref, cntv_ref, xs_ref, stage, zbuf, sem = refs
    else:
        cnt_s, start_s, tail_s, idx_ref, u_ref, cntv_ref, _, xs_ref, stage, zbuf, sem = refs
    tn = idx_ref.shape[0]

    if first:
        @pl.when(pl.program_id(0) == 0)
        def _():
            zbuf[...] = jnp.zeros_like(zbuf)

            def tail_copy(row):
                return pltpu.make_async_copy(zbuf, xs_ref.at[pl.ds(pl.multiple_of(row, SEG), SEG)], sem)

            def fill(e, n):
                lo = tail_s[0, e]
                nfull = (tail_s[1, e] - lo) // SEG

                def one(q, c):
                    tail_copy(lo + q * SEG).start()
                    return c

                lax.fori_loop(0, nfull, one, 0)
                return n + nfull

            n = lax.fori_loop(0, N_EXPERTS, fill, 0)

            def drain(q, c):
                tail_copy(0).wait()
                return c

            lax.fori_loop(0, n, drain, 0)

    hots, before = _tile_ranks(idx_ref[...], tn)
    cntf = cntv_ref[...].astype(F32)
    seg_len = jnp.ceil(cntf * (1.0 / SEG)) * SEG
    seg_off = _lane_prefix_exclusive(seg_len)
    lane_r = lax.broadcasted_iota(I32, (tn, STAGE_ROWS), 1)
    onehot = jnp.zeros((tn, STAGE_ROWS), F32)
    for k in range(TOP_K):
        stage_row = jnp.sum(jnp.where(hots[k], seg_off + before, 0.0), axis=-1, keepdims=True)
        onehot = jnp.where(lane_r == stage_row.astype(I32), 1.0, onehot)
    stage[...] = lax.dot_general(onehot.astype(BF16), u_ref[...], (((0,), (0,)), ((), ())),
                                 preferred_element_type=F32)

    def seg_copy(src_row, dst_row):
        return pltpu.make_async_copy(stage.at[pl.ds(pl.multiple_of(src_row, SEG), SEG)],
                                     xs_ref.at[pl.ds(pl.multiple_of(dst_row, SEG), SEG)], sem)

    def per_expert(e, carry):
        off, n = carry
        nseg = (cnt_s[0, e] + SEG - 1) // SEG
        dst = start_s[0, e]

        def one(q, c):
            seg_copy(off + q * SEG, dst + q * SEG).start()
            return c

        lax.fori_loop(0, nseg, one, 0)
        return off + nseg * SEG, n + nseg

    _, n = lax.fori_loop(0, N_EXPERTS, per_expert, (0, 0))

    def drain(q, c):
        seg_copy(0, 0).wait()
        return c

    lax.fori_loop(0, n, drain, 0)


def _dispatch(idx, u2, tile_cnt, tile_start, tails, xs, xs_rows):
    n, D = u2.shape
    tn = SCATTER_TILE
    nt = n // tn
    first = xs is None
    smem = lambda shape, imap: pl.BlockSpec(shape, imap, memory_space=pltpu.SMEM)
    kern = functools.partial(_dispatch_kernel, first=first)
    in_specs = [smem((None, 1, LANES), lambda i: (i, 0, 0)),
                smem((None, 1, LANES), lambda i: (i, 0, 0)),
                smem((2, LANES), lambda i: (0, 0)),
                pl.BlockSpec((tn, TOP_K), lambda i: (i, 0)),
                pl.BlockSpec((tn, D), lambda i: (i, 0)),
                pl.BlockSpec((None, 1, LANES), lambda i: (i, 0, 0))]
    args = [tile_cnt, tile_start, tails, idx, u2, tile_cnt]
    alias = {}
    if not first:
        alias = {len(args): 0}
        in_specs.append(pl.BlockSpec(memory_space=pl.ANY))
        args.append(xs)
    return pl.pallas_call(
        kern,
        grid=(nt,),
        in_specs=in_specs,
        out_specs=pl.BlockSpec(memory_space=pl.ANY),
        out_shape=jax.ShapeDtypeStruct((xs_rows, D), F32),
        scratch_shapes=[pltpu.VMEM((STAGE_ROWS, D), F32), pltpu.VMEM((SEG, D), F32), pltpu.SemaphoreType.DMA(())],
        input_output_aliases=alias,
        compiler_params=_cparams(("arbitrary",)),
        name="moe_dispatch",
    )(*args)


DEINT = 256


def _deinterleave_kernel(w_ref, p_ref, o_ref):
    n_groups = w_ref.shape[1] // DEINT
    half = DEINT // 2
    f = w_ref.shape[1] // 2
    for g in range(n_groups):
        wg = w_ref[:, g * DEINT:(g + 1) * DEINT].astype(BF16)
        y = jnp.dot(wg, p_ref[...], preferred_element_type=F32)
        o_ref[:, g * half:(g + 1) * half] = y[:, :half].astype(BF16)
        o_ref[:, f + g * half:f + (g + 1) * half] = y[:, half:].astype(BF16)


def _deinterleave_up(w_up):
    E, D, F2 = w_up.shape
    perm = np.zeros((DEINT, DEINT), np.float32)
    j = np.arange(DEINT // 2)
    perm[2 * j, j] = 1.0
    perm[2 * j + 1, DEINT // 2 + j] = 1.0
    return pl.pallas_call(
        _deinterleave_kernel,
        grid=(E,),
        in_specs=[pl.BlockSpec((None, D, F2), lambda e: (e, 0, 0)),
                  pl.BlockSpec((DEINT, DEINT), lambda e: (0, 0))],
        out_specs=pl.BlockSpec((None, D, F2), lambda e: (e, 0, 0)),
        out_shape=jax.ShapeDtypeStruct((E, D, F2), BF16),
        compiler_params=_cparams(("arbitrary",)),
        name="expert_weight_layout",
    )(w_up, jnp.asarray(perm, BF16))


def _expert_kernel(be_ref, nu_ref, x_ref, wu_ref, bu_ref, wd_ref, bd_ref, y_ref):
    del be_ref

    @pl.when(pl.program_id(0) < nu_ref[0])
    def _():
        x = x_ref[...].astype(BF16)
        hu = jnp.dot(x, wu_ref[...], preferred_element_type=F32) + bu_ref[...]
        glu = jnp.minimum(hu[:, :D_FF], SWIGLU_LIMIT)
        lin = jnp.clip(hu[:, D_FF:], -SWIGLU_LIMIT, SWIGLU_LIMIT)
        act = glu * (1.0 / (1.0 + jnp.exp(-SWIGLU_ALPHA * glu))) * (lin + 1.0)
        y_ref[...] = jnp.dot(act.astype(BF16), wd_ref[...], preferred_element_type=F32) + bd_ref[...]

    @pl.when(pl.program_id(0) >= nu_ref[0])
    def _():
        y_ref[...] = jnp.zeros_like(y_ref)


def _expert_blocks(xs, blk_expert, n_used, w_up_de, b_up_de, w_down_bf, b_down):
    n_rows, D = xs.shape
    nblk = n_rows // MOE_ROWS

    def xmap(j, be, nu):
        return (jnp.minimum(j, nu[0] - 1), 0)

    grid_spec = pltpu.PrefetchScalarGridSpec(
        num_scalar_prefetch=2,
        grid=(nblk,),
        in_specs=[pl.BlockSpec((MOE_ROWS, D), xmap),
                  pl.BlockSpec((None, D, 2 * D_FF), lambda j, be, nu: (be[j], 0, 0)),
                  pl.BlockSpec((None, 1, 2 * D_FF), lambda j, be, nu: (be[j], 0, 0)),
                  pl.BlockSpec((None, D_FF, D), lambda j, be, nu: (be[j], 0, 0)),
                  pl.BlockSpec((None, 1, D), lambda j, be, nu: (be[j], 0, 0))],
        out_specs=pl.BlockSpec((MOE_ROWS, D), lambda j, be, nu: (j, 0)),
    )
    return pl.pallas_call(
        _expert_kernel,
        grid_spec=grid_spec,
        out_shape=jax.ShapeDtypeStruct((n_rows, D), F32),
        compiler_params=_cparams(("arbitrary",)),
        name="moe_experts",
    )(blk_expert, n_used, xs, w_up_de, b_up_de, w_down_bf, b_down)


def _combine_kernel(cnt_s, start_s, idx_ref, gate_ref, cntv_ref, y_ref, x1_ref, mod_ref,
                    lg_ref, lb_ref, out_ref, stage, sem):
    D = D_MODEL
    tn = idx_ref.shape[0]

    def seg_copy(src_row, dst_row):
        return pltpu.make_async_copy(y_ref.at[pl.ds(pl.multiple_of(src_row, SEG), SEG)],
                                     stage.at[pl.ds(pl.multiple_of(dst_row, SEG), SEG)], sem)

    @pl.when(pl.program_id(0) == 0)
    def _():
        stage[...] = jnp.zeros_like(stage)

    def per_expert(e, carry):
        off, n = carry
        nseg = (cnt_s[0, e] + SEG - 1) // SEG
        src = start_s[0, e]

        def one(q, c):
            seg_copy(src + q * SEG, off + q * SEG).start()
            return c

        lax.fori_loop(0, nseg, one, 0)
        return off + nseg * SEG, n + nseg

    _, n_seg = lax.fori_loop(0, N_EXPERTS, per_expert, (0, 0))

    hots, before = _tile_ranks(idx_ref[...], tn)
    seg_len = jnp.ceil(cntv_ref[...].astype(F32) * (1.0 / SEG)) * SEG
    seg_off = _lane_prefix_exclusive(seg_len)
    gates = gate_ref[...]
    lane_r = lax.broadcasted_iota(I32, (tn, STAGE_ROWS), 1)
    weights = jnp.zeros((tn, STAGE_ROWS), F32)
    for k in range(TOP_K):
        stage_row = jnp.sum(jnp.where(hots[k], seg_off + before, 0.0), axis=-1, keepdims=True)
        weights = jnp.where(lane_r == stage_row.astype(I32), gates[:, k:k + 1], weights)

    def drain(q, c):
        seg_copy(0, 0).wait()
        return c

    lax.fori_loop(0, n_seg, drain, 0)

    y = jnp.dot(weights.astype(BF16), stage[...].astype(BF16), preferred_element_type=F32)
    gate_f = mod_ref[:, 5 * D:6 * D]
    h = DEEPNORM_ALPHA * x1_ref[...] + (1.0 + gate_f) * y
    out_ref[...] = _layer_norm(h, lg_ref[...], lb_ref[...])


def _combine(idx, gates, tile_cnt, tile_start, y_pad, x1, mod3, mod_row0, ln_g, ln_b, B, T):
    D = D_MODEL
    tn = SCATTER_TILE
    n_rows = B * T
    nt = n_rows // tn
    tps = T // tn
    smem = lambda imap: pl.BlockSpec((None, 1, LANES), imap, memory_space=pltpu.SMEM)
    out = pl.pallas_call(
        _combine_kernel,
        grid=(nt,),
        in_specs=[smem(lambda i: (i, 0, 0)), smem(lambda i: (i, 0, 0)),
                  pl.BlockSpec((tn, TOP_K), lambda i: (i, 0)),
                  pl.BlockSpec((tn, TOP_K), lambda i: (i, 0)),
                  pl.BlockSpec((None, 1, LANES), lambda i: (i, 0, 0)),
                  pl.BlockSpec(memory_space=pl.ANY),
                  pl.BlockSpec((tn, D), lambda i: (i, 0)),
                  pl.BlockSpec((None, 1, 6 * D), lambda i: (mod_row0 + i // tps, 0, 0)),
                  pl.BlockSpec((1, D), lambda i: (0, 0)),
                  pl.BlockSpec((1, D), lambda i: (0, 0))],
        out_specs=pl.BlockSpec((tn, D), lambda i: (i, 0)),
        out_shape=jax.ShapeDtypeStruct((n_rows, D), F32),
        scratch_shapes=[pltpu.VMEM((STAGE_ROWS, D), F32), pltpu.SemaphoreType.DMA(())],
        compiler_params=_cparams(("arbitrary",)),
        name="moe_combine",
    )(tile_cnt, tile_start, idx, gates, tile_cnt, y_pad, x1, mod3, ln_g, ln_b)
    return out.reshape(B, T, D)


def kernel(x_prompt, x_sample, c_prompt, c_sample, w_mod, b_mod, w_in, rpb, g_out_a, g_out_b, w_out, ln1_g, ln1_b,
           w_router, b_router, w_up, b_up, w_down, b_down, ln2_g, ln2_b):
    D = D_MODEL
    groups = [x_prompt, x_sample]
    conds = [c_prompt, c_sample]
    n_cond = sum(c.shape[0] for c in conds)
    pad_rows = -n_cond % 8
    c_all = jnp.concatenate(conds + [jnp.zeros((pad_rows, D), F32)], axis=0)
    mod3 = _modulation(c_all, w_mod[0], b_mod[0]).reshape(n_cond + pad_rows, 1, 6 * D)

    w_in_bf = w_in[0].astype(BF16)
    w_out_bf = w_out[0].astype(BF16)
    na_bias = _na_bias_table(rpb[0])
    g_a = g_out_a[0].reshape(1, WIDTH)
    g_b = g_out_b[0].reshape(1, WIDTH)
    ln1g, ln1b = ln1_g[0].reshape(1, D), ln1_b[0].reshape(1, D)
    ln2g, ln2b = ln2_g[0].reshape(1, D), ln2_b[0].reshape(1, D)
    w_router_pad = jnp.pad(w_router[0], ((0, 0), (0, LANES - N_EXPERTS)))
    b_router_pad = jnp.pad(b_router[0], (0, LANES - N_EXPERTS)).reshape(1, LANES)

    x1s, u2s, logit_list = [], [], []
    mod_row0 = 0
    mod_rows = []
    for x in groups:
        B, T, _ = x.shape
        cos_v, sin_v = _rope_tables(T)
        qa, ka, va, qb, kb, vb = _input_projection(x, mod3, w_in_bf, cos_v, sin_v, mod_row0)
        oa = _dilated_attention(qa, ka, va, B, T)
        ob = _neighbourhood_attention(qb, kb, vb, na_bias, B, T)
        x1, u2, logits = _output_projection(oa, ob, x, mod3, mod_row0, g_a, g_b, w_out_bf, ln1g, ln1b,
                                            w_router_pad, b_router_pad)
        x1s.append(x1)
        u2s.append(u2)
        logit_list.append(logits)
        mod_rows.append(mod_row0)
        mod_row0 += B

    logits_all = jnp.concatenate(logit_list, axis=0)
    N = logits_all.shape[0]
    idx, gates, tile_cnt = _routing(logits_all)

    tc = tile_cnt[:, 0, :N_EXPERTS]
    seg = (tc + SEG - 1) // SEG * SEG
    total = jnp.sum(seg, axis=0)
    padded = (total + MOE_ROWS - 1) // MOE_ROWS * MOE_ROWS
    pad_end = jnp.cumsum(padded)
    pad_start = pad_end - padded
    tile_start = pad_start[None, :] + jnp.cumsum(seg, axis=0) - seg
    n_tiles = N // SCATTER_TILE
    n_blocks = (N * TOP_K + n_tiles * N_EXPERTS * (SEG - 1)) // MOE_ROWS + N_EXPERTS
    n_rows = n_blocks * MOE_ROWS
    blk_end = pad_end // MOE_ROWS
    n_used = blk_end[-1:].astype(I32)
    blk_expert = jnp.minimum(
        jnp.sum(blk_end[None, :] <= jnp.arange(n_blocks, dtype=I32)[:, None], axis=1), N_EXPERTS - 1).astype(I32)
    lane_pad = ((0, 0), (0, LANES - N_EXPERTS))
    tile_start3 = jnp.pad(tile_start, lane_pad).astype(I32)[:, None, :]
    tails = jnp.pad(jnp.stack([pad_start + total, pad_end]), lane_pad).astype(I32)

    xs = None
    row0 = 0
    for u2 in u2s:
        n = u2.shape[0]
        t0, t1 = row0 // SCATTER_TILE, (row0 + n) // SCATTER_TILE
        xs = _dispatch(idx[row0:row0 + n], u2, tile_cnt[t0:t1], tile_start3[t0:t1], tails, xs, n_rows)
        row0 += n

    w_up_de = _deinterleave_up(w_up[0])
    b_up_de = jnp.concatenate([b_up[0][:, 0::2], b_up[0][:, 1::2]], axis=-1).reshape(N_EXPERTS, 1, 2 * D_FF)
    w_down_bf = w_down[0].astype(BF16)
    b_down3 = b_down[0].reshape(N_EXPERTS, 1, D)
    y_pad = _expert_blocks(xs, blk_expert, n_used, w_up_de, b_up_de, w_down_bf, b_down3)

    outs = []
    row0 = 0
    for x, x1, mrow in zip(groups, x1s, mod_rows):
        B, T, _ = x.shape
        n = B * T
        t0, t1 = row0 // SCATTER_TILE, (row0 + n) // SCATTER_TILE
        outs.append(_combine(idx[row0:row0 + n], gates[row0:row0 + n], tile_cnt[t0:t1], tile_start3[t0:t1],
                             y_pad, x1, mod3, mrow, ln2g, ln2b, B, T))
        row0 += n
    return tuple(outs)
```

```python
import functools

import numpy as np
import jax
import jax.numpy as jnp
from jax import lax
from jax.experimental import pallas as pl
from jax.experimental.pallas import tpu as pltpu

F32 = jnp.float32
BF16 = jnp.bfloat16
I32 = jnp.int32

D_MODEL = 1024
HEAD_DIM = 64
N_HEADS = 8
WIDTH = N_HEADS * HEAD_DIM
N_PAIRS = WIDTH // 128
ROPE_THETA = 10000.0
RADIUS = 64
GRID_W = 64
NA_ROWS = 8
NA_COLS = 16
N_EXPERTS = 32
TOP_K = 4
D_FF = 1024
SWIGLU_ALPHA = 1.702
SWIGLU_LIMIT = 7.0
DEEPNORM_ALPHA = 2.0 ** 0.25
LN_EPS = 1e-5
RMS_EPS = 1e-6
NEG = -1e30

LANES = 128
RES = 16
CHUNK = 128
RES_PER_STEP = 4
MOE_ROWS = 512
ROUTE_TILE = 512
SCATTER_TILE = 256
GATHER_TILE = 128
VMEM_LIMIT = 56 * 1024 * 1024


def _cparams(sem, vmem=VMEM_LIMIT):
    return pltpu.CompilerParams(dimension_semantics=sem, vmem_limit_bytes=vmem)


def _mod_kernel(c_ref, w_ref, b_ref, o_ref):
    c = c_ref[...]
    s = c * (1.0 / (1.0 + jnp.exp(-c)))
    o_ref[...] = jnp.dot(s.astype(BF16), w_ref[...].astype(BF16), preferred_element_type=F32) + b_ref[...]


def _modulation(c, w_mod, b_mod):
    rows = c.shape[0]
    n_out = w_mod.shape[1]
    tn = 1024
    return pl.pallas_call(
        _mod_kernel,
        grid=(n_out // tn,),
        in_specs=[pl.BlockSpec((rows, D_MODEL), lambda j: (0, 0)),
                  pl.BlockSpec((D_MODEL, tn), lambda j: (0, j)),
                  pl.BlockSpec((1, tn), lambda j: (0, j))],
        out_specs=pl.BlockSpec((rows, tn), lambda j: (0, j)),
        out_shape=jax.ShapeDtypeStruct((rows, n_out), F32),
        compiler_params=_cparams(("arbitrary",)),
        name="modulation",
    )(c, w_mod, b_mod.reshape(1, n_out))


def _residue_rows(ref, row0, r):
    return ref.at[pl.ds(row0, CHUNK), r]


def _inproj_kernel(x_hbm, mod_ref, w_ref, cos_ref, sin_ref,
                   qa_ref, ka_ref, va_ref, qb_ref, kb_ref, vb_ref, xbuf, u_scr, sem):
    D = D_MODEL
    row0 = pl.program_id(0) * CHUNK
    shift = mod_ref[:, 0:D]
    scale = mod_ref[:, D:2 * D]
    lane = lax.broadcasted_iota(I32, (1, WIDTH), 1)
    first_half = (lane % HEAD_DIM) < (HEAD_DIM // 2)
    qscale = HEAD_DIM ** -0.5

    def x_copy(r):
        return pltpu.make_async_copy(_residue_rows(x_hbm, row0, r), xbuf.at[r % 2], sem.at[r % 2])

    x_copy(0).start()
    for q in range(RES // RES_PER_STEP):
        for a in range(RES_PER_STEP):
            r = q * RES_PER_STEP + a
            x_copy(r).wait()
            if r + 1 < RES:
                x_copy(r + 1).start()
            u_scr[a * CHUNK:(a + 1) * CHUNK, :] = (xbuf[r % 2] * (1.0 + scale) + shift).astype(BF16)
        u = u_scr[...]
        lo = q * RES_PER_STEP * LANES
        cosf = jnp.concatenate([jnp.tile(cos_ref[:, lo + a * LANES:lo + (a + 1) * LANES], (1, N_PAIRS))
                                for a in range(RES_PER_STEP)], axis=0)
        sinf = jnp.concatenate([jnp.tile(sin_ref[:, lo + a * LANES:lo + (a + 1) * LANES], (1, N_PAIRS))
                                for a in range(RES_PER_STEP)], axis=0)

        def proj(i):
            return jnp.dot(u, w_ref[:, i * WIDTH:(i + 1) * WIDTH], preferred_element_type=F32)

        def rope(p):
            rot = jnp.where(first_half, pltpu.roll(p, WIDTH - HEAD_DIM // 2, 1), pltpu.roll(p, HEAD_DIM // 2, 1))
            return p * cosf + rot * sinf

        def emit_res_major(ref, p):
            for a in range(RES_PER_STEP):
                for hp in range(N_PAIRS):
                    ref[q * RES_PER_STEP + a, hp] = (
                        p[a * CHUNK:(a + 1) * CHUNK, hp * LANES:(hp + 1) * LANES].astype(BF16))

        def emit_natural(ref, p):
            for a in range(RES_PER_STEP):
                for hp in range(N_PAIRS):
                    ref[hp, :, lo + a * LANES:lo + (a + 1) * LANES] = (
                        p[a * CHUNK:(a + 1) * CHUNK, hp * LANES:(hp + 1) * LANES].astype(BF16))

        emit_res_major(qa_ref, rope(proj(0)) * qscale)
        emit_res_major(ka_ref, rope(proj(1)))
        emit_res_major(va_ref, proj(2))
        emit_natural(qb_ref, proj(3) * qscale)
        emit_natural(kb_ref, proj(4))
        emit_natural(vb_ref, proj(5))


def _input_projection(x, mod3, w_in_bf, cos_v, sin_v, mod_row0):
    B, T, D = x.shape
    L = T // RES
    BL = B * L
    cps = L // CHUNK
    x3 = x.reshape(BL, RES, D)
    res_shape = jax.ShapeDtypeStruct((RES, N_PAIRS, BL, LANES), BF16)
    nat_shape = jax.ShapeDtypeStruct((N_PAIRS, BL, RES * LANES), BF16)
    res_spec = pl.BlockSpec((RES, N_PAIRS, CHUNK, LANES), lambda i: (0, 0, i, 0))
    nat_spec = pl.BlockSpec((N_PAIRS, CHUNK, RES * LANES), lambda i: (0, i, 0))
    outs = pl.pallas_call(
        _inproj_kernel,
        grid=(BL // CHUNK,),
        in_specs=[pl.BlockSpec(memory_space=pl.ANY),
                  pl.BlockSpec((None, 1, 6 * D), lambda i: (mod_row0 + i // cps, 0, 0)),
                  pl.BlockSpec((D, 6 * WIDTH), lambda i: (0, 0)),
                  pl.BlockSpec((CHUNK, RES * LANES), lambda i: (i % cps, 0)),
                  pl.BlockSpec((CHUNK, RES * LANES), lambda i: (i % cps, 0))],
        out_specs=[res_spec, res_spec, res_spec, nat_spec, nat_spec, nat_spec],
        out_shape=[res_shape, res_shape, res_shape, nat_shape, nat_shape, nat_shape],
        scratch_shapes=[pltpu.VMEM((2, CHUNK, D), F32), pltpu.VMEM((RES_PER_STEP * CHUNK, D), BF16),
                        pltpu.SemaphoreType.DMA((2,))],
        compiler_params=_cparams(("arbitrary",)),
        name="input_projection",
    )(x3, mod3, w_in_bf, cos_v, sin_v)
    qa, ka, va, qb, kb, vb = outs
    N = B * T
    return qa, ka, va, qb.reshape(N_PAIRS, N, LANES), kb.reshape(N_PAIRS, N, LANES), vb.reshape(N_PAIRS, N, LANES)


def _rope_tables(T):
    half = HEAD_DIM // 2
    inv_freq = 1.0 / (ROPE_THETA ** (jnp.arange(half, dtype=F32) / half))
    ang = jnp.arange(T, dtype=F32)[:, None] * inv_freq[None, :]
    cos, sin = jnp.cos(ang), jnp.sin(ang)
    cos_h = jnp.concatenate([cos, cos], axis=-1)
    sin_h = jnp.concatenate([-sin, sin], axis=-1)
    cos2 = jnp.concatenate([cos_h, cos_h], axis=-1)
    sin2 = jnp.concatenate([sin_h, sin_h], axis=-1)
    L = T // RES
    return cos2.reshape(L, RES * LANES), sin2.reshape(L, RES * LANES)


P2_ROWS = 32
P2_KROWS = 64
P1_ROWS = 16
P1_KROWS = 32
P1_SHIFT = 8
HALO = 64
P3_BATCH = 4
P1_BATCH = 2


def _band_tables(has_halo):
    def mask(ok):
        return np.where(ok, 0.0, NEG).astype(np.float32)
    mq = np.arange(CHUNK)[:, None]
    koff3 = np.arange(2 * CHUNK) - HALO
    b3 = mask(np.abs(mq - koff3[None, :]) <= RADIUS)
    j = np.repeat(np.arange(4), P2_ROWS)[:, None]
    a = np.tile(np.arange(P2_ROWS), 4)[:, None]
    jk = np.repeat(np.arange(4), P2_KROWS)[None, :]
    bk = np.tile(np.arange(P2_KROWS), 4)[None, :]
    b2 = mask(np.abs(4 * (a - (bk - 16)) + (j - jk)) <= RADIUS)
    koff2 = (bk - 16).reshape(-1)
    r = np.repeat(np.arange(RES), P1_ROWS)[:, None]
    a = np.tile(np.arange(P1_ROWS), RES)[:, None]
    rk = np.repeat(np.arange(RES), P1_KROWS)[None, :]
    bk = np.tile(np.arange(P1_KROWS), RES)[None, :]
    b1 = mask(np.abs(RES * (a - (bk - P1_SHIFT)) + (r - rk)) <= RADIUS)
    koff1 = (bk - P1_SHIFT).reshape(-1)
    return (jnp.asarray(b3), jnp.asarray(koff3.astype(np.int32)[None, :]),
            jnp.asarray(b2), jnp.asarray(koff2.astype(np.int32)[None, :]),
            jnp.asarray(b1), jnp.asarray(koff1.astype(np.int32)[None, :]))


def _attend_pairs(problems, head0):
    scores = []
    for q, k, _, _ in problems:
        for h in range(2):
            sel = head0 if h == 0 else jnp.logical_not(head0)
            qh = jnp.where(sel, q, jnp.zeros_like(q))
            scores.append(lax.dot_general(qh, k, (((1,), (1,)), ((), ())), preferred_element_type=F32))
    probs, stats = [], []
    for i, s in enumerate(scores):
        s = s + problems[i // 2][3]
        m = jnp.max(s, axis=-1, keepdims=True)
        p = jnp.exp(s - m)
        stats.append((m, jnp.sum(p, axis=-1, keepdims=True)))
        probs.append(p.astype(BF16))
    outs = [jnp.dot(p, problems[i // 2][2], preferred_element_type=F32) for i, p in enumerate(probs)]
    results = []
    for i in range(len(problems)):
        (m0, l0), (m1, l1) = stats[2 * i], stats[2 * i + 1]
        results.append((jnp.where(head0, outs[2 * i], outs[2 * i + 1]),
                        jnp.where(head0, m0, m1), jnp.where(head0, l0, l1)))
    return results


def _dilated_kernel(*refs, has_halo, seq_rows, chunks_per_seq):
    if has_halo:
        (q_ref, k_ref, v_ref, kp_ref, kn_ref, vp_ref, vn_ref,
         b3_ref, o3_ref, b2_ref, o2_ref, b1_ref, o1_ref,
         out_ref, kf, vf, kf8, vf8, acc, ms, ls) = refs
    else:
        (q_ref, k_ref, v_ref, b3_ref, o3_ref, b2_ref, o2_ref, b1_ref, o1_ref,
         out_ref, kf, vf, kf8, vf8, acc, ms, ls) = refs

    c = pl.program_id(0) % chunks_per_seq
    row0 = c * CHUNK
    lane = lax.broadcasted_iota(I32, (1, LANES), 1)
    head0 = lane < HEAD_DIM

    for r in range(RES):
        kf[r, HALO:HALO + CHUNK, :] = k_ref[r]
        vf[r, HALO:HALO + CHUNK, :] = v_ref[r]
        if has_halo:
            kf[r, 0:HALO, :] = kp_ref[r]
            vf[r, 0:HALO, :] = vp_ref[r]
            kf[r, HALO + CHUNK:, :] = kn_ref[r]
            vf[r, HALO + CHUNK:, :] = vn_ref[r]
        else:
            zeros = jnp.zeros((HALO, LANES), BF16)
            kf[r, 0:HALO, :] = zeros
            vf[r, 0:HALO, :] = zeros
            kf[r, HALO + CHUNK:, :] = zeros
            vf[r, HALO + CHUNK:, :] = zeros
        for src, dst in ((kf, kf8), (vf, vf8)):
            w32 = src[r].astype(F32)
            dst[r] = jnp.concatenate([w32[P1_SHIFT:], w32[:P1_SHIFT]], axis=0).astype(BF16)

    def in_seq(base, off_ref):
        kv_row = base + off_ref[...]
        return jnp.where((kv_row >= 0) & (kv_row < seq_rows), 0.0, NEG).astype(F32)

    def attend(problems):
        return _attend_pairs(problems, head0)

    def merge(a_old, m_old, l_old, o, m, l):
        mn = jnp.maximum(m_old, m)
        wa = jnp.exp(m_old - mn)
        wb = jnp.exp(m - mn)
        return a_old * wa + o * wb, mn, l_old * wa + l * wb

    bias3 = b3_ref[...] + in_seq(row0, o3_ref)

    def body3(it, carry):
        rs = [it * P3_BATCH + u for u in range(P3_BATCH)]
        res = attend([(q_ref[r], kf[r], vf[r], bias3) for r in rs])
        for r, (o, m, l) in zip(rs, res):
            acc[r] = o
            ms[r] = m
            ls[r] = l
        return carry

    lax.fori_loop(0, RES // P3_BATCH, body3, 0)

    b2 = b2_ref[...]

    def body2(g, carry):
        qs = pl.multiple_of(g * P2_ROWS, P2_ROWS)
        ks = pl.multiple_of(HALO - 16 + g * P2_ROWS, 16)
        bias = b2 + in_seq(row0 + g * P2_ROWS, o2_ref)

        def gather(ref, r4, start, rows):
            return jnp.concatenate([ref[4 * j + r4, pl.ds(start, rows), :] for j in range(4)], axis=0)

        res = attend([(gather(q_ref, r4, qs, P2_ROWS), gather(kf, r4, ks, P2_KROWS),
                       gather(vf, r4, ks, P2_KROWS), bias) for r4 in range(4)])
        for r4, (o, m, l) in enumerate(res):
            a_new, m_new, l_new = merge(gather(acc, r4, qs, P2_ROWS), gather(ms, r4, qs, P2_ROWS),
                                        gather(ls, r4, qs, P2_ROWS), o, m, l)
            for j in range(4):
                acc[4 * j + r4, pl.ds(qs, P2_ROWS), :] = a_new[j * P2_ROWS:(j + 1) * P2_ROWS]
                ms[4 * j + r4, pl.ds(qs, P2_ROWS), :] = m_new[j * P2_ROWS:(j + 1) * P2_ROWS]
                ls[4 * j + r4, pl.ds(qs, P2_ROWS), :] = l_new[j * P2_ROWS:(j + 1) * P2_ROWS]
        return carry

    lax.fori_loop(0, CHUNK // P2_ROWS, body2, 0)

    b1 = b1_ref[...]

    def body1(it, carry):
        def gather(ref, start, rows):
            return jnp.concatenate([ref[r, pl.ds(start, rows), :] for r in range(RES)], axis=0)

        starts, problems = [], []
        for u in range(P1_BATCH):
            g = it * P1_BATCH + u
            qs = pl.multiple_of(g * P1_ROWS, P1_ROWS)
            ks = pl.multiple_of(HALO - 16 + g * P1_ROWS, 16)
            starts.append(qs)
            problems.append((gather(q_ref, qs, P1_ROWS), gather(kf8, ks, P1_KROWS), gather(vf8, ks, P1_KROWS),
                             b1 + in_seq(row0 + g * P1_ROWS, o1_ref)))
        for qs, (o, m, l) in zip(starts, attend(problems)):
            a_new, m_new, l_new = merge(gather(acc, qs, P1_ROWS), gather(ms, qs, P1_ROWS),
                                        gather(ls, qs, P1_ROWS), o, m, l)
            for r in range(RES):
                acc[r, pl.ds(qs, P1_ROWS), :] = a_new[r * P1_ROWS:(r + 1) * P1_ROWS]
                ms[r, pl.ds(qs, P1_ROWS), :] = m_new[r * P1_ROWS:(r + 1) * P1_ROWS]
                ls[r, pl.ds(qs, P1_ROWS), :] = l_new[r * P1_ROWS:(r + 1) * P1_ROWS]
        return carry

    lax.fori_loop(0, CHUNK // P1_ROWS // P1_BATCH, body1, 0)

    def body_out(r, carry):
        out_ref[r] = (acc[r] / ls[r]).astype(BF16)
        return carry

    lax.fori_loop(0, RES, body_out, 0)


def _dilated_attention(qa, ka, va, B, T):
    L = T // RES
    BL = B * L
    cps = L // CHUNK
    has_halo = cps > 1
    tables = _band_tables(has_halo)
    blk = (RES, None, CHUNK, LANES)
    center = pl.BlockSpec(blk, lambda i, hp: (0, hp, i, 0))
    in_specs = [center, center, center]
    args = [qa, ka, va]
    if has_halo:
        hblk = (RES, None, HALO, LANES)
        per = CHUNK // HALO

        def prev_map(i, hp):
            return (0, hp, jnp.maximum(per * i - 1, (i // cps) * cps * per), 0)

        def next_map(i, hp):
            return (0, hp, jnp.minimum(per * i + per, (i // cps + 1) * cps * per - 1), 0)

        in_specs += [pl.BlockSpec(hblk, prev_map), pl.BlockSpec(hblk, next_map),
                     pl.BlockSpec(hblk, prev_map), pl.BlockSpec(hblk, next_map)]
        args += [ka, ka, va, va]
    for t in tables:
        in_specs.append(pl.BlockSpec(t.shape, lambda i, hp: (0, 0)))
        args.append(t)
    kern = functools.partial(_dilated_kernel, has_halo=has_halo, seq_rows=L, chunks_per_seq=cps)
    return pl.pallas_call(
        kern,
        grid=(BL // CHUNK, N_PAIRS),
        in_specs=in_specs,
        out_specs=pl.BlockSpec(blk, lambda i, hp: (0, hp, i, 0)),
        out_shape=jax.ShapeDtypeStruct((RES, N_PAIRS, BL, LANES), BF16),
        scratch_shapes=[pltpu.VMEM((RES, 2 * CHUNK, LANES), BF16),
                        pltpu.VMEM((RES, 2 * CHUNK, LANES), BF16),
                        pltpu.VMEM((RES, 2 * CHUNK, LANES), BF16),
                        pltpu.VMEM((RES, 2 * CHUNK, LANES), BF16),
                        pltpu.VMEM((RES, CHUNK, LANES), F32),
                        pltpu.VMEM((RES, CHUNK, LANES), F32),
                        pltpu.VMEM((RES, CHUNK, LANES), F32)],
        compiler_params=_cparams(("arbitrary", "arbitrary")),
        name="dilated_attention",
    )(*args)


NA_KEYS = NA_ROWS * GRID_W
NA_BLOCK_ROWS = 32
NA_BATCH = 8


def _na_bias_table(rpb):
    c = np.arange(GRID_W)
    col_start = np.clip(c - NA_COLS // 2, 0, GRID_W - NA_COLS)
    col_mask = (c[None, :] >= col_start[:, None]) & (c[None, :] < col_start[:, None] + NA_COLS)
    dc_idx = np.clip(c[None, :] - c[:, None], -(NA_COLS - 1), NA_COLS - 1) + NA_COLS - 1
    rel = rpb.astype(F32)[:, :, dc_idx]
    rel = jnp.where(col_mask[None, None], rel, NEG)
    per_off = [rel[:, d0:d0 + NA_ROWS].transpose(0, 2, 1, 3).reshape(N_HEADS, GRID_W, NA_KEYS)
               for d0 in range(NA_ROWS)]
    return jnp.stack(per_off, axis=0).reshape(NA_ROWS, N_PAIRS, 2, GRID_W, NA_KEYS)


def _na_kernel(q_ref, k_ref, v_ref, bias_ref, out_ref, *, grid_rows, block_rows):
    gb = pl.program_id(2)
    lane = lax.broadcasted_iota(I32, (1, LANES), 1)
    head0 = lane < HEAD_DIM

    def body(it, carry):
        rows, scores = [], []
        for u in range(NA_BATCH):
            i = it * NA_BATCH + u
            g = gb * block_rows + i
            rs = jnp.clip(g - NA_ROWS // 2, 0, grid_rows - NA_ROWS)
            d0 = rs - g + NA_ROWS - 1
            qs = pl.multiple_of(i * GRID_W, GRID_W)
            ks = pl.multiple_of(rs * GRID_W, GRID_W)
            q = q_ref[pl.ds(qs, GRID_W), :]
            k = k_ref[pl.ds(ks, NA_KEYS), :]
            rows.append((qs, ks, d0))
            for h in range(2):
                sel = head0 if h == 0 else jnp.logical_not(head0)
                qh = jnp.where(sel, q, jnp.zeros_like(q))
                scores.append(lax.dot_general(qh, k, (((1,), (1,)), ((), ())), preferred_element_type=F32))
        probs, sums = [], []
        for n, s in enumerate(scores):
            s = s + bias_ref[rows[n // 2][2], n % 2]
            p = jnp.exp(s - jnp.max(s, axis=-1, keepdims=True))
            sums.append(jnp.sum(p, axis=-1, keepdims=True))
            probs.append(p.astype(BF16))
        outs = []
        for n, p in enumerate(probs):
            v = v_ref[pl.ds(rows[n // 2][1], NA_KEYS), :]
            outs.append(jnp.dot(p, v, preferred_element_type=F32) / sums[n])
        for u, (qs, _, _) in enumerate(rows):
            out_ref[pl.ds(qs, GRID_W), :] = jnp.where(head0, outs[2 * u], outs[2 * u + 1]).astype(BF16)
        return carry

    lax.fori_loop(0, block_rows // NA_BATCH, body, 0)


def _neighbourhood_attention(qb, kb, vb, bias, B, T):
    G = T // GRID_W
    rb = min(NA_BLOCK_ROWS, G)
    nb = G // rb
    kern = functools.partial(_na_kernel, grid_rows=G, block_rows=rb)
    seq = pl.BlockSpec((None, T, LANES), lambda hp, b, gb: (hp, b, 0))
    qblk = pl.BlockSpec((None, rb * GRID_W, LANES), lambda hp, b, gb: (hp, b * nb + gb, 0))
    return pl.pallas_call(
        kern,
        grid=(N_PAIRS, B, nb),
        in_specs=[qblk, seq, seq,
                  pl.BlockSpec((NA_ROWS, None, 2, GRID_W, NA_KEYS), lambda hp, b, gb: (0, hp, 0, 0, 0))],
        out_specs=qblk,
        out_shape=jax.ShapeDtypeStruct((N_PAIRS, B * T, LANES), BF16),
        compiler_params=_cparams(("arbitrary", "arbitrary", "arbitrary")),
        name="neighbourhood_attention",
    )(qb, kb, vb, bias)


def _layer_norm(h, g, b):
    mu = jnp.mean(h, axis=-1, keepdims=True)
    d = h - mu
    var = jnp.mean(d * d, axis=-1, keepdims=True)
    return d * lax.rsqrt(var + LN_EPS) * g + b


def _outproj_kernel(oa_ref, ob_ref, x_hbm, mod_ref, ga_ref, gb_ref, w_ref, lg_ref, lb_ref, wr_ref, br_ref,
                    x1_hbm, u2_hbm, logit_hbm, xbuf, x1buf, u2buf, lgbuf, mix_scr, in_sem, out_sem):
    D = D_MODEL
    row0 = pl.program_id(0) * CHUNK
    gate_a = mod_ref[:, 2 * D:3 * D]
    shift_f = mod_ref[:, 3 * D:4 * D]
    scale_f = mod_ref[:, 4 * D:5 * D]

    def rms(o, g):
        return o * lax.rsqrt(jnp.mean(o * o, axis=-1, keepdims=True) + RMS_EPS) * g

    def x_copy(r):
        return pltpu.make_async_copy(_residue_rows(x_hbm, row0, r), xbuf.at[r % 2], in_sem.at[r % 2])

    def out_copies(r):
        s = r % 2
        return (pltpu.make_async_copy(x1buf.at[s], _residue_rows(x1_hbm, row0, r), out_sem.at[s]),
                pltpu.make_async_copy(u2buf.at[s], _residue_rows(u2_hbm, row0, r), out_sem.at[s]),
                pltpu.make_async_copy(lgbuf.at[s], _residue_rows(logit_hbm, row0, r), out_sem.at[s]))

    wr = wr_ref[...]
    wr_hi = wr.astype(BF16)
    wr_lo = (wr - wr_hi.astype(F32)).astype(BF16)

    x_copy(0).start()
    for q in range(RES // RES_PER_STEP):
        lo = q * RES_PER_STEP * LANES
        for a in range(RES_PER_STEP):
            r = q * RES_PER_STEP + a
            oa = jnp.concatenate([oa_ref[r, hp] for hp in range(N_PAIRS)], axis=1).astype(F32)
            ob = jnp.concatenate([ob_ref[hp, :, lo + a * LANES:lo + (a + 1) * LANES] for hp in range(N_PAIRS)],
                                 axis=1).astype(F32)
            mix = jnp.concatenate([rms(oa, ga_ref[...]), rms(ob, gb_ref[...])], axis=1)
            mix_scr[a * CHUNK:(a + 1) * CHUNK, :] = mix.astype(BF16)
        y = jnp.dot(mix_scr[...], w_ref[...], preferred_element_type=F32)

        for a in range(RES_PER_STEP):
            r = q * RES_PER_STEP + a
            s = r % 2
            x_copy(r).wait()
            if r + 1 < RES:
                x_copy(r + 1).start()
            if r >= 2:
                for c in out_copies(r - 2):
                    c.wait()
            h = DEEPNORM_ALPHA * xbuf[s] + (1.0 + gate_a) * y[a * CHUNK:(a + 1) * CHUNK]
            x1 = _layer_norm(h, lg_ref[...], lb_ref[...])
            u2 = x1 * (1.0 + scale_f) + shift_f
            u_hi = u2.astype(BF16)
            u_lo = (u2 - u_hi.astype(F32)).astype(BF16)
            logits = (jnp.dot(u_hi, wr_hi, preferred_element_type=F32)
                      + jnp.dot(u_lo, wr_hi, preferred_element_type=F32)
                      + jnp.dot(u_hi, wr_lo, preferred_element_type=F32)) + br_ref[...]
            x1buf[s] = x1
            u2buf[s] = u2
            lgbuf[s] = logits
            for c in out_copies(r):
                c.start()
    for r in (RES - 2, RES - 1):
        for c in out_copies(r):
            c.wait()


def _output_projection(oa, ob, x, mod3, mod_row0, g_a, g_b, w_out_bf, ln_g, ln_b, w_router_pad, b_router_pad):
    B, T, D = x.shape
    L = T // RES
    BL = B * L
    cps = L // CHUNK
    x3 = x.reshape(BL, RES, D)
    obv = ob.reshape(N_PAIRS, BL, RES * LANES)
    const = lambda shape: pl.BlockSpec(shape, lambda i: tuple(0 for _ in shape))
    hbm = pl.BlockSpec(memory_space=pl.ANY)
    x1, u2, logits = pl.pallas_call(
        _outproj_kernel,
        grid=(BL // CHUNK,),
        in_specs=[pl.BlockSpec((RES, N_PAIRS, CHUNK, LANES), lambda i: (0, 0, i, 0)),
                  pl.BlockSpec((N_PAIRS, CHUNK, RES * LANES), lambda i: (0, i, 0)),
                  hbm,
                  pl.BlockSpec((None, 1, 6 * D), lambda i: (mod_row0 + i // cps, 0, 0)),
                  const((1, WIDTH)), const((1, WIDTH)), const((2 * WIDTH, D)),
                  const((1, D)), const((1, D)), const((D, LANES)), const((1, LANES))],
        out_specs=[hbm, hbm, hbm],
        out_shape=[jax.ShapeDtypeStruct((BL, RES, D), F32), jax.ShapeDtypeStruct((BL, RES, D), F32),
                   jax.ShapeDtypeStruct((BL, RES, LANES), F32)],
        scratch_shapes=[pltpu.VMEM((2, CHUNK, D), F32), pltpu.VMEM((2, CHUNK, D), F32),
                        pltpu.VMEM((2, CHUNK, D), F32), pltpu.VMEM((2, CHUNK, LANES), F32),
                        pltpu.VMEM((RES_PER_STEP * CHUNK, 2 * WIDTH), BF16),
                        pltpu.SemaphoreType.DMA((2,)), pltpu.SemaphoreType.DMA((2,))],
        compiler_params=_cparams(("arbitrary",)),
        name="output_projection",
    )(oa, obv, x3, mod3, g_a, g_b, w_out_bf, ln_g, ln_b, w_router_pad, b_router_pad)
    N = B * T
    return x1.reshape(N, D), u2.reshape(N, D), logits.reshape(N, LANES)


def _route_kernel(logit_ref, idx_ref, gate_ref, cnt_ref):
    tn = logit_ref.shape[0]
    lane = lax.broadcasted_iota(I32, (tn, LANES), 1)
    logits = jnp.where(lane < N_EXPERTS, logit_ref[...], -3.0e38)
    vals, idxs = [], []
    multi = jnp.zeros((tn, LANES), F32)
    for _ in range(TOP_K):
        m = jnp.max(logits, axis=-1, keepdims=True)
        idx = jnp.min(jnp.where(logits == m, lane, LANES), axis=-1, keepdims=True)
        hot = lane == idx
        vals.append(m)
        idxs.append(idx)
        multi = multi + hot.astype(F32)
        logits = jnp.where(hot, -3.0e38, logits)
    es = [jnp.exp(v - vals[0]) for v in vals]
    tot = es[0] + es[1] + es[2] + es[3]
    lane4 = lax.broadcasted_iota(I32, (tn, TOP_K), 1)
    sel = jnp.zeros((tn, TOP_K), I32)
    gates = jnp.zeros((tn, TOP_K), F32)
    for k in range(TOP_K):
        sel = jnp.where(lane4 == k, idxs[k], sel)
        gates = jnp.where(lane4 == k, es[k] / tot, gates)
    idx_ref[...] = sel
    gate_ref[...] = gates
    cnt_ref[...] = jnp.sum(multi, axis=0, keepdims=True).astype(I32)


def _routing(logits):
    N = logits.shape[0]
    tn = SCATTER_TILE
    nt = N // tn
    return pl.pallas_call(
        _route_kernel,
        grid=(nt,),
        in_specs=[pl.BlockSpec((tn, LANES), lambda i: (i, 0))],
        out_specs=[pl.BlockSpec((tn, TOP_K), lambda i: (i, 0)),
                   pl.BlockSpec((tn, TOP_K), lambda i: (i, 0)),
                   pl.BlockSpec((None, 1, LANES), lambda i: (i, 0, 0))],
        out_shape=[jax.ShapeDtypeStruct((N, TOP_K), I32), jax.ShapeDtypeStruct((N, TOP_K), F32),
                   jax.ShapeDtypeStruct((nt, 1, LANES), I32)],
        compiler_params=_cparams(("arbitrary",)),
        name="moe_routing",
    )(logits)


SEG = 8
STAGE_ROWS = SCATTER_TILE * TOP_K + N_EXPERTS * SEG


def _lane_prefix_exclusive(v):
    lane = lax.broadcasted_iota(I32, v.shape, 1)
    incl = v
    s = 1
    while s < LANES:
        incl = incl + jnp.where(lane >= s, pltpu.roll(incl, s, 1), 0.0)
        s *= 2
    return incl - v


def _tile_ranks(idx, tn):
    lane = lax.broadcasted_iota(I32, (tn, LANES), 1)
    hots = [lane == idx[:, k:k + 1] for k in range(TOP_K)]
    multi = jnp.zeros((tn, LANES), F32)
    for h in hots:
        multi = multi + h.astype(F32)
    row = lax.broadcasted_iota(I32, (tn, tn), 0)
    col = lax.broadcasted_iota(I32, (tn, tn), 1)
    lower = (col < row).astype(BF16)
    before = jnp.dot(lower, multi.astype(BF16), preferred_element_type=F32)
    return hots, before


def _dispatch_kernel(*refs, first):
    if first:
        cnt_s, start_s, tail_s, idx_ref, u_ref, cntv_ref, xs_ref, stage, zbuf, sem = refs
    else:
        cnt_s, start_s, tail_s, idx_ref, u_ref, cntv_ref, _, xs_ref, stage, zbuf, sem = refs
    tn = idx_ref.shape[0]

    if first:
        @pl.when(pl.program_id(0) == 0)
        def _():
            zbuf[...] = jnp.zeros_like(zbuf)

            def tail_copy(row):
                return pltpu.make_async_copy(zbuf, xs_ref.at[pl.ds(pl.multiple_of(row, SEG), SEG)], sem)

            def fill(e, n):
                lo = tail_s[0, e]
                nfull = (tail_s[1, e] - lo) // SEG

                def one(q, c):
                    tail_copy(lo + q * SEG).start()
                    return c

                lax.fori_loop(0, nfull, one, 0)
                return n + nfull

            n = lax.fori_loop(0, N_EXPERTS, fill, 0)

            def drain(q, c):
                tail_copy(0).wait()
                return c

            lax.fori_loop(0, n, drain, 0)

    hots, before = _tile_ranks(idx_ref[...], tn)
    cntf = cntv_ref[...].astype(F32)
    seg_len = jnp.ceil(cntf * (1.0 / SEG)) * SEG
    seg_off = _lane_prefix_exclusive(seg_len)
    lane_r = lax.broadcasted_iota(I32, (tn, STAGE_ROWS), 1)
    onehot = jnp.zeros((tn, STAGE_ROWS), F32)
    for k in range(TOP_K):
        stage_row = jnp.sum(jnp.where(hots[k], seg_off + before, 0.0), axis=-1, keepdims=True)
        onehot = jnp.where(lane_r == stage_row.astype(I32), 1.0, onehot)
    stage[...] = lax.dot_general(onehot.astype(BF16), u_ref[...].astype(BF16), (((0,), (0,)), ((), ())),
                                 preferred_element_type=F32)

    def seg_copy(src_row, dst_row):
        return pltpu.make_async_copy(stage.at[pl.ds(pl.multiple_of(src_row, SEG), SEG)],
                                     xs_ref.at[pl.ds(pl.multiple_of(dst_row, SEG), SEG)], sem)

    def per_expert(e, carry):
        off, n = carry
        nseg = (cnt_s[0, e] + SEG - 1) // SEG
        dst = start_s[0, e]

        def one(q, c):
            seg_copy(off + q * SEG, dst + q * SEG).start()
            return c

        lax.fori_loop(0, nseg, one, 0)
        return off + nseg * SEG, n + nseg

    _, n = lax.fori_loop(0, N_EXPERTS, per_expert, (0, 0))

    def drain(q, c):
        seg_copy(0, 0).wait()
        return c

    lax.fori_loop(0, n, drain, 0)


def _dispatch(idx, u2, tile_cnt, tile_start, tails, xs, xs_rows):
    n, D = u2.shape
    tn = SCATTER_TILE
    nt = n // tn
    first = xs is None
    smem = lambda shape, imap: pl.BlockSpec(shape, imap, memory_space=pltpu.SMEM)
    kern = functools.partial(_dispatch_kernel, first=first)
    in_specs = [smem((None, 1, LANES), lambda i: (i, 0, 0)),
                smem((None, 1, LANES), lambda i: (i, 0, 0)),
                smem((2, LANES), lambda i: (0, 0)),
                pl.BlockSpec((tn, TOP_K), lambda i: (i, 0)),
                pl.BlockSpec((tn, D), lambda i: (i, 0)),
                pl.BlockSpec((None, 1, LANES), lambda i: (i, 0, 0))]
    args = [tile_cnt, tile_start, tails, idx, u2, tile_cnt]
    alias = {}
    if not first:
        alias = {len(args): 0}
        in_specs.append(pl.BlockSpec(memory_space=pl.ANY))
        args.append(xs)
    return pl.pallas_call(
        kern,
        grid=(nt,),
        in_specs=in_specs,
        out_specs=pl.BlockSpec(memory_space=pl.ANY),
        out_shape=jax.ShapeDtypeStruct((xs_rows, D), F32),
        scratch_shapes=[pltpu.VMEM((STAGE_ROWS, D), F32), pltpu.VMEM((SEG, D), F32), pltpu.SemaphoreType.DMA(())],
        input_output_aliases=alias,
        compiler_params=_cparams(("arbitrary",)),
        name="moe_dispatch",
    )(*args)


DEINT = 256


def _deinterleave_kernel(w_ref, p_ref, o_ref):
    n_groups = w_ref.shape[1] // DEINT
    half = DEINT // 2
    f = w_ref.shape[1] // 2
    for g in range(n_groups):
        wg = w_ref[:, g * DEINT:(g + 1) * DEINT].astype(BF16)
        y = jnp.dot(wg, p_ref[...], preferred_element_type=F32)
        o_ref[:, g * half:(g + 1) * half] = y[:, :half].astype(BF16)
        o_ref[:, f + g * half:f + (g + 1) * half] = y[:, half:].astype(BF16)


def _deinterleave_up(w_up):
    E, D, F2 = w_up.shape
    perm = np.zeros((DEINT, DEINT), np.float32)
    j = np.arange(DEINT // 2)
    perm[2 * j, j] = 1.0
    perm[2 * j + 1, DEINT // 2 + j] = 1.0
    return pl.pallas_call(
        _deinterleave_kernel,
        grid=(E,),
        in_specs=[pl.BlockSpec((None, D, F2), lambda e: (e, 0, 0)),
                  pl.BlockSpec((DEINT, DEINT), lambda e: (0, 0))],
        out_specs=pl.BlockSpec((None, D, F2), lambda e: (e, 0, 0)),
        out_shape=jax.ShapeDtypeStruct((E, D, F2), BF16),
        compiler_params=_cparams(("arbitrary",)),
        name="expert_weight_layout",
    )(w_up, jnp.asarray(perm, BF16))


def _expert_kernel(be_ref, nu_ref, x_ref, wu_ref, bu_ref, wd_ref, bd_ref, y_ref):
    del be_ref

    @pl.when(pl.program_id(0) < nu_ref[0])
    def _():
        x = x_ref[...].astype(BF16)
        hu = jnp.dot(x, wu_ref[...], preferred_element_type=F32) + bu_ref[...]
        glu = jnp.minimum(hu[:, :D_FF], SWIGLU_LIMIT)
        lin = jnp.clip(hu[:, D_FF:], -SWIGLU_LIMIT, SWIGLU_LIMIT)
        act = glu * (1.0 / (1.0 + jnp.exp(-SWIGLU_ALPHA * glu))) * (lin + 1.0)
        y_ref[...] = jnp.dot(act.astype(BF16), wd_ref[...], preferred_element_type=F32) + bd_ref[...]

    @pl.when(pl.program_id(0) >= nu_ref[0])
    def _():
        y_ref[...] = jnp.zeros_like(y_ref)


def _expert_blocks(xs, blk_expert, n_used, w_up_de, b_up_de, w_down_bf, b_down):
    n_rows, D = xs.shape
    nblk = n_rows // MOE_ROWS

    def xmap(j, be, nu):
        return (jnp.minimum(j, nu[0] - 1), 0)

    grid_spec = pltpu.PrefetchScalarGridSpec(
        num_scalar_prefetch=2,
        grid=(nblk,),
        in_specs=[pl.BlockSpec((MOE_ROWS, D), xmap),
                  pl.BlockSpec((None, D, 2 * D_FF), lambda j, be, nu: (be[j], 0, 0)),
                  pl.BlockSpec((None, 1, 2 * D_FF), lambda j, be, nu: (be[j], 0, 0)),
                  pl.BlockSpec((None, D_FF, D), lambda j, be, nu: (be[j], 0, 0)),
                  pl.BlockSpec((None, 1, D), lambda j, be, nu: (be[j], 0, 0))],
        out_specs=pl.BlockSpec((MOE_ROWS, D), lambda j, be, nu: (j, 0)),
    )
    return pl.pallas_call(
        _expert_kernel,
        grid_spec=grid_spec,
        out_shape=jax.ShapeDtypeStruct((n_rows, D), F32),
        compiler_params=_cparams(("arbitrary",)),
        name="moe_experts",
    )(blk_expert, n_used, xs, w_up_de, b_up_de, w_down_bf, b_down)


def _combine_kernel(cnt_s, start_s, idx_ref, gate_ref, cntv_ref, y_ref, x1_ref, mod_ref,
                    lg_ref, lb_ref, out_ref, stage, sem):
    D = D_MODEL
    tn = idx_ref.shape[0]

    def seg_copy(src_row, dst_row):
        return pltpu.make_async_copy(y_ref.at[pl.ds(pl.multiple_of(src_row, SEG), SEG)],
                                     stage.at[pl.ds(pl.multiple_of(dst_row, SEG), SEG)], sem)

    @pl.when(pl.program_id(0) == 0)
    def _():
        stage[...] = jnp.zeros_like(stage)

    def per_expert(e, carry):
        off, n = carry
        nseg = (cnt_s[0, e] + SEG - 1) // SEG
        src = start_s[0, e]

        def one(q, c):
            seg_copy(src + q * SEG, off + q * SEG).start()
            return c

        lax.fori_loop(0, nseg, one, 0)
        return off + nseg * SEG, n + nseg

    _, n_seg = lax.fori_loop(0, N_EXPERTS, per_expert, (0, 0))

    hots, before = _tile_ranks(idx_ref[...], tn)
    seg_len = jnp.ceil(cntv_ref[...].astype(F32) * (1.0 / SEG)) * SEG
    seg_off = _lane_prefix_exclusive(seg_len)
    gates = gate_ref[...]
    lane_r = lax.broadcasted_iota(I32, (tn, STAGE_ROWS), 1)
    weights = jnp.zeros((tn, STAGE_ROWS), F32)
    for k in range(TOP_K):
        stage_row = jnp.sum(jnp.where(hots[k], seg_off + before, 0.0), axis=-1, keepdims=True)
        weights = jnp.where(lane_r == stage_row.astype(I32), gates[:, k:k + 1], weights)

    def drain(q, c):
        seg_copy(0, 0).wait()
        return c

    lax.fori_loop(0, n_seg, drain, 0)

    y = jnp.dot(weights.astype(BF16), stage[...].astype(BF16), preferred_element_type=F32)
    gate_f = mod_ref[:, 5 * D:6 * D]
    h = DEEPNORM_ALPHA * x1_ref[...] + (1.0 + gate_f) * y
    out_ref[...] = _layer_norm(h, lg_ref[...], lb_ref[...])


def _combine(idx, gates, tile_cnt, tile_start, y_pad, x1, mod3, mod_row0, ln_g, ln_b, B, T):
    D = D_MODEL
    tn = SCATTER_TILE
    n_rows = B * T
    nt = n_rows // tn
    tps = T // tn
    smem = lambda imap: pl.BlockSpec((None, 1, LANES), imap, memory_space=pltpu.SMEM)
    out = pl.pallas_call(
        _combine_kernel,
        grid=(nt,),
        in_specs=[smem(lambda i: (i, 0, 0)), smem(lambda i: (i, 0, 0)),
                  pl.BlockSpec((tn, TOP_K), lambda i: (i, 0)),
                  pl.BlockSpec((tn, TOP_K), lambda i: (i, 0)),
                  pl.BlockSpec((None, 1, LANES), lambda i: (i, 0, 0)),
                  pl.BlockSpec(memory_space=pl.ANY),
                  pl.BlockSpec((tn, D), lambda i: (i, 0)),
                  pl.BlockSpec((None, 1, 6 * D), lambda i: (mod_row0 + i // tps, 0, 0)),
                  pl.BlockSpec((1, D), lambda i: (0, 0)),
                  pl.BlockSpec((1, D), lambda i: (0, 0))],
        out_specs=pl.BlockSpec((tn, D), lambda i: (i, 0)),
        out_shape=jax.ShapeDtypeStruct((n_rows, D), F32),
        scratch_shapes=[pltpu.VMEM((STAGE_ROWS, D), F32), pltpu.SemaphoreType.DMA(())],
        compiler_params=_cparams(("arbitrary",)),
        name="moe_combine",
    )(tile_cnt, tile_start, idx, gates, tile_cnt, y_pad, x1, mod3, ln_g, ln_b)
    return out.reshape(B, T, D)


def kernel(x_prompt, x_sample, c_prompt, c_sample, w_mod, b_mod, w_in, rpb, g_out_a, g_out_b, w_out, ln1_g, ln1_b,
           w_router, b_router, w_up, b_up, w_down, b_down, ln2_g, ln2_b):
    D = D_MODEL
    groups = [x_prompt, x_sample]
    conds = [c_prompt, c_sample]
    n_cond = sum(c.shape[0] for c in conds)
    pad_rows = -n_cond % 8
    c_all = jnp.concatenate(conds + [jnp.zeros((pad_rows, D), F32)], axis=0)
    mod3 = _modulation(c_all, w_mod[0], b_mod[0]).reshape(n_cond + pad_rows, 1, 6 * D)

    w_in_bf = w_in[0].astype(BF16)
    w_out_bf = w_out[0].astype(BF16)
    na_bias = _na_bias_table(rpb[0])
    g_a = g_out_a[0].reshape(1, WIDTH)
    g_b = g_out_b[0].reshape(1, WIDTH)
    ln1g, ln1b = ln1_g[0].reshape(1, D), ln1_b[0].reshape(1, D)
    ln2g, ln2b = ln2_g[0].reshape(1, D), ln2_b[0].reshape(1, D)
    w_router_pad = jnp.pad(w_router[0], ((0, 0), (0, LANES - N_EXPERTS)))
    b_router_pad = jnp.pad(b_router[0], (0, LANES - N_EXPERTS)).reshape(1, LANES)

    x1s, u2s, logit_list = [], [], []
    mod_row0 = 0
    mod_rows = []
    for x in groups:
        B, T, _ = x.shape
        cos_v, sin_v = _rope_tables(T)
        qa, ka, va, qb, kb, vb = _input_projection(x, mod3, w_in_bf, cos_v, sin_v, mod_row0)
        oa = _dilated_attention(qa, ka, va, B, T)
        ob = _neighbourhood_attention(qb, kb, vb, na_bias, B, T)
        x1, u2, logits = _output_projection(oa, ob, x, mod3, mod_row0, g_a, g_b, w_out_bf, ln1g, ln1b,
                                            w_router_pad, b_router_pad)
        x1s.append(x1)
        u2s.append(u2)
        logit_list.append(logits)
        mod_rows.append(mod_row0)
        mod_row0 += B

    logits_all = jnp.concatenate(logit_list, axis=0)
    N = logits_all.shape[0]
    idx, gates, tile_cnt = _routing(logits_all)

    tc = tile_cnt[:, 0, :N_EXPERTS]
    seg = (tc + SEG - 1) // SEG * SEG
    total = jnp.sum(seg, axis=0)
    padded = (total + MOE_ROWS - 1) // MOE_ROWS * MOE_ROWS
    pad_end = jnp.cumsum(padded)
    pad_start = pad_end - padded
    tile_start = pad_start[None, :] + jnp.cumsum(seg, axis=0) - seg
    n_tiles = N // SCATTER_TILE
    n_blocks = (N * TOP_K + n_tiles * N_EXPERTS * (SEG - 1)) // MOE_ROWS + N_EXPERTS
    n_rows = n_blocks * MOE_ROWS
    blk_end = pad_end // MOE_ROWS
    n_used = blk_end[-1:].astype(I32)
    blk_expert = jnp.minimum(
        jnp.sum(blk_end[None, :] <= jnp.arange(n_blocks, dtype=I32)[:, None], axis=1), N_EXPERTS - 1).astype(I32)
    lane_pad = ((0, 0), (0, LANES - N_EXPERTS))
    tile_start3 = jnp.pad(tile_start, lane_pad).astype(I32)[:, None, :]
    tails = jnp.pad(jnp.stack([pad_start + total, pad_end]), lane_pad).astype(I32)

    xs = None
    row0 = 0
    for u2 in u2s:
        n = u2.shape[0]
        t0, t1 = row0 // SCATTER_TILE, (row0 + n) // SCATTER_TILE
        xs = _dispatch(idx[row0:row0 + n], u2, tile_cnt[t0:t1], tile_start3[t0:t1], tails, xs, n_rows)
        row0 += n

    w_up_de = _deinterleave_up(w_up[0])
    b_up_de = jnp.concatenate([b_up[0][:, 0::2], b_up[0][:, 1::2]], axis=-1).reshape(N_EXPERTS, 1, 2 * D_FF)
    w_down_bf = w_down[0].astype(BF16)
    b_down3 = b_down[0].reshape(N_EXPERTS, 1, D)
    y_pad = _expert_blocks(xs, blk_expert, n_used, w_up_de, b_up_de, w_down_bf, b_down3)

    outs = []
    row0 = 0
    for x, x1, mrow in zip(groups, x1s, mod_rows):
        B, T, _ = x.shape
        n = B * T
        t0, t1 = row0 // SCATTER_TILE, (row0 + n) // SCATTER_TILE
        outs.append(_combine(idx[row0:row0 + n], gates[row0:row0 + n], tile_cnt[t0:t1], tile_start3[t0:t1],
                             y_pad, x1, mod3, mrow, ln2g, ln2b, B, T))
        row0 += n
    return tuple(outs)
```

```python
import functools

import numpy as np
import jax
import jax.numpy as jnp
from jax import lax
from jax.experimental import pallas as pl
from jax.experimental.pallas import tpu as pltpu

F32 = jnp.float32
BF16 = jnp.bfloat16
I32 = jnp.int32

D_MODEL = 1024
HEAD_DIM = 64
N_HEADS = 8
WIDTH = N_HEADS * HEAD_DIM
N_PAIRS = WIDTH // 128
ROPE_THETA = 10000.0
RADIUS = 64
GRID_W = 64
NA_ROWS = 8
NA_COLS = 16
N_EXPERTS = 32
TOP_K = 4
D_FF = 1024
SWIGLU_ALPHA = 1.702
SWIGLU_LIMIT = 7.0
DEEPNORM_ALPHA = 2.0 ** 0.25
LN_EPS = 1e-5
RMS_EPS = 1e-6
NEG = -1e30

LANES = 128
RES = 16
CHUNK = 128
RES_PER_STEP = 4
MOE_ROWS = 512
ROUTE_TILE = 512
SCATTER_TILE = 256
GATHER_TILE = 128
VMEM_LIMIT = 56 * 1024 * 1024


def _cparams(sem, vmem=VMEM_LIMIT):
    return pltpu.CompilerParams(dimension_semantics=sem, vmem_limit_bytes=vmem)


def _mod_kernel(c_ref, w_ref, b_ref, o_ref):
    c = c_ref[...]
    s = c * (1.0 / (1.0 + jnp.exp(-c)))
    o_ref[...] = jnp.dot(s.astype(BF16), w_ref[...].astype(BF16), preferred_element_type=F32) + b_ref[...]


def _modulation(c, w_mod, b_mod):
    rows = c.shape[0]
    n_out = w_mod.shape[1]
    tn = 1024
    return pl.pallas_call(
        _mod_kernel,
        grid=(n_out // tn,),
        in_specs=[pl.BlockSpec((rows, D_MODEL), lambda j: (0, 0)),
                  pl.BlockSpec((D_MODEL, tn), lambda j: (0, j)),
                  pl.BlockSpec((1, tn), lambda j: (0, j))],
        out_specs=pl.BlockSpec((rows, tn), lambda j: (0, j)),
        out_shape=jax.ShapeDtypeStruct((rows, n_out), F32),
        compiler_params=_cparams(("arbitrary",)),
        name="modulation",
    )(c, w_mod, b_mod.reshape(1, n_out))


def _residue_rows(ref, row0, r):
    return ref.at[pl.ds(row0, CHUNK), r]


def _inproj_kernel(x_hbm, mod_ref, w_ref, cos_ref, sin_ref,
                   qa_ref, ka_ref, va_ref, qb_ref, kb_ref, vb_ref, xbuf, u_scr, sem):
    D = D_MODEL
    row0 = pl.program_id(0) * CHUNK
    shift = mod_ref[:, 0:D]
    scale = mod_ref[:, D:2 * D]
    lane = lax.broadcasted_iota(I32, (1, WIDTH), 1)
    first_half = (lane % HEAD_DIM) < (HEAD_DIM // 2)
    qscale = HEAD_DIM ** -0.5

    def x_copy(r):
        return pltpu.make_async_copy(_residue_rows(x_hbm, row0, r), xbuf.at[r % 2], sem.at[r % 2])

    x_copy(0).start()
    for q in range(RES // RES_PER_STEP):
        for a in range(RES_PER_STEP):
            r = q * RES_PER_STEP + a
            x_copy(r).wait()
            if r + 1 < RES:
                x_copy(r + 1).start()
            u_scr[a * CHUNK:(a + 1) * CHUNK, :] = (xbuf[r % 2] * (1.0 + scale) + shift).astype(BF16)
        u = u_scr[...]
        lo = q * RES_PER_STEP * LANES
        cosf = jnp.concatenate([jnp.tile(cos_ref[:, lo + a * LANES:lo + (a + 1) * LANES], (1, N_PAIRS))
                                for a in range(RES_PER_STEP)], axis=0)
        sinf = jnp.concatenate([jnp.tile(sin_ref[:, lo + a * LANES:lo + (a + 1) * LANES], (1, N_PAIRS))
                                for a in range(RES_PER_STEP)], axis=0)

        def proj(i):
            return jnp.dot(u, w_ref[:, i * WIDTH:(i + 1) * WIDTH], preferred_element_type=F32)

        def rope(p):
            rot = jnp.where(first_half, pltpu.roll(p, WIDTH - HEAD_DIM // 2, 1), pltpu.roll(p, HEAD_DIM // 2, 1))
            return p * cosf + rot * sinf

        def emit_res_major(ref, p):
            for a in range(RES_PER_STEP):
                for hp in range(N_PAIRS):
                    ref[q * RES_PER_STEP + a, hp] = (
                        p[a * CHUNK:(a + 1) * CHUNK, hp * LANES:(hp + 1) * LANES].astype(BF16))

        def emit_natural(ref, p):
            for a in range(RES_PER_STEP):
                for hp in range(N_PAIRS):
                    ref[hp, :, lo + a * LANES:lo + (a + 1) * LANES] = (
                        p[a * CHUNK:(a + 1) * CHUNK, hp * LANES:(hp + 1) * LANES].astype(BF16))

        emit_res_major(qa_ref, rope(proj(0)) * qscale)
        emit_res_major(ka_ref, rope(proj(1)))
        emit_res_major(va_ref, proj(2))
        emit_natural(qb_ref, proj(3) * qscale)
        emit_natural(kb_ref, proj(4))
        emit_natural(vb_ref, proj(5))


def _input_projection(x, mod3, w_in_bf, cos_v, sin_v, mod_row0):
    B, T, D = x.shape
    L = T // RES
    BL = B * L
    cps = L // CHUNK
    x3 = x.reshape(BL, RES, D)
    res_shape = jax.ShapeDtypeStruct((RES, N_PAIRS, BL, LANES), BF16)
    nat_shape = jax.ShapeDtypeStruct((N_PAIRS, BL, RES * LANES), BF16)
    res_spec = pl.BlockSpec((RES, N_PAIRS, CHUNK, LANES), lambda i: (0, 0, i, 0))
    nat_spec = pl.BlockSpec((N_PAIRS, CHUNK, RES * LANES), lambda i: (0, i, 0))
    outs = pl.pallas_call(
        _inproj_kernel,
        grid=(BL // CHUNK,),
        in_specs=[pl.BlockSpec(memory_space=pl.ANY),
                  pl.BlockSpec((None, 1, 6 * D), lambda i: (mod_row0 + i // cps, 0, 0)),
                  pl.BlockSpec((D, 6 * WIDTH), lambda i: (0, 0)),
                  pl.BlockSpec((CHUNK, RES * LANES), lambda i: (i % cps, 0)),
                  pl.BlockSpec((CHUNK, RES * LANES), lambda i: (i % cps, 0))],
        out_specs=[res_spec, res_spec, res_spec, nat_spec, nat_spec, nat_spec],
        out_shape=[res_shape, res_shape, res_shape, nat_shape, nat_shape, nat_shape],
        scratch_shapes=[pltpu.VMEM((2, CHUNK, D), F32), pltpu.VMEM((RES_PER_STEP * CHUNK, D), BF16),
                        pltpu.SemaphoreType.DMA((2,))],
        compiler_params=_cparams(("arbitrary",)),
        name="input_projection",
    )(x3, mod3, w_in_bf, cos_v, sin_v)
    qa, ka, va, qb, kb, vb = outs
    N = B * T
    return qa, ka, va, qb.reshape(N_PAIRS, N, LANES), kb.reshape(N_PAIRS, N, LANES), vb.reshape(N_PAIRS, N, LANES)


PERM = 256
PROJ_TOKENS = 1024
PROJ_ROWS = PROJ_TOKENS // RES


def _residue_permutation():
    p = np.zeros((PERM, PERM), np.float32)
    m, r = np.meshgrid(np.arange(PERM // RES), np.arange(RES), indexing="ij")
    p[(r * (PERM // RES) + m).reshape(-1), (m * RES + r).reshape(-1)] = 1.0
    return p


def _inproj_nat_kernel(x_ref, mod_ref, w_ref, cos_ref, sin_ref, perm_ref,
                       qa_ref, ka_ref, va_ref, qb_ref, kb_ref, vb_ref, u_nat, u_view):
    D = D_MODEL
    shift = mod_ref[:, 0:D]
    scale = mod_ref[:, D:2 * D]
    qscale = HEAD_DIM ** -0.5
    u_nat[...] = (x_ref[...] * (1.0 + scale) + shift).astype(BF16)

    sub = 512
    for s in range(PROJ_TOKENS // sub):
        u = u_nat[s * sub:(s + 1) * sub, :]
        for i, (ref, mul) in enumerate(((qb_ref, qscale), (kb_ref, 1.0), (vb_ref, 1.0))):
            p = jnp.dot(u, w_ref[:, (3 + i) * WIDTH:(4 + i) * WIDTH], preferred_element_type=F32) * mul
            for hp in range(N_PAIRS):
                ref[hp, s * sub:(s + 1) * sub, :] = p[:, hp * LANES:(hp + 1) * LANES].astype(BF16)

    rows_g = PERM // RES
    for g in range(PROJ_TOKENS // PERM):
        pv = jnp.dot(perm_ref[...], u_nat[g * PERM:(g + 1) * PERM, :], preferred_element_type=F32).astype(BF16)
        for r in range(RES):
            u_view[r, g * rows_g:(g + 1) * rows_g, :] = pv[r * rows_g:(r + 1) * rows_g]

    lane = lax.broadcasted_iota(I32, (1, WIDTH), 1)
    first_half = (lane % HEAD_DIM) < (HEAD_DIM // 2)
    for q in range(RES // RES_PER_STEP):
        u = jnp.concatenate([u_view[q * RES_PER_STEP + a] for a in range(RES_PER_STEP)], axis=0)
        lo = q * RES_PER_STEP * LANES
        cosf = jnp.concatenate([jnp.tile(cos_ref[:, lo + a * LANES:lo + (a + 1) * LANES], (1, N_PAIRS))
                                for a in range(RES_PER_STEP)], axis=0)
        sinf = jnp.concatenate([jnp.tile(sin_ref[:, lo + a * LANES:lo + (a + 1) * LANES], (1, N_PAIRS))
                                for a in range(RES_PER_STEP)], axis=0)

        def rope(p):
            rot = jnp.where(first_half, pltpu.roll(p, WIDTH - HEAD_DIM // 2, 1), pltpu.roll(p, HEAD_DIM // 2, 1))
            return p * cosf + rot * sinf

        for i, ref in enumerate((qa_ref, ka_ref, va_ref)):
            p = jnp.dot(u, w_ref[:, i * WIDTH:(i + 1) * WIDTH], preferred_element_type=F32)
            if i == 0:
                p = rope(p) * qscale
            elif i == 1:
                p = rope(p)
            for a in range(RES_PER_STEP):
                for hp in range(N_PAIRS):
                    ref[q * RES_PER_STEP + a, hp] = (
                        p[a * PROJ_ROWS:(a + 1) * PROJ_ROWS, hp * LANES:(hp + 1) * LANES].astype(BF16))


def _input_projection_nat(x, mod3, w_in_bf, cos_v, sin_v, perm, mod_row0):
    B, T, D = x.shape
    N = B * T
    BL = N // RES
    tps = T // PROJ_TOKENS
    res_shape = jax.ShapeDtypeStruct((RES, N_PAIRS, BL, LANES), BF16)
    nat_shape = jax.ShapeDtypeStruct((N_PAIRS, N, LANES), BF16)
    res_spec = pl.BlockSpec((RES, N_PAIRS, PROJ_ROWS, LANES), lambda i: (0, 0, i, 0))
    nat_spec = pl.BlockSpec((N_PAIRS, PROJ_TOKENS, LANES), lambda i: (0, i, 0))
    return pl.pallas_call(
        _inproj_nat_kernel,
        grid=(N // PROJ_TOKENS,),
        in_specs=[pl.BlockSpec((PROJ_TOKENS, D), lambda i: (i, 0)),
                  pl.BlockSpec((None, 1, 6 * D), lambda i: (mod_row0 + i // tps, 0, 0)),
                  pl.BlockSpec((D, 6 * WIDTH), lambda i: (0, 0)),
                  pl.BlockSpec((PROJ_ROWS, RES * LANES), lambda i: (i % tps, 0)),
                  pl.BlockSpec((PROJ_ROWS, RES * LANES), lambda i: (i % tps, 0)),
                  pl.BlockSpec((PERM, PERM), lambda i: (0, 0))],
        out_specs=[res_spec, res_spec, res_spec, nat_spec, nat_spec, nat_spec],
        out_shape=[res_shape, res_shape, res_shape, nat_shape, nat_shape, nat_shape],
        scratch_shapes=[pltpu.VMEM((PROJ_TOKENS, D), BF16), pltpu.VMEM((RES, PROJ_ROWS, D), BF16)],
        compiler_params=_cparams(("arbitrary",)),
        name="input_projection",
    )(x.reshape(N, D), mod3, w_in_bf, cos_v, sin_v, perm)


def _rope_tables(T):
    half = HEAD_DIM // 2
    inv_freq = 1.0 / (ROPE_THETA ** (jnp.arange(half, dtype=F32) / half))
    ang = jnp.arange(T, dtype=F32)[:, None] * inv_freq[None, :]
    cos, sin = jnp.cos(ang), jnp.sin(ang)
    cos_h = jnp.concatenate([cos, cos], axis=-1)
    sin_h = jnp.concatenate([-sin, sin], axis=-1)
    cos2 = jnp.concatenate([cos_h, cos_h], axis=-1)
    sin2 = jnp.concatenate([sin_h, sin_h], axis=-1)
    L = T // RES
    return cos2.reshape(L, RES * LANES), sin2.reshape(L, RES * LANES)


P2_ROWS = 32
P2_KROWS = 64
P1_ROWS = 16
P1_KROWS = 32
P1_SHIFT = 8
HALO = 64
P3_BATCH = 4
P1_BATCH = 2


def _band_tables(has_halo):
    def mask(ok):
        return np.where(ok, 0.0, NEG).astype(np.float32)
    mq = np.arange(CHUNK)[:, None]
    koff3 = np.arange(2 * CHUNK) - HALO
    b3 = mask(np.abs(mq - koff3[None, :]) <= RADIUS)
    j = np.repeat(np.arange(4), P2_ROWS)[:, None]
    a = np.tile(np.arange(P2_ROWS), 4)[:, None]
    jk = np.repeat(np.arange(4), P2_KROWS)[None, :]
    bk = np.tile(np.arange(P2_KROWS), 4)[None, :]
    b2 = mask(np.abs(4 * (a - (bk - 16)) + (j - jk)) <= RADIUS)
    koff2 = (bk - 16).reshape(-1)
    r = np.repeat(np.arange(RES), P1_ROWS)[:, None]
    a = np.tile(np.arange(P1_ROWS), RES)[:, None]
    rk = np.repeat(np.arange(RES), P1_KROWS)[None, :]
    bk = np.tile(np.arange(P1_KROWS), RES)[None, :]
    b1 = mask(np.abs(RES * (a - (bk - P1_SHIFT)) + (r - rk)) <= RADIUS)
    koff1 = (bk - P1_SHIFT).reshape(-1)
    return (jnp.asarray(b3), jnp.asarray(koff3.astype(np.int32)[None, :]),
            jnp.asarray(b2), jnp.asarray(koff2.astype(np.int32)[None, :]),
            jnp.asarray(b1), jnp.asarray(koff1.astype(np.int32)[None, :]))


def _attend_pairs(problems, head0):
    scores = []
    for q, k, _, _ in problems:
        for h in range(2):
            sel = head0 if h == 0 else jnp.logical_not(head0)
            qh = jnp.where(sel, q, jnp.zeros_like(q))
            scores.append(lax.dot_general(qh, k, (((1,), (1,)), ((), ())), preferred_element_type=F32))
    probs, stats = [], []
    for i, s in enumerate(scores):
        s = s + problems[i // 2][3]
        m = jnp.max(s, axis=-1, keepdims=True)
        p = jnp.exp(s - m)
        stats.append((m, jnp.sum(p, axis=-1, keepdims=True)))
        probs.append(p.astype(BF16))
    outs = [jnp.dot(p, problems[i // 2][2], preferred_element_type=F32) for i, p in enumerate(probs)]
    results = []
    for i in range(len(problems)):
        (m0, l0), (m1, l1) = stats[2 * i], stats[2 * i + 1]
        results.append((jnp.where(head0, outs[2 * i], outs[2 * i + 1]),
                        jnp.where(head0, m0, m1), jnp.where(head0, l0, l1)))
    return results


def _dilated_kernel(*refs, has_halo, seq_rows, chunks_per_seq):
    if has_halo:
        (q_ref, k_ref, v_ref, kp_ref, kn_ref, vp_ref, vn_ref,
         b3_ref, o3_ref, b2_ref, o2_ref, b1_ref, o1_ref,
         out_ref, kf, vf, kf8, vf8, acc, ms, ls) = refs
    else:
        (q_ref, k_ref, v_ref, b3_ref, o3_ref, b2_ref, o2_ref, b1_ref, o1_ref,
         out_ref, kf, vf, kf8, vf8, acc, ms, ls) = refs

    c = pl.program_id(0) % chunks_per_seq
    row0 = c * CHUNK
    lane = lax.broadcasted_iota(I32, (1, LANES), 1)
    head0 = lane < HEAD_DIM

    for r in range(RES):
        kf[r, HALO:HALO + CHUNK, :] = k_ref[r]
        vf[r, HALO:HALO + CHUNK, :] = v_ref[r]
        if has_halo:
            kf[r, 0:HALO, :] = kp_ref[r]
            vf[r, 0:HALO, :] = vp_ref[r]
            kf[r, HALO + CHUNK:, :] = kn_ref[r]
            vf[r, HALO + CHUNK:, :] = vn_ref[r]
        else:
            zeros = jnp.zeros((HALO, LANES), BF16)
            kf[r, 0:HALO, :] = zeros
            vf[r, 0:HALO, :] = zeros
            kf[r, HALO + CHUNK:, :] = zeros
            vf[r, HALO + CHUNK:, :] = zeros
        for src, dst in ((kf, kf8), (vf, vf8)):
            w32 = src[r].astype(F32)
            dst[r] = jnp.concatenate([w32[P1_SHIFT:], w32[:P1_SHIFT]], axis=0).astype(BF16)

    def in_seq(base, off_ref):
        kv_row = base + off_ref[...]
        return jnp.where((kv_row >= 0) & (kv_row < seq_rows), 0.0, NEG).astype(F32)

    def attend(problems):
        return _attend_pairs(problems, head0)

    def merge(a_old, m_old, l_old, o, m, l):
        mn = jnp.maximum(m_old, m)
        wa = jnp.exp(m_old - mn)
        wb = jnp.exp(m - mn)
        return a_old * wa + o * wb, mn, l_old * wa + l * wb

    bias3 = b3_ref[...] + in_seq(row0, o3_ref)

    def body3(it, carry):
        rs = [it * P3_BATCH + u for u in range(P3_BATCH)]
        res = attend([(q_ref[r], kf[r], vf[r], bias3) for r in rs])
        for r, (o, m, l) in zip(rs, res):
            acc[r] = o
            ms[r] = m
            ls[r] = l
        return carry

    lax.fori_loop(0, RES // P3_BATCH, body3, 0)

    b2 = b2_ref[...]

    def body2(g, carry):
        qs = pl.multiple_of(g * P2_ROWS, P2_ROWS)
        ks = pl.multiple_of(HALO - 16 + g * P2_ROWS, 16)
        bias = b2 + in_seq(row0 + g * P2_ROWS, o2_ref)

        def gather(ref, r4, start, rows):
            return jnp.concatenate([ref[4 * j + r4, pl.ds(start, rows), :] for j in range(4)], axis=0)

        res = attend([(gather(q_ref, r4, qs, P2_ROWS), gather(kf, r4, ks, P2_KROWS),
                       gather(vf, r4, ks, P2_KROWS), bias) for r4 in range(4)])
        for r4, (o, m, l) in enumerate(res):
            a_new, m_new, l_new = merge(gather(acc, r4, qs, P2_ROWS), gather(ms, r4, qs, P2_ROWS),
                                        gather(ls, r4, qs, P2_ROWS), o, m, l)
            for j in range(4):
                acc[4 * j + r4, pl.ds(qs, P2_ROWS), :] = a_new[j * P2_ROWS:(j + 1) * P2_ROWS]
                ms[4 * j + r4, pl.ds(qs, P2_ROWS), :] = m_new[j * P2_ROWS:(j + 1) * P2_ROWS]
                ls[4 * j + r4, pl.ds(qs, P2_ROWS), :] = l_new[j * P2_ROWS:(j + 1) * P2_ROWS]
        return carry

    lax.fori_loop(0, CHUNK // P2_ROWS, body2, 0)

    b1 = b1_ref[...]

    def body1(it, carry):
        def gather(ref, start, rows):
            return jnp.concatenate([ref[r, pl.ds(start, rows), :] for r in range(RES)], axis=0)

        starts, problems = [], []
        for u in range(P1_BATCH):
            g = it * P1_BATCH + u
            qs = pl.multiple_of(g * P1_ROWS, P1_ROWS)
            ks = pl.multiple_of(HALO - 16 + g * P1_ROWS, 16)
            starts.append(qs)
            problems.append((gather(q_ref, qs, P1_ROWS), gather(kf8, ks, P1_KROWS), gather(vf8, ks, P1_KROWS),
                             b1 + in_seq(row0 + g * P1_ROWS, o1_ref)))
        for qs, (o, m, l) in zip(starts, attend(problems)):
            a_new, m_new, l_new = merge(gather(acc, qs, P1_ROWS), gather(ms, qs, P1_ROWS),
                                        gather(ls, qs, P1_ROWS), o, m, l)
            for r in range(RES):
                acc[r, pl.ds(qs, P1_ROWS), :] = a_new[r * P1_ROWS:(r + 1) * P1_ROWS]
                ms[r, pl.ds(qs, P1_ROWS), :] = m_new[r * P1_ROWS:(r + 1) * P1_ROWS]
                ls[r, pl.ds(qs, P1_ROWS), :] = l_new[r * P1_ROWS:(r + 1) * P1_ROWS]
        return carry

    lax.fori_loop(0, CHUNK // P1_ROWS // P1_BATCH, body1, 0)

    def body_out(r, carry):
        out_ref[r] = (acc[r] / ls[r]).astype(BF16)
        return carry

    lax.fori_loop(0, RES, body_out, 0)


def _dilated_attention(qa, ka, va, B, T):
    L = T // RES
    BL = B * L
    cps = L // CHUNK
    has_halo = cps > 1
    tables = _band_tables(has_halo)
    blk = (RES, None, CHUNK, LANES)
    center = pl.BlockSpec(blk, lambda i, hp: (0, hp, i, 0))
    in_specs = [center, center, center]
    args = [qa, ka, va]
    if has_halo:
        hblk = (RES, None, HALO, LANES)
        per = CHUNK // HALO

        def prev_map(i, hp):
            return (0, hp, jnp.maximum(per * i - 1, (i // cps) * cps * per), 0)

        def next_map(i, hp):
            return (0, hp, jnp.minimum(per * i + per, (i // cps + 1) * cps * per - 1), 0)

        in_specs += [pl.BlockSpec(hblk, prev_map), pl.BlockSpec(hblk, next_map),
                     pl.BlockSpec(hblk, prev_map), pl.BlockSpec(hblk, next_map)]
        args += [ka, ka, va, va]
    for t in tables:
        in_specs.append(pl.BlockSpec(t.shape, lambda i, hp: (0, 0)))
        args.append(t)
    kern = functools.partial(_dilated_kernel, has_halo=has_halo, seq_rows=L, chunks_per_seq=cps)
    return pl.pallas_call(
        kern,
        grid=(BL // CHUNK, N_PAIRS),
        in_specs=in_specs,
        out_specs=pl.BlockSpec(blk, lambda i, hp: (0, hp, i, 0)),
        out_shape=jax.ShapeDtypeStruct((RES, N_PAIRS, BL, LANES), BF16),
        scratch_shapes=[pltpu.VMEM((RES, 2 * CHUNK, LANES), BF16),
                        pltpu.VMEM((RES, 2 * CHUNK, LANES), BF16),
                        pltpu.VMEM((RES, 2 * CHUNK, LANES), BF16),
                        pltpu.VMEM((RES, 2 * CHUNK, LANES), BF16),
                        pltpu.VMEM((RES, CHUNK, LANES), F32),
                        pltpu.VMEM((RES, CHUNK, LANES), F32),
                        pltpu.VMEM((RES, CHUNK, LANES), F32)],
        compiler_params=_cparams(("arbitrary", "arbitrary")),
        name="dilated_attention",
    )(*args)


NA_KEYS = NA_ROWS * GRID_W
NA_BLOCK_ROWS = 32
NA_BATCH = 8


def _na_bias_table(rpb):
    c = np.arange(GRID_W)
    col_start = np.clip(c - NA_COLS // 2, 0, GRID_W - NA_COLS)
    col_mask = (c[None, :] >= col_start[:, None]) & (c[None, :] < col_start[:, None] + NA_COLS)
    dc_idx = np.clip(c[None, :] - c[:, None], -(NA_COLS - 1), NA_COLS - 1) + NA_COLS - 1
    rel = rpb.astype(F32)[:, :, dc_idx]
    rel = jnp.where(col_mask[None, None], rel, NEG)
    per_off = [rel[:, d0:d0 + NA_ROWS].transpose(0, 2, 1, 3).reshape(N_HEADS, GRID_W, NA_KEYS)
               for d0 in range(NA_ROWS)]
    return jnp.stack(per_off, axis=0).reshape(NA_ROWS, N_PAIRS, 2, GRID_W, NA_KEYS)


def _na_kernel(q_ref, k_ref, v_ref, bias_ref, out_ref, *, grid_rows, block_rows):
    gb = pl.program_id(2)
    lane = lax.broadcasted_iota(I32, (1, LANES), 1)
    head0 = lane < HEAD_DIM

    def body(it, carry):
        rows, scores = [], []
        for u in range(NA_BATCH):
            i = it * NA_BATCH + u
            g = gb * block_rows + i
            rs = jnp.clip(g - NA_ROWS // 2, 0, grid_rows - NA_ROWS)
            d0 = rs - g + NA_ROWS - 1
            qs = pl.multiple_of(i * GRID_W, GRID_W)
            ks = pl.multiple_of(rs * GRID_W, GRID_W)
            q = q_ref[pl.ds(qs, GRID_W), :]
            k = k_ref[pl.ds(ks, NA_KEYS), :]
            rows.append((qs, ks, d0))
            for h in range(2):
                sel = head0 if h == 0 else jnp.logical_not(head0)
                qh = jnp.where(sel, q, jnp.zeros_like(q))
                scores.append(lax.dot_general(qh, k, (((1,), (1,)), ((), ())), preferred_element_type=F32))
        probs, sums = [], []
        for n, s in enumerate(scores):
            s = s + bias_ref[rows[n // 2][2], n % 2]
            p = jnp.exp(s - jnp.max(s, axis=-1, keepdims=True))
            sums.append(jnp.sum(p, axis=-1, keepdims=True))
            probs.append(p.astype(BF16))
        outs = []
        for n, p in enumerate(probs):
            v = v_ref[pl.ds(rows[n // 2][1], NA_KEYS), :]
            outs.append(jnp.dot(p, v, preferred_element_type=F32) / sums[n])
        for u, (qs, _, _) in enumerate(rows):
            out_ref[pl.ds(qs, GRID_W), :] = jnp.where(head0, outs[2 * u], outs[2 * u + 1]).astype(BF16)
        return carry

    lax.fori_loop(0, block_rows // NA_BATCH, body, 0)


def _neighbourhood_attention(qb, kb, vb, bias, B, T):
    G = T // GRID_W
    rb = min(NA_BLOCK_ROWS, G)
    nb = G // rb
    kern = functools.partial(_na_kernel, grid_rows=G, block_rows=rb)
    seq = pl.BlockSpec((None, T, LANES), lambda hp, b, gb: (hp, b, 0))
    qblk = pl.BlockSpec((None, rb * GRID_W, LANES), lambda hp, b, gb: (hp, b * nb + gb, 0))
    return pl.pallas_call(
        kern,
        grid=(N_PAIRS, B, nb),
        in_specs=[qblk, seq, seq,
                  pl.BlockSpec((NA_ROWS, None, 2, GRID_W, NA_KEYS), lambda hp, b, gb: (0, hp, 0, 0, 0))],
        out_specs=qblk,
        out_shape=jax.ShapeDtypeStruct((N_PAIRS, B * T, LANES), BF16),
        compiler_params=_cparams(("arbitrary", "arbitrary", "arbitrary")),
        name="neighbourhood_attention",
    )(qb, kb, vb, bias)


def _layer_norm(h, g, b):
    mu = jnp.mean(h, axis=-1, keepdims=True)
    d = h - mu
    var = jnp.mean(d * d, axis=-1, keepdims=True)
    return d * lax.rsqrt(var + LN_EPS) * g + b


def _outproj_nat_kernel(oa_ref, ob_ref, x_ref, mod_ref, ga_ref, gb_ref, w_ref, lg_ref, lb_ref, wr_ref, br_ref,
                        permt_ref, x1_ref, u2_ref, logit_ref, mix_scr):
    D = D_MODEL
    gate_a = mod_ref[:, 2 * D:3 * D]
    shift_f = mod_ref[:, 3 * D:4 * D]
    scale_f = mod_ref[:, 4 * D:5 * D]

    def rms(o, g):
        return o * lax.rsqrt(jnp.mean(o * o, axis=-1, keepdims=True) + RMS_EPS) * g

    rows_g = PERM // RES
    for g in range(PROJ_TOKENS // PERM):
        grouped = jnp.concatenate(
            [jnp.concatenate([oa_ref[r, hp, g * rows_g:(g + 1) * rows_g, :] for hp in range(N_PAIRS)], axis=1)
             for r in range(RES)], axis=0)
        oa = jnp.dot(permt_ref[...], grouped, preferred_element_type=F32)
        ob = jnp.concatenate([ob_ref[hp, g * PERM:(g + 1) * PERM, :] for hp in range(N_PAIRS)], axis=1).astype(F32)
        mix = jnp.concatenate([rms(oa, ga_ref[...]), rms(ob, gb_ref[...])], axis=1)
        mix_scr[g * PERM:(g + 1) * PERM, :] = mix.astype(BF16)

    wr = wr_ref[...]
    wr_hi = wr.astype(BF16)
    wr_lo = (wr - wr_hi.astype(F32)).astype(BF16)
    sub = 256
    for s in range(PROJ_TOKENS // sub):
        rows = slice(s * sub, (s + 1) * sub)
        y = jnp.dot(mix_scr[rows, :], w_ref[...], preferred_element_type=F32)
        h = DEEPNORM_ALPHA * x_ref[rows, :] + (1.0 + gate_a) * y
        x1 = _layer_norm(h, lg_ref[...], lb_ref[...])
        x1_ref[rows, :] = x1
        u2 = x1 * (1.0 + scale_f) + shift_f
        u2_ref[rows, :] = u2.astype(BF16)
        u_hi = u2.astype(BF16)
        u_lo = (u2 - u_hi.astype(F32)).astype(BF16)
        logit_ref[rows, :] = (jnp.dot(u_hi, wr_hi, preferred_element_type=F32)
                              + jnp.dot(u_lo, wr_hi, preferred_element_type=F32)
                              + jnp.dot(u_hi, wr_lo, preferred_element_type=F32)) + br_ref[...]


def _output_projection_nat(oa, ob, x, mod3, mod_row0, g_a, g_b, w_out_bf, ln_g, ln_b, w_router_pad, b_router_pad,
                           perm_t):
    B, T, D = x.shape
    N = B * T
    tps = T // PROJ_TOKENS
    const = lambda shape: pl.BlockSpec(shape, lambda i: tuple(0 for _ in shape))
    rows = lambda width: pl.BlockSpec((PROJ_TOKENS, width), lambda i: (i, 0))
    return pl.pallas_call(
        _outproj_nat_kernel,
        grid=(N // PROJ_TOKENS,),
        in_specs=[pl.BlockSpec((RES, N_PAIRS, PROJ_ROWS, LANES), lambda i: (0, 0, i, 0)),
                  pl.BlockSpec((N_PAIRS, PROJ_TOKENS, LANES), lambda i: (0, i, 0)),
                  rows(D),
                  pl.BlockSpec((None, 1, 6 * D), lambda i: (mod_row0 + i // tps, 0, 0)),
                  const((1, WIDTH)), const((1, WIDTH)), const((2 * WIDTH, D)),
                  const((1, D)), const((1, D)), const((D, LANES)), const((1, LANES)), const((PERM, PERM))],
        out_specs=[rows(D), rows(D), rows(LANES)],
        out_shape=[jax.ShapeDtypeStruct((N, D), F32), jax.ShapeDtypeStruct((N, D), BF16),
                   jax.ShapeDtypeStruct((N, LANES), F32)],
        scratch_shapes=[pltpu.VMEM((PROJ_TOKENS, 2 * WIDTH), BF16)],
        compiler_params=_cparams(("arbitrary",)),
        name="output_projection",
    )(oa, ob, x.reshape(N, D), mod3, g_a, g_b, w_out_bf, ln_g, ln_b, w_router_pad, b_router_pad, perm_t)


def _outproj_kernel(oa_ref, ob_ref, x_hbm, mod_ref, ga_ref, gb_ref, w_ref, lg_ref, lb_ref, wr_ref, br_ref,
                    x1_hbm, u2_hbm, logit_hbm, xbuf, x1buf, u2buf, lgbuf, mix_scr, in_sem, out_sem):
    D = D_MODEL
    row0 = pl.program_id(0) * CHUNK
    gate_a = mod_ref[:, 2 * D:3 * D]
    shift_f = mod_ref[:, 3 * D:4 * D]
    scale_f = mod_ref[:, 4 * D:5 * D]

    def rms(o, g):
        return o * lax.rsqrt(jnp.mean(o * o, axis=-1, keepdims=True) + RMS_EPS) * g

    def x_copy(r):
        return pltpu.make_async_copy(_residue_rows(x_hbm, row0, r), xbuf.at[r % 2], in_sem.at[r % 2])

    def out_copies(r):
        s = r % 2
        return (pltpu.make_async_copy(x1buf.at[s], _residue_rows(x1_hbm, row0, r), out_sem.at[s]),
                pltpu.make_async_copy(u2buf.at[s], _residue_rows(u2_hbm, row0, r), out_sem.at[s]),
                pltpu.make_async_copy(lgbuf.at[s], _residue_rows(logit_hbm, row0, r), out_sem.at[s]))

    wr = wr_ref[...]
    wr_hi = wr.astype(BF16)
    wr_lo = (wr - wr_hi.astype(F32)).astype(BF16)

    x_copy(0).start()
    for q in range(RES // RES_PER_STEP):
        lo = q * RES_PER_STEP * LANES
        for a in range(RES_PER_STEP):
            r = q * RES_PER_STEP + a
            oa = jnp.concatenate([oa_ref[r, hp] for hp in range(N_PAIRS)], axis=1).astype(F32)
            ob = jnp.concatenate([ob_ref[hp, :, lo + a * LANES:lo + (a + 1) * LANES] for hp in range(N_PAIRS)],
                                 axis=1).astype(F32)
            mix = jnp.concatenate([rms(oa, ga_ref[...]), rms(ob, gb_ref[...])], axis=1)
            mix_scr[a * CHUNK:(a + 1) * CHUNK, :] = mix.astype(BF16)
        y = jnp.dot(mix_scr[...], w_ref[...], preferred_element_type=F32)

        for a in range(RES_PER_STEP):
            r = q * RES_PER_STEP + a
            s = r % 2
            x_copy(r).wait()
            if r + 1 < RES:
                x_copy(r + 1).start()
            if r >= 2:
                for c in out_copies(r - 2):
                    c.wait()
            h = DEEPNORM_ALPHA * xbuf[s] + (1.0 + gate_a) * y[a * CHUNK:(a + 1) * CHUNK]
            x1 = _layer_norm(h, lg_ref[...], lb_ref[...])
            u2 = x1 * (1.0 + scale_f) + shift_f
            u_hi = u2.astype(BF16)
            u_lo = (u2 - u_hi.astype(F32)).astype(BF16)
            logits = (jnp.dot(u_hi, wr_hi, preferred_element_type=F32)
                      + jnp.dot(u_lo, wr_hi, preferred_element_type=F32)
                      + jnp.dot(u_hi, wr_lo, preferred_element_type=F32)) + br_ref[...]
            x1buf[s] = x1
            u2buf[s] = u2
            lgbuf[s] = logits
            for c in out_copies(r):
                c.start()
    for r in (RES - 2, RES - 1):
        for c in out_copies(r):
            c.wait()


def _output_projection(oa, ob, x, mod3, mod_row0, g_a, g_b, w_out_bf, ln_g, ln_b, w_router_pad, b_router_pad):
    B, T, D = x.shape
    L = T // RES
    BL = B * L
    cps = L // CHUNK
    x3 = x.reshape(BL, RES, D)
    obv = ob.reshape(N_PAIRS, BL, RES * LANES)
    const = lambda shape: pl.BlockSpec(shape, lambda i: tuple(0 for _ in shape))
    hbm = pl.BlockSpec(memory_space=pl.ANY)
    x1, u2, logits = pl.pallas_call(
        _outproj_kernel,
        grid=(BL // CHUNK,),
        in_specs=[pl.BlockSpec((RES, N_PAIRS, CHUNK, LANES), lambda i: (0, 0, i, 0)),
                  pl.BlockSpec((N_PAIRS, CHUNK, RES * LANES), lambda i: (0, i, 0)),
                  hbm,
                  pl.BlockSpec((None, 1, 6 * D), lambda i: (mod_row0 + i // cps, 0, 0)),
                  const((1, WIDTH)), const((1, WIDTH)), const((2 * WIDTH, D)),
                  const((1, D)), const((1, D)), const((D, LANES)), const((1, LANES))],
        out_specs=[hbm, hbm, hbm],
        out_shape=[jax.ShapeDtypeStruct((BL, RES, D), F32), jax.ShapeDtypeStruct((BL, RES, D), F32),
                   jax.ShapeDtypeStruct((BL, RES, LANES), F32)],
        scratch_shapes=[pltpu.VMEM((2, CHUNK, D), F32), pltpu.VMEM((2, CHUNK, D), F32),
                        pltpu.VMEM((2, CHUNK, D), F32), pltpu.VMEM((2, CHUNK, LANES), F32),
                        pltpu.VMEM((RES_PER_STEP * CHUNK, 2 * WIDTH), BF16),
                        pltpu.SemaphoreType.DMA((2,)), pltpu.SemaphoreType.DMA((2,))],
        compiler_params=_cparams(("arbitrary",)),
        name="output_projection",
    )(oa, obv, x3, mod3, g_a, g_b, w_out_bf, ln_g, ln_b, w_router_pad, b_router_pad)
    N = B * T
    return x1.reshape(N, D), u2.reshape(N, D), logits.reshape(N, LANES)


def _route_kernel(logit_ref, idx_ref, gate_ref, cnt_ref):
    tn = logit_ref.shape[0]
    lane = lax.broadcasted_iota(I32, (tn, LANES), 1)
    logits = jnp.where(lane < N_EXPERTS, logit_ref[...], -3.0e38)
    vals, idxs = [], []
    multi = jnp.zeros((tn, LANES), F32)
    for _ in range(TOP_K):
        m = jnp.max(logits, axis=-1, keepdims=True)
        idx = jnp.min(jnp.where(logits == m, lane, LANES), axis=-1, keepdims=True)
        hot = lane == idx
        vals.append(m)
        idxs.append(idx)
        multi = multi + hot.astype(F32)
        logits = jnp.where(hot, -3.0e38, logits)
    es = [jnp.exp(v - vals[0]) for v in vals]
    tot = es[0] + es[1] + es[2] + es[3]
    lane4 = lax.broadcasted_iota(I32, (tn, TOP_K), 1)
    sel = jnp.zeros((tn, TOP_K), I32)
    gates = jnp.zeros((tn, TOP_K), F32)
    for k in range(TOP_K):
        sel = jnp.where(lane4 == k, idxs[k], sel)
        gates = jnp.where(lane4 == k, es[k] / tot, gates)
    idx_ref[...] = sel
    gate_ref[...] = gates
    cnt_ref[...] = jnp.sum(multi, axis=0, keepdims=True).astype(I32)


def _routing(logits):
    N = logits.shape[0]
    tn = SCATTER_TILE
    nt = N // tn
    return pl.pallas_call(
        _route_kernel,
        grid=(nt,),
        in_specs=[pl.BlockSpec((tn, LANES), lambda i: (i, 0))],
        out_specs=[pl.BlockSpec((tn, TOP_K), lambda i: (i, 0)),
                   pl.BlockSpec((tn, TOP_K), lambda i: (i, 0)),
                   pl.BlockSpec((None, 1, LANES), lambda i: (i, 0, 0))],
        out_shape=[jax.ShapeDtypeStruct((N, TOP_K), I32), jax.ShapeDtypeStruct((N, TOP_K), F32),
                   jax.ShapeDtypeStruct((nt, 1, LANES), I32)],
        compiler_params=_cparams(("arbitrary",)),
        name="moe_routing",
    )(logits)


SEG = 8
STAGE_ROWS = SCATTER_TILE * TOP_K + N_EXPERTS * SEG


def _lane_prefix_exclusive(v):
    lane = lax.broadcasted_iota(I32, v.shape, 1)
    incl = v
    s = 1
    while s < LANES:
        incl = incl + jnp.where(lane >= s, pltpu.roll(incl, s, 1), 0.0)
        s *= 2
    return incl - v


def _tile_ranks(idx, tn):
    lane = lax.broadcasted_iota(I32, (tn, LANES), 1)
    hots = [lane == idx[:, k:k + 1] for k in range(TOP_K)]
    multi = jnp.zeros((tn, LANES), F32)
    for h in hots:
        multi = multi + h.astype(F32)
    row = lax.broadcasted_iota(I32, (tn, tn), 0)
    col = lax.broadcasted_iota(I32, (tn, tn), 1)
    lower = (col < row).astype(BF16)
    before = jnp.dot(lower, multi.astype(BF16), preferred_element_type=F32)
    return hots, before


BIG_SEG = 4 * SEG


def _segment_dma_loops(cnt_s, rows_of, copy):
    def per_expert(e, carry):
        off, n_big, n_small = carry
        nseg = (cnt_s[0, e] + SEG - 1) // SEG
        big = nseg // (BIG_SEG // SEG)
        small = nseg - big * (BIG_SEG // SEG)

        def one_big(q, c):
            src, dst = rows_of(e, off, q * BIG_SEG)
            copy(src, dst, BIG_SEG).start()
            return c

        def one_small(q, c):
            src, dst = rows_of(e, off, big * BIG_SEG + q * SEG)
            copy(src, dst, SEG).start()
            return c

        lax.fori_loop(0, big, one_big, 0)
        lax.fori_loop(0, small, one_small, 0)
        return off + nseg * SEG, n_big + big, n_small + small

    _, n_big, n_small = lax.fori_loop(0, N_EXPERTS, per_expert, (0, 0, 0))
    return n_big, n_small


def _drain(copy, n_big, n_small):
    def wait_big(q, c):
        copy(0, 0, BIG_SEG).wait()
        return c

    def wait_small(q, c):
        copy(0, 0, SEG).wait()
        return c

    lax.fori_loop(0, n_big, wait_big, 0)
    lax.fori_loop(0, n_small, wait_small, 0)


def _move_segments(cnt_s, rows_of, copy):
    n_big, n_small = _segment_dma_loops(cnt_s, rows_of, copy)
    _drain(copy, n_big, n_small)


def _dispatch_kernel(*refs, first):
    if first:
        cnt_s, start_s, tail_s, idx_ref, u_ref, cntv_ref, xs_ref, stage, zbuf, sem = refs
    else:
        cnt_s, start_s, tail_s, idx_ref, u_ref, cntv_ref, _, xs_ref, stage, zbuf, sem = refs
    tn = idx_ref.shape[0]

    if first:
        @pl.when(pl.program_id(0) == 0)
        def _():
            zbuf[...] = jnp.zeros_like(zbuf)

            def tail_copy(row):
                return pltpu.make_async_copy(zbuf, xs_ref.at[pl.ds(pl.multiple_of(row, SEG), SEG)], sem)

            def fill(e, n):
                lo = tail_s[0, e]
                nfull = (tail_s[1, e] - lo) // SEG

                def one(q, c):
                    tail_copy(lo + q * SEG).start()
                    return c

                lax.fori_loop(0, nfull, one, 0)
                return n + nfull

            n = lax.fori_loop(0, N_EXPERTS, fill, 0)

            def drain(q, c):
                tail_copy(0).wait()
                return c

            lax.fori_loop(0, n, drain, 0)

    hots, before = _tile_ranks(idx_ref[...], tn)
    cntf = cntv_ref[...].astype(F32)
    seg_len = jnp.ceil(cntf * (1.0 / SEG)) * SEG
    seg_off = _lane_prefix_exclusive(seg_len)
    lane_r = lax.broadcasted_iota(I32, (tn, STAGE_ROWS), 1)
    onehot = jnp.zeros((tn, STAGE_ROWS), F32)
    for k in range(TOP_K):
        stage_row = jnp.sum(jnp.where(hots[k], seg_off + before, 0.0), axis=-1, keepdims=True)
        onehot = jnp.where(lane_r == stage_row.astype(I32), 1.0, onehot)
    stage[...] = lax.dot_general(onehot.astype(BF16), u_ref[...].astype(BF16), (((0,), (0,)), ((), ())),
                                 preferred_element_type=F32)

    def seg_copy(src_row, dst_row, rows):
        return pltpu.make_async_copy(stage.at[pl.ds(pl.multiple_of(src_row, SEG), rows)],
                                     xs_ref.at[pl.ds(pl.multiple_of(dst_row, SEG), rows)], sem)

    _move_segments(cnt_s, lambda e, off, moved: (off + moved, start_s[0, e] + moved), seg_copy)


def _dispatch(idx, u2, tile_cnt, tile_start, tails, xs, xs_rows):
    n, D = u2.shape
    tn = SCATTER_TILE
    nt = n // tn
    first = xs is None
    smem = lambda shape, imap: pl.BlockSpec(shape, imap, memory_space=pltpu.SMEM)
    kern = functools.partial(_dispatch_kernel, first=first)
    in_specs = [smem((None, 1, LANES), lambda i: (i, 0, 0)),
                smem((None, 1, LANES), lambda i: (i, 0, 0)),
                smem((2, LANES), lambda i: (0, 0)),
                pl.BlockSpec((tn, TOP_K), lambda i: (i, 0)),
                pl.BlockSpec((tn, D), lambda i: (i, 0)),
                pl.BlockSpec((None, 1, LANES), lambda i: (i, 0, 0))]
    args = [tile_cnt, tile_start, tails, idx, u2, tile_cnt]
    alias = {}
    if not first:
        alias = {len(args): 0}
        in_specs.append(pl.BlockSpec(memory_space=pl.ANY))
        args.append(xs)
    return pl.pallas_call(
        kern,
        grid=(nt,),
        in_specs=in_specs,
        out_specs=pl.BlockSpec(memory_space=pl.ANY),
        out_shape=jax.ShapeDtypeStruct((xs_rows, D), F32),
        scratch_shapes=[pltpu.VMEM((STAGE_ROWS, D), F32), pltpu.VMEM((SEG, D), F32), pltpu.SemaphoreType.DMA(())],
        input_output_aliases=alias,
        compiler_params=_cparams(("arbitrary",)),
        name="moe_dispatch",
    )(*args)


DEINT = 256


def _deinterleave_kernel(w_ref, p_ref, o_ref):
    n_groups = w_ref.shape[1] // DEINT
    half = DEINT // 2
    f = w_ref.shape[1] // 2
    for g in range(n_groups):
        wg = w_ref[:, g * DEINT:(g + 1) * DEINT].astype(BF16)
        y = jnp.dot(wg, p_ref[...], preferred_element_type=F32)
        o_ref[:, g * half:(g + 1) * half] = y[:, :half].astype(BF16)
        o_ref[:, f + g * half:f + (g + 1) * half] = y[:, half:].astype(BF16)


def _deinterleave_up(w_up):
    E, D, F2 = w_up.shape
    perm = np.zeros((DEINT, DEINT), np.float32)
    j = np.arange(DEINT // 2)
    perm[2 * j, j] = 1.0
    perm[2 * j + 1, DEINT // 2 + j] = 1.0
    return pl.pallas_call(
        _deinterleave_kernel,
        grid=(E,),
        in_specs=[pl.BlockSpec((None, D, F2), lambda e: (e, 0, 0)),
                  pl.BlockSpec((DEINT, DEINT), lambda e: (0, 0))],
        out_specs=pl.BlockSpec((None, D, F2), lambda e: (e, 0, 0)),
        out_shape=jax.ShapeDtypeStruct((E, D, F2), BF16),
        compiler_params=_cparams(("arbitrary",)),
        name="expert_weight_layout",
    )(w_up, jnp.asarray(perm, BF16))


def _expert_kernel(be_ref, nu_ref, x_ref, wu_ref, bu_ref, wd_ref, bd_ref, y_ref):
    del be_ref

    @pl.when(pl.program_id(0) < nu_ref[0])
    def _():
        x = x_ref[...].astype(BF16)
        hu = jnp.dot(x, wu_ref[...], preferred_element_type=F32) + bu_ref[...]
        glu = jnp.minimum(hu[:, :D_FF], SWIGLU_LIMIT)
        lin = jnp.clip(hu[:, D_FF:], -SWIGLU_LIMIT, SWIGLU_LIMIT)
        act = glu * (1.0 / (1.0 + jnp.exp(-SWIGLU_ALPHA * glu))) * (lin + 1.0)
        y_ref[...] = jnp.dot(act.astype(BF16), wd_ref[...], preferred_element_type=F32) + bd_ref[...]

    @pl.when(pl.program_id(0) >= nu_ref[0])
    def _():
        y_ref[...] = jnp.zeros_like(y_ref)


def _expert_blocks(xs, blk_expert, n_used, w_up_de, b_up_de, w_down_bf, b_down):
    n_rows, D = xs.shape
    nblk = n_rows // MOE_ROWS

    def xmap(j, be, nu):
        return (jnp.minimum(j, nu[0] - 1), 0)

    grid_spec = pltpu.PrefetchScalarGridSpec(
        num_scalar_prefetch=2,
        grid=(nblk,),
        in_specs=[pl.BlockSpec((MOE_ROWS, D), xmap),
                  pl.BlockSpec((None, D, 2 * D_FF), lambda j, be, nu: (be[j], 0, 0)),
                  pl.BlockSpec((None, 1, 2 * D_FF), lambda j, be, nu: (be[j], 0, 0)),
                  pl.BlockSpec((None, D_FF, D), lambda j, be, nu: (be[j], 0, 0)),
                  pl.BlockSpec((None, 1, D), lambda j, be, nu: (be[j], 0, 0))],
        out_specs=pl.BlockSpec((MOE_ROWS, D), lambda j, be, nu: (j, 0)),
    )
    return pl.pallas_call(
        _expert_kernel,
        grid_spec=grid_spec,
        out_shape=jax.ShapeDtypeStruct((n_rows, D), F32),
        compiler_params=_cparams(("arbitrary",)),
        name="moe_experts",
    )(blk_expert, n_used, xs, w_up_de, b_up_de, w_down_bf, b_down)


def _combine_kernel(cnt_s, start_s, idx_ref, gate_ref, cntv_ref, y_ref, x1_ref, mod_ref,
                    lg_ref, lb_ref, out_ref, stage, sem):
    D = D_MODEL
    tn = idx_ref.shape[0]

    def seg_copy(src_row, dst_row, rows):
        return pltpu.make_async_copy(y_ref.at[pl.ds(pl.multiple_of(src_row, SEG), rows)],
                                     stage.at[pl.ds(pl.multiple_of(dst_row, SEG), rows)], sem)

    @pl.when(pl.program_id(0) == 0)
    def _():
        stage[...] = jnp.zeros_like(stage)

    n_big, n_small = _segment_dma_loops(
        cnt_s, lambda e, off, moved: (start_s[0, e] + moved, off + moved), seg_copy)

    hots, before = _tile_ranks(idx_ref[...], tn)
    seg_len = jnp.ceil(cntv_ref[...].astype(F32) * (1.0 / SEG)) * SEG
    seg_off = _lane_prefix_exclusive(seg_len)
    gates = gate_ref[...]
    lane_r = lax.broadcasted_iota(I32, (tn, STAGE_ROWS), 1)
    weights = jnp.zeros((tn, STAGE_ROWS), F32)
    for k in range(TOP_K):
        stage_row = jnp.sum(jnp.where(hots[k], seg_off + before, 0.0), axis=-1, keepdims=True)
        weights = jnp.where(lane_r == stage_row.astype(I32), gates[:, k:k + 1], weights)

    _drain(seg_copy, n_big, n_small)

    y = jnp.dot(weights.astype(BF16), stage[...].astype(BF16), preferred_element_type=F32)
    gate_f = mod_ref[:, 5 * D:6 * D]
    h = DEEPNORM_ALPHA * x1_ref[...] + (1.0 + gate_f) * y
    out_ref[...] = _layer_norm(h, lg_ref[...], lb_ref[...])


def _combine(idx, gates, tile_cnt, tile_start, y_pad, x1, mod3, mod_row0, ln_g, ln_b, B, T):
    D = D_MODEL
    tn = SCATTER_TILE
    n_rows = B * T
    nt = n_rows // tn
    tps = T // tn
    smem = lambda imap: pl.BlockSpec((None, 1, LANES), imap, memory_space=pltpu.SMEM)
    out = pl.pallas_call(
        _combine_kernel,
        grid=(nt,),
        in_specs=[smem(lambda i: (i, 0, 0)), smem(lambda i: (i, 0, 0)),
                  pl.BlockSpec((tn, TOP_K), lambda i: (i, 0)),
                  pl.BlockSpec((tn, TOP_K), lambda i: (i, 0)),
                  pl.BlockSpec((None, 1, LANES), lambda i: (i, 0, 0)),
                  pl.BlockSpec(memory_space=pl.ANY),
                  pl.BlockSpec((tn, D), lambda i: (i, 0)),
                  pl.BlockSpec((None, 1, 6 * D), lambda i: (mod_row0 + i // tps, 0, 0)),
                  pl.BlockSpec((1, D), lambda i: (0, 0)),
                  pl.BlockSpec((1, D), lambda i: (0, 0))],
        out_specs=pl.BlockSpec((tn, D), lambda i: (i, 0)),
        out_shape=jax.ShapeDtypeStruct((n_rows, D), F32),
        scratch_shapes=[pltpu.VMEM((STAGE_ROWS, D), F32), pltpu.SemaphoreType.DMA(())],
        compiler_params=_cparams(("arbitrary",)),
        name="moe_combine",
    )(tile_cnt, tile_start, idx, gates, tile_cnt, y_pad, x1, mod3, ln_g, ln_b)
    return out.reshape(B, T, D)


def kernel(x_prompt, x_sample, c_prompt, c_sample, w_mod, b_mod, w_in, rpb, g_out_a, g_out_b, w_out, ln1_g, ln1_b,
           w_router, b_router, w_up, b_up, w_down, b_down, ln2_g, ln2_b):
    D = D_MODEL
    groups = [x_prompt, x_sample]
    conds = [c_prompt, c_sample]
    n_cond = sum(c.shape[0] for c in conds)
    pad_rows = -n_cond % 8
    c_all = jnp.concatenate(conds + [jnp.zeros((pad_rows, D), F32)], axis=0)
    mod3 = _modulation(c_all, w_mod[0], b_mod[0]).reshape(n_cond + pad_rows, 1, 6 * D)

    w_in_bf = w_in[0].astype(BF16)
    w_out_bf = w_out[0].astype(BF16)
    na_bias = _na_bias_table(rpb[0])
    g_a = g_out_a[0].reshape(1, WIDTH)
    g_b = g_out_b[0].reshape(1, WIDTH)
    ln1g, ln1b = ln1_g[0].reshape(1, D), ln1_b[0].reshape(1, D)
    ln2g, ln2b = ln2_g[0].reshape(1, D), ln2_b[0].reshape(1, D)
    w_router_pad = jnp.pad(w_router[0], ((0, 0), (0, LANES - N_EXPERTS)))
    b_router_pad = jnp.pad(b_router[0], (0, LANES - N_EXPERTS)).reshape(1, LANES)

    perm_np = _residue_permutation()
    perm, perm_t = jnp.asarray(perm_np, BF16), jnp.asarray(perm_np.T, BF16)

    x1s, u2s, logit_list = [], [], []
    mod_row0 = 0
    mod_rows = []
    for x in groups:
        B, T, _ = x.shape
        cos_v, sin_v = _rope_tables(T)
        qa, ka, va, qb, kb, vb = _input_projection_nat(x, mod3, w_in_bf, cos_v, sin_v, perm, mod_row0)
        oa = _dilated_attention(qa, ka, va, B, T)
        ob = _neighbourhood_attention(qb, kb, vb, na_bias, B, T)
        x1, u2, logits = _output_projection_nat(oa, ob, x, mod3, mod_row0, g_a, g_b, w_out_bf, ln1g, ln1b,
                                                w_router_pad, b_router_pad, perm_t)
        x1s.append(x1)
        u2s.append(u2)
        logit_list.append(logits)
        mod_rows.append(mod_row0)
        mod_row0 += B

    logits_all = jnp.concatenate(logit_list, axis=0)
    N = logits_all.shape[0]
    idx, gates, tile_cnt = _routing(logits_all)

    tc = tile_cnt[:, 0, :N_EXPERTS]
    seg = (tc + SEG - 1) // SEG * SEG
    total = jnp.sum(seg, axis=0)
    padded = (total + MOE_ROWS - 1) // MOE_ROWS * MOE_ROWS
    pad_end = jnp.cumsum(padded)
    pad_start = pad_end - padded
    tile_start = pad_start[None, :] + jnp.cumsum(seg, axis=0) - seg
    n_tiles = N // SCATTER_TILE
    n_blocks = (N * TOP_K + n_tiles * N_EXPERTS * (SEG - 1)) // MOE_ROWS + N_EXPERTS
    n_rows = n_blocks * MOE_ROWS
    blk_end = pad_end // MOE_ROWS
    n_used = blk_end[-1:].astype(I32)
    blk_expert = jnp.minimum(
        jnp.sum(blk_end[None, :] <= jnp.arange(n_blocks, dtype=I32)[:, None], axis=1), N_EXPERTS - 1).astype(I32)
    lane_pad = ((0, 0), (0, LANES - N_EXPERTS))
    tile_start3 = jnp.pad(tile_start, lane_pad).astype(I32)[:, None, :]
    tails = jnp.pad(jnp.stack([pad_start + total, pad_end]), lane_pad).astype(I32)

    xs = None
    row0 = 0
    for u2 in u2s:
        n = u2.shape[0]
        t0, t1 = row0 // SCATTER_TILE, (row0 + n) // SCATTER_TILE
        xs = _dispatch(idx[row0:row0 + n], u2, tile_cnt[t0:t1], tile_start3[t0:t1], tails, xs, n_rows)
        row0 += n

    w_up_de = _deinterleave_up(w_up[0])
    b_up_de = jnp.concatenate([b_up[0][:, 0::2], b_up[0][:, 1::2]], axis=-1).reshape(N_EXPERTS, 1, 2 * D_FF)
    w_down_bf = w_down[0].astype(BF16)
    b_down3 = b_down[0].reshape(N_EXPERTS, 1, D)
    y_pad = _expert_blocks(xs, blk_expert, n_used, w_up_de, b_up_de, w_down_bf, b_down3)

    outs = []
    row0 = 0
    for x, x1, mrow in zip(groups, x1s, mod_rows):
        B, T, _ = x.shape
        n = B * T
        t0, t1 = row0 // SCATTER_TILE, (row0 + n) // SCATTER_TILE
        outs.append(_combine(idx[row0:row0 + n], gates[row0:row0 + n], tile_cnt[t0:t1], tile_start3[t0:t1],
                             y_pad, x1, mod3, mrow, ln2g, ln2b, B, T))
        row0 += n
    return tuple(outs)
```

```python
import functools

import numpy as np
import jax
import jax.numpy as jnp
from jax import lax
from jax.experimental import pallas as pl
from jax.experimental.pallas import tpu as pltpu

F32 = jnp.float32
BF16 = jnp.bfloat16
I32 = jnp.int32

D_MODEL = 1024
HEAD_DIM = 64
N_HEADS = 8
WIDTH = N_HEADS * HEAD_DIM
N_PAIRS = WIDTH // 128
ROPE_THETA = 10000.0
RADIUS = 64
GRID_W = 64
NA_ROWS = 8
NA_COLS = 16
N_EXPERTS = 32
TOP_K = 4
D_FF = 1024
SWIGLU_ALPHA = 1.702
SWIGLU_LIMIT = 7.0
DEEPNORM_ALPHA = 2.0 ** 0.25
LN_EPS = 1e-5
RMS_EPS = 1e-6
NEG = -1e30
LOG2E = 1.4426950408889634

LANES = 128
RES = 16
CHUNK = 128
RES_PER_STEP = 4
MOE_ROWS = 512
ROUTE_TILE = 1024
SCATTER_TILE = 256
GATHER_TILE = 128
VMEM_LIMIT = 56 * 1024 * 1024


def _cparams(sem, vmem=VMEM_LIMIT):
    return pltpu.CompilerParams(dimension_semantics=sem, vmem_limit_bytes=vmem)


def _mod_kernel(c_ref, w_ref, b_ref, o_ref):
    c = c_ref[...]
    s = c * (1.0 / (1.0 + jnp.exp(-c)))
    o_ref[...] = jnp.dot(s.astype(BF16), w_ref[...].astype(BF16), preferred_element_type=F32) + b_ref[...]


def _modulation(c, w_mod, b_mod):
    rows = c.shape[0]
    n_out = w_mod.shape[1]
    tn = 1024
    return pl.pallas_call(
        _mod_kernel,
        grid=(n_out // tn,),
        in_specs=[pl.BlockSpec((rows, D_MODEL), lambda j: (0, 0)),
                  pl.BlockSpec((D_MODEL, tn), lambda j: (0, j)),
                  pl.BlockSpec((1, tn), lambda j: (0, j))],
        out_specs=pl.BlockSpec((rows, tn), lambda j: (0, j)),
        out_shape=jax.ShapeDtypeStruct((rows, n_out), F32),
        compiler_params=_cparams(("arbitrary",)),
        name="modulation",
    )(c, w_mod, b_mod.reshape(1, n_out))


def _residue_rows(ref, row0, r):
    return ref.at[pl.ds(row0, CHUNK), r]


def _inproj_kernel(x_hbm, mod_ref, w_ref, cos_ref, sin_ref,
                   qa_ref, ka_ref, va_ref, qb_ref, kb_ref, vb_ref, xbuf, u_scr, sem):
    D = D_MODEL
    row0 = pl.program_id(0) * CHUNK
    shift = mod_ref[:, 0:D]
    scale = mod_ref[:, D:2 * D]
    lane = lax.broadcasted_iota(I32, (1, WIDTH), 1)
    first_half = (lane % HEAD_DIM) < (HEAD_DIM // 2)
    qscale = HEAD_DIM ** -0.5

    def x_copy(r):
        return pltpu.make_async_copy(_residue_rows(x_hbm, row0, r), xbuf.at[r % 2], sem.at[r % 2])

    x_copy(0).start()
    for q in range(RES // RES_PER_STEP):
        for a in range(RES_PER_STEP):
            r = q * RES_PER_STEP + a
            x_copy(r).wait()
            if r + 1 < RES:
                x_copy(r + 1).start()
            u_scr[a * CHUNK:(a + 1) * CHUNK, :] = (xbuf[r % 2] * (1.0 + scale) + shift).astype(BF16)
        u = u_scr[...]
        lo = q * RES_PER_STEP * LANES
        cosf = jnp.concatenate([jnp.tile(cos_ref[:, lo + a * LANES:lo + (a + 1) * LANES], (1, N_PAIRS))
                                for a in range(RES_PER_STEP)], axis=0)
        sinf = jnp.concatenate([jnp.tile(sin_ref[:, lo + a * LANES:lo + (a + 1) * LANES], (1, N_PAIRS))
                                for a in range(RES_PER_STEP)], axis=0)

        def proj(i):
            return jnp.dot(u, w_ref[:, i * WIDTH:(i + 1) * WIDTH], preferred_element_type=F32)

        def rope(p):
            rot = jnp.where(first_half, pltpu.roll(p, WIDTH - HEAD_DIM // 2, 1), pltpu.roll(p, HEAD_DIM // 2, 1))
            return p * cosf + rot * sinf

        def emit_res_major(ref, p):
            for a in range(RES_PER_STEP):
                for hp in range(N_PAIRS):
                    ref[q * RES_PER_STEP + a, hp] = (
                        p[a * CHUNK:(a + 1) * CHUNK, hp * LANES:(hp + 1) * LANES].astype(BF16))

        def emit_natural(ref, p):
            for a in range(RES_PER_STEP):
                for hp in range(N_PAIRS):
                    ref[hp, :, lo + a * LANES:lo + (a + 1) * LANES] = (
                        p[a * CHUNK:(a + 1) * CHUNK, hp * LANES:(hp + 1) * LANES].astype(BF16))

        emit_res_major(qa_ref, rope(proj(0)) * qscale)
        emit_res_major(ka_ref, rope(proj(1)))
        emit_res_major(va_ref, proj(2))
        emit_natural(qb_ref, proj(3) * qscale)
        emit_natural(kb_ref, proj(4))
        emit_natural(vb_ref, proj(5))


def _input_projection(x, mod3, w_in_bf, cos_v, sin_v, mod_row0):
    B, T, D = x.shape
    L = T // RES
    BL = B * L
    cps = L // CHUNK
    x3 = x.reshape(BL, RES, D)
    res_shape = jax.ShapeDtypeStruct((RES, N_PAIRS, BL, LANES), BF16)
    nat_shape = jax.ShapeDtypeStruct((N_PAIRS, BL, RES * LANES), BF16)
    res_spec = pl.BlockSpec((RES, N_PAIRS, CHUNK, LANES), lambda i: (0, 0, i, 0))
    nat_spec = pl.BlockSpec((N_PAIRS, CHUNK, RES * LANES), lambda i: (0, i, 0))
    outs = pl.pallas_call(
        _inproj_kernel,
        grid=(BL // CHUNK,),
        in_specs=[pl.BlockSpec(memory_space=pl.ANY),
                  pl.BlockSpec((None, 1, 6 * D), lambda i: (mod_row0 + i // cps, 0, 0)),
                  pl.BlockSpec((D, 6 * WIDTH), lambda i: (0, 0)),
                  pl.BlockSpec((CHUNK, RES * LANES), lambda i: (i % cps, 0)),
                  pl.BlockSpec((CHUNK, RES * LANES), lambda i: (i % cps, 0))],
        out_specs=[res_spec, res_spec, res_spec, nat_spec, nat_spec, nat_spec],
        out_shape=[res_shape, res_shape, res_shape, nat_shape, nat_shape, nat_shape],
        scratch_shapes=[pltpu.VMEM((2, CHUNK, D), F32), pltpu.VMEM((RES_PER_STEP * CHUNK, D), BF16),
                        pltpu.SemaphoreType.DMA((2,))],
        compiler_params=_cparams(("arbitrary",)),
        name="input_projection",
    )(x3, mod3, w_in_bf, cos_v, sin_v)
    qa, ka, va, qb, kb, vb = outs
    N = B * T
    return qa, ka, va, qb.reshape(N_PAIRS, N, LANES), kb.reshape(N_PAIRS, N, LANES), vb.reshape(N_PAIRS, N, LANES)


PERM = 256
PROJ_TOKENS = 1024
PROJ_ROWS = PROJ_TOKENS // RES


def _residue_permutation():
    p = np.zeros((PERM, PERM), np.float32)
    m, r = np.meshgrid(np.arange(PERM // RES), np.arange(RES), indexing="ij")
    p[(r * (PERM // RES) + m).reshape(-1), (m * RES + r).reshape(-1)] = 1.0
    return p


def _inproj_nat_kernel(x_ref, mod_ref, w_ref, cos_ref, sin_ref, perm_ref,
                       qa_ref, ka_ref, va_ref, qb_ref, kb_ref, vb_ref, u_nat, u_view):
    D = D_MODEL
    shift = mod_ref[:, 0:D]
    scale = mod_ref[:, D:2 * D]
    qscale = HEAD_DIM ** -0.5 * LOG2E
    u_nat[...] = (x_ref[...] * (1.0 + scale) + shift).astype(BF16)

    sub = 512
    for s in range(PROJ_TOKENS // sub):
        u = u_nat[s * sub:(s + 1) * sub, :]
        for i, (ref, mul) in enumerate(((qb_ref, qscale), (kb_ref, 1.0), (vb_ref, 1.0))):
            p = jnp.dot(u, w_ref[:, (3 + i) * WIDTH:(4 + i) * WIDTH], preferred_element_type=F32) * mul
            for hp in range(N_PAIRS):
                ref[hp, s * sub:(s + 1) * sub, :] = p[:, hp * LANES:(hp + 1) * LANES].astype(BF16)

    rows_g = PERM // RES
    for g in range(PROJ_TOKENS // PERM):
        pv = jnp.dot(perm_ref[...], u_nat[g * PERM:(g + 1) * PERM, :], preferred_element_type=F32).astype(BF16)
        for r in range(RES):
            u_view[r, g * rows_g:(g + 1) * rows_g, :] = pv[r * rows_g:(r + 1) * rows_g]

    lane = lax.broadcasted_iota(I32, (1, WIDTH), 1)
    first_half = (lane % HEAD_DIM) < (HEAD_DIM // 2)
    for q in range(RES // RES_PER_STEP):
        u = jnp.concatenate([u_view[q * RES_PER_STEP + a] for a in range(RES_PER_STEP)], axis=0)
        lo = q * RES_PER_STEP * LANES
        cosf = jnp.concatenate([jnp.tile(cos_ref[:, lo + a * LANES:lo + (a + 1) * LANES], (1, N_PAIRS))
                                for a in range(RES_PER_STEP)], axis=0)
        sinf = jnp.concatenate([jnp.tile(sin_ref[:, lo + a * LANES:lo + (a + 1) * LANES], (1, N_PAIRS))
                                for a in range(RES_PER_STEP)], axis=0)

        def rope(p):
            rot = jnp.where(first_half, pltpu.roll(p, WIDTH - HEAD_DIM // 2, 1), pltpu.roll(p, HEAD_DIM // 2, 1))
            return p * cosf + rot * sinf

        for i, ref in enumerate((qa_ref, ka_ref, va_ref)):
            p = jnp.dot(u, w_ref[:, i * WIDTH:(i + 1) * WIDTH], preferred_element_type=F32)
            if i == 0:
                p = rope(p) * qscale
            elif i == 1:
                p = rope(p)
            for a in range(RES_PER_STEP):
                for hp in range(N_PAIRS):
                    ref[q * RES_PER_STEP + a, hp] = (
                        p[a * PROJ_ROWS:(a + 1) * PROJ_ROWS, hp * LANES:(hp + 1) * LANES].astype(BF16))


def _input_projection_nat(x, mod3, w_in_bf, cos_v, sin_v, perm, mod_row0):
    B, T, D = x.shape
    N = B * T
    BL = N // RES
    tps = T // PROJ_TOKENS
    res_shape = jax.ShapeDtypeStruct((RES, N_PAIRS, BL, LANES), BF16)
    nat_shape = jax.ShapeDtypeStruct((N_PAIRS, N, LANES), BF16)
    res_spec = pl.BlockSpec((RES, N_PAIRS, PROJ_ROWS, LANES), lambda i: (0, 0, i, 0))
    nat_spec = pl.BlockSpec((N_PAIRS, PROJ_TOKENS, LANES), lambda i: (0, i, 0))
    return pl.pallas_call(
        _inproj_nat_kernel,
        grid=(N // PROJ_TOKENS,),
        in_specs=[pl.BlockSpec((PROJ_TOKENS, D), lambda i: (i, 0)),
                  pl.BlockSpec((None, 1, 6 * D), lambda i: (mod_row0 + i // tps, 0, 0)),
                  pl.BlockSpec((D, 6 * WIDTH), lambda i: (0, 0)),
                  pl.BlockSpec((PROJ_ROWS, RES * LANES), lambda i: (i % tps, 0)),
                  pl.BlockSpec((PROJ_ROWS, RES * LANES), lambda i: (i % tps, 0)),
                  pl.BlockSpec((PERM, PERM), lambda i: (0, 0))],
        out_specs=[res_spec, res_spec, res_spec, nat_spec, nat_spec, nat_spec],
        out_shape=[res_shape, res_shape, res_shape, nat_shape, nat_shape, nat_shape],
        scratch_shapes=[pltpu.VMEM((PROJ_TOKENS, D), BF16), pltpu.VMEM((RES, PROJ_ROWS, D), BF16)],
        compiler_params=_cparams(("arbitrary",)),
        name="input_projection",
    )(x.reshape(N, D), mod3, w_in_bf, cos_v, sin_v, perm)


def _rope_tables(T):
    half = HEAD_DIM // 2
    inv_freq = 1.0 / (ROPE_THETA ** (jnp.arange(half, dtype=F32) / half))
    ang = jnp.arange(T, dtype=F32)[:, None] * inv_freq[None, :]
    cos, sin = jnp.cos(ang), jnp.sin(ang)
    cos_h = jnp.concatenate([cos, cos], axis=-1)
    sin_h = jnp.concatenate([-sin, sin], axis=-1)
    cos2 = jnp.concatenate([cos_h, cos_h], axis=-1)
    sin2 = jnp.concatenate([sin_h, sin_h], axis=-1)
    L = T // RES
    return cos2.reshape(L, RES * LANES), sin2.reshape(L, RES * LANES)


P2_ROWS = 32
P2_KROWS = 64
P1_ROWS = 16
P1_KROWS = 32
P1_SHIFT = 8
HALO = 64
P3_BATCH = 4
P1_BATCH = 2


def _band_tables(has_halo):
    def mask(ok):
        return np.where(ok, 0.0, NEG).astype(np.float32)
    mq = np.arange(CHUNK)[:, None]
    koff3 = np.arange(2 * CHUNK) - HALO
    b3 = mask(np.abs(mq - koff3[None, :]) <= RADIUS)
    j = np.repeat(np.arange(4), P2_ROWS)[:, None]
    a = np.tile(np.arange(P2_ROWS), 4)[:, None]
    jk = np.repeat(np.arange(4), P2_KROWS)[None, :]
    bk = np.tile(np.arange(P2_KROWS), 4)[None, :]
    b2 = mask(np.abs(4 * (a - (bk - 16)) + (j - jk)) <= RADIUS)
    koff2 = (bk - 16).reshape(-1)
    r = np.repeat(np.arange(RES), P1_ROWS)[:, None]
    a = np.tile(np.arange(P1_ROWS), RES)[:, None]
    rk = np.repeat(np.arange(RES), P1_KROWS)[None, :]
    bk = np.tile(np.arange(P1_KROWS), RES)[None, :]
    b1 = mask(np.abs(RES * (a - (bk - P1_SHIFT)) + (r - rk)) <= RADIUS)
    koff1 = (bk - P1_SHIFT).reshape(-1)
    return (jnp.asarray(b3), jnp.asarray(koff3.astype(np.int32)[None, :]),
            jnp.asarray(b2), jnp.asarray(koff2.astype(np.int32)[None, :]),
            jnp.asarray(b1), jnp.asarray(koff1.astype(np.int32)[None, :]))


def _attend_pairs(problems, head0):
    scores = []
    for q, k, _, _ in problems:
        for h in range(2):
            sel = head0 if h == 0 else jnp.logical_not(head0)
            qh = jnp.where(sel, q, jnp.zeros_like(q))
            scores.append(lax.dot_general(qh, k, (((1,), (1,)), ((), ())), preferred_element_type=F32))
    probs, stats = [], []
    for i, s in enumerate(scores):
        s = s + problems[i // 2][3]
        m = jnp.max(s, axis=-1, keepdims=True)
        p = jnp.exp2(s - m)
        stats.append((m, jnp.sum(p, axis=-1, keepdims=True)))
        probs.append(p.astype(BF16))
    outs = [jnp.dot(p, problems[i // 2][2], preferred_element_type=F32) for i, p in enumerate(probs)]
    results = []
    for i in range(len(problems)):
        (m0, l0), (m1, l1) = stats[2 * i], stats[2 * i + 1]
        results.append((jnp.where(head0, outs[2 * i], outs[2 * i + 1]),
                        jnp.where(head0, m0, m1), jnp.where(head0, l0, l1)))
    return results


def _dilated_kernel(*refs, has_halo, seq_rows, chunks_per_seq):
    if has_halo:
        (q_ref, k_ref, v_ref, kp_ref, kn_ref, vp_ref, vn_ref,
         b3_ref, o3_ref, b2_ref, o2_ref, b1_ref, o1_ref,
         out_ref, kf, vf, kf8, vf8, acc, ms, ls) = refs
    else:
        (q_ref, k_ref, v_ref, b3_ref, o3_ref, b2_ref, o2_ref, b1_ref, o1_ref,
         out_ref, kf, vf, kf8, vf8, acc, ms, ls) = refs

    c = pl.program_id(0) % chunks_per_seq
    row0 = c * CHUNK
    lane = lax.broadcasted_iota(I32, (1, LANES), 1)
    head0 = lane < HEAD_DIM

    for r in range(RES):
        kf[r, HALO:HALO + CHUNK, :] = k_ref[r]
        vf[r, HALO:HALO + CHUNK, :] = v_ref[r]
        if has_halo:
            kf[r, 0:HALO, :] = kp_ref[r]
            vf[r, 0:HALO, :] = vp_ref[r]
            kf[r, HALO + CHUNK:, :] = kn_ref[r]
            vf[r, HALO + CHUNK:, :] = vn_ref[r]
        else:
            zeros = jnp.zeros((HALO, LANES), BF16)
            kf[r, 0:HALO, :] = zeros
            vf[r, 0:HALO, :] = zeros
            kf[r, HALO + CHUNK:, :] = zeros
            vf[r, HALO + CHUNK:, :] = zeros
        for src, dst in ((kf, kf8), (vf, vf8)):
            w32 = src[r].astype(F32)
            dst[r] = jnp.concatenate([w32[P1_SHIFT:], w32[:P1_SHIFT]], axis=0).astype(BF16)

    def in_seq(base, off_ref):
        kv_row = base + off_ref[...]
        return jnp.where((kv_row >= 0) & (kv_row < seq_rows), 0.0, NEG).astype(F32)

    def attend(problems):
        return _attend_pairs(problems, head0)

    def merge(a_old, m_old, l_old, o, m, l):
        mn = jnp.maximum(m_old, m)
        wa = jnp.exp2(m_old - mn)
        wb = jnp.exp2(m - mn)
        return a_old * wa + o * wb, mn, l_old * wa + l * wb

    bias3 = b3_ref[...] + in_seq(row0, o3_ref)

    def body3(it, carry):
        rs = [it * P3_BATCH + u for u in range(P3_BATCH)]
        res = attend([(q_ref[r], kf[r], vf[r], bias3) for r in rs])
        for r, (o, m, l) in zip(rs, res):
            acc[r] = o
            ms[r] = m
            ls[r] = l
        return carry

    lax.fori_loop(0, RES // P3_BATCH, body3, 0)

    b2 = b2_ref[...]

    def body2(g, carry):
        qs = pl.multiple_of(g * P2_ROWS, P2_ROWS)
        ks = pl.multiple_of(HALO - 16 + g * P2_ROWS, 16)
        bias = b2 + in_seq(row0 + g * P2_ROWS, o2_ref)

        def gather(ref, r4, start, rows):
            return jnp.concatenate([ref[4 * j + r4, pl.ds(start, rows), :] for j in range(4)], axis=0)

        res = attend([(gather(q_ref, r4, qs, P2_ROWS), gather(kf, r4, ks, P2_KROWS),
                       gather(vf, r4, ks, P2_KROWS), bias) for r4 in range(4)])
        for r4, (o, m, l) in enumerate(res):
            a_new, m_new, l_new = merge(gather(acc, r4, qs, P2_ROWS), gather(ms, r4, qs, P2_ROWS),
                                        gather(ls, r4, qs, P2_ROWS), o, m, l)
            for j in range(4):
                acc[4 * j + r4, pl.ds(qs, P2_ROWS), :] = a_new[j * P2_ROWS:(j + 1) * P2_ROWS]
                ms[4 * j + r4, pl.ds(qs, P2_ROWS), :] = m_new[j * P2_ROWS:(j + 1) * P2_ROWS]
                ls[4 * j + r4, pl.ds(qs, P2_ROWS), :] = l_new[j * P2_ROWS:(j + 1) * P2_ROWS]
        return carry

    lax.fori_loop(0, CHUNK // P2_ROWS, body2, 0)

    b1 = b1_ref[...]

    def body1(it, carry):
        def gather(ref, start, rows):
            return jnp.concatenate([ref[r, pl.ds(start, rows), :] for r in range(RES)], axis=0)

        starts, problems = [], []
        for u in range(P1_BATCH):
            g = it * P1_BATCH + u
            qs = pl.multiple_of(g * P1_ROWS, P1_ROWS)
            ks = pl.multiple_of(HALO - 16 + g * P1_ROWS, 16)
            starts.append(qs)
            problems.append((gather(q_ref, qs, P1_ROWS), gather(kf8, ks, P1_KROWS), gather(vf8, ks, P1_KROWS),
                             b1 + in_seq(row0 + g * P1_ROWS, o1_ref)))
        for qs, (o, m, l) in zip(starts, attend(problems)):
            a_new, m_new, l_new = merge(gather(acc, qs, P1_ROWS), gather(ms, qs, P1_ROWS),
                                        gather(ls, qs, P1_ROWS), o, m, l)
            for r in range(RES):
                acc[r, pl.ds(qs, P1_ROWS), :] = a_new[r * P1_ROWS:(r + 1) * P1_ROWS]
                ms[r, pl.ds(qs, P1_ROWS), :] = m_new[r * P1_ROWS:(r + 1) * P1_ROWS]
                ls[r, pl.ds(qs, P1_ROWS), :] = l_new[r * P1_ROWS:(r + 1) * P1_ROWS]
        return carry

    lax.fori_loop(0, CHUNK // P1_ROWS // P1_BATCH, body1, 0)

    def body_out(r, carry):
        out_ref[r] = (acc[r] / ls[r]).astype(BF16)
        return carry

    lax.fori_loop(0, RES, body_out, 0)


def _dilated_attention(qa, ka, va, B, T):
    L = T // RES
    BL = B * L
    cps = L // CHUNK
    has_halo = cps > 1
    tables = _band_tables(has_halo)
    blk = (RES, None, CHUNK, LANES)
    center = pl.BlockSpec(blk, lambda i, hp: (0, hp, i, 0))
    in_specs = [center, center, center]
    args = [qa, ka, va]
    if has_halo:
        hblk = (RES, None, HALO, LANES)
        per = CHUNK // HALO

        def prev_map(i, hp):
            return (0, hp, jnp.maximum(per * i - 1, (i // cps) * cps * per), 0)

        def next_map(i, hp):
            return (0, hp, jnp.minimum(per * i + per, (i // cps + 1) * cps * per - 1), 0)

        in_specs += [pl.BlockSpec(hblk, prev_map), pl.BlockSpec(hblk, next_map),
                     pl.BlockSpec(hblk, prev_map), pl.BlockSpec(hblk, next_map)]
        args += [ka, ka, va, va]
    for t in tables:
        in_specs.append(pl.BlockSpec(t.shape, lambda i, hp: (0, 0)))
        args.append(t)
    kern = functools.partial(_dilated_kernel, has_halo=has_halo, seq_rows=L, chunks_per_seq=cps)
    return pl.pallas_call(
        kern,
        grid=(BL // CHUNK, N_PAIRS),
        in_specs=in_specs,
        out_specs=pl.BlockSpec(blk, lambda i, hp: (0, hp, i, 0)),
        out_shape=jax.ShapeDtypeStruct((RES, N_PAIRS, BL, LANES), BF16),
        scratch_shapes=[pltpu.VMEM((RES, 2 * CHUNK, LANES), BF16),
                        pltpu.VMEM((RES, 2 * CHUNK, LANES), BF16),
                        pltpu.VMEM((RES, 2 * CHUNK, LANES), BF16),
                        pltpu.VMEM((RES, 2 * CHUNK, LANES), BF16),
                        pltpu.VMEM((RES, CHUNK, LANES), F32),
                        pltpu.VMEM((RES, CHUNK, LANES), F32),
                        pltpu.VMEM((RES, CHUNK, LANES), F32)],
        compiler_params=_cparams(("arbitrary", "arbitrary")),
        name="dilated_attention",
    )(*args)


NA_KEYS = NA_ROWS * GRID_W
NA_BLOCK_ROWS = 32
NA_BATCH = 8


def _na_bias_table(rpb):
    c = np.arange(GRID_W)
    col_start = np.clip(c - NA_COLS // 2, 0, GRID_W - NA_COLS)
    col_mask = (c[None, :] >= col_start[:, None]) & (c[None, :] < col_start[:, None] + NA_COLS)
    dc_idx = np.clip(c[None, :] - c[:, None], -(NA_COLS - 1), NA_COLS - 1) + NA_COLS - 1
    rel = rpb.astype(F32)[:, :, dc_idx] * LOG2E
    rel = jnp.where(col_mask[None, None], rel, NEG)
    per_off = [rel[:, d0:d0 + NA_ROWS].transpose(0, 2, 1, 3).reshape(N_HEADS, GRID_W, NA_KEYS)
               for d0 in range(NA_ROWS)]
    return jnp.stack(per_off, axis=0).reshape(NA_ROWS, N_PAIRS, 2, GRID_W, NA_KEYS)


def _na_kernel(q_ref, k_ref, v_ref, bias_ref, out_ref, *, grid_rows, block_rows):
    gb = pl.program_id(2)
    lane = lax.broadcasted_iota(I32, (1, LANES), 1)
    head0 = lane < HEAD_DIM

    def body(it, carry):
        rows, scores = [], []
        for u in range(NA_BATCH):
            i = it * NA_BATCH + u
            g = gb * block_rows + i
            rs = jnp.clip(g - NA_ROWS // 2, 0, grid_rows - NA_ROWS)
            d0 = rs - g + NA_ROWS - 1
            qs = pl.multiple_of(i * GRID_W, GRID_W)
            ks = pl.multiple_of(rs * GRID_W, GRID_W)
            q = q_ref[pl.ds(qs, GRID_W), :]
            k = k_ref[pl.ds(ks, NA_KEYS), :]
            rows.append((qs, ks, d0))
            for h in range(2):
                sel = head0 if h == 0 else jnp.logical_not(head0)
                qh = jnp.where(sel, q, jnp.zeros_like(q))
                scores.append(lax.dot_general(qh, k, (((1,), (1,)), ((), ())), preferred_element_type=F32))
        probs, sums = [], []
        for n, s in enumerate(scores):
            s = s + bias_ref[rows[n // 2][2], n % 2]
            p = jnp.exp2(s - jnp.max(s, axis=-1, keepdims=True))
            sums.append(jnp.sum(p, axis=-1, keepdims=True))
            probs.append(p.astype(BF16))
        outs = []
        for n, p in enumerate(probs):
            v = v_ref[pl.ds(rows[n // 2][1], NA_KEYS), :]
            outs.append(jnp.dot(p, v, preferred_element_type=F32) / sums[n])
        for u, (qs, _, _) in enumerate(rows):
            out_ref[pl.ds(qs, GRID_W), :] = jnp.where(head0, outs[2 * u], outs[2 * u + 1]).astype(BF16)
        return carry

    lax.fori_loop(0, block_rows // NA_BATCH, body, 0)


def _neighbourhood_attention(qb, kb, vb, bias, B, T):
    G = T // GRID_W
    rb = min(NA_BLOCK_ROWS, G)
    nb = G // rb
    kern = functools.partial(_na_kernel, grid_rows=G, block_rows=rb)
    seq = pl.BlockSpec((None, T, LANES), lambda hp, b, gb: (hp, b, 0))
    qblk = pl.BlockSpec((None, rb * GRID_W, LANES), lambda hp, b, gb: (hp, b * nb + gb, 0))
    return pl.pallas_call(
        kern,
        grid=(N_PAIRS, B, nb),
        in_specs=[qblk, seq, seq,
                  pl.BlockSpec((NA_ROWS, None, 2, GRID_W, NA_KEYS), lambda hp, b, gb: (0, hp, 0, 0, 0))],
        out_specs=qblk,
        out_shape=jax.ShapeDtypeStruct((N_PAIRS, B * T, LANES), BF16),
        compiler_params=_cparams(("arbitrary", "arbitrary", "arbitrary")),
        name="neighbourhood_attention",
    )(qb, kb, vb, bias)


def _layer_norm(h, g, b):
    mu = jnp.mean(h, axis=-1, keepdims=True)
    d = h - mu
    var = jnp.mean(d * d, axis=-1, keepdims=True)
    return d * lax.rsqrt(var + LN_EPS) * g + b


def _outproj_nat_kernel(oa_ref, ob_ref, x_ref, mod_ref, ga_ref, gb_ref, w_ref, lg_ref, lb_ref, wr_ref, br_ref,
                        permt_ref, x1_ref, u2_ref, logit_ref, mix_scr):
    D = D_MODEL
    gate_a = mod_ref[:, 2 * D:3 * D]
    shift_f = mod_ref[:, 3 * D:4 * D]
    scale_f = mod_ref[:, 4 * D:5 * D]

    def rms(o, g):
        return o * lax.rsqrt(jnp.mean(o * o, axis=-1, keepdims=True) + RMS_EPS) * g

    rows_g = PERM // RES
    for g in range(PROJ_TOKENS // PERM):
        grouped = jnp.concatenate(
            [jnp.concatenate([oa_ref[r, hp, g * rows_g:(g + 1) * rows_g, :] for hp in range(N_PAIRS)], axis=1)
             for r in range(RES)], axis=0)
        oa = jnp.dot(permt_ref[...], grouped, preferred_element_type=F32)
        ob = jnp.concatenate([ob_ref[hp, g * PERM:(g + 1) * PERM, :] for hp in range(N_PAIRS)], axis=1).astype(F32)
        mix = jnp.concatenate([rms(oa, ga_ref[...]), rms(ob, gb_ref[...])], axis=1)
        mix_scr[g * PERM:(g + 1) * PERM, :] = mix.astype(BF16)

    wr = wr_ref[...]
    wr_hi = wr.astype(BF16)
    wr_lo = (wr - wr_hi.astype(F32)).astype(BF16)
    sub = 256
    for s in range(PROJ_TOKENS // sub):
        rows = slice(s * sub, (s + 1) * sub)
        y = jnp.dot(mix_scr[rows, :], w_ref[...], preferred_element_type=F32)
        h = DEEPNORM_ALPHA * x_ref[rows, :] + (1.0 + gate_a) * y
        x1 = _layer_norm(h, lg_ref[...], lb_ref[...])
        x1_ref[rows, :] = x1
        u2 = x1 * (1.0 + scale_f) + shift_f
        u2_ref[rows, :] = u2.astype(BF16)
        u_hi = u2.astype(BF16)
        u_lo = (u2 - u_hi.astype(F32)).astype(BF16)
        logit_ref[rows, :] = (jnp.dot(u_hi, wr_hi, preferred_element_type=F32)
                              + jnp.dot(u_lo, wr_hi, preferred_element_type=F32)
                              + jnp.dot(u_hi, wr_lo, preferred_element_type=F32)) + br_ref[...]


def _output_projection_nat(oa, ob, x, mod3, mod_row0, g_a, g_b, w_out_bf, ln_g, ln_b, w_router_pad, b_router_pad,
                           perm_t):
    B, T, D = x.shape
    N = B * T
    tps = T // PROJ_TOKENS
    const = lambda shape: pl.BlockSpec(shape, lambda i: tuple(0 for _ in shape))
    rows = lambda width: pl.BlockSpec((PROJ_TOKENS, width), lambda i: (i, 0))
    return pl.pallas_call(
        _outproj_nat_kernel,
        grid=(N // PROJ_TOKENS,),
        in_specs=[pl.BlockSpec((RES, N_PAIRS, PROJ_ROWS, LANES), lambda i: (0, 0, i, 0)),
                  pl.BlockSpec((N_PAIRS, PROJ_TOKENS, LANES), lambda i: (0, i, 0)),
                  rows(D),
                  pl.BlockSpec((None, 1, 6 * D), lambda i: (mod_row0 + i // tps, 0, 0)),
                  const((1, WIDTH)), const((1, WIDTH)), const((2 * WIDTH, D)),
                  const((1, D)), const((1, D)), const((D, LANES)), const((1, LANES)), const((PERM, PERM))],
        out_specs=[rows(D), rows(D), rows(LANES)],
        out_shape=[jax.ShapeDtypeStruct((N, D), F32), jax.ShapeDtypeStruct((N, D), BF16),
                   jax.ShapeDtypeStruct((N, LANES), F32)],
        scratch_shapes=[pltpu.VMEM((PROJ_TOKENS, 2 * WIDTH), BF16)],
        compiler_params=_cparams(("arbitrary",)),
        name="output_projection",
    )(oa, ob, x.reshape(N, D), mod3, g_a, g_b, w_out_bf, ln_g, ln_b, w_router_pad, b_router_pad, perm_t)


def _outproj_kernel(oa_ref, ob_ref, x_hbm, mod_ref, ga_ref, gb_ref, w_ref, lg_ref, lb_ref, wr_ref, br_ref,
                    x1_hbm, u2_hbm, logit_hbm, xbuf, x1buf, u2buf, lgbuf, mix_scr, in_sem, out_sem):
    D = D_MODEL
    row0 = pl.program_id(0) * CHUNK
    gate_a = mod_ref[:, 2 * D:3 * D]
    shift_f = mod_ref[:, 3 * D:4 * D]
    scale_f = mod_ref[:, 4 * D:5 * D]

    def rms(o, g):
        return o * lax.rsqrt(jnp.mean(o * o, axis=-1, keepdims=True) + RMS_EPS) * g

    def x_copy(r):
        return pltpu.make_async_copy(_residue_rows(x_hbm, row0, r), xbuf.at[r % 2], in_sem.at[r % 2])

    def out_copies(r):
        s = r % 2
        return (pltpu.make_async_copy(x1buf.at[s], _residue_rows(x1_hbm, row0, r), out_sem.at[s]),
                pltpu.make_async_copy(u2buf.at[s], _residue_rows(u2_hbm, row0, r), out_sem.at[s]),
                pltpu.make_async_copy(lgbuf.at[s], _residue_rows(logit_hbm, row0, r), out_sem.at[s]))

    wr = wr_ref[...]
    wr_hi = wr.astype(BF16)
    wr_lo = (wr - wr_hi.astype(F32)).astype(BF16)

    x_copy(0).start()
    for q in range(RES // RES_PER_STEP):
        lo = q * RES_PER_STEP * LANES
        for a in range(RES_PER_STEP):
            r = q * RES_PER_STEP + a
            oa = jnp.concatenate([oa_ref[r, hp] for hp in range(N_PAIRS)], axis=1).astype(F32)
            ob = jnp.concatenate([ob_ref[hp, :, lo + a * LANES:lo + (a + 1) * LANES] for hp in range(N_PAIRS)],
                                 axis=1).astype(F32)
            mix = jnp.concatenate([rms(oa, ga_ref[...]), rms(ob, gb_ref[...])], axis=1)
            mix_scr[a * CHUNK:(a + 1) * CHUNK, :] = mix.astype(BF16)
        y = jnp.dot(mix_scr[...], w_ref[...], preferred_element_type=F32)

        for a in range(RES_PER_STEP):
            r = q * RES_PER_STEP + a
            s = r % 2
            x_copy(r).wait()
            if r + 1 < RES:
                x_copy(r + 1).start()
            if r >= 2:
                for c in out_copies(r - 2):
                    c.wait()
            h = DEEPNORM_ALPHA * xbuf[s] + (1.0 + gate_a) * y[a * CHUNK:(a + 1) * CHUNK]
            x1 = _layer_norm(h, lg_ref[...], lb_ref[...])
            u2 = x1 * (1.0 + scale_f) + shift_f
            u_hi = u2.astype(BF16)
            u_lo = (u2 - u_hi.astype(F32)).astype(BF16)
            logits = (jnp.dot(u_hi, wr_hi, preferred_element_type=F32)
                      + jnp.dot(u_lo, wr_hi, preferred_element_type=F32)
                      + jnp.dot(u_hi, wr_lo, preferred_element_type=F32)) + br_ref[...]
            x1buf[s] = x1
            u2buf[s] = u2
            lgbuf[s] = logits
            for c in out_copies(r):
                c.start()
    for r in (RES - 2, RES - 1):
        for c in out_copies(r):
            c.wait()


def _output_projection(oa, ob, x, mod3, mod_row0, g_a, g_b, w_out_bf, ln_g, ln_b, w_router_pad, b_router_pad):
    B, T, D = x.shape
    L = T // RES
    BL = B * L
    cps = L // CHUNK
    x3 = x.reshape(BL, RES, D)
    obv = ob.reshape(N_PAIRS, BL, RES * LANES)
    const = lambda shape: pl.BlockSpec(shape, lambda i: tuple(0 for _ in shape))
    hbm = pl.BlockSpec(memory_space=pl.ANY)
    x1, u2, logits = pl.pallas_call(
        _outproj_kernel,
        grid=(BL // CHUNK,),
        in_specs=[pl.BlockSpec((RES, N_PAIRS, CHUNK, LANES), lambda i: (0, 0, i, 0)),
                  pl.BlockSpec((N_PAIRS, CHUNK, RES * LANES), lambda i: (0, i, 0)),
                  hbm,
                  pl.BlockSpec((None, 1, 6 * D), lambda i: (mod_row0 + i // cps, 0, 0)),
                  const((1, WIDTH)), const((1, WIDTH)), const((2 * WIDTH, D)),
                  const((1, D)), const((1, D)), const((D, LANES)), const((1, LANES))],
        out_specs=[hbm, hbm, hbm],
        out_shape=[jax.ShapeDtypeStruct((BL, RES, D), F32), jax.ShapeDtypeStruct((BL, RES, D), F32),
                   jax.ShapeDtypeStruct((BL, RES, LANES), F32)],
        scratch_shapes=[pltpu.VMEM((2, CHUNK, D), F32), pltpu.VMEM((2, CHUNK, D), F32),
                        pltpu.VMEM((2, CHUNK, D), F32), pltpu.VMEM((2, CHUNK, LANES), F32),
                        pltpu.VMEM((RES_PER_STEP * CHUNK, 2 * WIDTH), BF16),
                        pltpu.SemaphoreType.DMA((2,)), pltpu.SemaphoreType.DMA((2,))],
        compiler_params=_cparams(("arbitrary",)),
        name="output_projection",
    )(oa, obv, x3, mod3, g_a, g_b, w_out_bf, ln_g, ln_b, w_router_pad, b_router_pad)
    N = B * T
    return x1.reshape(N, D), u2.reshape(N, D), logits.reshape(N, LANES)


def _route_kernel(logit_ref, idx_ref, gate_ref, cnt_ref):
    for t in range(ROUTE_TILE // SCATTER_TILE):
        rows = slice(t * SCATTER_TILE, (t + 1) * SCATTER_TILE)
        sel, gates, cnt = _route_tile(logit_ref[rows, :])
        idx_ref[rows, :] = sel
        gate_ref[rows, :] = gates
        cnt_ref[t] = cnt


def _route_tile(raw):
    tn = raw.shape[0]
    lane = lax.broadcasted_iota(I32, (tn, LANES), 1)
    logits = jnp.where(lane < N_EXPERTS, raw, -3.0e38)
    vals, idxs = [], []
    multi = jnp.zeros((tn, LANES), F32)
    for _ in range(TOP_K):
        m = jnp.max(logits, axis=-1, keepdims=True)
        idx = jnp.min(jnp.where(logits == m, lane, LANES), axis=-1, keepdims=True)
        hot = lane == idx
        vals.append(m)
        idxs.append(idx)
        multi = multi + hot.astype(F32)
        logits = jnp.where(hot, -3.0e38, logits)
    es = [jnp.exp(v - vals[0]) for v in vals]
    tot = es[0] + es[1] + es[2] + es[3]
    lane4 = lax.broadcasted_iota(I32, (tn, TOP_K), 1)
    sel = jnp.zeros((tn, TOP_K), I32)
    gates = jnp.zeros((tn, TOP_K), F32)
    for k in range(TOP_K):
        sel = jnp.where(lane4 == k, idxs[k], sel)
        gates = jnp.where(lane4 == k, es[k] / tot, gates)
    return sel, gates, jnp.sum(multi, axis=0, keepdims=True).astype(I32)


def _routing(logits):
    N = logits.shape[0]
    tn = ROUTE_TILE
    nt = N // SCATTER_TILE
    per_step = ROUTE_TILE // SCATTER_TILE
    return pl.pallas_call(
        _route_kernel,
        grid=(N // tn,),
        in_specs=[pl.BlockSpec((tn, LANES), lambda i: (i, 0))],
        out_specs=[pl.BlockSpec((tn, TOP_K), lambda i: (i, 0)),
                   pl.BlockSpec((tn, TOP_K), lambda i: (i, 0)),
                   pl.BlockSpec((per_step, 1, LANES), lambda i: (i, 0, 0))],
        out_shape=[jax.ShapeDtypeStruct((N, TOP_K), I32), jax.ShapeDtypeStruct((N, TOP_K), F32),
                   jax.ShapeDtypeStruct((nt, 1, LANES), I32)],
        compiler_params=_cparams(("arbitrary",)),
        name="moe_routing",
    )(logits)


SEG = 8
STAGE_ROWS = SCATTER_TILE * TOP_K + N_EXPERTS * SEG


def _lane_prefix_exclusive(v):
    lane = lax.broadcasted_iota(I32, v.shape, 1)
    incl = v
    s = 1
    while s < LANES:
        incl = incl + jnp.where(lane >= s, pltpu.roll(incl, s, 1), 0.0)
        s *= 2
    return incl - v


def _tile_ranks(idx, tn):
    lane = lax.broadcasted_iota(I32, (tn, LANES), 1)
    hots = [lane == idx[:, k:k + 1] for k in range(TOP_K)]
    multi = jnp.zeros((tn, LANES), F32)
    for h in hots:
        multi = multi + h.astype(F32)
    row = lax.broadcasted_iota(I32, (tn, tn), 0)
    col = lax.broadcasted_iota(I32, (tn, tn), 1)
    lower = (col < row).astype(BF16)
    before = jnp.dot(lower, multi.astype(BF16), preferred_element_type=F32)
    return hots, before


BIG_SEG = 4 * SEG


def _segment_dma_loops(cnt_s, rows_of, copy):
    def per_expert(e, carry):
        off, n_big, n_small = carry
        nseg = (cnt_s[0, e] + SEG - 1) // SEG
        big = nseg // (BIG_SEG // SEG)
        small = nseg - big * (BIG_SEG // SEG)

        def one_big(q, c):
            src, dst = rows_of(e, off, q * BIG_SEG)
            copy(src, dst, BIG_SEG).start()
            return c

        def one_small(q, c):
            src, dst = rows_of(e, off, big * BIG_SEG + q * SEG)
            copy(src, dst, SEG).start()
            return c

        lax.fori_loop(0, big, one_big, 0)
        lax.fori_loop(0, small, one_small, 0)
        return off + nseg * SEG, n_big + big, n_small + small

    _, n_big, n_small = lax.fori_loop(0, N_EXPERTS, per_expert, (0, 0, 0))
    return n_big, n_small


def _drain(copy, n_big, n_small):
    def wait_big(q, c):
        copy(0, 0, BIG_SEG).wait()
        return c

    def wait_small(q, c):
        copy(0, 0, SEG).wait()
        return c

    lax.fori_loop(0, n_big, wait_big, 0)
    lax.fori_loop(0, n_small, wait_small, 0)


def _move_segments(cnt_s, rows_of, copy):
    n_big, n_small = _segment_dma_loops(cnt_s, rows_of, copy)
    _drain(copy, n_big, n_small)


def _count_segment_dmas(cnt_s):
    def per_expert(e, carry):
        n_big, n_small = carry
        nseg = (cnt_s[0, e] + SEG - 1) // SEG
        big = nseg // (BIG_SEG // SEG)
        return n_big + big, n_small + nseg - big * (BIG_SEG // SEG)

    return lax.fori_loop(0, N_EXPERTS, per_expert, (0, 0))


def _dispatch_kernel(*refs, first, n_steps):
    if first:
        cnt_s, start_s, prev_cnt_s, tail_s, idx_ref, u_ref, cntv_ref, xs_ref, stage, zbuf, sem, zsem = refs
    else:
        cnt_s, start_s, prev_cnt_s, tail_s, idx_ref, u_ref, cntv_ref, _, xs_ref, stage, zbuf, sem, zsem = refs
    tn = idx_ref.shape[0]
    step = pl.program_id(0)
    slot = step % 2

    if first:
        @pl.when(pl.program_id(0) == 0)
        def _():
            zbuf[...] = jnp.zeros_like(zbuf)

            def tail_copy(row):
                return pltpu.make_async_copy(zbuf, xs_ref.at[pl.ds(pl.multiple_of(row, SEG), SEG)], zsem)

            def fill(e, n):
                lo = tail_s[0, e]
                nfull = (tail_s[1, e] - lo) // SEG

                def one(q, c):
                    tail_copy(lo + q * SEG).start()
                    return c

                lax.fori_loop(0, nfull, one, 0)
                return n + nfull

            n = lax.fori_loop(0, N_EXPERTS, fill, 0)

            def drain(q, c):
                tail_copy(0).wait()
                return c

            lax.fori_loop(0, n, drain, 0)

    hots, before = _tile_ranks(idx_ref[...], tn)
    cntf = cntv_ref[...].astype(F32)
    seg_len = jnp.ceil(cntf * (1.0 / SEG)) * SEG
    seg_off = _lane_prefix_exclusive(seg_len)
    lane_r = lax.broadcasted_iota(I32, (tn, STAGE_ROWS), 1)
    onehot = jnp.zeros((tn, STAGE_ROWS), F32)
    for k in range(TOP_K):
        stage_row = jnp.sum(jnp.where(hots[k], seg_off + before, 0.0), axis=-1, keepdims=True)
        onehot = jnp.where(lane_r == stage_row.astype(I32), 1.0, onehot)
    stage[slot] = lax.dot_general(onehot.astype(BF16), u_ref[...].astype(BF16), (((0,), (0,)), ((), ())),
                                  preferred_element_type=F32)

    def copier(buf):
        def seg_copy(src_row, dst_row, rows):
            return pltpu.make_async_copy(stage.at[buf, pl.ds(pl.multiple_of(src_row, SEG), rows)],
                                         xs_ref.at[pl.ds(pl.multiple_of(dst_row, SEG), rows)], sem.at[buf])
        return seg_copy

    @pl.when(step > 0)
    def _():
        _drain(copier(1 - slot), *_count_segment_dmas(prev_cnt_s))

    _segment_dma_loops(cnt_s, lambda e, off, moved: (off + moved, start_s[0, e] + moved), copier(slot))

    @pl.when(step == n_steps - 1)
    def _():
        _drain(copier(slot), *_count_segment_dmas(cnt_s))


def _dispatch(idx, u2, tile_cnt, tile_start, tails, xs, xs_rows):
    n, D = u2.shape
    tn = SCATTER_TILE
    nt = n // tn
    first = xs is None
    smem = lambda shape, imap: pl.BlockSpec(shape, imap, memory_space=pltpu.SMEM)
    kern = functools.partial(_dispatch_kernel, first=first, n_steps=nt)
    in_specs = [smem((None, 1, LANES), lambda i: (i, 0, 0)),
                smem((None, 1, LANES), lambda i: (i, 0, 0)),
                smem((None, 1, LANES), lambda i: (jnp.maximum(i - 1, 0), 0, 0)),
                smem((2, LANES), lambda i: (0, 0)),
                pl.BlockSpec((tn, TOP_K), lambda i: (i, 0)),
                pl.BlockSpec((tn, D), lambda i: (i, 0)),
                pl.BlockSpec((None, 1, LANES), lambda i: (i, 0, 0))]
    args = [tile_cnt, tile_start, tile_cnt, tails, idx, u2, tile_cnt]
    alias = {}
    if not first:
        alias = {len(args): 0}
        in_specs.append(pl.BlockSpec(memory_space=pl.ANY))
        args.append(xs)
    return pl.pallas_call(
        kern,
        grid=(nt,),
        in_specs=in_specs,
        out_specs=pl.BlockSpec(memory_space=pl.ANY),
        out_shape=jax.ShapeDtypeStruct((xs_rows, D), F32),
        scratch_shapes=[pltpu.VMEM((2, STAGE_ROWS, D), F32), pltpu.VMEM((SEG, D), F32),
                        pltpu.SemaphoreType.DMA((2,)), pltpu.SemaphoreType.DMA(())],
        input_output_aliases=alias,
        compiler_params=_cparams(("arbitrary",)),
        name="moe_dispatch",
    )(*args)


DEINT = 256


def _deinterleave_kernel(w_ref, p_ref, o_ref):
    n_groups = w_ref.shape[1] // DEINT
    half = DEINT // 2
    f = w_ref.shape[1] // 2
    for g in range(n_groups):
        wg = w_ref[:, g * DEINT:(g + 1) * DEINT].astype(BF16)
        y = jnp.dot(wg, p_ref[...], preferred_element_type=F32)
        o_ref[:, g * half:(g + 1) * half] = y[:, :half].astype(BF16)
        o_ref[:, f + g * half:f + (g + 1) * half] = y[:, half:].astype(BF16)


def _deinterleave_up(w_up):
    E, D, F2 = w_up.shape
    perm = np.zeros((DEINT, DEINT), np.float32)
    j = np.arange(DEINT // 2)
    perm[2 * j, j] = 1.0
    perm[2 * j + 1, DEINT // 2 + j] = 1.0
    return pl.pallas_call(
        _deinterleave_kernel,
        grid=(E,),
        in_specs=[pl.BlockSpec((None, D, F2), lambda e: (e, 0, 0)),
                  pl.BlockSpec((DEINT, DEINT), lambda e: (0, 0))],
        out_specs=pl.BlockSpec((None, D, F2), lambda e: (e, 0, 0)),
        out_shape=jax.ShapeDtypeStruct((E, D, F2), BF16),
        compiler_params=_cparams(("arbitrary",)),
        name="expert_weight_layout",
    )(w_up, jnp.asarray(perm, BF16))


def _expert_kernel(be_ref, nu_ref, x_ref, wu_ref, bu_ref, wd_ref, bd_ref, y_ref):
    del be_ref

    @pl.when(pl.program_id(0) < nu_ref[0])
    def _():
        x = x_ref[...].astype(BF16)
        hu = jnp.dot(x, wu_ref[...], preferred_element_type=F32) + bu_ref[...]
        glu = jnp.minimum(hu[:, :D_FF], SWIGLU_LIMIT)
        lin = jnp.clip(hu[:, D_FF:], -SWIGLU_LIMIT, SWIGLU_LIMIT)
        act = glu * (1.0 / (1.0 + jnp.exp(-SWIGLU_ALPHA * glu))) * (lin + 1.0)
        y_ref[...] = jnp.dot(act.astype(BF16), wd_ref[...], preferred_element_type=F32) + bd_ref[...]

    @pl.when(pl.program_id(0) >= nu_ref[0])
    def _():
        y_ref[...] = jnp.zeros_like(y_ref)


def _expert_blocks(xs, blk_expert, n_used, w_up_de, b_up_de, w_down_bf, b_down):
    n_rows, D = xs.shape
    nblk = n_rows // MOE_ROWS

    def xmap(j, be, nu):
        return (jnp.minimum(j, nu[0] - 1), 0)

    grid_spec = pltpu.PrefetchScalarGridSpec(
        num_scalar_prefetch=2,
        grid=(nblk,),
        in_specs=[pl.BlockSpec((MOE_ROWS, D), xmap),
                  pl.BlockSpec((None, D, 2 * D_FF), lambda j, be, nu: (be[j], 0, 0)),
                  pl.BlockSpec((None, 1, 2 * D_FF), lambda j, be, nu: (be[j], 0, 0)),
                  pl.BlockSpec((None, D_FF, D), lambda j, be, nu: (be[j], 0, 0)),
                  pl.BlockSpec((None, 1, D), lambda j, be, nu: (be[j], 0, 0))],
        out_specs=pl.BlockSpec((MOE_ROWS, D), lambda j, be, nu: (j, 0)),
    )
    return pl.pallas_call(
        _expert_kernel,
        grid_spec=grid_spec,
        out_shape=jax.ShapeDtypeStruct((n_rows, D), F32),
        compiler_params=_cparams(("arbitrary",)),
        name="moe_experts",
    )(blk_expert, n_used, xs, w_up_de, b_up_de, w_down_bf, b_down)


def _combine_kernel(cnt_s, start_s, idx_ref, gate_ref, cntv_ref, y_ref, x1_ref, mod_ref,
                    lg_ref, lb_ref, out_ref, stage, sem):
    D = D_MODEL
    tn = idx_ref.shape[0]

    def seg_copy(src_row, dst_row, rows):
        return pltpu.make_async_copy(y_ref.at[pl.ds(pl.multiple_of(src_row, SEG), rows)],
                                     stage.at[pl.ds(pl.multiple_of(dst_row, SEG), rows)], sem)

    @pl.when(pl.program_id(0) == 0)
    def _():
        stage[...] = jnp.zeros_like(stage)

    n_big, n_small = _segment_dma_loops(
        cnt_s, lambda e, off, moved: (start_s[0, e] + moved, off + moved), seg_copy)

    hots, before = _tile_ranks(idx_ref[...], tn)
    seg_len = jnp.ceil(cntv_ref[...].astype(F32) * (1.0 / SEG)) * SEG
    seg_off = _lane_prefix_exclusive(seg_len)
    gates = gate_ref[...]
    lane_r = lax.broadcasted_iota(I32, (tn, STAGE_ROWS), 1)
    weights = jnp.zeros((tn, STAGE_ROWS), F32)
    for k in range(TOP_K):
        stage_row = jnp.sum(jnp.where(hots[k], seg_off + before, 0.0), axis=-1, keepdims=True)
        weights = jnp.where(lane_r == stage_row.astype(I32), gates[:, k:k + 1], weights)

    _drain(seg_copy, n_big, n_small)

    y = jnp.dot(weights.astype(BF16), stage[...].astype(BF16), preferred_element_type=F32)
    gate_f = mod_ref[:, 5 * D:6 * D]
    h = DEEPNORM_ALPHA * x1_ref[...] + (1.0 + gate_f) * y
    out_ref[...] = _layer_norm(h, lg_ref[...], lb_ref[...])


def _combine(idx, gates, tile_cnt, tile_start, y_pad, x1, mod3, mod_row0, ln_g, ln_b, B, T):
    D = D_MODEL
    tn = SCATTER_TILE
    n_rows = B * T
    nt = n_rows // tn
    tps = T // tn
    smem = lambda imap: pl.BlockSpec((None, 1, LANES), imap, memory_space=pltpu.SMEM)
    out = pl.pallas_call(
        _combine_kernel,
        grid=(nt,),
        in_specs=[smem(lambda i: (i, 0, 0)), smem(lambda i: (i, 0, 0)),
                  pl.BlockSpec((tn, TOP_K), lambda i: (i, 0)),
                  pl.BlockSpec((tn, TOP_K), lambda i: (i, 0)),
                  pl.BlockSpec((None, 1, LANES), lambda i: (i, 0, 0)),
                  pl.BlockSpec(memory_space=pl.ANY),
                  pl.BlockSpec((tn, D), lambda i: (i, 0)),
                  pl.BlockSpec((None, 1, 6 * D), lambda i: (mod_row0 + i // tps, 0, 0)),
                  pl.BlockSpec((1, D), lambda i: (0, 0)),
                  pl.BlockSpec((1, D), lambda i: (0, 0))],
        out_specs=pl.BlockSpec((tn, D), lambda i: (i, 0)),
        out_shape=jax.ShapeDtypeStruct((n_rows, D), F32),
        scratch_shapes=[pltpu.VMEM((STAGE_ROWS, D), F32), pltpu.SemaphoreType.DMA(())],
        compiler_params=_cparams(("arbitrary",)),
        name="moe_combine",
    )(tile_cnt, tile_start, idx, gates, tile_cnt, y_pad, x1, mod3, ln_g, ln_b)
    return out.reshape(B, T, D)


def kernel(x_prompt, x_sample, c_prompt, c_sample, w_mod, b_mod, w_in, rpb, g_out_a, g_out_b, w_out, ln1_g, ln1_b,
           w_router, b_router, w_up, b_up, w_down, b_down, ln2_g, ln2_b):
    D = D_MODEL
    groups = [x_prompt, x_sample]
    conds = [c_prompt, c_sample]
    n_cond = sum(c.shape[0] for c in conds)
    pad_rows = -n_cond % 8
    c_all = jnp.concatenate(conds + [jnp.zeros((pad_rows, D), F32)], axis=0)
    mod3 = _modulation(c_all, w_mod[0], b_mod[0]).reshape(n_cond + pad_rows, 1, 6 * D)

    w_in_bf = w_in[0].astype(BF16)
    w_out_bf = w_out[0].astype(BF16)
    na_bias = _na_bias_table(rpb[0])
    g_a = g_out_a[0].reshape(1, WIDTH)
    g_b = g_out_b[0].reshape(1, WIDTH)
    ln1g, ln1b = ln1_g[0].reshape(1, D), ln1_b[0].reshape(1, D)
    ln2g, ln2b = ln2_g[0].reshape(1, D), ln2_b[0].reshape(1, D)
    w_router_pad = jnp.pad(w_router[0], ((0, 0), (0, LANES - N_EXPERTS)))
    b_router_pad = jnp.pad(b_router[0], (0, LANES - N_EXPERTS)).reshape(1, LANES)

    perm_np = _residue_permutation()
    perm, perm_t = jnp.asarray(perm_np, BF16), jnp.asarray(perm_np.T, BF16)

    x1s, u2s, logit_list = [], [], []
    mod_row0 = 0
    mod_rows = []
    for x in groups:
        B, T, _ = x.shape
        cos_v, sin_v = _rope_tables(T)
        qa, ka, va, qb, kb, vb = _input_projection_nat(x, mod3, w_in_bf, cos_v, sin_v, perm, mod_row0)
        oa = _dilated_attention(qa, ka, va, B, T)
        ob = _neighbourhood_attention(qb, kb, vb, na_bias, B, T)
        x1, u2, logits = _output_projection_nat(oa, ob, x, mod3, mod_row0, g_a, g_b, w_out_bf, ln1g, ln1b,
                                                w_router_pad, b_router_pad, perm_t)
        x1s.append(x1)
        u2s.append(u2)
        logit_list.append(logits)
        mod_rows.append(mod_row0)
        mod_row0 += B

    logits_all = jnp.concatenate(logit_list, axis=0)
    N = logits_all.shape[0]
    idx, gates, tile_cnt = _routing(logits_all)

    tc = tile_cnt[:, 0, :N_EXPERTS]
    seg = (tc + SEG - 1) // SEG * SEG
    total = jnp.sum(seg, axis=0)
    padded = (total + MOE_ROWS - 1) // MOE_ROWS * MOE_ROWS
    pad_end = jnp.cumsum(padded)
    pad_start = pad_end - padded
    tile_start = pad_start[None, :] + jnp.cumsum(seg, axis=0) - seg
    n_tiles = N // SCATTER_TILE
    n_blocks = (N * TOP_K + n_tiles * N_EXPERTS * (SEG - 1)) // MOE_ROWS + N_EXPERTS
    n_rows = n_blocks * MOE_ROWS
    blk_end = pad_end // MOE_ROWS
    n_used = blk_end[-1:].astype(I32)
    blk_expert = jnp.minimum(
        jnp.sum(blk_end[None, :] <= jnp.arange(n_blocks, dtype=I32)[:, None], axis=1), N_EXPERTS - 1).astype(I32)
    lane_pad = ((0, 0), (0, LANES - N_EXPERTS))
    tile_start3 = jnp.pad(tile_start, lane_pad).astype(I32)[:, None, :]
    tails = jnp.pad(jnp.stack([pad_start + total, pad_end]), lane_pad).astype(I32)

    xs = None
    row0 = 0
    for u2 in u2s:
        n = u2.shape[0]
        t0, t1 = row0 // SCATTER_TILE, (row0 + n) // SCATTER_TILE
        xs = _dispatch(idx[row0:row0 + n], u2, tile_cnt[t0:t1], tile_start3[t0:t1], tails, xs, n_rows)
        row0 += n

    w_up_de = _deinterleave_up(w_up[0])
    b_up_de = jnp.concatenate([b_up[0][:, 0::2], b_up[0][:, 1::2]], axis=-1).reshape(N_EXPERTS, 1, 2 * D_FF)
    w_down_bf = w_down[0].astype(BF16)
    b_down3 = b_down[0].reshape(N_EXPERTS, 1, D)
    y_pad = _expert_blocks(xs, blk_expert, n_used, w_up_de, b_up_de, w_down_bf, b_down3)

    outs = []
    row0 = 0
    for x, x1, mrow in zip(groups, x1s, mod_rows):
        B, T, _ = x.shape
        n = B * T
        t0, t1 = row0 // SCATTER_TILE, (row0 + n) // SCATTER_TILE
        outs.append(_combine(idx[row0:row0 + n], gates[row0:row0 + n], tile_cnt[t0:t1], tile_start3[t0:t1],
                             y_pad, x1, mod3, mrow, ln2g, ln2b, B, T))
        row0 += n
    return tuple(outs)
```

```python
import functools

import numpy as np
import jax
import jax.numpy as jnp
from jax import lax
from jax.experimental import pallas as pl
from jax.experimental.pallas import tpu as pltpu

F32 = jnp.float32
BF16 = jnp.bfloat16
I32 = jnp.int32

D_MODEL = 1024
HEAD_DIM = 64
N_HEADS = 8
WIDTH = N_HEADS * HEAD_DIM
N_PAIRS = WIDTH // 128
ROPE_THETA = 10000.0
RADIUS = 64
GRID_W = 64
NA_ROWS = 8
NA_COLS = 16
N_EXPERTS = 32
TOP_K = 4
D_FF = 1024
SWIGLU_ALPHA = 1.702
SWIGLU_LIMIT = 7.0
DEEPNORM_ALPHA = 2.0 ** 0.25
LN_EPS = 1e-5
RMS_EPS = 1e-6
NEG = -1e30
LOG2E = 1.4426950408889634

LANES = 128
RES = 16
CHUNK = 128
RES_PER_STEP = 4
MOE_ROWS = 512
ROUTE_TILE = 1024
SCATTER_TILE = 256
VMEM_LIMIT = 56 * 1024 * 1024


def _cparams(sem, vmem=VMEM_LIMIT):
    return pltpu.CompilerParams(dimension_semantics=sem, vmem_limit_bytes=vmem)


def _mod_kernel(c_ref, w_ref, b_ref, o_ref):
    c = c_ref[...]
    s = c * (1.0 / (1.0 + jnp.exp(-c)))
    o_ref[...] = jnp.dot(s.astype(BF16), w_ref[...].astype(BF16), preferred_element_type=F32) + b_ref[...]


def _modulation(c, w_mod, b_mod):
    rows = c.shape[0]
    n_out = w_mod.shape[1]
    tn = 1024
    return pl.pallas_call(
        _mod_kernel,
        grid=(n_out // tn,),
        in_specs=[pl.BlockSpec((rows, D_MODEL), lambda j: (0, 0)),
                  pl.BlockSpec((D_MODEL, tn), lambda j: (0, j)),
                  pl.BlockSpec((1, tn), lambda j: (0, j))],
        out_specs=pl.BlockSpec((rows, tn), lambda j: (0, j)),
        out_shape=jax.ShapeDtypeStruct((rows, n_out), F32),
        compiler_params=_cparams(("arbitrary",)),
        name="modulation",
    )(c, w_mod, b_mod.reshape(1, n_out))


PERM = 256
PROJ_TOKENS = 1024
PROJ_ROWS = PROJ_TOKENS // RES


def _residue_permutation():
    p = np.zeros((PERM, PERM), np.float32)
    m, r = np.meshgrid(np.arange(PERM // RES), np.arange(RES), indexing="ij")
    p[(r * (PERM // RES) + m).reshape(-1), (m * RES + r).reshape(-1)] = 1.0
    return p


def _inproj_nat_kernel(x_ref, mod_ref, w_ref, cos_ref, sin_ref, perm_ref,
                       qa_ref, ka_ref, va_ref, qb_ref, kb_ref, vb_ref, u_nat, u_view):
    D = D_MODEL
    shift = mod_ref[:, 0:D]
    scale = mod_ref[:, D:2 * D]
    qscale = HEAD_DIM ** -0.5 * LOG2E
    u_nat[...] = (x_ref[...] * (1.0 + scale) + shift).astype(BF16)

    sub = 512
    for s in range(PROJ_TOKENS // sub):
        u = u_nat[s * sub:(s + 1) * sub, :]
        for i, (ref, mul) in enumerate(((qb_ref, qscale), (kb_ref, 1.0), (vb_ref, 1.0))):
            p = jnp.dot(u, w_ref[:, (3 + i) * WIDTH:(4 + i) * WIDTH], preferred_element_type=F32) * mul
            for hp in range(N_PAIRS):
                ref[hp, s * sub:(s + 1) * sub, :] = p[:, hp * LANES:(hp + 1) * LANES].astype(BF16)

    rows_g = PERM // RES
    for g in range(PROJ_TOKENS // PERM):
        pv = jnp.dot(perm_ref[...], u_nat[g * PERM:(g + 1) * PERM, :], preferred_element_type=F32).astype(BF16)
        for r in range(RES):
            u_view[r, g * rows_g:(g + 1) * rows_g, :] = pv[r * rows_g:(r + 1) * rows_g]

    lane = lax.broadcasted_iota(I32, (1, WIDTH), 1)
    first_half = (lane % HEAD_DIM) < (HEAD_DIM // 2)
    for q in range(RES // RES_PER_STEP):
        u = jnp.concatenate([u_view[q * RES_PER_STEP + a] for a in range(RES_PER_STEP)], axis=0)
        lo = q * RES_PER_STEP * LANES
        cosf = jnp.concatenate([jnp.tile(cos_ref[:, lo + a * LANES:lo + (a + 1) * LANES], (1, N_PAIRS))
                                for a in range(RES_PER_STEP)], axis=0)
        sinf = jnp.concatenate([jnp.tile(sin_ref[:, lo + a * LANES:lo + (a + 1) * LANES], (1, N_PAIRS))
                                for a in range(RES_PER_STEP)], axis=0)

        def rope(p):
            rot = jnp.where(first_half, pltpu.roll(p, WIDTH - HEAD_DIM // 2, 1), pltpu.roll(p, HEAD_DIM // 2, 1))
            return p * cosf + rot * sinf

        for i, ref in enumerate((qa_ref, ka_ref, va_ref)):
            p = jnp.dot(u, w_ref[:, i * WIDTH:(i + 1) * WIDTH], preferred_element_type=F32)
            if i == 0:
                p = rope(p) * qscale
            elif i == 1:
                p = rope(p)
            for a in range(RES_PER_STEP):
                for hp in range(N_PAIRS):
                    ref[q * RES_PER_STEP + a, hp] = (
                        p[a * PROJ_ROWS:(a + 1) * PROJ_ROWS, hp * LANES:(hp + 1) * LANES].astype(BF16))


def _input_projection_nat(x, mod3, w_in_bf, cos_v, sin_v, perm, mod_row0):
    B, T, D = x.shape
    N = B * T
    BL = N // RES
    tps = T // PROJ_TOKENS
    res_shape = jax.ShapeDtypeStruct((RES, N_PAIRS, BL, LANES), BF16)
    nat_shape = jax.ShapeDtypeStruct((N_PAIRS, N, LANES), BF16)
    res_spec = pl.BlockSpec((RES, N_PAIRS, PROJ_ROWS, LANES), lambda i: (0, 0, i, 0))
    nat_spec = pl.BlockSpec((N_PAIRS, PROJ_TOKENS, LANES), lambda i: (0, i, 0))
    return pl.pallas_call(
        _inproj_nat_kernel,
        grid=(N // PROJ_TOKENS,),
        in_specs=[pl.BlockSpec((PROJ_TOKENS, D), lambda i: (i, 0)),
                  pl.BlockSpec((None, 1, 6 * D), lambda i: (mod_row0 + i // tps, 0, 0)),
                  pl.BlockSpec((D, 6 * WIDTH), lambda i: (0, 0)),
                  pl.BlockSpec((PROJ_ROWS, RES * LANES), lambda i: (i % tps, 0)),
                  pl.BlockSpec((PROJ_ROWS, RES * LANES), lambda i: (i % tps, 0)),
                  pl.BlockSpec((PERM, PERM), lambda i: (0, 0))],
        out_specs=[res_spec, res_spec, res_spec, nat_spec, nat_spec, nat_spec],
        out_shape=[res_shape, res_shape, res_shape, nat_shape, nat_shape, nat_shape],
        scratch_shapes=[pltpu.VMEM((PROJ_TOKENS, D), BF16), pltpu.VMEM((RES, PROJ_ROWS, D), BF16)],
        compiler_params=_cparams(("arbitrary",)),
        name="input_projection",
    )(x.reshape(N, D), mod3, w_in_bf, cos_v, sin_v, perm)


def _rope_tables(T):
    half = HEAD_DIM // 2
    inv_freq = 1.0 / (ROPE_THETA ** (jnp.arange(half, dtype=F32) / half))
    ang = jnp.arange(T, dtype=F32)[:, None] * inv_freq[None, :]
    cos, sin = jnp.cos(ang), jnp.sin(ang)
    cos_h = jnp.concatenate([cos, cos], axis=-1)
    sin_h = jnp.concatenate([-sin, sin], axis=-1)
    cos2 = jnp.concatenate([cos_h, cos_h], axis=-1)
    sin2 = jnp.concatenate([sin_h, sin_h], axis=-1)
    L = T // RES
    return cos2.reshape(L, RES * LANES), sin2.reshape(L, RES * LANES)


P2_ROWS = 32
P2_KROWS = 64
P1_ROWS = 16
P1_KROWS = 32
P1_SHIFT = 8
HALO = 64
P3_BATCH = 8
P2_BATCH = 2
P1_BATCH = 2


def _band_tables(has_halo):
    def mask(ok):
        return np.where(ok, 0.0, NEG).astype(np.float32)
    mq = np.arange(CHUNK)[:, None]
    koff3 = np.arange(2 * CHUNK) - HALO
    b3 = mask(np.abs(mq - koff3[None, :]) <= RADIUS)
    j = np.repeat(np.arange(4), P2_ROWS)[:, None]
    a = np.tile(np.arange(P2_ROWS), 4)[:, None]
    jk = np.repeat(np.arange(4), P2_KROWS)[None, :]
    bk = np.tile(np.arange(P2_KROWS), 4)[None, :]
    b2 = mask(np.abs(4 * (a - (bk - 16)) + (j - jk)) <= RADIUS)
    koff2 = (bk - 16).reshape(-1)
    r = np.repeat(np.arange(RES), P1_ROWS)[:, None]
    a = np.tile(np.arange(P1_ROWS), RES)[:, None]
    rk = np.repeat(np.arange(RES), P1_KROWS)[None, :]
    bk = np.tile(np.arange(P1_KROWS), RES)[None, :]
    b1 = mask(np.abs(RES * (a - (bk - P1_SHIFT)) + (r - rk)) <= RADIUS)
    koff1 = (bk - P1_SHIFT).reshape(-1)
    return (jnp.asarray(b3), jnp.asarray(koff3.astype(np.int32)[None, :]),
            jnp.asarray(b2), jnp.asarray(koff2.astype(np.int32)[None, :]),
            jnp.asarray(b1), jnp.asarray(koff1.astype(np.int32)[None, :]))


def _attend_pairs(problems, head0):
    scores = []
    for q, k, _, _ in problems:
        for h in range(2):
            sel = head0 if h == 0 else jnp.logical_not(head0)
            qh = jnp.where(sel, q, jnp.zeros_like(q))
            scores.append(lax.dot_general(qh, k, (((1,), (1,)), ((), ())), preferred_element_type=F32))
    probs, stats = [], []
    for i, s in enumerate(scores):
        s = s + problems[i // 2][3]
        m = jnp.max(s, axis=-1, keepdims=True)
        p = jnp.exp2(s - m)
        stats.append((m, jnp.sum(p, axis=-1, keepdims=True)))
        probs.append(p.astype(BF16))
    outs = [jnp.dot(p, problems[i // 2][2], preferred_element_type=F32) for i, p in enumerate(probs)]
    results = []
    for i in range(len(problems)):
        (m0, l0), (m1, l1) = stats[2 * i], stats[2 * i + 1]
        results.append((jnp.where(head0, outs[2 * i], outs[2 * i + 1]),
                        jnp.where(head0, m0, m1), jnp.where(head0, l0, l1)))
    return results


def _dilated_kernel(*refs, has_halo, seq_rows, chunks_per_seq):
    if has_halo:
        (q_ref, k_ref, v_ref, kp_ref, kn_ref, vp_ref, vn_ref,
         b3_ref, o3_ref, b2_ref, o2_ref, b1_ref, o1_ref,
         out_ref, kf, vf, kf8, vf8, acc, ms, ls) = refs
    else:
        (q_ref, k_ref, v_ref, b3_ref, o3_ref, b2_ref, o2_ref, b1_ref, o1_ref,
         out_ref, kf, vf, kf8, vf8, acc, ms, ls) = refs

    c = pl.program_id(0) % chunks_per_seq
    row0 = c * CHUNK
    lane = lax.broadcasted_iota(I32, (1, LANES), 1)
    head0 = lane < HEAD_DIM

    for r in range(RES):
        kf[r, HALO:HALO + CHUNK, :] = k_ref[r]
        vf[r, HALO:HALO + CHUNK, :] = v_ref[r]
        if has_halo:
            kf[r, 0:HALO, :] = kp_ref[r]
            vf[r, 0:HALO, :] = vp_ref[r]
            kf[r, HALO + CHUNK:, :] = kn_ref[r]
            vf[r, HALO + CHUNK:, :] = vn_ref[r]
        else:
            zeros = jnp.zeros((HALO, LANES), BF16)
            kf[r, 0:HALO, :] = zeros
            vf[r, 0:HALO, :] = zeros
            kf[r, HALO + CHUNK:, :] = zeros
            vf[r, HALO + CHUNK:, :] = zeros
        for src, dst in ((kf, kf8), (vf, vf8)):
            w32 = src[r].astype(F32)
            dst[r] = jnp.concatenate([w32[P1_SHIFT:], w32[:P1_SHIFT]], axis=0).astype(BF16)

    def in_seq(base, off_ref):
        kv_row = base + off_ref[...]
        return jnp.where((kv_row >= 0) & (kv_row < seq_rows), 0.0, NEG).astype(F32)

    def attend(problems):
        return _attend_pairs(problems, head0)

    def merge(a_old, m_old, l_old, o, m, l):
        mn = jnp.maximum(m_old, m)
        wa = jnp.exp2(m_old - mn)
        wb = jnp.exp2(m - mn)
        return a_old * wa + o * wb, mn, l_old * wa + l * wb

    bias3 = b3_ref[...] + in_seq(row0, o3_ref)

    def body3(it, carry):
        rs = [it * P3_BATCH + u for u in range(P3_BATCH)]
        res = attend([(q_ref[r], kf[r], vf[r], bias3) for r in rs])
        for r, (o, m, l) in zip(rs, res):
            acc[r] = o
            ms[r] = m
            ls[r] = l
        return carry

    lax.fori_loop(0, RES // P3_BATCH, body3, 0)

    b2 = b2_ref[...]

    def body2(it, carry):
        def gather(ref, r4, start, rows):
            return jnp.concatenate([ref[4 * j + r4, pl.ds(start, rows), :] for j in range(4)], axis=0)

        where, problems = [], []
        for u in range(P2_BATCH):
            g = it * P2_BATCH + u
            qs = pl.multiple_of(g * P2_ROWS, P2_ROWS)
            ks = pl.multiple_of(HALO - 16 + g * P2_ROWS, 16)
            bias = b2 + in_seq(row0 + g * P2_ROWS, o2_ref)
            for r4 in range(4):
                where.append((r4, qs))
                problems.append((gather(q_ref, r4, qs, P2_ROWS), gather(kf, r4, ks, P2_KROWS),
                                 gather(vf, r4, ks, P2_KROWS), bias))
        for (r4, qs), (o, m, l) in zip(where, attend(problems)):
            a_new, m_new, l_new = merge(gather(acc, r4, qs, P2_ROWS), gather(ms, r4, qs, P2_ROWS),
                                        gather(ls, r4, qs, P2_ROWS), o, m, l)
            for j in range(4):
                acc[4 * j + r4, pl.ds(qs, P2_ROWS), :] = a_new[j * P2_ROWS:(j + 1) * P2_ROWS]
                ms[4 * j + r4, pl.ds(qs, P2_ROWS), :] = m_new[j * P2_ROWS:(j + 1) * P2_ROWS]
                ls[4 * j + r4, pl.ds(qs, P2_ROWS), :] = l_new[j * P2_ROWS:(j + 1) * P2_ROWS]
        return carry

    lax.fori_loop(0, CHUNK // P2_ROWS // P2_BATCH, body2, 0)

    b1 = b1_ref[...]

    def body1(it, carry):
        def gather(ref, start, rows):
            return jnp.concatenate([ref[r, pl.ds(start, rows), :] for r in range(RES)], axis=0)

        starts, problems = [], []
        for u in range(P1_BATCH):
            g = it * P1_BATCH + u
            qs = pl.multiple_of(g * P1_ROWS, P1_ROWS)
            ks = pl.multiple_of(HALO - 16 + g * P1_ROWS, 16)
            starts.append(qs)
            problems.append((gather(q_ref, qs, P1_ROWS), gather(kf8, ks, P1_KROWS), gather(vf8, ks, P1_KROWS),
                             b1 + in_seq(row0 + g * P1_ROWS, o1_ref)))
        for qs, (o, m, l) in zip(starts, attend(problems)):
            a_new, m_new, l_new = merge(gather(acc, qs, P1_ROWS), gather(ms, qs, P1_ROWS),
                                        gather(ls, qs, P1_ROWS), o, m, l)
            for r in range(RES):
                acc[r, pl.ds(qs, P1_ROWS), :] = a_new[r * P1_ROWS:(r + 1) * P1_ROWS]
                ms[r, pl.ds(qs, P1_ROWS), :] = m_new[r * P1_ROWS:(r + 1) * P1_ROWS]
                ls[r, pl.ds(qs, P1_ROWS), :] = l_new[r * P1_ROWS:(r + 1) * P1_ROWS]
        return carry

    lax.fori_loop(0, CHUNK // P1_ROWS // P1_BATCH, body1, 0)

    def body_out(r, carry):
        out_ref[r] = (acc[r] / ls[r]).astype(BF16)
        return carry

    lax.fori_loop(0, RES, body_out, 0)


def _dilated_attention(qa, ka, va, B, T):
    L = T // RES
    BL = B * L
    cps = L // CHUNK
    has_halo = cps > 1
    tables = _band_tables(has_halo)
    blk = (RES, None, CHUNK, LANES)
    center = pl.BlockSpec(blk, lambda i, hp: (0, hp, i, 0))
    in_specs = [center, center, center]
    args = [qa, ka, va]
    if has_halo:
        hblk = (RES, None, HALO, LANES)
        per = CHUNK // HALO

        def prev_map(i, hp):
            return (0, hp, jnp.maximum(per * i - 1, (i // cps) * cps * per), 0)

        def next_map(i, hp):
            return (0, hp, jnp.minimum(per * i + per, (i // cps + 1) * cps * per - 1), 0)

        in_specs += [pl.BlockSpec(hblk, prev_map), pl.BlockSpec(hblk, next_map),
                     pl.BlockSpec(hblk, prev_map), pl.BlockSpec(hblk, next_map)]
        args += [ka, ka, va, va]
    for t in tables:
        in_specs.append(pl.BlockSpec(t.shape, lambda i, hp: (0, 0)))
        args.append(t)
    kern = functools.partial(_dilated_kernel, has_halo=has_halo, seq_rows=L, chunks_per_seq=cps)
    return pl.pallas_call(
        kern,
        grid=(BL // CHUNK, N_PAIRS),
        in_specs=in_specs,
        out_specs=pl.BlockSpec(blk, lambda i, hp: (0, hp, i, 0)),
        out_shape=jax.ShapeDtypeStruct((RES, N_PAIRS, BL, LANES), BF16),
        scratch_shapes=[pltpu.VMEM((RES, 2 * CHUNK, LANES), BF16),
                        pltpu.VMEM((RES, 2 * CHUNK, LANES), BF16),
                        pltpu.VMEM((RES, 2 * CHUNK, LANES), BF16),
                        pltpu.VMEM((RES, 2 * CHUNK, LANES), BF16),
                        pltpu.VMEM((RES, CHUNK, LANES), F32),
                        pltpu.VMEM((RES, CHUNK, LANES), F32),
                        pltpu.VMEM((RES, CHUNK, LANES), F32)],
        compiler_params=_cparams(("arbitrary", "arbitrary")),
        name="dilated_attention",
    )(*args)


NA_KEYS = NA_ROWS * GRID_W
NA_BLOCK_ROWS = 32
NA_BATCH = 32


def _na_bias_table(rpb):
    c = np.arange(GRID_W)
    col_start = np.clip(c - NA_COLS // 2, 0, GRID_W - NA_COLS)
    col_mask = (c[None, :] >= col_start[:, None]) & (c[None, :] < col_start[:, None] + NA_COLS)
    dc_idx = np.clip(c[None, :] - c[:, None], -(NA_COLS - 1), NA_COLS - 1) + NA_COLS - 1
    rel = rpb.astype(F32)[:, :, dc_idx] * LOG2E
    rel = jnp.where(col_mask[None, None], rel, NEG)
    per_off = [rel[:, d0:d0 + NA_ROWS].transpose(0, 2, 1, 3).reshape(N_HEADS, GRID_W, NA_KEYS)
               for d0 in range(NA_ROWS)]
    return jnp.stack(per_off, axis=0).reshape(NA_ROWS, N_PAIRS, 2, GRID_W, NA_KEYS)


def _na_kernel(q_ref, k_ref, v_ref, bias_ref, out_ref, *, grid_rows, block_rows):
    gb = pl.program_id(2)
    lane = lax.broadcasted_iota(I32, (1, LANES), 1)
    head0 = lane < HEAD_DIM

    def body(it, carry):
        rows, scores = [], []
        for u in range(NA_BATCH):
            i = it * NA_BATCH + u
            g = gb * block_rows + i
            rs = jnp.clip(g - NA_ROWS // 2, 0, grid_rows - NA_ROWS)
            d0 = rs - g + NA_ROWS - 1
            qs = pl.multiple_of(i * GRID_W, GRID_W)
            ks = pl.multiple_of(rs * GRID_W, GRID_W)
            q = q_ref[pl.ds(qs, GRID_W), :]
            k = k_ref[pl.ds(ks, NA_KEYS), :]
            rows.append((qs, ks, d0))
            for h in range(2):
                sel = head0 if h == 0 else jnp.logical_not(head0)
                qh = jnp.where(sel, q, jnp.zeros_like(q))
                scores.append(lax.dot_general(qh, k, (((1,), (1,)), ((), ())), preferred_element_type=F32))
        probs, sums = [], []
        for n, s in enumerate(scores):
            s = s + bias_ref[rows[n // 2][2], n % 2]
            p = jnp.exp2(s - jnp.max(s, axis=-1, keepdims=True))
            sums.append(jnp.sum(p, axis=-1, keepdims=True))
            probs.append(p.astype(BF16))
        outs = []
        for n, p in enumerate(probs):
            v = v_ref[pl.ds(rows[n // 2][1], NA_KEYS), :]
            outs.append(jnp.dot(p, v, preferred_element_type=F32) / sums[n])
        for u, (qs, _, _) in enumerate(rows):
            out_ref[pl.ds(qs, GRID_W), :] = jnp.where(head0, outs[2 * u], outs[2 * u + 1]).astype(BF16)
        return carry

    lax.fori_loop(0, block_rows // NA_BATCH, body, 0)


def _neighbourhood_attention(qb, kb, vb, bias, B, T):
    G = T // GRID_W
    rb = min(NA_BLOCK_ROWS, G)
    nb = G // rb
    kern = functools.partial(_na_kernel, grid_rows=G, block_rows=rb)
    seq = pl.BlockSpec((None, T, LANES), lambda hp, b, gb: (hp, b, 0))
    qblk = pl.BlockSpec((None, rb * GRID_W, LANES), lambda hp, b, gb: (hp, b * nb + gb, 0))
    return pl.pallas_call(
        kern,
        grid=(N_PAIRS, B, nb),
        in_specs=[qblk, seq, seq,
                  pl.BlockSpec((NA_ROWS, None, 2, GRID_W, NA_KEYS), lambda hp, b, gb: (0, hp, 0, 0, 0))],
        out_specs=qblk,
        out_shape=jax.ShapeDtypeStruct((N_PAIRS, B * T, LANES), BF16),
        compiler_params=_cparams(("arbitrary", "arbitrary", "arbitrary")),
        name="neighbourhood_attention",
    )(qb, kb, vb, bias)


def _layer_norm(h, g, b):
    mu = jnp.mean(h, axis=-1, keepdims=True)
    d = h - mu
    var = jnp.mean(d * d, axis=-1, keepdims=True)
    return d * lax.rsqrt(var + LN_EPS) * g + b


def _outproj_nat_kernel(oa_ref, ob_ref, x_ref, mod_ref, ga_ref, gb_ref, w_ref, lg_ref, lb_ref, wr_ref, br_ref,
                        permt_ref, x1_ref, u2_ref, logit_ref, mix_scr):
    D = D_MODEL
    gate_a = mod_ref[:, 2 * D:3 * D]
    shift_f = mod_ref[:, 3 * D:4 * D]
    scale_f = mod_ref[:, 4 * D:5 * D]

    def rms(o, g):
        return o * lax.rsqrt(jnp.mean(o * o, axis=-1, keepdims=True) + RMS_EPS) * g

    rows_g = PERM // RES
    for g in range(PROJ_TOKENS // PERM):
        grouped = jnp.concatenate(
            [jnp.concatenate([oa_ref[r, hp, g * rows_g:(g + 1) * rows_g, :] for hp in range(N_PAIRS)], axis=1)
             for r in range(RES)], axis=0)
        oa = jnp.dot(permt_ref[...], grouped, preferred_element_type=F32)
        ob = jnp.concatenate([ob_ref[hp, g * PERM:(g + 1) * PERM, :] for hp in range(N_PAIRS)], axis=1).astype(F32)
        mix = jnp.concatenate([rms(oa, ga_ref[...]), rms(ob, gb_ref[...])], axis=1)
        mix_scr[g * PERM:(g + 1) * PERM, :] = mix.astype(BF16)

    wr = wr_ref[...]
    wr_hi = wr.astype(BF16)
    wr_lo = (wr - wr_hi.astype(F32)).astype(BF16)
    sub = 256
    for s in range(PROJ_TOKENS // sub):
        rows = slice(s * sub, (s + 1) * sub)
        y = jnp.dot(mix_scr[rows, :], w_ref[...], preferred_element_type=F32)
        h = DEEPNORM_ALPHA * x_ref[rows, :] + (1.0 + gate_a) * y
        x1 = _layer_norm(h, lg_ref[...], lb_ref[...])
        x1_ref[rows, :] = x1
        u2 = x1 * (1.0 + scale_f) + shift_f
        u2_ref[rows, :] = u2.astype(BF16)
        u_hi = u2.astype(BF16)
        u_lo = (u2 - u_hi.astype(F32)).astype(BF16)
        logit_ref[rows, :] = (jnp.dot(u_hi, wr_hi, preferred_element_type=F32)
                              + jnp.dot(u_lo, wr_hi, preferred_element_type=F32)
                              + jnp.dot(u_hi, wr_lo, preferred_element_type=F32)) + br_ref[...]


def _output_projection_nat(oa, ob, x, mod3, mod_row0, g_a, g_b, w_out_bf, ln_g, ln_b, w_router_pad, b_router_pad,
                           perm_t):
    B, T, D = x.shape
    N = B * T
    tps = T // PROJ_TOKENS
    const = lambda shape: pl.BlockSpec(shape, lambda i: tuple(0 for _ in shape))
    rows = lambda width: pl.BlockSpec((PROJ_TOKENS, width), lambda i: (i, 0))
    return pl.pallas_call(
        _outproj_nat_kernel,
        grid=(N // PROJ_TOKENS,),
        in_specs=[pl.BlockSpec((RES, N_PAIRS, PROJ_ROWS, LANES), lambda i: (0, 0, i, 0)),
                  pl.BlockSpec((N_PAIRS, PROJ_TOKENS, LANES), lambda i: (0, i, 0)),
                  rows(D),
                  pl.BlockSpec((None, 1, 6 * D), lambda i: (mod_row0 + i // tps, 0, 0)),
                  const((1, WIDTH)), const((1, WIDTH)), const((2 * WIDTH, D)),
                  const((1, D)), const((1, D)), const((D, LANES)), const((1, LANES)), const((PERM, PERM))],
        out_specs=[rows(D), rows(D), rows(LANES)],
        out_shape=[jax.ShapeDtypeStruct((N, D), F32), jax.ShapeDtypeStruct((N, D), BF16),
                   jax.ShapeDtypeStruct((N, LANES), F32)],
        scratch_shapes=[pltpu.VMEM((PROJ_TOKENS, 2 * WIDTH), BF16)],
        compiler_params=_cparams(("arbitrary",)),
        name="output_projection",
    )(oa, ob, x.reshape(N, D), mod3, g_a, g_b, w_out_bf, ln_g, ln_b, w_router_pad, b_router_pad, perm_t)


def _route_kernel(logit_ref, idx_ref, gate_ref, cnt_ref):
    for t in range(ROUTE_TILE // SCATTER_TILE):
        rows = slice(t * SCATTER_TILE, (t + 1) * SCATTER_TILE)
        sel, gates, cnt = _route_tile(logit_ref[rows, :])
        idx_ref[rows, :] = sel
        gate_ref[rows, :] = gates
        cnt_ref[t] = cnt


def _route_tile(raw):
    tn = raw.shape[0]
    lane = lax.broadcasted_iota(I32, (tn, LANES), 1)
    logits = jnp.where(lane < N_EXPERTS, raw, -3.0e38)
    vals, idxs = [], []
    multi = jnp.zeros((tn, LANES), F32)
    for _ in range(TOP_K):
        m = jnp.max(logits, axis=-1, keepdims=True)
        idx = jnp.min(jnp.where(logits == m, lane, LANES), axis=-1, keepdims=True)
        hot = lane == idx
        vals.append(m)
        idxs.append(idx)
        multi = multi + hot.astype(F32)
        logits = jnp.where(hot, -3.0e38, logits)
    es = [jnp.exp(v - vals[0]) for v in vals]
    tot = es[0] + es[1] + es[2] + es[3]
    lane4 = lax.broadcasted_iota(I32, (tn, TOP_K), 1)
    sel = jnp.zeros((tn, TOP_K), I32)
    gates = jnp.zeros((tn, TOP_K), F32)
    for k in range(TOP_K):
        sel = jnp.where(lane4 == k, idxs[k], sel)
        gates = jnp.where(lane4 == k, es[k] / tot, gates)
    return sel, gates, jnp.sum(multi, axis=0, keepdims=True).astype(I32)


def _routing(logits):
    N = logits.shape[0]
    tn = ROUTE_TILE
    nt = N // SCATTER_TILE
    per_step = ROUTE_TILE // SCATTER_TILE
    return pl.pallas_call(
        _route_kernel,
        grid=(N // tn,),
        in_specs=[pl.BlockSpec((tn, LANES), lambda i: (i, 0))],
        out_specs=[pl.BlockSpec((tn, TOP_K), lambda i: (i, 0)),
                   pl.BlockSpec((tn, TOP_K), lambda i: (i, 0)),
                   pl.BlockSpec((per_step, 1, LANES), lambda i: (i, 0, 0))],
        out_shape=[jax.ShapeDtypeStruct((N, TOP_K), I32), jax.ShapeDtypeStruct((N, TOP_K), F32),
                   jax.ShapeDtypeStruct((nt, 1, LANES), I32)],
        compiler_params=_cparams(("arbitrary",)),
        name="moe_routing",
    )(logits)


SEG = 8
STAGE_ROWS = SCATTER_TILE * TOP_K + N_EXPERTS * SEG


def _lane_prefix_exclusive(v):
    lane = lax.broadcasted_iota(I32, v.shape, 1)
    incl = v
    s = 1
    while s < LANES:
        incl = incl + jnp.where(lane >= s, pltpu.roll(incl, s, 1), 0.0)
        s *= 2
    return incl - v


def _tile_ranks(idx, tn):
    lane = lax.broadcasted_iota(I32, (tn, LANES), 1)
    hots = [lane == idx[:, k:k + 1] for k in range(TOP_K)]
    multi = jnp.zeros((tn, LANES), F32)
    for h in hots:
        multi = multi + h.astype(F32)
    row = lax.broadcasted_iota(I32, (tn, tn), 0)
    col = lax.broadcasted_iota(I32, (tn, tn), 1)
    lower = (col < row).astype(BF16)
    before = jnp.dot(lower, multi.astype(BF16), preferred_element_type=F32)
    return hots, before


BIG_SEG = 4 * SEG


def _segment_dma_loops(cnt_s, rows_of, copy):
    def per_expert(e, carry):
        off, n_big, n_small = carry
        nseg = (cnt_s[0, e] + SEG - 1) // SEG
        big = nseg // (BIG_SEG // SEG)
        small = nseg - big * (BIG_SEG // SEG)

        def one_big(q, c):
            src, dst = rows_of(e, off, q * BIG_SEG)
            copy(src, dst, BIG_SEG).start()
            return c

        def one_small(q, c):
            src, dst = rows_of(e, off, big * BIG_SEG + q * SEG)
            copy(src, dst, SEG).start()
            return c

        lax.fori_loop(0, big, one_big, 0)
        lax.fori_loop(0, small, one_small, 0)
        return off + nseg * SEG, n_big + big, n_small + small

    _, n_big, n_small = lax.fori_loop(0, N_EXPERTS, per_expert, (0, 0, 0))
    return n_big, n_small


def _drain(copy, n_big, n_small):
    def wait_big(q, c):
        copy(0, 0, BIG_SEG).wait()
        return c

    def wait_small(q, c):
        copy(0, 0, SEG).wait()
        return c

    lax.fori_loop(0, n_big, wait_big, 0)
    lax.fori_loop(0, n_small, wait_small, 0)


def _dispatch_kernel(*refs, first, n_steps):
    if first:
        cnt_s, start_s, tail_s, idx_ref, u_ref, cntv_ref, xs_ref, stage, zbuf, pending, sem, zsem = refs
    else:
        cnt_s, start_s, tail_s, idx_ref, u_ref, cntv_ref, _, xs_ref, stage, zbuf, pending, sem, zsem = refs
    tn = idx_ref.shape[0]
    step = pl.program_id(0)
    slot = step % 2

    if first:
        @pl.when(pl.program_id(0) == 0)
        def _():
            zbuf[...] = jnp.zeros_like(zbuf)

            def tail_copy(row):
                return pltpu.make_async_copy(zbuf, xs_ref.at[pl.ds(pl.multiple_of(row, SEG), SEG)], zsem)

            def fill(e, n):
                lo = tail_s[0, e]
                nfull = (tail_s[1, e] - lo) // SEG

                def one(q, c):
                    tail_copy(lo + q * SEG).start()
                    return c

                lax.fori_loop(0, nfull, one, 0)
                return n + nfull

            n = lax.fori_loop(0, N_EXPERTS, fill, 0)

            def drain(q, c):
                tail_copy(0).wait()
                return c

            lax.fori_loop(0, n, drain, 0)

    hots, before = _tile_ranks(idx_ref[...], tn)
    cntf = cntv_ref[...].astype(F32)
    seg_len = jnp.ceil(cntf * (1.0 / SEG)) * SEG
    seg_off = _lane_prefix_exclusive(seg_len)
    lane_r = lax.broadcasted_iota(I32, (tn, STAGE_ROWS), 1)
    onehot = jnp.zeros((tn, STAGE_ROWS), F32)
    for k in range(TOP_K):
        stage_row = jnp.sum(jnp.where(hots[k], seg_off + before, 0.0), axis=-1, keepdims=True)
        onehot = jnp.where(lane_r == stage_row.astype(I32), 1.0, onehot)
    stage[slot] = lax.dot_general(onehot.astype(BF16), u_ref[...].astype(BF16), (((0,), (0,)), ((), ())),
                                  preferred_element_type=F32)

    def copier(buf):
        def seg_copy(src_row, dst_row, rows):
            return pltpu.make_async_copy(stage.at[buf, pl.ds(pl.multiple_of(src_row, SEG), rows)],
                                         xs_ref.at[pl.ds(pl.multiple_of(dst_row, SEG), rows)], sem.at[buf])
        return seg_copy

    @pl.when(step > 0)
    def _():
        _drain(copier(1 - slot), pending[0], pending[1])

    n_big, n_small = _segment_dma_loops(
        cnt_s, lambda e, off, moved: (off + moved, start_s[0, e] + moved), copier(slot))
    pending[0] = n_big
    pending[1] = n_small

    @pl.when(step == n_steps - 1)
    def _():
        _drain(copier(slot), n_big, n_small)


def _dispatch(idx, u2, tile_cnt, tile_start, tails, xs, xs_rows):
    n, D = u2.shape
    tn = SCATTER_TILE
    nt = n // tn
    first = xs is None
    smem = lambda shape, imap: pl.BlockSpec(shape, imap, memory_space=pltpu.SMEM)
    kern = functools.partial(_dispatch_kernel, first=first, n_steps=nt)
    in_specs = [smem((None, 1, LANES), lambda i: (i, 0, 0)),
                smem((None, 1, LANES), lambda i: (i, 0, 0)),
                smem((2, LANES), lambda i: (0, 0)),
                pl.BlockSpec((tn, TOP_K), lambda i: (i, 0)),
                pl.BlockSpec((tn, D), lambda i: (i, 0)),
                pl.BlockSpec((None, 1, LANES), lambda i: (i, 0, 0))]
    args = [tile_cnt, tile_start, tails, idx, u2, tile_cnt]
    alias = {}
    if not first:
        alias = {len(args): 0}
        in_specs.append(pl.BlockSpec(memory_space=pl.ANY))
        args.append(xs)
    return pl.pallas_call(
        kern,
        grid=(nt,),
        in_specs=in_specs,
        out_specs=pl.BlockSpec(memory_space=pl.ANY),
        out_shape=jax.ShapeDtypeStruct((xs_rows, D), F32),
        scratch_shapes=[pltpu.VMEM((2, STAGE_ROWS, D), F32), pltpu.VMEM((SEG, D), F32), pltpu.SMEM((2,), I32),
                        pltpu.SemaphoreType.DMA((2,)), pltpu.SemaphoreType.DMA(())],
        input_output_aliases=alias,
        compiler_params=_cparams(("arbitrary",)),
        name="moe_dispatch",
    )(*args)


DEINT = 256


def _deinterleave_kernel(w_ref, p_ref, o_ref):
    n_groups = w_ref.shape[1] // DEINT
    half = DEINT // 2
    f = w_ref.shape[1] // 2
    for g in range(n_groups):
        wg = w_ref[:, g * DEINT:(g + 1) * DEINT].astype(BF16)
        y = jnp.dot(wg, p_ref[...], preferred_element_type=F32)
        o_ref[:, g * half:(g + 1) * half] = y[:, :half].astype(BF16)
        o_ref[:, f + g * half:f + (g + 1) * half] = y[:, half:].astype(BF16)


def _deinterleave_up(w_up):
    E, D, F2 = w_up.shape
    perm = np.zeros((DEINT, DEINT), np.float32)
    j = np.arange(DEINT // 2)
    perm[2 * j, j] = 1.0
    perm[2 * j + 1, DEINT // 2 + j] = 1.0
    return pl.pallas_call(
        _deinterleave_kernel,
        grid=(E,),
        in_specs=[pl.BlockSpec((None, D, F2), lambda e: (e, 0, 0)),
                  pl.BlockSpec((DEINT, DEINT), lambda e: (0, 0))],
        out_specs=pl.BlockSpec((None, D, F2), lambda e: (e, 0, 0)),
        out_shape=jax.ShapeDtypeStruct((E, D, F2), BF16),
        compiler_params=_cparams(("arbitrary",)),
        name="expert_weight_layout",
    )(w_up, jnp.asarray(perm, BF16))


def _expert_kernel(be_ref, nu_ref, x_ref, wu_ref, bu_ref, wd_ref, bd_ref, y_ref):
    del be_ref

    @pl.when(pl.program_id(0) < nu_ref[0])
    def _():
        x = x_ref[...].astype(BF16)
        hu = jnp.dot(x, wu_ref[...], preferred_element_type=F32) + bu_ref[...]
        glu = jnp.minimum(hu[:, :D_FF], SWIGLU_LIMIT)
        lin = jnp.clip(hu[:, D_FF:], -SWIGLU_LIMIT, SWIGLU_LIMIT)
        act = glu * (1.0 / (1.0 + jnp.exp(-SWIGLU_ALPHA * glu))) * (lin + 1.0)
        y_ref[...] = jnp.dot(act.astype(BF16), wd_ref[...], preferred_element_type=F32) + bd_ref[...]

    @pl.when(pl.program_id(0) >= nu_ref[0])
    def _():
        y_ref[...] = jnp.zeros_like(y_ref)


def _expert_blocks(xs, blk_expert, n_used, w_up_de, b_up_de, w_down_bf, b_down):
    n_rows, D = xs.shape
    nblk = n_rows // MOE_ROWS

    def xmap(j, be, nu):
        return (jnp.minimum(j, nu[0] - 1), 0)

    grid_spec = pltpu.PrefetchScalarGridSpec(
        num_scalar_prefetch=2,
        grid=(nblk,),
        in_specs=[pl.BlockSpec((MOE_ROWS, D), xmap),
                  pl.BlockSpec((None, D, 2 * D_FF), lambda j, be, nu: (be[j], 0, 0)),
                  pl.BlockSpec((None, 1, 2 * D_FF), lambda j, be, nu: (be[j], 0, 0)),
                  pl.BlockSpec((None, D_FF, D), lambda j, be, nu: (be[j], 0, 0)),
                  pl.BlockSpec((None, 1, D), lambda j, be, nu: (be[j], 0, 0))],
        out_specs=pl.BlockSpec((MOE_ROWS, D), lambda j, be, nu: (j, 0)),
    )
    return pl.pallas_call(
        _expert_kernel,
        grid_spec=grid_spec,
        out_shape=jax.ShapeDtypeStruct((n_rows, D), F32),
        compiler_params=_cparams(("arbitrary",)),
        name="moe_experts",
    )(blk_expert, n_used, xs, w_up_de, b_up_de, w_down_bf, b_down)


def _combine_kernel(cnt_s, start_s, idx_ref, gate_ref, cntv_ref, y_ref, x1_ref, mod_ref,
                    lg_ref, lb_ref, out_ref, stage, sem):
    D = D_MODEL
    tn = idx_ref.shape[0]

    def seg_copy(src_row, dst_row, rows):
        return pltpu.make_async_copy(y_ref.at[pl.ds(pl.multiple_of(src_row, SEG), rows)],
                                     stage.at[pl.ds(pl.multiple_of(dst_row, SEG), rows)], sem)

    @pl.when(pl.program_id(0) == 0)
    def _():
        stage[...] = jnp.zeros_like(stage)

    n_big, n_small = _segment_dma_loops(
        cnt_s, lambda e, off, moved: (start_s[0, e] + moved, off + moved), seg_copy)

    hots, before = _tile_ranks(idx_ref[...], tn)
    seg_len = jnp.ceil(cntv_ref[...].astype(F32) * (1.0 / SEG)) * SEG
    seg_off = _lane_prefix_exclusive(seg_len)
    gates = gate_ref[...]
    lane_r = lax.broadcasted_iota(I32, (tn, STAGE_ROWS), 1)
    weights = jnp.zeros((tn, STAGE_ROWS), F32)
    for k in range(TOP_K):
        stage_row = jnp.sum(jnp.where(hots[k], seg_off + before, 0.0), axis=-1, keepdims=True)
        weights = jnp.where(lane_r == stage_row.astype(I32), gates[:, k:k + 1], weights)

    _drain(seg_copy, n_big, n_small)

    y = jnp.dot(weights.astype(BF16), stage[...].astype(BF16), preferred_element_type=F32)
    gate_f = mod_ref[:, 5 * D:6 * D]
    h = DEEPNORM_ALPHA * x1_ref[...] + (1.0 + gate_f) * y
    out_ref[...] = _layer_norm(h, lg_ref[...], lb_ref[...])


def _combine(idx, gates, tile_cnt, tile_start, y_pad, x1, mod3, mod_row0, ln_g, ln_b, B, T):
    D = D_MODEL
    tn = SCATTER_TILE
    n_rows = B * T
    nt = n_rows // tn
    tps = T // tn
    smem = lambda imap: pl.BlockSpec((None, 1, LANES), imap, memory_space=pltpu.SMEM)
    out = pl.pallas_call(
        _combine_kernel,
        grid=(nt,),
        in_specs=[smem(lambda i: (i, 0, 0)), smem(lambda i: (i, 0, 0)),
                  pl.BlockSpec((tn, TOP_K), lambda i: (i, 0)),
                  pl.BlockSpec((tn, TOP_K), lambda i: (i, 0)),
                  pl.BlockSpec((None, 1, LANES), lambda i: (i, 0, 0)),
                  pl.BlockSpec(memory_space=pl.ANY),
                  pl.BlockSpec((tn, D), lambda i: (i, 0)),
                  pl.BlockSpec((None, 1, 6 * D), lambda i: (mod_row0 + i // tps, 0, 0)),
                  pl.BlockSpec((1, D), lambda i: (0, 0)),
                  pl.BlockSpec((1, D), lambda i: (0, 0))],
        out_specs=pl.BlockSpec((tn, D), lambda i: (i, 0)),
        out_shape=jax.ShapeDtypeStruct((n_rows, D), F32),
        scratch_shapes=[pltpu.VMEM((STAGE_ROWS, D), F32), pltpu.SemaphoreType.DMA(())],
        compiler_params=_cparams(("arbitrary",)),
        name="moe_combine",
    )(tile_cnt, tile_start, idx, gates, tile_cnt, y_pad, x1, mod3, ln_g, ln_b)
    return out.reshape(B, T, D)


def kernel(x_prompt, x_sample, c_prompt, c_sample, w_mod, b_mod, w_in, rpb, g_out_a, g_out_b, w_out, ln1_g, ln1_b,
           w_router, b_router, w_up, b_up, w_down, b_down, ln2_g, ln2_b):
    D = D_MODEL
    groups = [x_prompt, x_sample]
    conds = [c_prompt, c_sample]
    n_cond = sum(c.shape[0] for c in conds)
    pad_rows = -n_cond % 8
    c_all = jnp.concatenate(conds + [jnp.zeros((pad_rows, D), F32)], axis=0)
    mod3 = _modulation(c_all, w_mod[0], b_mod[0]).reshape(n_cond + pad_rows, 1, 6 * D)

    w_in_bf = w_in[0].astype(BF16)
    w_out_bf = w_out[0].astype(BF16)
    na_bias = _na_bias_table(rpb[0])
    g_a = g_out_a[0].reshape(1, WIDTH)
    g_b = g_out_b[0].reshape(1, WIDTH)
    ln1g, ln1b = ln1_g[0].reshape(1, D), ln1_b[0].reshape(1, D)
    ln2g, ln2b = ln2_g[0].reshape(1, D), ln2_b[0].reshape(1, D)
    w_router_pad = jnp.pad(w_router[0], ((0, 0), (0, LANES - N_EXPERTS)))
    b_router_pad = jnp.pad(b_router[0], (0, LANES - N_EXPERTS)).reshape(1, LANES)

    perm_np = _residue_permutation()
    perm, perm_t = jnp.asarray(perm_np, BF16), jnp.asarray(perm_np.T, BF16)

    x1s, u2s, logit_list = [], [], []
    mod_row0 = 0
    mod_rows = []
    for x in groups:
        B, T, _ = x.shape
        cos_v, sin_v = _rope_tables(T)
        qa, ka, va, qb, kb, vb = _input_projection_nat(x, mod3, w_in_bf, cos_v, sin_v, perm, mod_row0)
        oa = _dilated_attention(qa, ka, va, B, T)
        ob = _neighbourhood_attention(qb, kb, vb, na_bias, B, T)
        x1, u2, logits = _output_projection_nat(oa, ob, x, mod3, mod_row0, g_a, g_b, w_out_bf, ln1g, ln1b,
                                                w_router_pad, b_router_pad, perm_t)
        x1s.append(x1)
        u2s.append(u2)
        logit_list.append(logits)
        mod_rows.append(mod_row0)
        mod_row0 += B

    logits_all = jnp.concatenate(logit_list, axis=0)
    N = logits_all.shape[0]
    idx, gates, tile_cnt = _routing(logits_all)

    tc = tile_cnt[:, 0, :N_EXPERTS]
    seg = (tc + SEG - 1) // SEG * SEG
    total = jnp.sum(seg, axis=0)
    padded = (total + MOE_ROWS - 1) // MOE_ROWS * MOE_ROWS
    pad_end = jnp.cumsum(padded)
    pad_start = pad_end - padded
    tile_start = pad_start[None, :] + jnp.cumsum(seg, axis=0) - seg
    n_tiles = N // SCATTER_TILE
    n_blocks = (N * TOP_K + n_tiles * N_EXPERTS * (SEG - 1)) // MOE_ROWS + N_EXPERTS
    n_rows = n_blocks * MOE_ROWS
    blk_end = pad_end // MOE_ROWS
    n_used = blk_end[-1:].astype(I32)
    blk_expert = jnp.minimum(
        jnp.sum(blk_end[None, :] <= jnp.arange(n_blocks, dtype=I32)[:, None], axis=1), N_EXPERTS - 1).astype(I32)
    lane_pad = ((0, 0), (0, LANES - N_EXPERTS))
    tile_start3 = jnp.pad(tile_start, lane_pad).astype(I32)[:, None, :]
    tails = jnp.pad(jnp.stack([pad_start + total, pad_end]), lane_pad).astype(I32)

    xs = None
    row0 = 0
    for u2 in u2s:
        n = u2.shape[0]
        t0, t1 = row0 // SCATTER_TILE, (row0 + n) // SCATTER_TILE
        xs = _dispatch(idx[row0:row0 + n], u2, tile_cnt[t0:t1], tile_start3[t0:t1], tails, xs, n_rows)
        row0 += n

    w_up_de = _deinterleave_up(w_up[0])
    b_up_de = jnp.concatenate([b_up[0][:, 0::2], b_up[0][:, 1::2]], axis=-1).reshape(N_EXPERTS, 1, 2 * D_FF)
    w_down_bf = w_down[0].astype(BF16)
    b_down3 = b_down[0].reshape(N_EXPERTS, 1, D)
    y_pad = _expert_blocks(xs, blk_expert, n_used, w_up_de, b_up_de, w_down_bf, b_down3)

    outs = []
    row0 = 0
    for x, x1, mrow in zip(groups, x1s, mod_rows):
        B, T, _ = x.shape
        n = B * T
        t0, t1 = row0 // SCATTER_TILE, (row0 + n) // SCATTER_TILE
        outs.append(_combine(idx[row0:row0 + n], gates[row0:row0 + n], tile_cnt[t0:t1], tile_start3[t0:t1],
                             y_pad, x1, mod3, mrow, ln2g, ln2b, B, T))
        row0 += n
    return tuple(outs)
```

```python
import functools

import numpy as np
import jax
import jax.numpy as jnp
from jax import lax
from jax.experimental import pallas as pl
from jax.experimental.pallas import tpu as pltpu

F32 = jnp.float32
BF16 = jnp.bfloat16
I32 = jnp.int32

D_MODEL = 1024
HEAD_DIM = 64
N_HEADS = 8
WIDTH = N_HEADS * HEAD_DIM
N_PAIRS = WIDTH // 128
ROPE_THETA = 10000.0
RADIUS = 64
GRID_W = 64
NA_ROWS = 8
NA_COLS = 16
N_EXPERTS = 32
TOP_K = 4
D_FF = 1024
SWIGLU_ALPHA = 1.702
SWIGLU_LIMIT = 7.0
DEEPNORM_ALPHA = 2.0 ** 0.25
LN_EPS = 1e-5
RMS_EPS = 1e-6
NEG = -1e30
LOG2E = 1.4426950408889634

LANES = 128
RES = 16
CHUNK = 128
RES_PER_STEP = 4
MOE_ROWS = 512
ROUTE_TILE = 1024
SCATTER_TILE = 256
VMEM_LIMIT = 56 * 1024 * 1024


def _cparams(sem, vmem=VMEM_LIMIT):
    return pltpu.CompilerParams(dimension_semantics=sem, vmem_limit_bytes=vmem)


def _mod_kernel(c_ref, w_ref, b_ref, o_ref):
    c = c_ref[...]
    s = c * (1.0 / (1.0 + jnp.exp(-c)))
    o_ref[...] = jnp.dot(s.astype(BF16), w_ref[...].astype(BF16), preferred_element_type=F32) + b_ref[...]


def _modulation(c, w_mod, b_mod):
    rows = c.shape[0]
    n_out = w_mod.shape[1]
    tn = 1024
    return pl.pallas_call(
        _mod_kernel,
        grid=(n_out // tn,),
        in_specs=[pl.BlockSpec((rows, D_MODEL), lambda j: (0, 0)),
                  pl.BlockSpec((D_MODEL, tn), lambda j: (0, j)),
                  pl.BlockSpec((1, tn), lambda j: (0, j))],
        out_specs=pl.BlockSpec((rows, tn), lambda j: (0, j)),
        out_shape=jax.ShapeDtypeStruct((rows, n_out), F32),
        compiler_params=_cparams(("arbitrary",)),
        name="modulation",
    )(c, w_mod, b_mod.reshape(1, n_out))


PERM = 256
PROJ_TOKENS = 1024
PROJ_ROWS = PROJ_TOKENS // RES


def _residue_permutation():
    p = np.zeros((PERM, PERM), np.float32)
    m, r = np.meshgrid(np.arange(PERM // RES), np.arange(RES), indexing="ij")
    p[(r * (PERM // RES) + m).reshape(-1), (m * RES + r).reshape(-1)] = 1.0
    return p


def _inproj_nat_kernel(x_ref, mod_ref, w_ref, cos_ref, sin_ref, perm_ref,
                       qa_ref, ka_ref, va_ref, qb_ref, kb_ref, vb_ref, u_nat, u_view):
    D = D_MODEL
    shift = mod_ref[:, 0:D]
    scale = mod_ref[:, D:2 * D]
    qscale = HEAD_DIM ** -0.5 * LOG2E
    u_nat[...] = (x_ref[...] * (1.0 + scale) + shift).astype(BF16)

    sub = 512
    for s in range(PROJ_TOKENS // sub):
        u = u_nat[s * sub:(s + 1) * sub, :]
        for i, (ref, mul) in enumerate(((qb_ref, qscale), (kb_ref, 1.0), (vb_ref, 1.0))):
            p = jnp.dot(u, w_ref[:, (3 + i) * WIDTH:(4 + i) * WIDTH], preferred_element_type=F32) * mul
            for hp in range(N_PAIRS):
                ref[hp, s * sub:(s + 1) * sub, :] = p[:, hp * LANES:(hp + 1) * LANES].astype(BF16)

    rows_g = PERM // RES
    for g in range(PROJ_TOKENS // PERM):
        pv = jnp.dot(perm_ref[...], u_nat[g * PERM:(g + 1) * PERM, :], preferred_element_type=F32).astype(BF16)
        for r in range(RES):
            u_view[r, g * rows_g:(g + 1) * rows_g, :] = pv[r * rows_g:(r + 1) * rows_g]

    lane = lax.broadcasted_iota(I32, (1, WIDTH), 1)
    first_half = (lane % HEAD_DIM) < (HEAD_DIM // 2)
    for q in range(RES // RES_PER_STEP):
        u = jnp.concatenate([u_view[q * RES_PER_STEP + a] for a in range(RES_PER_STEP)], axis=0)
        lo = q * RES_PER_STEP * LANES
        cosf = jnp.concatenate([jnp.tile(cos_ref[:, lo + a * LANES:lo + (a + 1) * LANES], (1, N_PAIRS))
                                for a in range(RES_PER_STEP)], axis=0)
        sinf = jnp.concatenate([jnp.tile(sin_ref[:, lo + a * LANES:lo + (a + 1) * LANES], (1, N_PAIRS))
                                for a in range(RES_PER_STEP)], axis=0)

        def rope(p):
            rot = jnp.where(first_half, pltpu.roll(p, WIDTH - HEAD_DIM // 2, 1), pltpu.roll(p, HEAD_DIM // 2, 1))
            return p * cosf + rot * sinf

        for i, ref in enumerate((qa_ref, ka_ref, va_ref)):
            p = jnp.dot(u, w_ref[:, i * WIDTH:(i + 1) * WIDTH], preferred_element_type=F32)
            if i == 0:
                p = rope(p) * qscale
            elif i == 1:
                p = rope(p)
            for a in range(RES_PER_STEP):
                for hp in range(N_PAIRS):
                    ref[q * RES_PER_STEP + a, hp] = (
                        p[a * PROJ_ROWS:(a + 1) * PROJ_ROWS, hp * LANES:(hp + 1) * LANES].astype(BF16))


def _input_projection_nat(x, mod3, w_in_bf, cos_v, sin_v, perm, mod_row0):
    B, T, D = x.shape
    N = B * T
    BL = N // RES
    tps = T // PROJ_TOKENS
    res_shape = jax.ShapeDtypeStruct((RES, N_PAIRS, BL, LANES), BF16)
    nat_shape = jax.ShapeDtypeStruct((N_PAIRS, N, LANES), BF16)
    res_spec = pl.BlockSpec((RES, N_PAIRS, PROJ_ROWS, LANES), lambda i: (0, 0, i, 0))
    nat_spec = pl.BlockSpec((N_PAIRS, PROJ_TOKENS, LANES), lambda i: (0, i, 0))
    return pl.pallas_call(
        _inproj_nat_kernel,
        grid=(N // PROJ_TOKENS,),
        in_specs=[pl.BlockSpec((PROJ_TOKENS, D), lambda i: (i, 0)),
                  pl.BlockSpec((None, 1, 6 * D), lambda i: (mod_row0 + i // tps, 0, 0)),
                  pl.BlockSpec((D, 6 * WIDTH), lambda i: (0, 0)),
                  pl.BlockSpec((PROJ_ROWS, RES * LANES), lambda i: (i % tps, 0)),
                  pl.BlockSpec((PROJ_ROWS, RES * LANES), lambda i: (i % tps, 0)),
                  pl.BlockSpec((PERM, PERM), lambda i: (0, 0))],
        out_specs=[res_spec, res_spec, res_spec, nat_spec, nat_spec, nat_spec],
        out_shape=[res_shape, res_shape, res_shape, nat_shape, nat_shape, nat_shape],
        scratch_shapes=[pltpu.VMEM((PROJ_TOKENS, D), BF16), pltpu.VMEM((RES, PROJ_ROWS, D), BF16)],
        compiler_params=_cparams(("arbitrary",)),
        name="input_projection",
    )(x.reshape(N, D), mod3, w_in_bf, cos_v, sin_v, perm)


def _rope_tables(T):
    half = HEAD_DIM // 2
    inv_freq = 1.0 / (ROPE_THETA ** (jnp.arange(half, dtype=F32) / half))
    ang = jnp.arange(T, dtype=F32)[:, None] * inv_freq[None, :]
    cos, sin = jnp.cos(ang), jnp.sin(ang)
    cos_h = jnp.concatenate([cos, cos], axis=-1)
    sin_h = jnp.concatenate([-sin, sin], axis=-1)
    cos2 = jnp.concatenate([cos_h, cos_h], axis=-1)
    sin2 = jnp.concatenate([sin_h, sin_h], axis=-1)
    L = T // RES
    return cos2.reshape(L, RES * LANES), sin2.reshape(L, RES * LANES)


P2_ROWS = 32
P2_KROWS = 64
P1_ROWS = 16
P1_KROWS = 32
P1_SHIFT = 8
HALO = 64
P3_BATCH = 8
P2_BATCH = 2
P1_BATCH = 2


def _band_tables(has_halo):
    def mask(ok):
        return np.where(ok, 0.0, NEG).astype(np.float32)
    mq = np.arange(CHUNK)[:, None]
    koff3 = np.arange(2 * CHUNK) - HALO
    b3 = mask(np.abs(mq - koff3[None, :]) <= RADIUS)
    j = np.repeat(np.arange(4), P2_ROWS)[:, None]
    a = np.tile(np.arange(P2_ROWS), 4)[:, None]
    jk = np.repeat(np.arange(4), P2_KROWS)[None, :]
    bk = np.tile(np.arange(P2_KROWS), 4)[None, :]
    b2 = mask(np.abs(4 * (a - (bk - 16)) + (j - jk)) <= RADIUS)
    koff2 = (bk - 16).reshape(-1)
    r = np.repeat(np.arange(RES), P1_ROWS)[:, None]
    a = np.tile(np.arange(P1_ROWS), RES)[:, None]
    rk = np.repeat(np.arange(RES), P1_KROWS)[None, :]
    bk = np.tile(np.arange(P1_KROWS), RES)[None, :]
    b1 = mask(np.abs(RES * (a - (bk - P1_SHIFT)) + (r - rk)) <= RADIUS)
    koff1 = (bk - P1_SHIFT).reshape(-1)
    return (jnp.asarray(b3), jnp.asarray(koff3.astype(np.int32)[None, :]),
            jnp.asarray(b2), jnp.asarray(koff2.astype(np.int32)[None, :]),
            jnp.asarray(b1), jnp.asarray(koff1.astype(np.int32)[None, :]))


def _attend_pairs(problems, head0):
    scores = []
    for q, k, _, _ in problems:
        for h in range(2):
            sel = head0 if h == 0 else jnp.logical_not(head0)
            qh = jnp.where(sel, q, jnp.zeros_like(q))
            scores.append(lax.dot_general(qh, k, (((1,), (1,)), ((), ())), preferred_element_type=F32))
    probs, stats = [], []
    for i, s in enumerate(scores):
        s = s + problems[i // 2][3]
        m = jnp.max(s, axis=-1, keepdims=True)
        p = jnp.exp2(s - m)
        stats.append((m, jnp.sum(p, axis=-1, keepdims=True)))
        probs.append(p.astype(BF16))
    outs = [jnp.dot(p, problems[i // 2][2], preferred_element_type=F32) for i, p in enumerate(probs)]
    results = []
    for i in range(len(problems)):
        (m0, l0), (m1, l1) = stats[2 * i], stats[2 * i + 1]
        results.append((jnp.where(head0, outs[2 * i], outs[2 * i + 1]),
                        jnp.where(head0, m0, m1), jnp.where(head0, l0, l1)))
    return results


def _dilated_kernel(*refs, has_halo, seq_rows, chunks_per_seq):
    if has_halo:
        (q_ref, k_ref, v_ref, kp_ref, kn_ref, vp_ref, vn_ref,
         b3_ref, o3_ref, b2_ref, o2_ref, b1_ref, o1_ref,
         out_ref, kf, vf, kf8, vf8, acc, ms, ls) = refs
    else:
        (q_ref, k_ref, v_ref, b3_ref, o3_ref, b2_ref, o2_ref, b1_ref, o1_ref,
         out_ref, kf, vf, kf8, vf8, acc, ms, ls) = refs

    c = pl.program_id(0) % chunks_per_seq
    row0 = c * CHUNK
    lane = lax.broadcasted_iota(I32, (1, LANES), 1)
    head0 = lane < HEAD_DIM

    for r in range(RES):
        kf[r, HALO:HALO + CHUNK, :] = k_ref[r]
        vf[r, HALO:HALO + CHUNK, :] = v_ref[r]
        if has_halo:
            kf[r, 0:HALO, :] = kp_ref[r]
            vf[r, 0:HALO, :] = vp_ref[r]
            kf[r, HALO + CHUNK:, :] = kn_ref[r]
            vf[r, HALO + CHUNK:, :] = vn_ref[r]
        else:
            zeros = jnp.zeros((HALO, LANES), BF16)
            kf[r, 0:HALO, :] = zeros
            vf[r, 0:HALO, :] = zeros
            kf[r, HALO + CHUNK:, :] = zeros
            vf[r, HALO + CHUNK:, :] = zeros
        for src, dst in ((kf, kf8), (vf, vf8)):
            w32 = src[r].astype(F32)
            dst[r] = jnp.concatenate([w32[P1_SHIFT:], w32[:P1_SHIFT]], axis=0).astype(BF16)

    def in_seq(base, off_ref):
        kv_row = base + off_ref[...]
        return jnp.where((kv_row >= 0) & (kv_row < seq_rows), 0.0, NEG).astype(F32)

    def attend(problems):
        return _attend_pairs(problems, head0)

    def merge(a_old, m_old, l_old, o, m, l):
        mn = jnp.maximum(m_old, m)
        wa = jnp.exp2(m_old - mn)
        wb = jnp.exp2(m - mn)
        return a_old * wa + o * wb, mn, l_old * wa + l * wb

    bias3 = b3_ref[...] + in_seq(row0, o3_ref)

    def body3(it, carry):
        rs = [it * P3_BATCH + u for u in range(P3_BATCH)]
        res = attend([(q_ref[r], kf[r], vf[r], bias3) for r in rs])
        for r, (o, m, l) in zip(rs, res):
            acc[r] = o
            ms[r] = m
            ls[r] = l
        return carry

    lax.fori_loop(0, RES // P3_BATCH, body3, 0)

    b2 = b2_ref[...]

    def body2(it, carry):
        def gather(ref, r4, start, rows):
            return jnp.concatenate([ref[4 * j + r4, pl.ds(start, rows), :] for j in range(4)], axis=0)

        where, problems = [], []
        for u in range(P2_BATCH):
            g = it * P2_BATCH + u
            qs = pl.multiple_of(g * P2_ROWS, P2_ROWS)
            ks = pl.multiple_of(HALO - 16 + g * P2_ROWS, 16)
            bias = b2 + in_seq(row0 + g * P2_ROWS, o2_ref)
            for r4 in range(4):
                where.append((r4, qs))
                problems.append((gather(q_ref, r4, qs, P2_ROWS), gather(kf, r4, ks, P2_KROWS),
                                 gather(vf, r4, ks, P2_KROWS), bias))
        for (r4, qs), (o, m, l) in zip(where, attend(problems)):
            a_new, m_new, l_new = merge(gather(acc, r4, qs, P2_ROWS), gather(ms, r4, qs, P2_ROWS),
                                        gather(ls, r4, qs, P2_ROWS), o, m, l)
            for j in range(4):
                acc[4 * j + r4, pl.ds(qs, P2_ROWS), :] = a_new[j * P2_ROWS:(j + 1) * P2_ROWS]
                ms[4 * j + r4, pl.ds(qs, P2_ROWS), :] = m_new[j * P2_ROWS:(j + 1) * P2_ROWS]
                ls[4 * j + r4, pl.ds(qs, P2_ROWS), :] = l_new[j * P2_ROWS:(j + 1) * P2_ROWS]
        return carry

    lax.fori_loop(0, CHUNK // P2_ROWS // P2_BATCH, body2, 0)

    b1 = b1_ref[...]

    def body1(it, carry):
        def gather(ref, start, rows):
            return jnp.concatenate([ref[r, pl.ds(start, rows), :] for r in range(RES)], axis=0)

        starts, problems = [], []
        for u in range(P1_BATCH):
            g = it * P1_BATCH + u
            qs = pl.multiple_of(g * P1_ROWS, P1_ROWS)
            ks = pl.multiple_of(HALO - 16 + g * P1_ROWS, 16)
            starts.append(qs)
            problems.append((gather(q_ref, qs, P1_ROWS), gather(kf8, ks, P1_KROWS), gather(vf8, ks, P1_KROWS),
                             b1 + in_seq(row0 + g * P1_ROWS, o1_ref)))
        for qs, (o, m, l) in zip(starts, attend(problems)):
            a_new, m_new, l_new = merge(gather(acc, qs, P1_ROWS), gather(ms, qs, P1_ROWS),
                                        gather(ls, qs, P1_ROWS), o, m, l)
            for r in range(RES):
                acc[r, pl.ds(qs, P1_ROWS), :] = a_new[r * P1_ROWS:(r + 1) * P1_ROWS]
                ms[r, pl.ds(qs, P1_ROWS), :] = m_new[r * P1_ROWS:(r + 1) * P1_ROWS]
                ls[r, pl.ds(qs, P1_ROWS), :] = l_new[r * P1_ROWS:(r + 1) * P1_ROWS]
        return carry

    lax.fori_loop(0, CHUNK // P1_ROWS // P1_BATCH, body1, 0)

    def body_out(r, carry):
        out_ref[r] = (acc[r] / ls[r]).astype(BF16)
        return carry

    lax.fori_loop(0, RES, body_out, 0)


def _dilated_attention(qa, ka, va, B, T):
    L = T // RES
    BL = B * L
    cps = L // CHUNK
    has_halo = cps > 1
    tables = _band_tables(has_halo)
    blk = (RES, None, CHUNK, LANES)
    center = pl.BlockSpec(blk, lambda i, hp: (0, hp, i, 0))
    in_specs = [center, center, center]
    args = [qa, ka, va]
    if has_halo:
        hblk = (RES, None, HALO, LANES)
        per = CHUNK // HALO

        def prev_map(i, hp):
            return (0, hp, jnp.maximum(per * i - 1, (i // cps) * cps * per), 0)

        def next_map(i, hp):
            return (0, hp, jnp.minimum(per * i + per, (i // cps + 1) * cps * per - 1), 0)

        in_specs += [pl.BlockSpec(hblk, prev_map), pl.BlockSpec(hblk, next_map),
                     pl.BlockSpec(hblk, prev_map), pl.BlockSpec(hblk, next_map)]
        args += [ka, ka, va, va]
    for t in tables:
        in_specs.append(pl.BlockSpec(t.shape, lambda i, hp: (0, 0)))
        args.append(t)
    kern = functools.partial(_dilated_kernel, has_halo=has_halo, seq_rows=L, chunks_per_seq=cps)
    return pl.pallas_call(
        kern,
        grid=(BL // CHUNK, N_PAIRS),
        in_specs=in_specs,
        out_specs=pl.BlockSpec(blk, lambda i, hp: (0, hp, i, 0)),
        out_shape=jax.ShapeDtypeStruct((RES, N_PAIRS, BL, LANES), BF16),
        scratch_shapes=[pltpu.VMEM((RES, 2 * CHUNK, LANES), BF16),
                        pltpu.VMEM((RES, 2 * CHUNK, LANES), BF16),
                        pltpu.VMEM((RES, 2 * CHUNK, LANES), BF16),
                        pltpu.VMEM((RES, 2 * CHUNK, LANES), BF16),
                        pltpu.VMEM((RES, CHUNK, LANES), F32),
                        pltpu.VMEM((RES, CHUNK, LANES), F32),
                        pltpu.VMEM((RES, CHUNK, LANES), F32)],
        compiler_params=_cparams(("arbitrary", "arbitrary")),
        name="dilated_attention",
    )(*args)


NA_KEYS = NA_ROWS * GRID_W
NA_BLOCK_ROWS = 32
NA_BATCH = 32


def _na_bias_table(rpb):
    c = np.arange(GRID_W)
    col_start = np.clip(c - NA_COLS // 2, 0, GRID_W - NA_COLS)
    col_mask = (c[None, :] >= col_start[:, None]) & (c[None, :] < col_start[:, None] + NA_COLS)
    dc_idx = np.clip(c[None, :] - c[:, None], -(NA_COLS - 1), NA_COLS - 1) + NA_COLS - 1
    rel = rpb.astype(F32)[:, :, dc_idx] * LOG2E
    rel = jnp.where(col_mask[None, None], rel, NEG)
    per_off = [rel[:, d0:d0 + NA_ROWS].transpose(0, 2, 1, 3).reshape(N_HEADS, GRID_W, NA_KEYS)
               for d0 in range(NA_ROWS)]
    return jnp.stack(per_off, axis=0).reshape(NA_ROWS, N_PAIRS, 2, GRID_W, NA_KEYS)


def _na_kernel(q_ref, k_ref, v_ref, bias_ref, out_ref, *, grid_rows, block_rows):
    gb = pl.program_id(2)
    lane = lax.broadcasted_iota(I32, (1, LANES), 1)
    head0 = lane < HEAD_DIM

    def body(it, carry):
        rows, scores = [], []
        for u in range(NA_BATCH):
            i = it * NA_BATCH + u
            g = gb * block_rows + i
            rs = jnp.clip(g - NA_ROWS // 2, 0, grid_rows - NA_ROWS)
            d0 = rs - g + NA_ROWS - 1
            qs = pl.multiple_of(i * GRID_W, GRID_W)
            ks = pl.multiple_of(rs * GRID_W, GRID_W)
            q = q_ref[pl.ds(qs, GRID_W), :]
            k = k_ref[pl.ds(ks, NA_KEYS), :]
            rows.append((qs, ks, d0))
            for h in range(2):
                sel = head0 if h == 0 else jnp.logical_not(head0)
                qh = jnp.where(sel, q, jnp.zeros_like(q))
                scores.append(lax.dot_general(qh, k, (((1,), (1,)), ((), ())), preferred_element_type=F32))
        probs, sums = [], []
        for n, s in enumerate(scores):
            s = s + bias_ref[rows[n // 2][2], n % 2]
            p = jnp.exp2(s - jnp.max(s, axis=-1, keepdims=True))
            sums.append(jnp.sum(p, axis=-1, keepdims=True))
            probs.append(p.astype(BF16))
        outs = []
        for n, p in enumerate(probs):
            v = v_ref[pl.ds(rows[n // 2][1], NA_KEYS), :]
            outs.append(jnp.dot(p, v, preferred_element_type=F32) / sums[n])
        for u, (qs, _, _) in enumerate(rows):
            out_ref[pl.ds(qs, GRID_W), :] = jnp.where(head0, outs[2 * u], outs[2 * u + 1]).astype(BF16)
        return carry

    lax.fori_loop(0, block_rows // NA_BATCH, body, 0)


def _neighbourhood_attention(qb, kb, vb, bias, B, T):
    G = T // GRID_W
    rb = min(NA_BLOCK_ROWS, G)
    nb = G // rb
    kern = functools.partial(_na_kernel, grid_rows=G, block_rows=rb)
    seq = pl.BlockSpec((None, T, LANES), lambda hp, b, gb: (hp, b, 0))
    qblk = pl.BlockSpec((None, rb * GRID_W, LANES), lambda hp, b, gb: (hp, b * nb + gb, 0))
    return pl.pallas_call(
        kern,
        grid=(N_PAIRS, B, nb),
        in_specs=[qblk, seq, seq,
                  pl.BlockSpec((NA_ROWS, None, 2, GRID_W, NA_KEYS), lambda hp, b, gb: (0, hp, 0, 0, 0))],
        out_specs=qblk,
        out_shape=jax.ShapeDtypeStruct((N_PAIRS, B * T, LANES), BF16),
        compiler_params=_cparams(("arbitrary", "arbitrary", "arbitrary")),
        name="neighbourhood_attention",
    )(qb, kb, vb, bias)


def _layer_norm(h, g, b):
    mu = jnp.mean(h, axis=-1, keepdims=True)
    d = h - mu
    var = jnp.mean(d * d, axis=-1, keepdims=True)
    return d * lax.rsqrt(var + LN_EPS) * g + b


def _outproj_nat_kernel(oa_ref, ob_ref, x_ref, mod_ref, ga_ref, gb_ref, w_ref, lg_ref, lb_ref, wr_ref, br_ref,
                        permt_ref, x1_ref, u2_ref, logit_ref, mix_scr):
    D = D_MODEL
    gate_a = mod_ref[:, 2 * D:3 * D]
    shift_f = mod_ref[:, 3 * D:4 * D]
    scale_f = mod_ref[:, 4 * D:5 * D]

    def rms(o, g):
        return o * lax.rsqrt(jnp.mean(o * o, axis=-1, keepdims=True) + RMS_EPS) * g

    rows_g = PERM // RES
    for g in range(PROJ_TOKENS // PERM):
        grouped = jnp.concatenate(
            [jnp.concatenate([oa_ref[r, hp, g * rows_g:(g + 1) * rows_g, :] for hp in range(N_PAIRS)], axis=1)
             for r in range(RES)], axis=0)
        oa = jnp.dot(permt_ref[...], grouped, preferred_element_type=F32)
        ob = jnp.concatenate([ob_ref[hp, g * PERM:(g + 1) * PERM, :] for hp in range(N_PAIRS)], axis=1).astype(F32)
        mix = jnp.concatenate([rms(oa, ga_ref[...]), rms(ob, gb_ref[...])], axis=1)
        mix_scr[g * PERM:(g + 1) * PERM, :] = mix.astype(BF16)

    wr = wr_ref[...]
    wr_hi = wr.astype(BF16)
    wr_lo = (wr - wr_hi.astype(F32)).astype(BF16)
    sub = 256
    for s in range(PROJ_TOKENS // sub):
        rows = slice(s * sub, (s + 1) * sub)
        y = jnp.dot(mix_scr[rows, :], w_ref[...], preferred_element_type=F32)
        h = DEEPNORM_ALPHA * x_ref[rows, :] + (1.0 + gate_a) * y
        x1 = _layer_norm(h, lg_ref[...], lb_ref[...])
        x1_ref[rows, :] = x1
        u2 = x1 * (1.0 + scale_f) + shift_f
        u2_ref[rows, :] = u2.astype(BF16)
        u_hi = u2.astype(BF16)
        u_lo = (u2 - u_hi.astype(F32)).astype(BF16)
        logit_ref[rows, :] = (jnp.dot(u_hi, wr_hi, preferred_element_type=F32)
                              + jnp.dot(u_lo, wr_hi, preferred_element_type=F32)
                              + jnp.dot(u_hi, wr_lo, preferred_element_type=F32)) + br_ref[...]


def _output_projection_nat(oa, ob, x, mod3, mod_row0, g_a, g_b, w_out_bf, ln_g, ln_b, w_router_pad, b_router_pad,
                           perm_t):
    B, T, D = x.shape
    N = B * T
    tps = T // PROJ_TOKENS
    const = lambda shape: pl.BlockSpec(shape, lambda i: tuple(0 for _ in shape))
    rows = lambda width: pl.BlockSpec((PROJ_TOKENS, width), lambda i: (i, 0))
    return pl.pallas_call(
        _outproj_nat_kernel,
        grid=(N // PROJ_TOKENS,),
        in_specs=[pl.BlockSpec((RES, N_PAIRS, PROJ_ROWS, LANES), lambda i: (0, 0, i, 0)),
                  pl.BlockSpec((N_PAIRS, PROJ_TOKENS, LANES), lambda i: (0, i, 0)),
                  rows(D),
                  pl.BlockSpec((None, 1, 6 * D), lambda i: (mod_row0 + i // tps, 0, 0)),
                  const((1, WIDTH)), const((1, WIDTH)), const((2 * WIDTH, D)),
                  const((1, D)), const((1, D)), const((D, LANES)), const((1, LANES)), const((PERM, PERM))],
        out_specs=[rows(D), rows(D), rows(LANES)],
        out_shape=[jax.ShapeDtypeStruct((N, D), F32), jax.ShapeDtypeStruct((N, D), BF16),
                   jax.ShapeDtypeStruct((N, LANES), F32)],
        scratch_shapes=[pltpu.VMEM((PROJ_TOKENS, 2 * WIDTH), BF16)],
        compiler_params=_cparams(("arbitrary",)),
        name="output_projection",
    )(oa, ob, x.reshape(N, D), mod3, g_a, g_b, w_out_bf, ln_g, ln_b, w_router_pad, b_router_pad, perm_t)


def _route_kernel(logit_ref, idx_ref, gate_ref, cnt_ref):
    for t in range(ROUTE_TILE // SCATTER_TILE):
        rows = slice(t * SCATTER_TILE, (t + 1) * SCATTER_TILE)
        sel, gates, cnt = _route_tile(logit_ref[rows, :])
        idx_ref[rows, :] = sel
        gate_ref[rows, :] = gates
        cnt_ref[t] = cnt


def _route_tile(raw):
    tn = raw.shape[0]
    lane = lax.broadcasted_iota(I32, (tn, LANES), 1)
    logits = jnp.where(lane < N_EXPERTS, raw, -3.0e38)
    vals, idxs = [], []
    multi = jnp.zeros((tn, LANES), F32)
    for _ in range(TOP_K):
        m = jnp.max(logits, axis=-1, keepdims=True)
        idx = jnp.min(jnp.where(logits == m, lane, LANES), axis=-1, keepdims=True)
        hot = lane == idx
        vals.append(m)
        idxs.append(idx)
        multi = multi + hot.astype(F32)
        logits = jnp.where(hot, -3.0e38, logits)
    es = [jnp.exp(v - vals[0]) for v in vals]
    tot = es[0] + es[1] + es[2] + es[3]
    lane4 = lax.broadcasted_iota(I32, (tn, TOP_K), 1)
    sel = jnp.zeros((tn, TOP_K), I32)
    gates = jnp.zeros((tn, TOP_K), F32)
    for k in range(TOP_K):
        sel = jnp.where(lane4 == k, idxs[k], sel)
        gates = jnp.where(lane4 == k, es[k] / tot, gates)
    return sel, gates, jnp.sum(multi, axis=0, keepdims=True).astype(I32)


def _routing(logits):
    N = logits.shape[0]
    tn = ROUTE_TILE
    nt = N // SCATTER_TILE
    per_step = ROUTE_TILE // SCATTER_TILE
    return pl.pallas_call(
        _route_kernel,
        grid=(N // tn,),
        in_specs=[pl.BlockSpec((tn, LANES), lambda i: (i, 0))],
        out_specs=[pl.BlockSpec((tn, TOP_K), lambda i: (i, 0)),
                   pl.BlockSpec((tn, TOP_K), lambda i: (i, 0)),
                   pl.BlockSpec((per_step, 1, LANES), lambda i: (i, 0, 0))],
        out_shape=[jax.ShapeDtypeStruct((N, TOP_K), I32), jax.ShapeDtypeStruct((N, TOP_K), F32),
                   jax.ShapeDtypeStruct((nt, 1, LANES), I32)],
        compiler_params=_cparams(("arbitrary",)),
        name="moe_routing",
    )(logits)


SEG = 8
STAGE_ROWS = SCATTER_TILE * TOP_K + N_EXPERTS * SEG


def _lane_prefix_exclusive(v):
    lane = lax.broadcasted_iota(I32, v.shape, 1)
    incl = v
    s = 1
    while s < LANES:
        incl = incl + jnp.where(lane >= s, pltpu.roll(incl, s, 1), 0.0)
        s *= 2
    return incl - v


def _tile_ranks(idx, tn):
    lane = lax.broadcasted_iota(I32, (tn, LANES), 1)
    hots = [lane == idx[:, k:k + 1] for k in range(TOP_K)]
    multi = jnp.zeros((tn, LANES), F32)
    for h in hots:
        multi = multi + h.astype(F32)
    row = lax.broadcasted_iota(I32, (tn, tn), 0)
    col = lax.broadcasted_iota(I32, (tn, tn), 1)
    lower = (col < row).astype(BF16)
    before = jnp.dot(lower, multi.astype(BF16), preferred_element_type=F32)
    return hots, before


BIG_SEG = 4 * SEG


def _segment_dma_loops(cnt_s, rows_of, copy):
    def per_expert(e, carry):
        off, n_big, n_small = carry
        nseg = (cnt_s[0, e] + SEG - 1) // SEG
        big = nseg // (BIG_SEG // SEG)
        small = nseg - big * (BIG_SEG // SEG)

        def one_big(q, c):
            src, dst = rows_of(e, off, q * BIG_SEG)
            copy(src, dst, BIG_SEG).start()
            return c

        def one_small(q, c):
            src, dst = rows_of(e, off, big * BIG_SEG + q * SEG)
            copy(src, dst, SEG).start()
            return c

        lax.fori_loop(0, big, one_big, 0)
        lax.fori_loop(0, small, one_small, 0)
        return off + nseg * SEG, n_big + big, n_small + small

    _, n_big, n_small = lax.fori_loop(0, N_EXPERTS, per_expert, (0, 0, 0))
    return n_big, n_small


def _drain(copy, n_big, n_small):
    def wait_big(q, c):
        copy(0, 0, BIG_SEG).wait()
        return c

    def wait_small(q, c):
        copy(0, 0, SEG).wait()
        return c

    lax.fori_loop(0, n_big, wait_big, 0)
    lax.fori_loop(0, n_small, wait_small, 0)


def _dispatch_kernel(*refs, first, n_steps):
    if first:
        cnt_s, start_s, tail_s, idx_ref, u_ref, cntv_ref, xs_ref, stage, zbuf, pending, sem, zsem = refs
    else:
        cnt_s, start_s, tail_s, idx_ref, u_ref, cntv_ref, _, xs_ref, stage, zbuf, pending, sem, zsem = refs
    tn = idx_ref.shape[0]
    step = pl.program_id(0)
    slot = step % 2

    if first:
        @pl.when(pl.program_id(0) == 0)
        def _():
            zbuf[...] = jnp.zeros_like(zbuf)

            def tail_copy(row):
                return pltpu.make_async_copy(zbuf, xs_ref.at[pl.ds(pl.multiple_of(row, SEG), SEG)], zsem)

            def fill(e, n):
                lo = tail_s[0, e]
                nfull = (tail_s[1, e] - lo) // SEG

                def one(q, c):
                    tail_copy(lo + q * SEG).start()
                    return c

                lax.fori_loop(0, nfull, one, 0)
                return n + nfull

            n = lax.fori_loop(0, N_EXPERTS, fill, 0)

            def drain(q, c):
                tail_copy(0).wait()
                return c

            lax.fori_loop(0, n, drain, 0)

    hots, before = _tile_ranks(idx_ref[...], tn)
    cntf = cntv_ref[...].astype(F32)
    seg_len = jnp.ceil(cntf * (1.0 / SEG)) * SEG
    seg_off = _lane_prefix_exclusive(seg_len)
    lane_r = lax.broadcasted_iota(I32, (tn, STAGE_ROWS), 1)
    onehot = jnp.zeros((tn, STAGE_ROWS), F32)
    for k in range(TOP_K):
        stage_row = jnp.sum(jnp.where(hots[k], seg_off + before, 0.0), axis=-1, keepdims=True)
        onehot = jnp.where(lane_r == stage_row.astype(I32), 1.0, onehot)
    stage[slot] = lax.dot_general(onehot.astype(BF16), u_ref[...].astype(BF16), (((0,), (0,)), ((), ())),
                                  preferred_element_type=F32)

    def copier(buf):
        def seg_copy(src_row, dst_row, rows):
            return pltpu.make_async_copy(stage.at[buf, pl.ds(pl.multiple_of(src_row, SEG), rows)],
                                         xs_ref.at[pl.ds(pl.multiple_of(dst_row, SEG), rows)], sem.at[buf])
        return seg_copy

    @pl.when(step > 0)
    def _():
        _drain(copier(1 - slot), pending[0], pending[1])

    n_big, n_small = _segment_dma_loops(
        cnt_s, lambda e, off, moved: (off + moved, start_s[0, e] + moved), copier(slot))
    pending[0] = n_big
    pending[1] = n_small

    @pl.when(step == n_steps - 1)
    def _():
        _drain(copier(slot), n_big, n_small)


def _dispatch(idx, u2, tile_cnt, tile_start, tails, xs, xs_rows, t0):
    n, D = u2.shape
    tn = SCATTER_TILE
    nt = n // tn
    first = xs is None
    smem = lambda shape, imap: pl.BlockSpec(shape, imap, memory_space=pltpu.SMEM)
    kern = functools.partial(_dispatch_kernel, first=first, n_steps=nt)
    in_specs = [smem((None, 1, LANES), lambda i: (t0 + i, 0, 0)),
                smem((None, 1, LANES), lambda i: (t0 + i, 0, 0)),
                smem((2, LANES), lambda i: (0, 0)),
                pl.BlockSpec((tn, TOP_K), lambda i: (t0 + i, 0)),
                pl.BlockSpec((tn, D), lambda i: (i, 0)),
                pl.BlockSpec((None, 1, LANES), lambda i: (t0 + i, 0, 0))]
    args = [tile_cnt, tile_start, tails, idx, u2, tile_cnt]
    alias = {}
    if not first:
        alias = {len(args): 0}
        in_specs.append(pl.BlockSpec(memory_space=pl.ANY))
        args.append(xs)
    return pl.pallas_call(
        kern,
        grid=(nt,),
        in_specs=in_specs,
        out_specs=pl.BlockSpec(memory_space=pl.ANY),
        out_shape=jax.ShapeDtypeStruct((xs_rows, D), F32),
        scratch_shapes=[pltpu.VMEM((2, STAGE_ROWS, D), F32), pltpu.VMEM((SEG, D), F32), pltpu.SMEM((2,), I32),
                        pltpu.SemaphoreType.DMA((2,)), pltpu.SemaphoreType.DMA(())],
        input_output_aliases=alias,
        compiler_params=_cparams(("arbitrary",)),
        name="moe_dispatch",
    )(*args)


DEINT = 256


def _deinterleave_matrix():
    perm = np.zeros((DEINT, DEINT), np.float32)
    j = np.arange(DEINT // 2)
    perm[2 * j, j] = 1.0
    perm[2 * j + 1, DEINT // 2 + j] = 1.0
    return perm


def _expert_kernel(be_ref, nu_ref, x_ref, wu_ref, bu_ref, wd_ref, bd_ref, p_ref, y_ref, wu_bf, wd_bf):
    j = pl.program_id(0)

    @pl.when((j == 0) | (be_ref[j] != be_ref[jnp.maximum(j - 1, 0)]))
    def _():
        half = DEINT // 2
        for g in range(2 * D_FF // DEINT):
            wg = wu_ref[:, g * DEINT:(g + 1) * DEINT].astype(BF16)
            t = jnp.dot(wg, p_ref[...], preferred_element_type=F32)
            wu_bf[:, g * half:(g + 1) * half] = t[:, :half].astype(BF16)
            wu_bf[:, D_FF + g * half:D_FF + (g + 1) * half] = t[:, half:].astype(BF16)
        wd_bf[...] = wd_ref[...].astype(BF16)

    @pl.when(j < nu_ref[0])
    def _():
        x = x_ref[...].astype(BF16)
        hu = jnp.dot(x, wu_bf[...], preferred_element_type=F32) + bu_ref[...]
        glu = jnp.minimum(hu[:, :D_FF], SWIGLU_LIMIT)
        lin = jnp.clip(hu[:, D_FF:], -SWIGLU_LIMIT, SWIGLU_LIMIT)
        act = glu * (1.0 / (1.0 + jnp.exp(-SWIGLU_ALPHA * glu))) * (lin + 1.0)
        y_ref[...] = jnp.dot(act.astype(BF16), wd_bf[...], preferred_element_type=F32) + bd_ref[...]

    @pl.when(pl.program_id(0) >= nu_ref[0])
    def _():
        y_ref[...] = jnp.zeros_like(y_ref)


def _expert_blocks(xs, blk_expert, n_used, w_up, b_up_de, w_down, b_down):
    n_rows, D = xs.shape
    nblk = n_rows // MOE_ROWS

    def xmap(j, be, nu):
        return (jnp.minimum(j, nu[0] - 1), 0)

    grid_spec = pltpu.PrefetchScalarGridSpec(
        num_scalar_prefetch=2,
        grid=(nblk,),
        in_specs=[pl.BlockSpec((MOE_ROWS, D), xmap),
                  pl.BlockSpec((None, D, 2 * D_FF), lambda j, be, nu: (be[j], 0, 0)),
                  pl.BlockSpec((None, 1, 2 * D_FF), lambda j, be, nu: (be[j], 0, 0)),
                  pl.BlockSpec((None, D_FF, D), lambda j, be, nu: (be[j], 0, 0)),
                  pl.BlockSpec((None, 1, D), lambda j, be, nu: (be[j], 0, 0)),
                  pl.BlockSpec((DEINT, DEINT), lambda j, be, nu: (0, 0))],
        out_specs=pl.BlockSpec((MOE_ROWS, D), lambda j, be, nu: (j, 0)),
        scratch_shapes=[pltpu.VMEM((D, 2 * D_FF), BF16), pltpu.VMEM((D_FF, D), BF16)],
    )
    return pl.pallas_call(
        _expert_kernel,
        grid_spec=grid_spec,
        out_shape=jax.ShapeDtypeStruct((n_rows, D), F32),
        compiler_params=_cparams(("arbitrary",)),
        name="moe_experts",
    )(blk_expert, n_used, xs, w_up, b_up_de, w_down, b_down, jnp.asarray(_deinterleave_matrix(), BF16))


def _combine_kernel(cnt_s, start_s, idx_ref, gate_ref, cntv_ref, y_ref, x1_ref, mod_ref,
                    lg_ref, lb_ref, out_ref, stage, sem):
    D = D_MODEL
    tn = idx_ref.shape[0]

    def seg_copy(src_row, dst_row, rows):
        return pltpu.make_async_copy(y_ref.at[pl.ds(pl.multiple_of(src_row, SEG), rows)],
                                     stage.at[pl.ds(pl.multiple_of(dst_row, SEG), rows)], sem)

    @pl.when(pl.program_id(0) == 0)
    def _():
        stage[...] = jnp.zeros_like(stage)

    n_big, n_small = _segment_dma_loops(
        cnt_s, lambda e, off, moved: (start_s[0, e] + moved, off + moved), seg_copy)

    hots, before = _tile_ranks(idx_ref[...], tn)
    seg_len = jnp.ceil(cntv_ref[...].astype(F32) * (1.0 / SEG)) * SEG
    seg_off = _lane_prefix_exclusive(seg_len)
    gates = gate_ref[...]
    lane_r = lax.broadcasted_iota(I32, (tn, STAGE_ROWS), 1)
    weights = jnp.zeros((tn, STAGE_ROWS), F32)
    for k in range(TOP_K):
        stage_row = jnp.sum(jnp.where(hots[k], seg_off + before, 0.0), axis=-1, keepdims=True)
        weights = jnp.where(lane_r == stage_row.astype(I32), gates[:, k:k + 1], weights)

    _drain(seg_copy, n_big, n_small)

    y = jnp.dot(weights.astype(BF16), stage[...].astype(BF16), preferred_element_type=F32)
    gate_f = mod_ref[:, 5 * D:6 * D]
    h = DEEPNORM_ALPHA * x1_ref[...] + (1.0 + gate_f) * y
    out_ref[...] = _layer_norm(h, lg_ref[...], lb_ref[...])


def _combine(idx, gates, tile_cnt, tile_start, y_pad, x1, mod3, mod_row0, ln_g, ln_b, B, T, t0):
    D = D_MODEL
    tn = SCATTER_TILE
    n_rows = B * T
    nt = n_rows // tn
    tps = T // tn
    smem = lambda imap: pl.BlockSpec((None, 1, LANES), imap, memory_space=pltpu.SMEM)
    out = pl.pallas_call(
        _combine_kernel,
        grid=(nt,),
        in_specs=[smem(lambda i: (t0 + i, 0, 0)), smem(lambda i: (t0 + i, 0, 0)),
                  pl.BlockSpec((tn, TOP_K), lambda i: (t0 + i, 0)),
                  pl.BlockSpec((tn, TOP_K), lambda i: (t0 + i, 0)),
                  pl.BlockSpec((None, 1, LANES), lambda i: (t0 + i, 0, 0)),
                  pl.BlockSpec(memory_space=pl.ANY),
                  pl.BlockSpec((tn, D), lambda i: (i, 0)),
                  pl.BlockSpec((None, 1, 6 * D), lambda i: (mod_row0 + i // tps, 0, 0)),
                  pl.BlockSpec((1, D), lambda i: (0, 0)),
                  pl.BlockSpec((1, D), lambda i: (0, 0))],
        out_specs=pl.BlockSpec((tn, D), lambda i: (i, 0)),
        out_shape=jax.ShapeDtypeStruct((n_rows, D), F32),
        scratch_shapes=[pltpu.VMEM((STAGE_ROWS, D), F32), pltpu.SemaphoreType.DMA(())],
        compiler_params=_cparams(("arbitrary",)),
        name="moe_combine",
    )(tile_cnt, tile_start, idx, gates, tile_cnt, y_pad, x1, mod3, ln_g, ln_b)
    return out.reshape(B, T, D)


def kernel(x_prompt, x_sample, c_prompt, c_sample, w_mod, b_mod, w_in, rpb, g_out_a, g_out_b, w_out, ln1_g, ln1_b,
           w_router, b_router, w_up, b_up, w_down, b_down, ln2_g, ln2_b):
    D = D_MODEL
    groups = [x_prompt, x_sample]
    conds = [c_prompt, c_sample]
    n_cond = sum(c.shape[0] for c in conds)
    pad_rows = -n_cond % 8
    c_all = jnp.concatenate(conds + [jnp.zeros((pad_rows, D), F32)], axis=0)
    mod3 = _modulation(c_all, w_mod[0], b_mod[0]).reshape(n_cond + pad_rows, 1, 6 * D)

    w_in_bf = w_in[0].astype(BF16)
    w_out_bf = w_out[0].astype(BF16)
    na_bias = _na_bias_table(rpb[0])
    g_a = g_out_a[0].reshape(1, WIDTH)
    g_b = g_out_b[0].reshape(1, WIDTH)
    ln1g, ln1b = ln1_g[0].reshape(1, D), ln1_b[0].reshape(1, D)
    ln2g, ln2b = ln2_g[0].reshape(1, D), ln2_b[0].reshape(1, D)
    w_router_pad = jnp.pad(w_router[0], ((0, 0), (0, LANES - N_EXPERTS)))
    b_router_pad = jnp.pad(b_router[0], (0, LANES - N_EXPERTS)).reshape(1, LANES)

    perm_np = _residue_permutation()
    perm, perm_t = jnp.asarray(perm_np, BF16), jnp.asarray(perm_np.T, BF16)

    x1s, u2s, logit_list = [], [], []
    mod_row0 = 0
    mod_rows = []
    for x in groups:
        B, T, _ = x.shape
        cos_v, sin_v = _rope_tables(T)
        qa, ka, va, qb, kb, vb = _input_projection_nat(x, mod3, w_in_bf, cos_v, sin_v, perm, mod_row0)
        oa = _dilated_attention(qa, ka, va, B, T)
        ob = _neighbourhood_attention(qb, kb, vb, na_bias, B, T)
        x1, u2, logits = _output_projection_nat(oa, ob, x, mod3, mod_row0, g_a, g_b, w_out_bf, ln1g, ln1b,
                                                w_router_pad, b_router_pad, perm_t)
        x1s.append(x1)
        u2s.append(u2)
        logit_list.append(logits)
        mod_rows.append(mod_row0)
        mod_row0 += B

    logits_all = jnp.concatenate(logit_list, axis=0)
    N = logits_all.shape[0]
    idx, gates, tile_cnt = _routing(logits_all)

    tc = tile_cnt[:, 0, :N_EXPERTS]
    seg = (tc + SEG - 1) // SEG * SEG
    total = jnp.sum(seg, axis=0)
    padded = (total + MOE_ROWS - 1) // MOE_ROWS * MOE_ROWS
    pad_end = jnp.cumsum(padded)
    pad_start = pad_end - padded
    tile_start = pad_start[None, :] + jnp.cumsum(seg, axis=0) - seg
    n_tiles = N // SCATTER_TILE
    n_blocks = (N * TOP_K + n_tiles * N_EXPERTS * (SEG - 1)) // MOE_ROWS + N_EXPERTS
    n_rows = n_blocks * MOE_ROWS
    blk_end = pad_end // MOE_ROWS
    n_used = blk_end[-1:].astype(I32)
    blk_expert = jnp.minimum(
        jnp.sum(blk_end[None, :] <= jnp.arange(n_blocks, dtype=I32)[:, None], axis=1), N_EXPERTS - 1).astype(I32)
    lane_pad = ((0, 0), (0, LANES - N_EXPERTS))
    tile_start3 = jnp.pad(tile_start, lane_pad).astype(I32)[:, None, :]
    tails = jnp.pad(jnp.stack([pad_start + total, pad_end]), lane_pad).astype(I32)

    xs = None
    row0 = 0
    for u2 in u2s:
        n = u2.shape[0]
        xs = _dispatch(idx, u2, tile_cnt, tile_start3, tails, xs, n_rows, row0 // SCATTER_TILE)
        row0 += n

    b_up_de = jnp.concatenate([b_up[0][:, 0::2], b_up[0][:, 1::2]], axis=-1).reshape(N_EXPERTS, 1, 2 * D_FF)
    b_down3 = b_down[0].reshape(N_EXPERTS, 1, D)
    y_pad = _expert_blocks(xs, blk_expert, n_used, w_up[0], b_up_de, w_down[0], b_down3)

    outs = []
    row0 = 0
    for x, x1, mrow in zip(groups, x1s, mod_rows):
        B, T, _ = x.shape
        n = B * T
        outs.append(_combine(idx, gates, tile_cnt, tile_start3, y_pad, x1, mod3, mrow, ln2g, ln2b, B, T,
                             row0 // SCATTER_TILE))
        row0 += n
    return tuple(outs)
```

```python
import functools

import numpy as np
import jax
import jax.numpy as jnp
from jax import lax
from jax.experimental import pallas as pl
from jax.experimental.pallas import tpu as pltpu

F32 = jnp.float32
BF16 = jnp.bfloat16
I32 = jnp.int32

D_MODEL = 1024
HEAD_DIM = 64
N_HEADS = 8
WIDTH = N_HEADS * HEAD_DIM
N_PAIRS = WIDTH // 128
ROPE_THETA = 10000.0
RADIUS = 64
GRID_W = 64
NA_ROWS = 8
NA_COLS = 16
N_EXPERTS = 32
TOP_K = 4
D_FF = 1024
SWIGLU_ALPHA = 1.702
SWIGLU_LIMIT = 7.0
DEEPNORM_ALPHA = 2.0 ** 0.25
LN_EPS = 1e-5
RMS_EPS = 1e-6
NEG = -1e30
LOG2E = 1.4426950408889634

LANES = 128
RES = 16
CHUNK = 128
RES_PER_STEP = 4
MOE_ROWS = 512
ROUTE_TILE = 1024
SCATTER_TILE = 512
VMEM_LIMIT = 56 * 1024 * 1024


def _cparams(sem, vmem=VMEM_LIMIT):
    return pltpu.CompilerParams(dimension_semantics=sem, vmem_limit_bytes=vmem)


def _mod_kernel(c_ref, w_ref, b_ref, o_ref):
    c = c_ref[...]
    s = c * (1.0 / (1.0 + jnp.exp(-c)))
    o_ref[...] = jnp.dot(s.astype(BF16), w_ref[...].astype(BF16), preferred_element_type=F32) + b_ref[...]


def _modulation(c, w_mod, b_mod):
    rows = c.shape[0]
    n_out = w_mod.shape[1]
    tn = 1024
    return pl.pallas_call(
        _mod_kernel,
        grid=(n_out // tn,),
        in_specs=[pl.BlockSpec((rows, D_MODEL), lambda j: (0, 0)),
                  pl.BlockSpec((D_MODEL, tn), lambda j: (0, j)),
                  pl.BlockSpec((1, tn), lambda j: (0, j))],
        out_specs=pl.BlockSpec((rows, tn), lambda j: (0, j)),
        out_shape=jax.ShapeDtypeStruct((rows, n_out), F32),
        compiler_params=_cparams(("arbitrary",)),
        name="modulation",
    )(c, w_mod, b_mod.reshape(1, n_out))


PERM = 256
PROJ_TOKENS = 1024
PROJ_ROWS = PROJ_TOKENS // RES


def _residue_permutation():
    p = np.zeros((PERM, PERM), np.float32)
    m, r = np.meshgrid(np.arange(PERM // RES), np.arange(RES), indexing="ij")
    p[(r * (PERM // RES) + m).reshape(-1), (m * RES + r).reshape(-1)] = 1.0
    return p


def _inproj_nat_kernel(x_ref, mod_ref, w_ref, cos_ref, sin_ref, perm_ref,
                       qa_ref, ka_ref, va_ref, qb_ref, kb_ref, vb_ref, u_nat, u_view):
    D = D_MODEL
    shift = mod_ref[:, 0:D]
    scale = mod_ref[:, D:2 * D]
    qscale = HEAD_DIM ** -0.5 * LOG2E
    u_nat[...] = (x_ref[...] * (1.0 + scale) + shift).astype(BF16)

    sub = 512
    for s in range(PROJ_TOKENS // sub):
        u = u_nat[s * sub:(s + 1) * sub, :]
        for i, (ref, mul) in enumerate(((qb_ref, qscale), (kb_ref, 1.0), (vb_ref, 1.0))):
            p = jnp.dot(u, w_ref[:, (3 + i) * WIDTH:(4 + i) * WIDTH], preferred_element_type=F32) * mul
            for hp in range(N_PAIRS):
                ref[hp, s * sub:(s + 1) * sub, :] = p[:, hp * LANES:(hp + 1) * LANES].astype(BF16)

    rows_g = PERM // RES
    for g in range(PROJ_TOKENS // PERM):
        pv = jnp.dot(perm_ref[...], u_nat[g * PERM:(g + 1) * PERM, :], preferred_element_type=F32).astype(BF16)
        for r in range(RES):
            u_view[r, g * rows_g:(g + 1) * rows_g, :] = pv[r * rows_g:(r + 1) * rows_g]

    lane = lax.broadcasted_iota(I32, (1, WIDTH), 1)
    first_half = (lane % HEAD_DIM) < (HEAD_DIM // 2)
    for q in range(RES // RES_PER_STEP):
        u = jnp.concatenate([u_view[q * RES_PER_STEP + a] for a in range(RES_PER_STEP)], axis=0)
        lo = q * RES_PER_STEP * LANES
        cosf = jnp.concatenate([jnp.tile(cos_ref[:, lo + a * LANES:lo + (a + 1) * LANES], (1, N_PAIRS))
                                for a in range(RES_PER_STEP)], axis=0)
        sinf = jnp.concatenate([jnp.tile(sin_ref[:, lo + a * LANES:lo + (a + 1) * LANES], (1, N_PAIRS))
                                for a in range(RES_PER_STEP)], axis=0)

        def rope(p):
            rot = jnp.where(first_half, pltpu.roll(p, WIDTH - HEAD_DIM // 2, 1), pltpu.roll(p, HEAD_DIM // 2, 1))
            return p * cosf + rot * sinf

        for i, ref in enumerate((qa_ref, ka_ref, va_ref)):
            p = jnp.dot(u, w_ref[:, i * WIDTH:(i + 1) * WIDTH], preferred_element_type=F32)
            if i == 0:
                p = rope(p) * qscale
            elif i == 1:
                p = rope(p)
            for a in range(RES_PER_STEP):
                for hp in range(N_PAIRS):
                    ref[q * RES_PER_STEP + a, hp] = (
                        p[a * PROJ_ROWS:(a + 1) * PROJ_ROWS, hp * LANES:(hp + 1) * LANES].astype(BF16))


def _input_projection_nat(x, mod3, w_in_bf, cos_v, sin_v, perm, mod_row0):
    B, T, D = x.shape
    N = B * T
    BL = N // RES
    tps = T // PROJ_TOKENS
    res_shape = jax.ShapeDtypeStruct((RES, N_PAIRS, BL, LANES), BF16)
    nat_shape = jax.ShapeDtypeStruct((N_PAIRS, N, LANES), BF16)
    res_spec = pl.BlockSpec((RES, N_PAIRS, PROJ_ROWS, LANES), lambda i: (0, 0, i, 0))
    nat_spec = pl.BlockSpec((N_PAIRS, PROJ_TOKENS, LANES), lambda i: (0, i, 0))
    return pl.pallas_call(
        _inproj_nat_kernel,
        grid=(N // PROJ_TOKENS,),
        in_specs=[pl.BlockSpec((PROJ_TOKENS, D), lambda i: (i, 0)),
                  pl.BlockSpec((None, 1, 6 * D), lambda i: (mod_row0 + i // tps, 0, 0)),
                  pl.BlockSpec((D, 6 * WIDTH), lambda i: (0, 0)),
                  pl.BlockSpec((PROJ_ROWS, RES * LANES), lambda i: (i % tps, 0)),
                  pl.BlockSpec((PROJ_ROWS, RES * LANES), lambda i: (i % tps, 0)),
                  pl.BlockSpec((PERM, PERM), lambda i: (0, 0))],
        out_specs=[res_spec, res_spec, res_spec, nat_spec, nat_spec, nat_spec],
        out_shape=[res_shape, res_shape, res_shape, nat_shape, nat_shape, nat_shape],
        scratch_shapes=[pltpu.VMEM((PROJ_TOKENS, D), BF16), pltpu.VMEM((RES, PROJ_ROWS, D), BF16)],
        compiler_params=_cparams(("arbitrary",)),
        name="input_projection",
    )(x.reshape(N, D), mod3, w_in_bf, cos_v, sin_v, perm)


def _rope_tables(T):
    half = HEAD_DIM // 2
    inv_freq = 1.0 / (ROPE_THETA ** (jnp.arange(half, dtype=F32) / half))
    ang = jnp.arange(T, dtype=F32)[:, None] * inv_freq[None, :]
    cos, sin = jnp.cos(ang), jnp.sin(ang)
    cos_h = jnp.concatenate([cos, cos], axis=-1)
    sin_h = jnp.concatenate([-sin, sin], axis=-1)
    cos2 = jnp.concatenate([cos_h, cos_h], axis=-1)
    sin2 = jnp.concatenate([sin_h, sin_h], axis=-1)
    L = T // RES
    return cos2.reshape(L, RES * LANES), sin2.reshape(L, RES * LANES)


P2_ROWS = 32
P2_KROWS = 64
P1_ROWS = 16
P1_KROWS = 32
P1_SHIFT = 8
HALO = 64
P3_BATCH = 8
P2_BATCH = 2
P1_BATCH = 2


def _band_tables(has_halo):
    def mask(ok):
        return np.where(ok, 0.0, NEG).astype(np.float32)
    mq = np.arange(CHUNK)[:, None]
    koff3 = np.arange(2 * CHUNK) - HALO
    b3 = mask(np.abs(mq - koff3[None, :]) <= RADIUS)
    j = np.repeat(np.arange(4), P2_ROWS)[:, None]
    a = np.tile(np.arange(P2_ROWS), 4)[:, None]
    jk = np.repeat(np.arange(4), P2_KROWS)[None, :]
    bk = np.tile(np.arange(P2_KROWS), 4)[None, :]
    b2 = mask(np.abs(4 * (a - (bk - 16)) + (j - jk)) <= RADIUS)
    koff2 = (bk - 16).reshape(-1)
    r = np.repeat(np.arange(RES), P1_ROWS)[:, None]
    a = np.tile(np.arange(P1_ROWS), RES)[:, None]
    rk = np.repeat(np.arange(RES), P1_KROWS)[None, :]
    bk = np.tile(np.arange(P1_KROWS), RES)[None, :]
    b1 = mask(np.abs(RES * (a - (bk - P1_SHIFT)) + (r - rk)) <= RADIUS)
    koff1 = (bk - P1_SHIFT).reshape(-1)
    return (jnp.asarray(b3), jnp.asarray(koff3.astype(np.int32)[None, :]),
            jnp.asarray(b2), jnp.asarray(koff2.astype(np.int32)[None, :]),
            jnp.asarray(b1), jnp.asarray(koff1.astype(np.int32)[None, :]))


def _attend_pairs(problems, head0):
    scores = []
    for q, k, _, _ in problems:
        for h in range(2):
            sel = head0 if h == 0 else jnp.logical_not(head0)
            qh = jnp.where(sel, q, jnp.zeros_like(q))
            scores.append(lax.dot_general(qh, k, (((1,), (1,)), ((), ())), preferred_element_type=F32))
    probs, stats = [], []
    for i, s in enumerate(scores):
        s = s + problems[i // 2][3]
        m = jnp.max(s, axis=-1, keepdims=True)
        p = jnp.exp2(s - m)
        stats.append((m, jnp.sum(p, axis=-1, keepdims=True)))
        probs.append(p.astype(BF16))
    outs = [jnp.dot(p, problems[i // 2][2], preferred_element_type=F32) for i, p in enumerate(probs)]
    results = []
    for i in range(len(problems)):
        (m0, l0), (m1, l1) = stats[2 * i], stats[2 * i + 1]
        results.append((jnp.where(head0, outs[2 * i], outs[2 * i + 1]),
                        jnp.where(head0, m0, m1), jnp.where(head0, l0, l1)))
    return results


def _dilated_kernel(*refs, has_halo, seq_rows, chunks_per_seq):
    if has_halo:
        (q_ref, k_ref, v_ref, kp_ref, kn_ref, vp_ref, vn_ref,
         b3_ref, o3_ref, b2_ref, o2_ref, b1_ref, o1_ref,
         out_ref, kf, vf, kf8, vf8, acc, ms, ls) = refs
    else:
        (q_ref, k_ref, v_ref, b3_ref, o3_ref, b2_ref, o2_ref, b1_ref, o1_ref,
         out_ref, kf, vf, kf8, vf8, acc, ms, ls) = refs

    c = pl.program_id(0) % chunks_per_seq
    row0 = c * CHUNK
    lane = lax.broadcasted_iota(I32, (1, LANES), 1)
    head0 = lane < HEAD_DIM

    for r in range(RES):
        kf[r, HALO:HALO + CHUNK, :] = k_ref[r]
        vf[r, HALO:HALO + CHUNK, :] = v_ref[r]
        if has_halo:
            kf[r, 0:HALO, :] = kp_ref[r]
            vf[r, 0:HALO, :] = vp_ref[r]
            kf[r, HALO + CHUNK:, :] = kn_ref[r]
            vf[r, HALO + CHUNK:, :] = vn_ref[r]
        else:
            zeros = jnp.zeros((HALO, LANES), BF16)
            for ref in (kf, vf):
                ref[r, 0:HALO, :] = zeros
                ref[r, HALO + CHUNK:, :] = zeros
        for src, dst in ((kf, kf8), (vf, vf8)):
            w32 = src[r].astype(F32)
            dst[r] = jnp.concatenate([w32[P1_SHIFT:], w32[:P1_SHIFT]], axis=0).astype(BF16)

    def in_seq(base, off_ref):
        kv_row = base + off_ref[...]
        return jnp.where((kv_row >= 0) & (kv_row < seq_rows), 0.0, NEG).astype(F32)

    def attend(problems):
        return _attend_pairs(problems, head0)

    def merge(a_old, m_old, l_old, o, m, l):
        mn = jnp.maximum(m_old, m)
        wa = jnp.exp2(m_old - mn)
        wb = jnp.exp2(m - mn)
        return a_old * wa + o * wb, mn, l_old * wa + l * wb

    bias3 = b3_ref[...] + in_seq(row0, o3_ref)

    def body3(it, carry):
        rs = [it * P3_BATCH + u for u in range(P3_BATCH)]
        res = attend([(q_ref[r], kf[r], vf[r], bias3) for r in rs])
        for r, (o, m, l) in zip(rs, res):
            acc[r] = o
            ms[r] = m
            ls[r] = l
        return carry

    lax.fori_loop(0, RES // P3_BATCH, body3, 0)

    b2 = b2_ref[...]

    def body2(it, carry):
        def gather(ref, r4, start, rows):
            return jnp.concatenate([ref[4 * j + r4, pl.ds(start, rows), :] for j in range(4)], axis=0)

        where, problems = [], []
        for u in range(P2_BATCH):
            g = it * P2_BATCH + u
            qs = pl.multiple_of(g * P2_ROWS, P2_ROWS)
            ks = pl.multiple_of(HALO - 16 + g * P2_ROWS, 16)
            bias = b2 + in_seq(row0 + g * P2_ROWS, o2_ref)
            for r4 in range(4):
                where.append((r4, qs))
                problems.append((gather(q_ref, r4, qs, P2_ROWS), gather(kf, r4, ks, P2_KROWS),
                                 gather(vf, r4, ks, P2_KROWS), bias))
        for (r4, qs), (o, m, l) in zip(where, attend(problems)):
            a_new, m_new, l_new = merge(gather(acc, r4, qs, P2_ROWS), gather(ms, r4, qs, P2_ROWS),
                                        gather(ls, r4, qs, P2_ROWS), o, m, l)
            for j in range(4):
                acc[4 * j + r4, pl.ds(qs, P2_ROWS), :] = a_new[j * P2_ROWS:(j + 1) * P2_ROWS]
                ms[4 * j + r4, pl.ds(qs, P2_ROWS), :] = m_new[j * P2_ROWS:(j + 1) * P2_ROWS]
                ls[4 * j + r4, pl.ds(qs, P2_ROWS), :] = l_new[j * P2_ROWS:(j + 1) * P2_ROWS]
        return carry

    lax.fori_loop(0, CHUNK // P2_ROWS // P2_BATCH, body2, 0)

    b1 = b1_ref[...]

    def body1(it, carry):
        def gather(ref, start, rows):
            return jnp.concatenate([ref[r, pl.ds(start, rows), :] for r in range(RES)], axis=0)

        starts, problems = [], []
        for u in range(P1_BATCH):
            g = it * P1_BATCH + u
            qs = pl.multiple_of(g * P1_ROWS, P1_ROWS)
            ks = pl.multiple_of(HALO - 16 + g * P1_ROWS, 16)
            starts.append(qs)
            problems.append((gather(q_ref, qs, P1_ROWS), gather(kf8, ks, P1_KROWS), gather(vf8, ks, P1_KROWS),
                             b1 + in_seq(row0 + g * P1_ROWS, o1_ref)))
        for qs, (o, m, l) in zip(starts, attend(problems)):
            a_new, m_new, l_new = merge(gather(acc, qs, P1_ROWS), gather(ms, qs, P1_ROWS),
                                        gather(ls, qs, P1_ROWS), o, m, l)
            for r in range(RES):
                acc[r, pl.ds(qs, P1_ROWS), :] = a_new[r * P1_ROWS:(r + 1) * P1_ROWS]
                ms[r, pl.ds(qs, P1_ROWS), :] = m_new[r * P1_ROWS:(r + 1) * P1_ROWS]
                ls[r, pl.ds(qs, P1_ROWS), :] = l_new[r * P1_ROWS:(r + 1) * P1_ROWS]
        return carry

    lax.fori_loop(0, CHUNK // P1_ROWS // P1_BATCH, body1, 0)

    def body_out(r, carry):
        out_ref[r] = (acc[r] / ls[r]).astype(BF16)
        return carry

    lax.fori_loop(0, RES, body_out, 0)


def _dilated_attention(qa, ka, va, B, T):
    L = T // RES
    BL = B * L
    cps = L // CHUNK
    has_halo = cps > 1
    tables = _band_tables(has_halo)
    blk = (RES, None, CHUNK, LANES)
    center = pl.BlockSpec(blk, lambda i, hp: (0, hp, i, 0))
    in_specs = [center, center, center]
    args = [qa, ka, va]
    if has_halo:
        hblk = (RES, None, HALO, LANES)
        per = CHUNK // HALO

        def prev_map(i, hp):
            return (0, hp, jnp.maximum(per * i - 1, (i // cps) * cps * per), 0)

        def next_map(i, hp):
            return (0, hp, jnp.minimum(per * i + per, (i // cps + 1) * cps * per - 1), 0)

        in_specs += [pl.BlockSpec(hblk, prev_map), pl.BlockSpec(hblk, next_map),
                     pl.BlockSpec(hblk, prev_map), pl.BlockSpec(hblk, next_map)]
        args += [ka, ka, va, va]
    for t in tables:
        in_specs.append(pl.BlockSpec(t.shape, lambda i, hp: (0, 0)))
        args.append(t)
    kern = functools.partial(_dilated_kernel, has_halo=has_halo, seq_rows=L, chunks_per_seq=cps)
    return pl.pallas_call(
        kern,
        grid=(BL // CHUNK, N_PAIRS),
        in_specs=in_specs,
        out_specs=pl.BlockSpec(blk, lambda i, hp: (0, hp, i, 0)),
        out_shape=jax.ShapeDtypeStruct((RES, N_PAIRS, BL, LANES), BF16),
        scratch_shapes=[pltpu.VMEM((RES, 2 * CHUNK, LANES), BF16)] * 4 + [
                        pltpu.VMEM((RES, CHUNK, LANES), F32),
                        pltpu.VMEM((RES, CHUNK, LANES), F32),
                        pltpu.VMEM((RES, CHUNK, LANES), F32)],
        compiler_params=_cparams(("arbitrary", "arbitrary")),
        name="dilated_attention",
    )(*args)


NA_KEYS = NA_ROWS * GRID_W
NA_BLOCK_ROWS = 32
NA_BATCH = 32


def _na_bias_table(rpb):
    c = np.arange(GRID_W)
    col_start = np.clip(c - NA_COLS // 2, 0, GRID_W - NA_COLS)
    col_mask = (c[None, :] >= col_start[:, None]) & (c[None, :] < col_start[:, None] + NA_COLS)
    dc_idx = np.clip(c[None, :] - c[:, None], -(NA_COLS - 1), NA_COLS - 1) + NA_COLS - 1
    rel = rpb.astype(F32)[:, :, dc_idx] * LOG2E
    rel = jnp.where(col_mask[None, None], rel, NEG)
    per_off = [rel[:, d0:d0 + NA_ROWS].transpose(0, 2, 1, 3).reshape(N_HEADS, GRID_W, NA_KEYS)
               for d0 in range(NA_ROWS)]
    return jnp.stack(per_off, axis=0).reshape(NA_ROWS, N_PAIRS, 2, GRID_W, NA_KEYS)


def _na_kernel(q_ref, k_ref, v_ref, bias_ref, out_ref, *, grid_rows, block_rows):
    gb = pl.program_id(2)
    lane = lax.broadcasted_iota(I32, (1, LANES), 1)
    head0 = lane < HEAD_DIM

    def body(it, carry):
        rows, scores = [], []
        for u in range(NA_BATCH):
            i = it * NA_BATCH + u
            g = gb * block_rows + i
            rs = jnp.clip(g - NA_ROWS // 2, 0, grid_rows - NA_ROWS)
            d0 = rs - g + NA_ROWS - 1
            qs = pl.multiple_of(i * GRID_W, GRID_W)
            ks = pl.multiple_of(rs * GRID_W, GRID_W)
            q = q_ref[pl.ds(qs, GRID_W), :]
            k = k_ref[pl.ds(ks, NA_KEYS), :]
            rows.append((qs, ks, d0))
            for h in range(2):
                sel = head0 if h == 0 else jnp.logical_not(head0)
                qh = jnp.where(sel, q, jnp.zeros_like(q))
                scores.append(lax.dot_general(qh, k, (((1,), (1,)), ((), ())), preferred_element_type=F32))
        probs, sums = [], []
        for n, s in enumerate(scores):
            s = s + bias_ref[rows[n // 2][2], n % 2]
            p = jnp.exp2(s - jnp.max(s, axis=-1, keepdims=True))
            sums.append(jnp.sum(p, axis=-1, keepdims=True))
            probs.append(p.astype(BF16))
        outs = []
        for n, p in enumerate(probs):
            v = v_ref[pl.ds(rows[n // 2][1], NA_KEYS), :]
            outs.append(jnp.dot(p, v, preferred_element_type=F32) / sums[n])
        for u, (qs, _, _) in enumerate(rows):
            out_ref[pl.ds(qs, GRID_W), :] = jnp.where(head0, outs[2 * u], outs[2 * u + 1]).astype(BF16)
        return carry

    lax.fori_loop(0, block_rows // NA_BATCH, body, 0)


def _neighbourhood_attention(qb, kb, vb, bias, B, T):
    G = T // GRID_W
    rb = min(NA_BLOCK_ROWS, G)
    nb = G // rb
    kern = functools.partial(_na_kernel, grid_rows=G, block_rows=rb)
    seq = pl.BlockSpec((None, T, LANES), lambda hp, b, gb: (hp, b, 0))
    qblk = pl.BlockSpec((None, rb * GRID_W, LANES), lambda hp, b, gb: (hp, b * nb + gb, 0))
    return pl.pallas_call(
        kern,
        grid=(N_PAIRS, B, nb),
        in_specs=[qblk, seq, seq,
                  pl.BlockSpec((NA_ROWS, None, 2, GRID_W, NA_KEYS), lambda hp, b, gb: (0, hp, 0, 0, 0))],
        out_specs=qblk,
        out_shape=jax.ShapeDtypeStruct((N_PAIRS, B * T, LANES), BF16),
        compiler_params=_cparams(("arbitrary", "arbitrary", "arbitrary")),
        name="neighbourhood_attention",
    )(qb, kb, vb, bias)


def _layer_norm(h, g, b):
    mu = jnp.mean(h, axis=-1, keepdims=True)
    d = h - mu
    var = jnp.mean(d * d, axis=-1, keepdims=True)
    return d * lax.rsqrt(var + LN_EPS) * g + b


def _outproj_nat_kernel(oa_ref, ob_ref, x_ref, mod_ref, ga_ref, gb_ref, w_ref, lg_ref, lb_ref, wr_ref, br_ref,
                        permt_ref, x1_ref, u2_ref, logit_ref, mix_scr):
    D = D_MODEL
    gate_a = mod_ref[:, 2 * D:3 * D]
    shift_f = mod_ref[:, 3 * D:4 * D]
    scale_f = mod_ref[:, 4 * D:5 * D]

    def rms(o, g):
        return o * lax.rsqrt(jnp.mean(o * o, axis=-1, keepdims=True) + RMS_EPS) * g

    rows_g = PERM // RES
    for g in range(PROJ_TOKENS // PERM):
        grouped = jnp.concatenate(
            [jnp.concatenate([oa_ref[r, hp, g * rows_g:(g + 1) * rows_g, :] for hp in range(N_PAIRS)], axis=1)
             for r in range(RES)], axis=0)
        oa = jnp.dot(permt_ref[...], grouped, preferred_element_type=F32)
        ob = jnp.concatenate([ob_ref[hp, g * PERM:(g + 1) * PERM, :] for hp in range(N_PAIRS)], axis=1).astype(F32)
        mix = jnp.concatenate([rms(oa, ga_ref[...]), rms(ob, gb_ref[...])], axis=1)
        mix_scr[g * PERM:(g + 1) * PERM, :] = mix.astype(BF16)

    wr = wr_ref[...]
    wr_hi = wr.astype(BF16)
    wr_lo = (wr - wr_hi.astype(F32)).astype(BF16)
    sub = 512
    for s in range(PROJ_TOKENS // sub):
        rows = slice(s * sub, (s + 1) * sub)
        y = jnp.dot(mix_scr[rows, :], w_ref[...], preferred_element_type=F32)
        h = DEEPNORM_ALPHA * x_ref[rows, :] + (1.0 + gate_a) * y
        x1 = _layer_norm(h, lg_ref[...], lb_ref[...])
        x1_ref[rows, :] = x1
        u2 = x1 * (1.0 + scale_f) + shift_f
        u2_ref[rows, :] = u2.astype(BF16)
        u_hi = u2.astype(BF16)
        u_lo = (u2 - u_hi.astype(F32)).astype(BF16)
        logit_ref[rows, :] = (jnp.dot(u_hi, wr_hi, preferred_element_type=F32)
                              + jnp.dot(u_lo, wr_hi, preferred_element_type=F32)
                              + jnp.dot(u_hi, wr_lo, preferred_element_type=F32)) + br_ref[...]


def _output_projection_nat(oa, ob, x, mod3, mod_row0, g_a, g_b, w_out_bf, ln_g, ln_b, w_router_pad, b_router_pad,
                           perm_t):
    B, T, D = x.shape
    N = B * T
    tps = T // PROJ_TOKENS
    const = lambda shape: pl.BlockSpec(shape, lambda i: tuple(0 for _ in shape))
    rows = lambda width: pl.BlockSpec((PROJ_TOKENS, width), lambda i: (i, 0))
    return pl.pallas_call(
        _outproj_nat_kernel,
        grid=(N // PROJ_TOKENS,),
        in_specs=[pl.BlockSpec((RES, N_PAIRS, PROJ_ROWS, LANES), lambda i: (0, 0, i, 0)),
                  pl.BlockSpec((N_PAIRS, PROJ_TOKENS, LANES), lambda i: (0, i, 0)),
                  rows(D),
                  pl.BlockSpec((None, 1, 6 * D), lambda i: (mod_row0 + i // tps, 0, 0)),
                  const((1, WIDTH)), const((1, WIDTH)), const((2 * WIDTH, D)),
                  const((1, D)), const((1, D)), const((D, LANES)), const((1, LANES)), const((PERM, PERM))],
        out_specs=[rows(D), rows(D), rows(LANES)],
        out_shape=[jax.ShapeDtypeStruct((N, D), F32), jax.ShapeDtypeStruct((N, D), BF16),
                   jax.ShapeDtypeStruct((N, LANES), F32)],
        scratch_shapes=[pltpu.VMEM((PROJ_TOKENS, 2 * WIDTH), BF16)],
        compiler_params=_cparams(("arbitrary",)),
        name="output_projection",
    )(oa, ob, x.reshape(N, D), mod3, g_a, g_b, w_out_bf, ln_g, ln_b, w_router_pad, b_router_pad, perm_t)


def _route_kernel(logit_ref, idx_ref, gate_ref, cnt_ref):
    for t in range(ROUTE_TILE // SCATTER_TILE):
        rows = slice(t * SCATTER_TILE, (t + 1) * SCATTER_TILE)
        sel, gates, cnt = _route_tile(logit_ref[rows, :])
        idx_ref[rows, :] = sel
        gate_ref[rows, :] = gates
        cnt_ref[t] = cnt


def _route_tile(raw):
    tn = raw.shape[0]
    lane = lax.broadcasted_iota(I32, (tn, LANES), 1)
    logits = jnp.where(lane < N_EXPERTS, raw, -3.0e38)
    vals, idxs = [], []
    multi = jnp.zeros((tn, LANES), F32)
    for _ in range(TOP_K):
        m = jnp.max(logits, axis=-1, keepdims=True)
        idx = jnp.min(jnp.where(logits == m, lane, LANES), axis=-1, keepdims=True)
        hot = lane == idx
        vals.append(m)
        idxs.append(idx)
        multi = multi + hot.astype(F32)
        logits = jnp.where(hot, -3.0e38, logits)
    es = [jnp.exp(v - vals[0]) for v in vals]
    tot = es[0] + es[1] + es[2] + es[3]
    lane4 = lax.broadcasted_iota(I32, (tn, TOP_K), 1)
    sel = jnp.zeros((tn, TOP_K), I32)
    gates = jnp.zeros((tn, TOP_K), F32)
    for k in range(TOP_K):
        sel = jnp.where(lane4 == k, idxs[k], sel)
        gates = jnp.where(lane4 == k, es[k] / tot, gates)
    return sel, gates, jnp.sum(multi, axis=0, keepdims=True).astype(I32)


def _routing(logits):
    N = logits.shape[0]
    tn = ROUTE_TILE
    nt = N // SCATTER_TILE
    per_step = ROUTE_TILE // SCATTER_TILE
    return pl.pallas_call(
        _route_kernel,
        grid=(N // tn,),
        in_specs=[pl.BlockSpec((tn, LANES), lambda i: (i, 0))],
        out_specs=[pl.BlockSpec((tn, TOP_K), lambda i: (i, 0)),
                   pl.BlockSpec((tn, TOP_K), lambda i: (i, 0)),
                   pl.BlockSpec((per_step, 1, LANES), lambda i: (i, 0, 0))],
        out_shape=[jax.ShapeDtypeStruct((N, TOP_K), I32), jax.ShapeDtypeStruct((N, TOP_K), F32),
                   jax.ShapeDtypeStruct((nt, 1, LANES), I32)],
        compiler_params=_cparams(("arbitrary",)),
        name="moe_routing",
    )(logits)


SEG = 8
STAGE_ROWS = SCATTER_TILE * TOP_K + N_EXPERTS * SEG


def _lane_prefix_exclusive(v):
    lane = lax.broadcasted_iota(I32, v.shape, 1)
    incl = v
    s = 1
    while s < LANES:
        incl = incl + jnp.where(lane >= s, pltpu.roll(incl, s, 1), 0.0)
        s *= 2
    return incl - v


def _tile_ranks(idx, tn):
    lane = lax.broadcasted_iota(I32, (tn, LANES), 1)
    hots = [lane == idx[:, k:k + 1] for k in range(TOP_K)]
    multi = jnp.zeros((tn, LANES), F32)
    for h in hots:
        multi = multi + h.astype(F32)
    row = lax.broadcasted_iota(I32, (tn, tn), 0)
    col = lax.broadcasted_iota(I32, (tn, tn), 1)
    lower = (col < row).astype(BF16)
    before = jnp.dot(lower, multi.astype(BF16), preferred_element_type=F32)
    return hots, before


BIG_SEG = 4 * SEG


def _segment_dma_loops(cnt_s, rows_of, copy):
    def per_expert(e, carry):
        off, n_big, n_small = carry
        nseg = (cnt_s[0, e] + SEG - 1) // SEG
        big = nseg // (BIG_SEG // SEG)
        small = nseg - big * (BIG_SEG // SEG)

        def one_big(q, c):
            src, dst = rows_of(e, off, q * BIG_SEG)
            copy(src, dst, BIG_SEG).start()
            return c

        def one_small(q, c):
            src, dst = rows_of(e, off, big * BIG_SEG + q * SEG)
            copy(src, dst, SEG).start()
            return c

        lax.fori_loop(0, big, one_big, 0)
        lax.fori_loop(0, small, one_small, 0)
        return off + nseg * SEG, n_big + big, n_small + small

    _, n_big, n_small = lax.fori_loop(0, N_EXPERTS, per_expert, (0, 0, 0))
    return n_big, n_small


def _drain(copy, n_big, n_small):
    def wait_big(q, c):
        copy(0, 0, BIG_SEG).wait()
        return c

    def wait_small(q, c):
        copy(0, 0, SEG).wait()
        return c

    lax.fori_loop(0, n_big, wait_big, 0)
    lax.fori_loop(0, n_small, wait_small, 0)


def _dispatch_kernel(*refs, first, n_steps):
    if first:
        cnt_s, start_s, tail_s, idx_ref, u_ref, cntv_ref, xs_ref, stage, zbuf, pending, sem, zsem = refs
    else:
        cnt_s, start_s, tail_s, idx_ref, u_ref, cntv_ref, _, xs_ref, stage, zbuf, pending, sem, zsem = refs
    tn = idx_ref.shape[0]
    step = pl.program_id(0)
    slot = step % 2

    if first:
        @pl.when(pl.program_id(0) == 0)
        def _():
            zbuf[...] = jnp.zeros_like(zbuf)

            def tail_copy(row):
                return pltpu.make_async_copy(zbuf, xs_ref.at[pl.ds(pl.multiple_of(row, SEG), SEG)], zsem)

            def fill(e, n):
                lo = tail_s[0, e]
                nfull = (tail_s[1, e] - lo) // SEG

                def one(q, c):
                    tail_copy(lo + q * SEG).start()
                    return c

                lax.fori_loop(0, nfull, one, 0)
                return n + nfull

            n = lax.fori_loop(0, N_EXPERTS, fill, 0)

            def drain(q, c):
                tail_copy(0).wait()
                return c

            lax.fori_loop(0, n, drain, 0)

    hots, before = _tile_ranks(idx_ref[...], tn)
    cntf = cntv_ref[...].astype(F32)
    seg_len = jnp.ceil(cntf * (1.0 / SEG)) * SEG
    seg_off = _lane_prefix_exclusive(seg_len)
    lane_r = lax.broadcasted_iota(I32, (tn, STAGE_ROWS), 1)
    onehot = jnp.zeros((tn, STAGE_ROWS), F32)
    for k in range(TOP_K):
        stage_row = jnp.sum(jnp.where(hots[k], seg_off + before, 0.0), axis=-1, keepdims=True)
        onehot = jnp.where(lane_r == stage_row.astype(I32), 1.0, onehot)
    stage[slot] = lax.dot_general(onehot.astype(BF16), u_ref[...].astype(BF16), (((0,), (0,)), ((), ())),
                                  preferred_element_type=F32)

    def copier(buf):
        def seg_copy(src_row, dst_row, rows):
            return pltpu.make_async_copy(stage.at[buf, pl.ds(pl.multiple_of(src_row, SEG), rows)],
                                         xs_ref.at[pl.ds(pl.multiple_of(dst_row, SEG), rows)], sem.at[buf])
        return seg_copy

    @pl.when(step > 0)
    def _():
        _drain(copier(1 - slot), pending[0], pending[1])

    n_big, n_small = _segment_dma_loops(
        cnt_s, lambda e, off, moved: (off + moved, start_s[0, e] + moved), copier(slot))
    pending[0] = n_big
    pending[1] = n_small

    @pl.when(step == n_steps - 1)
    def _():
        _drain(copier(slot), n_big, n_small)


def _dispatch(idx, u2, tile_cnt, tile_start, tails, xs, xs_rows, t0):
    n, D = u2.shape
    tn = SCATTER_TILE
    nt = n // tn
    first = xs is None
    smem = lambda shape, imap: pl.BlockSpec(shape, imap, memory_space=pltpu.SMEM)
    kern = functools.partial(_dispatch_kernel, first=first, n_steps=nt)
    in_specs = [smem((None, 1, LANES), lambda i: (t0 + i, 0, 0)),
                smem((None, 1, LANES), lambda i: (t0 + i, 0, 0)),
                smem((2, LANES), lambda i: (0, 0)),
                pl.BlockSpec((tn, TOP_K), lambda i: (t0 + i, 0)),
                pl.BlockSpec((tn, D), lambda i: (i, 0)),
                pl.BlockSpec((None, 1, LANES), lambda i: (t0 + i, 0, 0))]
    args = [tile_cnt, tile_start, tails, idx, u2, tile_cnt]
    alias = {}
    if not first:
        alias = {len(args): 0}
        in_specs.append(pl.BlockSpec(memory_space=pl.ANY))
        args.append(xs)
    return pl.pallas_call(
        kern,
        grid=(nt,),
        in_specs=in_specs,
        out_specs=pl.BlockSpec(memory_space=pl.ANY),
        out_shape=jax.ShapeDtypeStruct((xs_rows, D), F32),
        scratch_shapes=[pltpu.VMEM((2, STAGE_ROWS, D), F32), pltpu.VMEM((SEG, D), F32), pltpu.SMEM((2,), I32),
                        pltpu.SemaphoreType.DMA((2,)), pltpu.SemaphoreType.DMA(())],
        input_output_aliases=alias,
        compiler_params=_cparams(("arbitrary",)),
        name="moe_dispatch",
    )(*args)


DEINT = 256


def _deinterleave_matrix():
    perm = np.zeros((DEINT, DEINT), np.float32)
    j = np.arange(DEINT // 2)
    perm[2 * j, j] = 1.0
    perm[2 * j + 1, DEINT // 2 + j] = 1.0
    return perm


def _expert_kernel(be_ref, nu_ref, x_ref, wu_ref, bu_ref, wd_ref, bd_ref, p_ref, y_ref, wu_bf, wd_bf):
    j = pl.program_id(0)

    @pl.when((j == 0) | (be_ref[j] != be_ref[jnp.maximum(j - 1, 0)]))
    def _():
        half = DEINT // 2
        for g in range(2 * D_FF // DEINT):
            wg = wu_ref[:, g * DEINT:(g + 1) * DEINT].astype(BF16)
            t = jnp.dot(wg, p_ref[...], preferred_element_type=F32)
            wu_bf[:, g * half:(g + 1) * half] = t[:, :half].astype(BF16)
            wu_bf[:, D_FF + g * half:D_FF + (g + 1) * half] = t[:, half:].astype(BF16)
        wd_bf[...] = wd_ref[...].astype(BF16)

    @pl.when(j < nu_ref[0])
    def _():
        x = x_ref[...].astype(BF16)
        hu = jnp.dot(x, wu_bf[...], preferred_element_type=F32) + bu_ref[...]
        glu = jnp.minimum(hu[:, :D_FF], SWIGLU_LIMIT)
        lin = jnp.clip(hu[:, D_FF:], -SWIGLU_LIMIT, SWIGLU_LIMIT)
        act = glu * (1.0 / (1.0 + jnp.exp(-SWIGLU_ALPHA * glu))) * (lin + 1.0)
        y_ref[...] = jnp.dot(act.astype(BF16), wd_bf[...], preferred_element_type=F32) + bd_ref[...]

    @pl.when(pl.program_id(0) >= nu_ref[0])
    def _():
        y_ref[...] = jnp.zeros_like(y_ref)


def _expert_blocks(xs, blk_expert, n_used, w_up, b_up_de, w_down, b_down):
    n_rows, D = xs.shape
    nblk = n_rows // MOE_ROWS

    def xmap(j, be, nu):
        return (jnp.minimum(j, nu[0] - 1), 0)

    grid_spec = pltpu.PrefetchScalarGridSpec(
        num_scalar_prefetch=2,
        grid=(nblk,),
        in_specs=[pl.BlockSpec((MOE_ROWS, D), xmap),
                  pl.BlockSpec((None, D, 2 * D_FF), lambda j, be, nu: (be[j], 0, 0)),
                  pl.BlockSpec((None, 1, 2 * D_FF), lambda j, be, nu: (be[j], 0, 0)),
                  pl.BlockSpec((None, D_FF, D), lambda j, be, nu: (be[j], 0, 0)),
                  pl.BlockSpec((None, 1, D), lambda j, be, nu: (be[j], 0, 0)),
                  pl.BlockSpec((DEINT, DEINT), lambda j, be, nu: (0, 0))],
        out_specs=pl.BlockSpec((MOE_ROWS, D), lambda j, be, nu: (j, 0)),
        scratch_shapes=[pltpu.VMEM((D, 2 * D_FF), BF16), pltpu.VMEM((D_FF, D), BF16)],
    )
    return pl.pallas_call(
        _expert_kernel,
        grid_spec=grid_spec,
        out_shape=jax.ShapeDtypeStruct((n_rows, D), F32),
        compiler_params=_cparams(("arbitrary",)),
        name="moe_experts",
    )(blk_expert, n_used, xs, w_up, b_up_de, w_down, b_down, jnp.asarray(_deinterleave_matrix(), BF16))


def _combine_kernel(cnt_s, start_s, idx_ref, gate_ref, cntv_ref, y_ref, x1_ref, mod_ref,
                    lg_ref, lb_ref, out_ref, stage, sem):
    D = D_MODEL
    tn = idx_ref.shape[0]

    def seg_copy(src_row, dst_row, rows):
        return pltpu.make_async_copy(y_ref.at[pl.ds(pl.multiple_of(src_row, SEG), rows)],
                                     stage.at[pl.ds(pl.multiple_of(dst_row, SEG), rows)], sem)

    @pl.when(pl.program_id(0) == 0)
    def _():
        stage[...] = jnp.zeros_like(stage)

    n_big, n_small = _segment_dma_loops(
        cnt_s, lambda e, off, moved: (start_s[0, e] + moved, off + moved), seg_copy)

    hots, before = _tile_ranks(idx_ref[...], tn)
    seg_len = jnp.ceil(cntv_ref[...].astype(F32) * (1.0 / SEG)) * SEG
    seg_off = _lane_prefix_exclusive(seg_len)
    gates = gate_ref[...]
    lane_r = lax.broadcasted_iota(I32, (tn, STAGE_ROWS), 1)
    weights = jnp.zeros((tn, STAGE_ROWS), F32)
    for k in range(TOP_K):
        stage_row = jnp.sum(jnp.where(hots[k], seg_off + before, 0.0), axis=-1, keepdims=True)
        weights = jnp.where(lane_r == stage_row.astype(I32), gates[:, k:k + 1], weights)

    _drain(seg_copy, n_big, n_small)

    y = jnp.dot(weights.astype(BF16), stage[...].astype(BF16), preferred_element_type=F32)
    gate_f = mod_ref[:, 5 * D:6 * D]
    h = DEEPNORM_ALPHA * x1_ref[...] + (1.0 + gate_f) * y
    out_ref[...] = _layer_norm(h, lg_ref[...], lb_ref[...])


def _combine(idx, gates, tile_cnt, tile_start, y_pad, x1, mod3, mod_row0, ln_g, ln_b, B, T, t0):
    D = D_MODEL
    tn = SCATTER_TILE
    n_rows = B * T
    nt = n_rows // tn
    tps = T // tn
    smem = lambda imap: pl.BlockSpec((None, 1, LANES), imap, memory_space=pltpu.SMEM)
    out = pl.pallas_call(
        _combine_kernel,
        grid=(nt,),
        in_specs=[smem(lambda i: (t0 + i, 0, 0)), smem(lambda i: (t0 + i, 0, 0)),
                  pl.BlockSpec((tn, TOP_K), lambda i: (t0 + i, 0)),
                  pl.BlockSpec((tn, TOP_K), lambda i: (t0 + i, 0)),
                  pl.BlockSpec((None, 1, LANES), lambda i: (t0 + i, 0, 0)),
                  pl.BlockSpec(memory_space=pl.ANY),
                  pl.BlockSpec((tn, D), lambda i: (i, 0)),
                  pl.BlockSpec((None, 1, 6 * D), lambda i: (mod_row0 + i // tps, 0, 0)),
                  pl.BlockSpec((1, D), lambda i: (0, 0)),
                  pl.BlockSpec((1, D), lambda i: (0, 0))],
        out_specs=pl.BlockSpec((tn, D), lambda i: (i, 0)),
        out_shape=jax.ShapeDtypeStruct((n_rows, D), F32),
        scratch_shapes=[pltpu.VMEM((STAGE_ROWS, D), F32), pltpu.SemaphoreType.DMA(())],
        compiler_params=_cparams(("arbitrary",)),
        name="moe_combine",
    )(tile_cnt, tile_start, idx, gates, tile_cnt, y_pad, x1, mod3, ln_g, ln_b)
    return out.reshape(B, T, D)


def kernel(x_prompt, x_sample, c_prompt, c_sample, w_mod, b_mod, w_in, rpb, g_out_a, g_out_b, w_out, ln1_g, ln1_b,
           w_router, b_router, w_up, b_up, w_down, b_down, ln2_g, ln2_b):
    D = D_MODEL
    groups = [x_prompt, x_sample]
    conds = [c_prompt, c_sample]
    n_cond = sum(c.shape[0] for c in conds)
    pad_rows = -n_cond % 8
    c_all = jnp.concatenate(conds + [jnp.zeros((pad_rows, D), F32)], axis=0)
    mod3 = _modulation(c_all, w_mod[0], b_mod[0]).reshape(n_cond + pad_rows, 1, 6 * D)

    w_in_bf = w_in[0].astype(BF16)
    w_out_bf = w_out[0].astype(BF16)
    na_bias = _na_bias_table(rpb[0])
    g_a = g_out_a[0].reshape(1, WIDTH)
    g_b = g_out_b[0].reshape(1, WIDTH)
    ln1g, ln1b = ln1_g[0].reshape(1, D), ln1_b[0].reshape(1, D)
    ln2g, ln2b = ln2_g[0].reshape(1, D), ln2_b[0].reshape(1, D)
    w_router_pad = jnp.pad(w_router[0], ((0, 0), (0, LANES - N_EXPERTS)))
    b_router_pad = jnp.pad(b_router[0], (0, LANES - N_EXPERTS)).reshape(1, LANES)

    perm_np = _residue_permutation()
    perm, perm_t = jnp.asarray(perm_np, BF16), jnp.asarray(perm_np.T, BF16)

    x1s, u2s, logit_list = [], [], []
    mod_row0 = 0
    mod_rows = []
    for x in groups:
        B, T, _ = x.shape
        cos_v, sin_v = _rope_tables(T)
        qa, ka, va, qb, kb, vb = _input_projection_nat(x, mod3, w_in_bf, cos_v, sin_v, perm, mod_row0)
        oa = _dilated_attention(qa, ka, va, B, T)
        ob = _neighbourhood_attention(qb, kb, vb, na_bias, B, T)
        x1, u2, logits = _output_projection_nat(oa, ob, x, mod3, mod_row0, g_a, g_b, w_out_bf, ln1g, ln1b,
                                                w_router_pad, b_router_pad, perm_t)
        x1s.append(x1)
        u2s.append(u2)
        logit_list.append(logits)
        mod_rows.append(mod_row0)
        mod_row0 += B

    logits_all = jnp.concatenate(logit_list, axis=0)
    N = logits_all.shape[0]
    idx, gates, tile_cnt = _routing(logits_all)

    tc = tile_cnt[:, 0, :N_EXPERTS]
    seg = (tc + SEG - 1) // SEG * SEG
    total = jnp.sum(seg, axis=0)
    padded = (total + MOE_ROWS - 1) // MOE_ROWS * MOE_ROWS
    pad_end = jnp.cumsum(padded)
    pad_start = pad_end - padded
    tile_start = pad_start[None, :] + jnp.cumsum(seg, axis=0) - seg
    n_tiles = N // SCATTER_TILE
    n_blocks = (N * TOP_K + n_tiles * N_EXPERTS * (SEG - 1)) // MOE_ROWS + N_EXPERTS
    n_rows = n_blocks * MOE_ROWS
    blk_end = pad_end // MOE_ROWS
    n_used = blk_end[-1:].astype(I32)
    blk_expert = jnp.minimum(
        jnp.sum(blk_end[None, :] <= jnp.arange(n_blocks, dtype=I32)[:, None], axis=1), N_EXPERTS - 1).astype(I32)
    lane_pad = ((0, 0), (0, LANES - N_EXPERTS))
    tile_start3 = jnp.pad(tile_start, lane_pad).astype(I32)[:, None, :]
    tails = jnp.pad(jnp.stack([pad_start + total, pad_end]), lane_pad).astype(I32)

    xs = None
    row0 = 0
    for u2 in u2s:
        n = u2.shape[0]
        xs = _dispatch(idx, u2, tile_cnt, tile_start3, tails, xs, n_rows, row0 // SCATTER_TILE)
        row0 += n

    b_up_de = jnp.concatenate([b_up[0][:, 0::2], b_up[0][:, 1::2]], axis=-1).reshape(N_EXPERTS, 1, 2 * D_FF)
    b_down3 = b_down[0].reshape(N_EXPERTS, 1, D)
    y_pad = _expert_blocks(xs, blk_expert, n_used, w_up[0], b_up_de, w_down[0], b_down3)

    outs = []
    row0 = 0
    for x, x1, mrow in zip(groups, x1s, mod_rows):
        B, T, _ = x.shape
        n = B * T
        outs.append(_combine(idx, gates, tile_cnt, tile_start3, y_pad, x1, mod3, mrow, ln2g, ln2b, B, T,
                             row0 // SCATTER_TILE))
        row0 += n
    return tuple(outs)
```

```python
import functools

import numpy as np
import jax
import jax.numpy as jnp
from jax import lax
from jax.experimental import pallas as pl
from jax.experimental.pallas import tpu as pltpu

F32 = jnp.float32
BF16 = jnp.bfloat16
I32 = jnp.int32

D_MODEL = 1024
HEAD_DIM = 64
N_HEADS = 8
WIDTH = N_HEADS * HEAD_DIM
N_PAIRS = WIDTH // 128
ROPE_THETA = 10000.0
RADIUS = 64
GRID_W = 64
NA_ROWS = 8
NA_COLS = 16
N_EXPERTS = 32
TOP_K = 4
D_FF = 1024
SWIGLU_ALPHA = 1.702
SWIGLU_LIMIT = 7.0
DEEPNORM_ALPHA = 2.0 ** 0.25
LN_EPS = 1e-5
RMS_EPS = 1e-6
NEG = -1e30
LOG2E = 1.4426950408889634

LANES = 128
RES = 16
CHUNK = 128
RES_PER_STEP = 4
MOE_ROWS = 512
ROUTE_TILE = 1024
SCATTER_TILE = 512
VMEM_LIMIT = 56 * 1024 * 1024


def _cparams(sem, vmem=VMEM_LIMIT):
    return pltpu.CompilerParams(dimension_semantics=sem, vmem_limit_bytes=vmem)


def _mod_kernel(c_ref, w_ref, b_ref, o_ref):
    c = c_ref[...]
    s = c * (1.0 / (1.0 + jnp.exp(-c)))
    o_ref[...] = jnp.dot(s.astype(BF16), w_ref[...].astype(BF16), preferred_element_type=F32) + b_ref[...]


def _modulation(c, w_mod, b_mod):
    rows = c.shape[0]
    n_out = w_mod.shape[1]
    tn = 1024
    return pl.pallas_call(
        _mod_kernel,
        grid=(n_out // tn,),
        in_specs=[pl.BlockSpec((rows, D_MODEL), lambda j: (0, 0)),
                  pl.BlockSpec((D_MODEL, tn), lambda j: (0, j)),
                  pl.BlockSpec((1, tn), lambda j: (0, j))],
        out_specs=pl.BlockSpec((rows, tn), lambda j: (0, j)),
        out_shape=jax.ShapeDtypeStruct((rows, n_out), F32),
        compiler_params=_cparams(("arbitrary",)),
        name="modulation",
    )(c, w_mod, b_mod.reshape(1, n_out))


PERM = 256
PROJ_TOKENS = 1024
PROJ_ROWS = PROJ_TOKENS // RES


def _residue_permutation():
    p = np.zeros((PERM, PERM), np.float32)
    m, r = np.meshgrid(np.arange(PERM // RES), np.arange(RES), indexing="ij")
    p[(r * (PERM // RES) + m).reshape(-1), (m * RES + r).reshape(-1)] = 1.0
    return p


def _inproj_nat_kernel(x_ref, mod_ref, w_ref, cos_ref, sin_ref, perm_ref,
                       qa_ref, ka_ref, va_ref, qb_ref, kb_ref, vb_ref, u_nat, u_view):
    D = D_MODEL
    shift = mod_ref[:, 0:D]
    scale = mod_ref[:, D:2 * D]
    qscale = HEAD_DIM ** -0.5 * LOG2E
    u_nat[...] = (x_ref[...] * (1.0 + scale) + shift).astype(BF16)

    sub = 512
    for s in range(PROJ_TOKENS // sub):
        u = u_nat[s * sub:(s + 1) * sub, :]
        for i, (ref, mul) in enumerate(((qb_ref, qscale), (kb_ref, 1.0), (vb_ref, 1.0))):
            p = jnp.dot(u, w_ref[:, (3 + i) * WIDTH:(4 + i) * WIDTH], preferred_element_type=F32) * mul
            for hp in range(N_PAIRS):
                ref[hp, s * sub:(s + 1) * sub, :] = p[:, hp * LANES:(hp + 1) * LANES].astype(BF16)

    rows_g = PERM // RES
    for g in range(PROJ_TOKENS // PERM):
        pv = jnp.dot(perm_ref[...], u_nat[g * PERM:(g + 1) * PERM, :], preferred_element_type=F32).astype(BF16)
        for r in range(RES):
            u_view[r, g * rows_g:(g + 1) * rows_g, :] = pv[r * rows_g:(r + 1) * rows_g]

    lane = lax.broadcasted_iota(I32, (1, WIDTH), 1)
    first_half = (lane % HEAD_DIM) < (HEAD_DIM // 2)
    for q in range(RES // RES_PER_STEP):
        u = jnp.concatenate([u_view[q * RES_PER_STEP + a] for a in range(RES_PER_STEP)], axis=0)
        lo = q * RES_PER_STEP * LANES
        cosf = jnp.concatenate([jnp.tile(cos_ref[:, lo + a * LANES:lo + (a + 1) * LANES], (1, N_PAIRS))
                                for a in range(RES_PER_STEP)], axis=0)
        sinf = jnp.concatenate([jnp.tile(sin_ref[:, lo + a * LANES:lo + (a + 1) * LANES], (1, N_PAIRS))
                                for a in range(RES_PER_STEP)], axis=0)

        def rope(p):
            rot = jnp.where(first_half, pltpu.roll(p, WIDTH - HEAD_DIM // 2, 1), pltpu.roll(p, HEAD_DIM // 2, 1))
            return p * cosf + rot * sinf

        for i, ref in enumerate((qa_ref, ka_ref, va_ref)):
            p = jnp.dot(u, w_ref[:, i * WIDTH:(i + 1) * WIDTH], preferred_element_type=F32)
            if i == 0:
                p = rope(p) * qscale
            elif i == 1:
                p = rope(p)
            for a in range(RES_PER_STEP):
                for hp in range(N_PAIRS):
                    ref[q * RES_PER_STEP + a, hp] = (
                        p[a * PROJ_ROWS:(a + 1) * PROJ_ROWS, hp * LANES:(hp + 1) * LANES].astype(BF16))


def _input_projection_nat(x, mod3, w_in_bf, cos_v, sin_v, perm, mod_row0):
    B, T, D = x.shape
    N = B * T
    BL = N // RES
    tps = T // PROJ_TOKENS
    res_shape = jax.ShapeDtypeStruct((RES, N_PAIRS, BL, LANES), BF16)
    nat_shape = jax.ShapeDtypeStruct((N_PAIRS, N, LANES), BF16)
    res_spec = pl.BlockSpec((RES, N_PAIRS, PROJ_ROWS, LANES), lambda i: (0, 0, i, 0))
    nat_spec = pl.BlockSpec((N_PAIRS, PROJ_TOKENS, LANES), lambda i: (0, i, 0))
    return pl.pallas_call(
        _inproj_nat_kernel,
        grid=(N // PROJ_TOKENS,),
        in_specs=[pl.BlockSpec((PROJ_TOKENS, D), lambda i: (i, 0)),
                  pl.BlockSpec((None, 1, 6 * D), lambda i: (mod_row0 + i // tps, 0, 0)),
                  pl.BlockSpec((D, 6 * WIDTH), lambda i: (0, 0)),
                  pl.BlockSpec((PROJ_ROWS, RES * LANES), lambda i: (i % tps, 0)),
                  pl.BlockSpec((PROJ_ROWS, RES * LANES), lambda i: (i % tps, 0)),
                  pl.BlockSpec((PERM, PERM), lambda i: (0, 0))],
        out_specs=[res_spec, res_spec, res_spec, nat_spec, nat_spec, nat_spec],
        out_shape=[res_shape, res_shape, res_shape, nat_shape, nat_shape, nat_shape],
        scratch_shapes=[pltpu.VMEM((PROJ_TOKENS, D), BF16), pltpu.VMEM((RES, PROJ_ROWS, D), BF16)],
        compiler_params=_cparams(("arbitrary",)),
        name="input_projection",
    )(x.reshape(N, D), mod3, w_in_bf, cos_v, sin_v, perm)


def _rope_tables(T):
    half = HEAD_DIM // 2
    inv_freq = 1.0 / (ROPE_THETA ** (jnp.arange(half, dtype=F32) / half))
    ang = jnp.arange(T, dtype=F32)[:, None] * inv_freq[None, :]
    cos, sin = jnp.cos(ang), jnp.sin(ang)
    cos_h = jnp.concatenate([cos, cos], axis=-1)
    sin_h = jnp.concatenate([-sin, sin], axis=-1)
    cos2 = jnp.concatenate([cos_h, cos_h], axis=-1)
    sin2 = jnp.concatenate([sin_h, sin_h], axis=-1)
    L = T // RES
    return cos2.reshape(L, RES * LANES), sin2.reshape(L, RES * LANES)


P2_ROWS = 32
P2_KROWS = 64
P1_ROWS = 16
P1_KROWS = 32
P1_SHIFT = 8
HALO = 64
P3_BATCH = 8
P2_BATCH = 2
P1_BATCH = 2


def _band_tables(has_halo):
    def mask(ok):
        return np.where(ok, 0.0, NEG).astype(np.float32)
    mq = np.arange(CHUNK)[:, None]
    koff3 = np.arange(2 * CHUNK) - HALO
    b3 = mask(np.abs(mq - koff3[None, :]) <= RADIUS)
    j = np.repeat(np.arange(4), P2_ROWS)[:, None]
    a = np.tile(np.arange(P2_ROWS), 4)[:, None]
    jk = np.repeat(np.arange(4), P2_KROWS)[None, :]
    bk = np.tile(np.arange(P2_KROWS), 4)[None, :]
    b2 = mask(np.abs(4 * (a - (bk - 16)) + (j - jk)) <= RADIUS)
    koff2 = (bk - 16).reshape(-1)
    r = np.repeat(np.arange(RES), P1_ROWS)[:, None]
    a = np.tile(np.arange(P1_ROWS), RES)[:, None]
    rk = np.repeat(np.arange(RES), P1_KROWS)[None, :]
    bk = np.tile(np.arange(P1_KROWS), RES)[None, :]
    b1 = mask(np.abs(RES * (a - (bk - P1_SHIFT)) + (r - rk)) <= RADIUS)
    koff1 = (bk - P1_SHIFT).reshape(-1)
    return (jnp.asarray(b3), jnp.asarray(koff3.astype(np.int32)[None, :]),
            jnp.asarray(b2), jnp.asarray(koff2.astype(np.int32)[None, :]),
            jnp.asarray(b1), jnp.asarray(koff1.astype(np.int32)[None, :]))


def _attend_pairs(problems, head0):
    scores = []
    for q, k, _, _ in problems:
        for h in range(2):
            sel = head0 if h == 0 else jnp.logical_not(head0)
            qh = jnp.where(sel, q, jnp.zeros_like(q))
            scores.append(lax.dot_general(qh, k, (((1,), (1,)), ((), ())), preferred_element_type=F32))
    probs, stats = [], []
    for i, s in enumerate(scores):
        s = s + problems[i // 2][3]
        m = jnp.max(s, axis=-1, keepdims=True)
        p = jnp.exp2(s - m)
        stats.append((m, jnp.sum(p, axis=-1, keepdims=True)))
        probs.append(p.astype(BF16))
    outs = [jnp.dot(p, problems[i // 2][2], preferred_element_type=F32) for i, p in enumerate(probs)]
    results = []
    for i in range(len(problems)):
        (m0, l0), (m1, l1) = stats[2 * i], stats[2 * i + 1]
        results.append((jnp.where(head0, outs[2 * i], outs[2 * i + 1]),
                        jnp.where(head0, m0, m1), jnp.where(head0, l0, l1)))
    return results


def _dilated_kernel(*refs, has_halo, seq_rows, chunks_per_seq):
    if has_halo:
        (q_ref, k_ref, v_ref, kp_ref, kn_ref, vp_ref, vn_ref,
         b3_ref, o3_ref, b2_ref, o2_ref, b1_ref, o1_ref,
         out_ref, kf, vf, kf8, vf8, acc, ms, ls) = refs
    else:
        (q_ref, k_ref, v_ref, b3_ref, o3_ref, b2_ref, o2_ref, b1_ref, o1_ref,
         out_ref, kf, vf, kf8, vf8, acc, ms, ls) = refs

    c = pl.program_id(0) % chunks_per_seq
    row0 = c * CHUNK
    lane = lax.broadcasted_iota(I32, (1, LANES), 1)
    head0 = lane < HEAD_DIM

    for r in range(RES):
        kf[r, HALO:HALO + CHUNK, :] = k_ref[r]
        vf[r, HALO:HALO + CHUNK, :] = v_ref[r]
        if has_halo:
            kf[r, 0:HALO, :] = kp_ref[r]
            vf[r, 0:HALO, :] = vp_ref[r]
            kf[r, HALO + CHUNK:, :] = kn_ref[r]
            vf[r, HALO + CHUNK:, :] = vn_ref[r]
        else:
            zeros = jnp.zeros((HALO, LANES), BF16)
            for ref in (kf, vf):
                ref[r, 0:HALO, :] = zeros
                ref[r, HALO + CHUNK:, :] = zeros
        for src, dst in ((kf, kf8), (vf, vf8)):
            w32 = src[r].astype(F32)
            dst[r] = jnp.concatenate([w32[P1_SHIFT:], w32[:P1_SHIFT]], axis=0).astype(BF16)

    def in_seq(base, off_ref):
        kv_row = base + off_ref[...]
        return jnp.where((kv_row >= 0) & (kv_row < seq_rows), 0.0, NEG).astype(F32)

    def attend(problems):
        return _attend_pairs(problems, head0)

    def merge(a_old, m_old, l_old, o, m, l):
        mn = jnp.maximum(m_old, m)
        wa = jnp.exp2(m_old - mn)
        wb = jnp.exp2(m - mn)
        return a_old * wa + o * wb, mn, l_old * wa + l * wb

    bias3 = b3_ref[...] + in_seq(row0, o3_ref)

    def body3(it, carry):
        rs = [it * P3_BATCH + u for u in range(P3_BATCH)]
        res = attend([(q_ref[r], kf[r], vf[r], bias3) for r in rs])
        for r, (o, m, l) in zip(rs, res):
            acc[r] = o
            ms[r] = m
            ls[r] = l
        return carry

    lax.fori_loop(0, RES // P3_BATCH, body3, 0)

    b2 = b2_ref[...]

    def body2(it, carry):
        def gather(ref, r4, start, rows):
            return jnp.concatenate([ref[4 * j + r4, pl.ds(start, rows), :] for j in range(4)], axis=0)

        where, problems = [], []
        for u in range(P2_BATCH):
            g = it * P2_BATCH + u
            qs = pl.multiple_of(g * P2_ROWS, P2_ROWS)
            ks = pl.multiple_of(HALO - 16 + g * P2_ROWS, 16)
            bias = b2 + in_seq(row0 + g * P2_ROWS, o2_ref)
            for r4 in range(4):
                where.append((r4, qs))
                problems.append((gather(q_ref, r4, qs, P2_ROWS), gather(kf, r4, ks, P2_KROWS),
                                 gather(vf, r4, ks, P2_KROWS), bias))
        for (r4, qs), (o, m, l) in zip(where, attend(problems)):
            a_new, m_new, l_new = merge(gather(acc, r4, qs, P2_ROWS), gather(ms, r4, qs, P2_ROWS),
                                        gather(ls, r4, qs, P2_ROWS), o, m, l)
            for j in range(4):
                acc[4 * j + r4, pl.ds(qs, P2_ROWS), :] = a_new[j * P2_ROWS:(j + 1) * P2_ROWS]
                ms[4 * j + r4, pl.ds(qs, P2_ROWS), :] = m_new[j * P2_ROWS:(j + 1) * P2_ROWS]
                ls[4 * j + r4, pl.ds(qs, P2_ROWS), :] = l_new[j * P2_ROWS:(j + 1) * P2_ROWS]
        return carry

    lax.fori_loop(0, CHUNK // P2_ROWS // P2_BATCH, body2, 0)

    b1 = b1_ref[...]

    def body1(it, carry):
        def gather(ref, start, rows):
            return jnp.concatenate([ref[r, pl.ds(start, rows), :] for r in range(RES)], axis=0)

        starts, problems = [], []
        for u in range(P1_BATCH):
            g = it * P1_BATCH + u
            qs = pl.multiple_of(g * P1_ROWS, P1_ROWS)
            ks = pl.multiple_of(HALO - 16 + g * P1_ROWS, 16)
            starts.append(qs)
            problems.append((gather(q_ref, qs, P1_ROWS), gather(kf8, ks, P1_KROWS), gather(vf8, ks, P1_KROWS),
                             b1 + in_seq(row0 + g * P1_ROWS, o1_ref)))
        for qs, (o, m, l) in zip(starts, attend(problems)):
            a_new, m_new, l_new = merge(gather(acc, qs, P1_ROWS), gather(ms, qs, P1_ROWS),
                                        gather(ls, qs, P1_ROWS), o, m, l)
            for r in range(RES):
                acc[r, pl.ds(qs, P1_ROWS), :] = a_new[r * P1_ROWS:(r + 1) * P1_ROWS]
                ms[r, pl.ds(qs, P1_ROWS), :] = m_new[r * P1_ROWS:(r + 1) * P1_ROWS]
                ls[r, pl.ds(qs, P1_ROWS), :] = l_new[r * P1_ROWS:(r + 1) * P1_ROWS]
        return carry

    lax.fori_loop(0, CHUNK // P1_ROWS // P1_BATCH, body1, 0)

    def body_out(r, carry):
        out_ref[r] = (acc[r] / ls[r]).astype(BF16)
        return carry

    lax.fori_loop(0, RES, body_out, 0)


def _dilated_attention(qa, ka, va, B, T):
    L = T // RES
    BL = B * L
    cps = L // CHUNK
    has_halo = cps > 1
    tables = _band_tables(has_halo)
    blk = (RES, None, CHUNK, LANES)
    center = pl.BlockSpec(blk, lambda i, hp: (0, hp, i, 0))
    in_specs = [center, center, center]
    args = [qa, ka, va]
    if has_halo:
        hblk = (RES, None, HALO, LANES)
        per = CHUNK // HALO

        def prev_map(i, hp):
            return (0, hp, jnp.maximum(per * i - 1, (i // cps) * cps * per), 0)

        def next_map(i, hp):
            return (0, hp, jnp.minimum(per * i + per, (i // cps + 1) * cps * per - 1), 0)

        in_specs += [pl.BlockSpec(hblk, prev_map), pl.BlockSpec(hblk, next_map),
                     pl.BlockSpec(hblk, prev_map), pl.BlockSpec(hblk, next_map)]
        args += [ka, ka, va, va]
    for t in tables:
        in_specs.append(pl.BlockSpec(t.shape, lambda i, hp: (0, 0)))
        args.append(t)
    kern = functools.partial(_dilated_kernel, has_halo=has_halo, seq_rows=L, chunks_per_seq=cps)
    return pl.pallas_call(
        kern,
        grid=(BL // CHUNK, N_PAIRS),
        in_specs=in_specs,
        out_specs=pl.BlockSpec(blk, lambda i, hp: (0, hp, i, 0)),
        out_shape=jax.ShapeDtypeStruct((RES, N_PAIRS, BL, LANES), BF16),
        scratch_shapes=[pltpu.VMEM((RES, 2 * CHUNK, LANES), BF16)] * 4 + [
                        pltpu.VMEM((RES, CHUNK, LANES), F32),
                        pltpu.VMEM((RES, CHUNK, LANES), F32),
                        pltpu.VMEM((RES, CHUNK, LANES), F32)],
        compiler_params=_cparams(("arbitrary", "arbitrary")),
        name="dilated_attention",
    )(*args)


NA_KEYS = NA_ROWS * GRID_W
NA_BLOCK_ROWS = 32
NA_BATCH = 32


def _na_bias_table(rpb):
    c = np.arange(GRID_W)
    col_start = np.clip(c - NA_COLS // 2, 0, GRID_W - NA_COLS)
    col_mask = (c[None, :] >= col_start[:, None]) & (c[None, :] < col_start[:, None] + NA_COLS)
    dc_idx = np.clip(c[None, :] - c[:, None], -(NA_COLS - 1), NA_COLS - 1) + NA_COLS - 1
    rel = rpb.astype(F32)[:, :, dc_idx] * LOG2E
    rel = jnp.where(col_mask[None, None], rel, NEG)
    per_off = [rel[:, d0:d0 + NA_ROWS].transpose(0, 2, 1, 3).reshape(N_HEADS, GRID_W, NA_KEYS)
               for d0 in range(NA_ROWS)]
    return jnp.stack(per_off, axis=0).reshape(NA_ROWS, N_PAIRS, 2, GRID_W, NA_KEYS)


def _na_kernel(q_ref, k_ref, v_ref, bias_ref, out_ref, *, grid_rows, block_rows):
    gb = pl.program_id(2)
    lane = lax.broadcasted_iota(I32, (1, LANES), 1)
    head0 = lane < HEAD_DIM

    def body(it, carry):
        rows, scores = [], []
        for u in range(NA_BATCH):
            i = it * NA_BATCH + u
            g = gb * block_rows + i
            rs = jnp.clip(g - NA_ROWS // 2, 0, grid_rows - NA_ROWS)
            d0 = rs - g + NA_ROWS - 1
            qs = pl.multiple_of(i * GRID_W, GRID_W)
            ks = pl.multiple_of(rs * GRID_W, GRID_W)
            q = q_ref[pl.ds(qs, GRID_W), :]
            k = k_ref[pl.ds(ks, NA_KEYS), :]
            rows.append((qs, ks, d0))
            for h in range(2):
                sel = head0 if h == 0 else jnp.logical_not(head0)
                qh = jnp.where(sel, q, jnp.zeros_like(q))
                scores.append(lax.dot_general(qh, k, (((1,), (1,)), ((), ())), preferred_element_type=F32))
        probs, sums = [], []
        for n, s in enumerate(scores):
            s = s + bias_ref[rows[n // 2][2], n % 2]
            p = jnp.exp2(s - jnp.max(s, axis=-1, keepdims=True))
            sums.append(jnp.sum(p, axis=-1, keepdims=True))
            probs.append(p.astype(BF16))
        outs = []
        for n, p in enumerate(probs):
            v = v_ref[pl.ds(rows[n // 2][1], NA_KEYS), :]
            outs.append(jnp.dot(p, v, preferred_element_type=F32) / sums[n])
        for u, (qs, _, _) in enumerate(rows):
            out_ref[pl.ds(qs, GRID_W), :] = jnp.where(head0, outs[2 * u], outs[2 * u + 1]).astype(BF16)
        return carry

    lax.fori_loop(0, block_rows // NA_BATCH, body, 0)


def _neighbourhood_attention(qb, kb, vb, bias, B, T):
    G = T // GRID_W
    rb = min(NA_BLOCK_ROWS, G)
    nb = G // rb
    kern = functools.partial(_na_kernel, grid_rows=G, block_rows=rb)
    seq = pl.BlockSpec((None, T, LANES), lambda hp, b, gb: (hp, b, 0))
    qblk = pl.BlockSpec((None, rb * GRID_W, LANES), lambda hp, b, gb: (hp, b * nb + gb, 0))
    return pl.pallas_call(
        kern,
        grid=(N_PAIRS, B, nb),
        in_specs=[qblk, seq, seq,
                  pl.BlockSpec((NA_ROWS, None, 2, GRID_W, NA_KEYS), lambda hp, b, gb: (0, hp, 0, 0, 0))],
        out_specs=qblk,
        out_shape=jax.ShapeDtypeStruct((N_PAIRS, B * T, LANES), BF16),
        compiler_params=_cparams(("arbitrary", "arbitrary", "arbitrary")),
        name="neighbourhood_attention",
    )(qb, kb, vb, bias)


def _layer_norm(h, g, b):
    mu = jnp.mean(h, axis=-1, keepdims=True)
    d = h - mu
    var = jnp.mean(d * d, axis=-1, keepdims=True)
    return d * lax.rsqrt(var + LN_EPS) * g + b


def _outproj_nat_kernel(oa_ref, ob_ref, x_ref, mod_ref, ga_ref, gb_ref, w_ref, lg_ref, lb_ref, wr_ref, br_ref,
                        permt_ref, x1_ref, u2_ref, logit_ref, mix_scr):
    D = D_MODEL
    gate_a = mod_ref[:, 2 * D:3 * D]
    shift_f = mod_ref[:, 3 * D:4 * D]
    scale_f = mod_ref[:, 4 * D:5 * D]

    def rms(o, g):
        return o * lax.rsqrt(jnp.mean(o * o, axis=-1, keepdims=True) + RMS_EPS) * g

    rows_g = PERM // RES
    for g in range(PROJ_TOKENS // PERM):
        grouped = jnp.concatenate(
            [jnp.concatenate([oa_ref[r, hp, g * rows_g:(g + 1) * rows_g, :] for hp in range(N_PAIRS)], axis=1)
             for r in range(RES)], axis=0)
        oa = jnp.dot(permt_ref[...], grouped, preferred_element_type=F32)
        ob = jnp.concatenate([ob_ref[hp, g * PERM:(g + 1) * PERM, :] for hp in range(N_PAIRS)], axis=1).astype(F32)
        mix = jnp.concatenate([rms(oa, ga_ref[...]), rms(ob, gb_ref[...])], axis=1)
        mix_scr[g * PERM:(g + 1) * PERM, :] = mix.astype(BF16)

    wr = wr_ref[...]
    wr_hi = wr.astype(BF16)
    wr_lo = (wr - wr_hi.astype(F32)).astype(BF16)
    sub = 512
    for s in range(PROJ_TOKENS // sub):
        rows = slice(s * sub, (s + 1) * sub)
        y = jnp.dot(mix_scr[rows, :], w_ref[...], preferred_element_type=F32)
        h = DEEPNORM_ALPHA * x_ref[rows, :] + (1.0 + gate_a) * y
        x1 = _layer_norm(h, lg_ref[...], lb_ref[...])
        x1_ref[rows, :] = x1
        u2 = x1 * (1.0 + scale_f) + shift_f
        u2_ref[rows, :] = u2.astype(BF16)
        u_hi = u2.astype(BF16)
        u_lo = (u2 - u_hi.astype(F32)).astype(BF16)
        logit_ref[rows, :] = (jnp.dot(u_hi, wr_hi, preferred_element_type=F32)
                              + jnp.dot(u_lo, wr_hi, preferred_element_type=F32)
                              + jnp.dot(u_hi, wr_lo, preferred_element_type=F32)) + br_ref[...]


def _output_projection_nat(oa, ob, x, mod3, mod_row0, g_a, g_b, w_out_bf, ln_g, ln_b, w_router_pad, b_router_pad,
                           perm_t):
    B, T, D = x.shape
    N = B * T
    tps = T // PROJ_TOKENS
    const = lambda shape: pl.BlockSpec(shape, lambda i: tuple(0 for _ in shape))
    rows = lambda width: pl.BlockSpec((PROJ_TOKENS, width), lambda i: (i, 0))
    return pl.pallas_call(
        _outproj_nat_kernel,
        grid=(N // PROJ_TOKENS,),
        in_specs=[pl.BlockSpec((RES, N_PAIRS, PROJ_ROWS, LANES), lambda i: (0, 0, i, 0)),
                  pl.BlockSpec((N_PAIRS, PROJ_TOKENS, LANES), lambda i: (0, i, 0)),
                  rows(D),
                  pl.BlockSpec((None, 1, 6 * D), lambda i: (mod_row0 + i // tps, 0, 0)),
                  const((1, WIDTH)), const((1, WIDTH)), const((2 * WIDTH, D)),
                  const((1, D)), const((1, D)), const((D, LANES)), const((1, LANES)), const((PERM, PERM))],
        out_specs=[rows(D), rows(D), rows(LANES)],
        out_shape=[jax.ShapeDtypeStruct((N, D), F32), jax.ShapeDtypeStruct((N, D), BF16),
                   jax.ShapeDtypeStruct((N, LANES), F32)],
        scratch_shapes=[pltpu.VMEM((PROJ_TOKENS, 2 * WIDTH), BF16)],
        compiler_params=_cparams(("arbitrary",)),
        name="output_projection",
    )(oa, ob, x.reshape(N, D), mod3, g_a, g_b, w_out_bf, ln_g, ln_b, w_router_pad, b_router_pad, perm_t)


def _route_kernel(logit_ref, idx_ref, gate_ref, cnt_ref):
    for t in range(ROUTE_TILE // SCATTER_TILE):
        rows = slice(t * SCATTER_TILE, (t + 1) * SCATTER_TILE)
        sel, gates, cnt = _route_tile(logit_ref[rows, :])
        idx_ref[rows, :] = sel
        gate_ref[rows, :] = gates
        cnt_ref[t] = cnt


def _route_tile(raw):
    tn = raw.shape[0]
    lane = lax.broadcasted_iota(I32, (tn, LANES), 1)
    logits = jnp.where(lane < N_EXPERTS, raw, -3.0e38)
    vals, idxs = [], []
    multi = jnp.zeros((tn, LANES), F32)
    for _ in range(TOP_K):
        m = jnp.max(logits, axis=-1, keepdims=True)
        idx = jnp.min(jnp.where(logits == m, lane, LANES), axis=-1, keepdims=True)
        hot = lane == idx
        vals.append(m)
        idxs.append(idx)
        multi = multi + hot.astype(F32)
        logits = jnp.where(hot, -3.0e38, logits)
    es = [jnp.exp(v - vals[0]) for v in vals]
    tot = es[0] + es[1] + es[2] + es[3]
    lane4 = lax.broadcasted_iota(I32, (tn, TOP_K), 1)
    sel = jnp.zeros((tn, TOP_K), I32)
    gates = jnp.zeros((tn, TOP_K), F32)
    for k in range(TOP_K):
        sel = jnp.where(lane4 == k, idxs[k], sel)
        gates = jnp.where(lane4 == k, es[k] / tot, gates)
    return sel, gates, jnp.sum(multi, axis=0, keepdims=True).astype(I32)


def _routing(logits):
    N = logits.shape[0]
    tn = ROUTE_TILE
    nt = N // SCATTER_TILE
    per_step = ROUTE_TILE // SCATTER_TILE
    return pl.pallas_call(
        _route_kernel,
        grid=(N // tn,),
        in_specs=[pl.BlockSpec((tn, LANES), lambda i: (i, 0))],
        out_specs=[pl.BlockSpec((tn, TOP_K), lambda i: (i, 0)),
                   pl.BlockSpec((tn, TOP_K), lambda i: (i, 0)),
                   pl.BlockSpec((per_step, 1, LANES), lambda i: (i, 0, 0))],
        out_shape=[jax.ShapeDtypeStruct((N, TOP_K), I32), jax.ShapeDtypeStruct((N, TOP_K), F32),
                   jax.ShapeDtypeStruct((nt, 1, LANES), I32)],
        compiler_params=_cparams(("arbitrary",)),
        name="moe_routing",
    )(logits)


SEG = 8
STAGE_ROWS = SCATTER_TILE * TOP_K + N_EXPERTS * SEG


def _lane_prefix_exclusive(v):
    lane = lax.broadcasted_iota(I32, v.shape, 1)
    incl = v
    s = 1
    while s < LANES:
        incl = incl + jnp.where(lane >= s, pltpu.roll(incl, s, 1), 0.0)
        s *= 2
    return incl - v


def _tile_ranks(idx, tn):
    lane = lax.broadcasted_iota(I32, (tn, LANES), 1)
    hots = [lane == idx[:, k:k + 1] for k in range(TOP_K)]
    multi = jnp.zeros((tn, LANES), F32)
    for h in hots:
        multi = multi + h.astype(F32)
    row = lax.broadcasted_iota(I32, (tn, tn), 0)
    col = lax.broadcasted_iota(I32, (tn, tn), 1)
    lower = (col < row).astype(BF16)
    before = jnp.dot(lower, multi.astype(BF16), preferred_element_type=F32)
    return hots, before


BIG_SEG = 4 * SEG


def _segment_dma_loops(cnt_s, rows_of, copy):
    def per_expert(e, carry):
        off, n_big, n_small = carry
        nseg = (cnt_s[0, e] + SEG - 1) // SEG
        big = nseg // (BIG_SEG // SEG)
        small = nseg - big * (BIG_SEG // SEG)

        def one_big(q, c):
            src, dst = rows_of(e, off, q * BIG_SEG)
            copy(src, dst, BIG_SEG).start()
            return c

        def one_small(q, c):
            src, dst = rows_of(e, off, big * BIG_SEG + q * SEG)
            copy(src, dst, SEG).start()
            return c

        lax.fori_loop(0, big, one_big, 0)
        lax.fori_loop(0, small, one_small, 0)
        return off + nseg * SEG, n_big + big, n_small + small

    _, n_big, n_small = lax.fori_loop(0, N_EXPERTS, per_expert, (0, 0, 0))
    return n_big, n_small


def _drain(copy, n_big, n_small):
    def wait_big(q, c):
        copy(0, 0, BIG_SEG).wait()
        return c

    def wait_small(q, c):
        copy(0, 0, SEG).wait()
        return c

    lax.fori_loop(0, n_big, wait_big, 0)
    lax.fori_loop(0, n_small, wait_small, 0)


def _dispatch_kernel(*refs, first, n_steps):
    if first:
        cnt_s, start_s, tail_s, idx_ref, u_ref, cntv_ref, xs_ref, stage, zbuf, pending, sem, zsem = refs
    else:
        cnt_s, start_s, tail_s, idx_ref, u_ref, cntv_ref, _, xs_ref, stage, zbuf, pending, sem, zsem = refs
    tn = idx_ref.shape[0]
    step = pl.program_id(0)
    slot = step % 2

    if first:
        @pl.when(pl.program_id(0) == 0)
        def _():
            zbuf[...] = jnp.zeros_like(zbuf)

            def tail_copy(row):
                return pltpu.make_async_copy(zbuf, xs_ref.at[pl.ds(pl.multiple_of(row, SEG), SEG)], zsem)

            def fill(e, n):
                lo = tail_s[0, e]
                nfull = (tail_s[1, e] - lo) // SEG

                def one(q, c):
                    tail_copy(lo + q * SEG).start()
                    return c

                lax.fori_loop(0, nfull, one, 0)
                return n + nfull

            n = lax.fori_loop(0, N_EXPERTS, fill, 0)

            def drain(q, c):
                tail_copy(0).wait()
                return c

            lax.fori_loop(0, n, drain, 0)

    hots, before = _tile_ranks(idx_ref[...], tn)
    cntf = cntv_ref[...].astype(F32)
    seg_len = jnp.ceil(cntf * (1.0 / SEG)) * SEG
    seg_off = _lane_prefix_exclusive(seg_len)
    lane_r = lax.broadcasted_iota(I32, (tn, STAGE_ROWS), 1)
    onehot = jnp.zeros((tn, STAGE_ROWS), F32)
    for k in range(TOP_K):
        stage_row = jnp.sum(jnp.where(hots[k], seg_off + before, 0.0), axis=-1, keepdims=True)
        onehot = jnp.where(lane_r == stage_row.astype(I32), 1.0, onehot)
    stage[slot] = lax.dot_general(onehot.astype(BF16), u_ref[...].astype(BF16), (((0,), (0,)), ((), ())),
                                  preferred_element_type=F32)

    def copier(buf):
        def seg_copy(src_row, dst_row, rows):
            return pltpu.make_async_copy(stage.at[buf, pl.ds(pl.multiple_of(src_row, SEG), rows)],
                                         xs_ref.at[pl.ds(pl.multiple_of(dst_row, SEG), rows)], sem.at[buf])
        return seg_copy

    @pl.when(step > 0)
    def _():
        _drain(copier(1 - slot), pending[0], pending[1])

    n_big, n_small = _segment_dma_loops(
        cnt_s, lambda e, off, moved: (off + moved, start_s[0, e] + moved), copier(slot))
    pending[0] = n_big
    pending[1] = n_small

    @pl.when(step == n_steps - 1)
    def _():
        _drain(copier(slot), n_big, n_small)


def _dispatch(idx, u2, tile_cnt, tile_start, tails, xs, xs_rows, t0):
    n, D = u2.shape
    tn = SCATTER_TILE
    nt = n // tn
    first = xs is None
    smem = lambda shape, imap: pl.BlockSpec(shape, imap, memory_space=pltpu.SMEM)
    kern = functools.partial(_dispatch_kernel, first=first, n_steps=nt)
    in_specs = [smem((None, 1, LANES), lambda i: (t0 + i, 0, 0)),
                smem((None, 1, LANES), lambda i: (t0 + i, 0, 0)),
                smem((2, LANES), lambda i: (0, 0)),
                pl.BlockSpec((tn, TOP_K), lambda i: (t0 + i, 0)),
                pl.BlockSpec((tn, D), lambda i: (i, 0)),
                pl.BlockSpec((None, 1, LANES), lambda i: (t0 + i, 0, 0))]
    args = [tile_cnt, tile_start, tails, idx, u2, tile_cnt]
    alias = {}
    if not first:
        alias = {len(args): 0}
        in_specs.append(pl.BlockSpec(memory_space=pl.ANY))
        args.append(xs)
    return pl.pallas_call(
        kern,
        grid=(nt,),
        in_specs=in_specs,
        out_specs=pl.BlockSpec(memory_space=pl.ANY),
        out_shape=jax.ShapeDtypeStruct((xs_rows, D), F32),
        scratch_shapes=[pltpu.VMEM((2, STAGE_ROWS, D), F32), pltpu.VMEM((SEG, D), F32), pltpu.SMEM((2,), I32),
                        pltpu.SemaphoreType.DMA((2,)), pltpu.SemaphoreType.DMA(())],
        input_output_aliases=alias,
        compiler_params=_cparams(("arbitrary",)),
        name="moe_dispatch",
    )(*args)


DEINT = 256


def _deinterleave_matrix():
    perm = np.zeros((DEINT, DEINT), np.float32)
    j = np.arange(DEINT // 2)
    perm[2 * j, j] = 1.0
    perm[2 * j + 1, DEINT // 2 + j] = 1.0
    return perm


def _expert_kernel(be_ref, nu_ref, x_ref, wu_ref, bu_ref, wd_ref, bd_ref, p_ref, y_ref, wu_bf, wd_bf):
    j = pl.program_id(0)

    @pl.when((j == 0) | (be_ref[j] != be_ref[jnp.maximum(j - 1, 0)]))
    def _():
        half = DEINT // 2
        for g in range(2 * D_FF // DEINT):
            wg = wu_ref[:, g * DEINT:(g + 1) * DEINT].astype(BF16)
            t = jnp.dot(wg, p_ref[...], preferred_element_type=F32)
            wu_bf[:, g * half:(g + 1) * half] = t[:, :half].astype(BF16)
            wu_bf[:, D_FF + g * half:D_FF + (g + 1) * half] = t[:, half:].astype(BF16)
        wd_bf[...] = wd_ref[...].astype(BF16)

    @pl.when(j < nu_ref[0])
    def _():
        x = x_ref[...].astype(BF16)
        hu = jnp.dot(x, wu_bf[...], preferred_element_type=F32) + bu_ref[...]
        glu = jnp.minimum(hu[:, :D_FF], SWIGLU_LIMIT)
        lin = jnp.clip(hu[:, D_FF:], -SWIGLU_LIMIT, SWIGLU_LIMIT)
        act = glu * (1.0 / (1.0 + jnp.exp(-SWIGLU_ALPHA * glu))) * (lin + 1.0)
        y_ref[...] = jnp.dot(act.astype(BF16), wd_bf[...], preferred_element_type=F32) + bd_ref[...]

    @pl.when(pl.program_id(0) >= nu_ref[0])
    def _():
        y_ref[...] = jnp.zeros_like(y_ref)


def _expert_blocks(xs, blk_expert, n_used, w_up, b_up_de, w_down, b_down):
    n_rows, D = xs.shape
    nblk = n_rows // MOE_ROWS

    def xmap(j, be, nu):
        return (jnp.minimum(j, nu[0] - 1), 0)

    grid_spec = pltpu.PrefetchScalarGridSpec(
        num_scalar_prefetch=2,
        grid=(nblk,),
        in_specs=[pl.BlockSpec((MOE_ROWS, D), xmap),
                  pl.BlockSpec((None, D, 2 * D_FF), lambda j, be, nu: (be[j], 0, 0)),
                  pl.BlockSpec((None, 1, 2 * D_FF), lambda j, be, nu: (be[j], 0, 0)),
                  pl.BlockSpec((None, D_FF, D), lambda j, be, nu: (be[j], 0, 0)),
                  pl.BlockSpec((None, 1, D), lambda j, be, nu: (be[j], 0, 0)),
                  pl.BlockSpec((DEINT, DEINT), lambda j, be, nu: (0, 0))],
        out_specs=pl.BlockSpec((MOE_ROWS, D), lambda j, be, nu: (j, 0)),
        scratch_shapes=[pltpu.VMEM((D, 2 * D_FF), BF16), pltpu.VMEM((D_FF, D), BF16)],
    )
    return pl.pallas_call(
        _expert_kernel,
        grid_spec=grid_spec,
        out_shape=jax.ShapeDtypeStruct((n_rows, D), F32),
        compiler_params=_cparams(("arbitrary",)),
        name="moe_experts",
    )(blk_expert, n_used, xs, w_up, b_up_de, w_down, b_down, jnp.asarray(_deinterleave_matrix(), BF16))


def _combine_kernel(cnt_s, start_s, next_cnt_s, next_start_s, idx_ref, gate_ref, cntv_ref, y_ref, x1_ref, mod_ref,
                    lg_ref, lb_ref, out_ref, stage, pending, sem, *, n_steps):
    D = D_MODEL
    tn = idx_ref.shape[0]
    step = pl.program_id(0)
    slot = step % 2

    def copier(buf):
        def seg_copy(src_row, dst_row, rows):
            return pltpu.make_async_copy(y_ref.at[pl.ds(pl.multiple_of(src_row, SEG), rows)],
                                         stage.at[buf, pl.ds(pl.multiple_of(dst_row, SEG), rows)], sem.at[buf])
        return seg_copy

    def fetch(cnt_ref, start_ref, buf):
        n_big, n_small = _segment_dma_loops(
            cnt_ref, lambda e, off, moved: (start_ref[0, e] + moved, off + moved), copier(buf))
        pending[buf, 0] = n_big
        pending[buf, 1] = n_small

    @pl.when(step == 0)
    def _():
        stage[...] = jnp.zeros_like(stage)
        fetch(cnt_s, start_s, 0)

    @pl.when(step + 1 < n_steps)
    def _():
        fetch(next_cnt_s, next_start_s, 1 - slot)

    hots, before = _tile_ranks(idx_ref[...], tn)
    seg_len = jnp.ceil(cntv_ref[...].astype(F32) * (1.0 / SEG)) * SEG
    seg_off = _lane_prefix_exclusive(seg_len)
    gates = gate_ref[...]
    lane_r = lax.broadcasted_iota(I32, (tn, STAGE_ROWS), 1)
    weights = jnp.zeros((tn, STAGE_ROWS), F32)
    for k in range(TOP_K):
        stage_row = jnp.sum(jnp.where(hots[k], seg_off + before, 0.0), axis=-1, keepdims=True)
        weights = jnp.where(lane_r == stage_row.astype(I32), gates[:, k:k + 1], weights)

    _drain(copier(slot), pending[slot, 0], pending[slot, 1])

    y = jnp.dot(weights.astype(BF16), stage[slot].astype(BF16), preferred_element_type=F32)
    gate_f = mod_ref[:, 5 * D:6 * D]
    h = DEEPNORM_ALPHA * x1_ref[...] + (1.0 + gate_f) * y
    out_ref[...] = _layer_norm(h, lg_ref[...], lb_ref[...])


def _combine(idx, gates, tile_cnt, tile_start, y_pad, x1, mod3, mod_row0, ln_g, ln_b, B, T, t0):
    D = D_MODEL
    tn = SCATTER_TILE
    n_rows = B * T
    nt = n_rows // tn
    tps = T // tn
    smem = lambda imap: pl.BlockSpec((None, 1, LANES), imap, memory_space=pltpu.SMEM)
    this_tile = lambda i: (t0 + i, 0, 0)
    next_tile = lambda i: (t0 + jnp.minimum(i + 1, nt - 1), 0, 0)
    out = pl.pallas_call(
        functools.partial(_combine_kernel, n_steps=nt),
        grid=(nt,),
        in_specs=[smem(this_tile), smem(this_tile), smem(next_tile), smem(next_tile),
                  pl.BlockSpec((tn, TOP_K), lambda i: (t0 + i, 0)),
                  pl.BlockSpec((tn, TOP_K), lambda i: (t0 + i, 0)),
                  pl.BlockSpec((None, 1, LANES), lambda i: (t0 + i, 0, 0)),
                  pl.BlockSpec(memory_space=pl.ANY),
                  pl.BlockSpec((tn, D), lambda i: (i, 0)),
                  pl.BlockSpec((None, 1, 6 * D), lambda i: (mod_row0 + i // tps, 0, 0)),
                  pl.BlockSpec((1, D), lambda i: (0, 0)),
                  pl.BlockSpec((1, D), lambda i: (0, 0))],
        out_specs=pl.BlockSpec((tn, D), lambda i: (i, 0)),
        out_shape=jax.ShapeDtypeStruct((n_rows, D), F32),
        scratch_shapes=[pltpu.VMEM((2, STAGE_ROWS, D), F32), pltpu.SMEM((2, 2), I32),
                        pltpu.SemaphoreType.DMA((2,))],
        compiler_params=_cparams(("arbitrary",)),
        name="moe_combine",
    )(tile_cnt, tile_start, tile_cnt, tile_start, idx, gates, tile_cnt, y_pad, x1, mod3, ln_g, ln_b)
    return out.reshape(B, T, D)


def kernel(x_prompt, x_sample, c_prompt, c_sample, w_mod, b_mod, w_in, rpb, g_out_a, g_out_b, w_out, ln1_g, ln1_b,
           w_router, b_router, w_up, b_up, w_down, b_down, ln2_g, ln2_b):
    D = D_MODEL
    groups = [x_prompt, x_sample]
    conds = [c_prompt, c_sample]
    n_cond = sum(c.shape[0] for c in conds)
    pad_rows = -n_cond % 8
    c_all = jnp.concatenate(conds + [jnp.zeros((pad_rows, D), F32)], axis=0)
    mod3 = _modulation(c_all, w_mod[0], b_mod[0]).reshape(n_cond + pad_rows, 1, 6 * D)

    w_in_bf = w_in[0].astype(BF16)
    w_out_bf = w_out[0].astype(BF16)
    na_bias = _na_bias_table(rpb[0])
    g_a = g_out_a[0].reshape(1, WIDTH)
    g_b = g_out_b[0].reshape(1, WIDTH)
    ln1g, ln1b = ln1_g[0].reshape(1, D), ln1_b[0].reshape(1, D)
    ln2g, ln2b = ln2_g[0].reshape(1, D), ln2_b[0].reshape(1, D)
    w_router_pad = jnp.pad(w_router[0], ((0, 0), (0, LANES - N_EXPERTS)))
    b_router_pad = jnp.pad(b_router[0], (0, LANES - N_EXPERTS)).reshape(1, LANES)

    perm_np = _residue_permutation()
    perm, perm_t = jnp.asarray(perm_np, BF16), jnp.asarray(perm_np.T, BF16)

    x1s, u2s, logit_list = [], [], []
    mod_row0 = 0
    mod_rows = []
    for x in groups:
        B, T, _ = x.shape
        cos_v, sin_v = _rope_tables(T)
        qa, ka, va, qb, kb, vb = _input_projection_nat(x, mod3, w_in_bf, cos_v, sin_v, perm, mod_row0)
        oa = _dilated_attention(qa, ka, va, B, T)
        ob = _neighbourhood_attention(qb, kb, vb, na_bias, B, T)
        x1, u2, logits = _output_projection_nat(oa, ob, x, mod3, mod_row0, g_a, g_b, w_out_bf, ln1g, ln1b,
                                                w_router_pad, b_router_pad, perm_t)
        x1s.append(x1)
        u2s.append(u2)
        logit_list.append(logits)
        mod_rows.append(mod_row0)
        mod_row0 += B

    logits_all = jnp.concatenate(logit_list, axis=0)
    N = logits_all.shape[0]
    idx, gates, tile_cnt = _routing(logits_all)

    tc = tile_cnt[:, 0, :N_EXPERTS]
    seg = (tc + SEG - 1) // SEG * SEG
    total = jnp.sum(seg, axis=0)
    padded = (total + MOE_ROWS - 1) // MOE_ROWS * MOE_ROWS
    pad_end = jnp.cumsum(padded)
    pad_start = pad_end - padded
    tile_start = pad_start[None, :] + jnp.cumsum(seg, axis=0) - seg
    n_tiles = N // SCATTER_TILE
    n_blocks = (N * TOP_K + n_tiles * N_EXPERTS * (SEG - 1)) // MOE_ROWS + N_EXPERTS
    n_rows = n_blocks * MOE_ROWS
    blk_end = pad_end // MOE_ROWS
    n_used = blk_end[-1:].astype(I32)
    blk_expert = jnp.minimum(
        jnp.sum(blk_end[None, :] <= jnp.arange(n_blocks, dtype=I32)[:, None], axis=1), N_EXPERTS - 1).astype(I32)
    lane_pad = ((0, 0), (0, LANES - N_EXPERTS))
    tile_start3 = jnp.pad(tile_start, lane_pad).astype(I32)[:, None, :]
    tails = jnp.pad(jnp.stack([pad_start + total, pad_end]), lane_pad).astype(I32)

    xs = None
    row0 = 0
    for u2 in u2s:
        n = u2.shape[0]
        xs = _dispatch(idx, u2, tile_cnt, tile_start3, tails, xs, n_rows, row0 // SCATTER_TILE)
        row0 += n

    b_up_de = jnp.concatenate([b_up[0][:, 0::2], b_up[0][:, 1::2]], axis=-1).reshape(N_EXPERTS, 1, 2 * D_FF)
    b_down3 = b_down[0].reshape(N_EXPERTS, 1, D)
    y_pad = _expert_blocks(xs, blk_expert, n_used, w_up[0], b_up_de, w_down[0], b_down3)

    outs = []
    row0 = 0
    for x, x1, mrow in zip(groups, x1s, mod_rows):
        B, T, _ = x.shape
        n = B * T
        outs.append(_combine(idx, gates, tile_cnt, tile_start3, y_pad, x1, mod3, mrow, ln2g, ln2b, B, T,
                             row0 // SCATTER_TILE))
        row0 += n
    return tuple(outs)
```

```python
import functools

import numpy as np
import jax
import jax.numpy as jnp
from jax import lax
from jax.experimental import pallas as pl
from jax.experimental.pallas import tpu as pltpu

F32 = jnp.float32
BF16 = jnp.bfloat16
I32 = jnp.int32

D_MODEL = 1024
HEAD_DIM = 64
N_HEADS = 8
WIDTH = N_HEADS * HEAD_DIM
N_PAIRS = WIDTH // 128
ROPE_THETA = 10000.0
RADIUS = 64
GRID_W = 64
NA_ROWS = 8
NA_COLS = 16
N_EXPERTS = 32
TOP_K = 4
D_FF = 1024
SWIGLU_ALPHA = 1.702
SWIGLU_LIMIT = 7.0
DEEPNORM_ALPHA = 2.0 ** 0.25
LN_EPS = 1e-5
RMS_EPS = 1e-6
NEG = -1e30
LOG2E = 1.4426950408889634

LANES = 128
RES = 16
CHUNK = 128
RES_PER_STEP = 4
MOE_ROWS = 512
ROUTE_TILE = 1024
SCATTER_TILE = 512
VMEM_LIMIT = 56 * 1024 * 1024


def _cparams(sem, vmem=VMEM_LIMIT):
    return pltpu.CompilerParams(dimension_semantics=sem, vmem_limit_bytes=vmem)


def _mod_kernel(c_ref, w_ref, b_ref, o_ref):
    c = c_ref[...]
    s = c * (1.0 / (1.0 + jnp.exp(-c)))
    o_ref[...] = jnp.dot(s.astype(BF16), w_ref[...].astype(BF16), preferred_element_type=F32) + b_ref[...]


def _modulation(c, w_mod, b_mod):
    rows = c.shape[0]
    n_out = w_mod.shape[1]
    tn = 1024
    return pl.pallas_call(
        _mod_kernel,
        grid=(n_out // tn,),
        in_specs=[pl.BlockSpec((rows, D_MODEL), lambda j: (0, 0)),
                  pl.BlockSpec((D_MODEL, tn), lambda j: (0, j)),
                  pl.BlockSpec((1, tn), lambda j: (0, j))],
        out_specs=pl.BlockSpec((rows, tn), lambda j: (0, j)),
        out_shape=jax.ShapeDtypeStruct((rows, n_out), F32),
        compiler_params=_cparams(("arbitrary",)),
        name="modulation",
    )(c, w_mod, b_mod.reshape(1, n_out))


PERM = 256
PROJ_TOKENS = 1024
PROJ_ROWS = PROJ_TOKENS // RES


def _residue_permutation():
    p = np.zeros((PERM, PERM), np.float32)
    m, r = np.meshgrid(np.arange(PERM // RES), np.arange(RES), indexing="ij")
    p[(r * (PERM // RES) + m).reshape(-1), (m * RES + r).reshape(-1)] = 1.0
    return p


def _inproj_nat_kernel(x_ref, mod_ref, w_ref, cos_ref, sin_ref, perm_ref,
                       qa_ref, ka_ref, va_ref, qb_ref, kb_ref, vb_ref, u_nat, u_view):
    D = D_MODEL
    shift = mod_ref[:, 0:D]
    scale = mod_ref[:, D:2 * D]
    qscale = HEAD_DIM ** -0.5 * LOG2E
    u_nat[...] = (x_ref[...] * (1.0 + scale) + shift).astype(BF16)

    sub = 512
    for s in range(PROJ_TOKENS // sub):
        u = u_nat[s * sub:(s + 1) * sub, :]
        for i, (ref, mul) in enumerate(((qb_ref, qscale), (kb_ref, 1.0), (vb_ref, 1.0))):
            p = jnp.dot(u, w_ref[:, (3 + i) * WIDTH:(4 + i) * WIDTH], preferred_element_type=F32) * mul
            for hp in range(N_PAIRS):
                ref[hp, s * sub:(s + 1) * sub, :] = p[:, hp * LANES:(hp + 1) * LANES].astype(BF16)

    rows_g = PERM // RES
    for g in range(PROJ_TOKENS // PERM):
        pv = jnp.dot(perm_ref[...], u_nat[g * PERM:(g + 1) * PERM, :], preferred_element_type=F32).astype(BF16)
        for r in range(RES):
            u_view[r, g * rows_g:(g + 1) * rows_g, :] = pv[r * rows_g:(r + 1) * rows_g]

    lane = lax.broadcasted_iota(I32, (1, WIDTH), 1)
    first_half = (lane % HEAD_DIM) < (HEAD_DIM // 2)
    for q in range(RES // RES_PER_STEP):
        u = jnp.concatenate([u_view[q * RES_PER_STEP + a] for a in range(RES_PER_STEP)], axis=0)
        lo = q * RES_PER_STEP * LANES
        cosf = jnp.concatenate([jnp.tile(cos_ref[:, lo + a * LANES:lo + (a + 1) * LANES], (1, N_PAIRS))
                                for a in range(RES_PER_STEP)], axis=0)
        sinf = jnp.concatenate([jnp.tile(sin_ref[:, lo + a * LANES:lo + (a + 1) * LANES], (1, N_PAIRS))
                                for a in range(RES_PER_STEP)], axis=0)

        def rope(p):
            rot = jnp.where(first_half, pltpu.roll(p, WIDTH - HEAD_DIM // 2, 1), pltpu.roll(p, HEAD_DIM // 2, 1))
            return p * cosf + rot * sinf

        for i, ref in enumerate((qa_ref, ka_ref, va_ref)):
            p = jnp.dot(u, w_ref[:, i * WIDTH:(i + 1) * WIDTH], preferred_element_type=F32)
            if i == 0:
                p = rope(p) * qscale
            elif i == 1:
                p = rope(p)
            for a in range(RES_PER_STEP):
                for hp in range(N_PAIRS):
                    ref[q * RES_PER_STEP + a, hp] = (
                        p[a * PROJ_ROWS:(a + 1) * PROJ_ROWS, hp * LANES:(hp + 1) * LANES].astype(BF16))


def _input_projection_nat(x, mod3, w_in_bf, cos_v, sin_v, perm, mod_row0):
    B, T, D = x.shape
    N = B * T
    BL = N // RES
    tps = T // PROJ_TOKENS
    res_shape = jax.ShapeDtypeStruct((RES, N_PAIRS, BL, LANES), BF16)
    nat_shape = jax.ShapeDtypeStruct((N_PAIRS, N, LANES), BF16)
    res_spec = pl.BlockSpec((RES, N_PAIRS, PROJ_ROWS, LANES), lambda i: (0, 0, i, 0))
    nat_spec = pl.BlockSpec((N_PAIRS, PROJ_TOKENS, LANES), lambda i: (0, i, 0))
    return pl.pallas_call(
        _inproj_nat_kernel,
        grid=(N // PROJ_TOKENS,),
        in_specs=[pl.BlockSpec((PROJ_TOKENS, D), lambda i: (i, 0)),
                  pl.BlockSpec((None, 1, 6 * D), lambda i: (mod_row0 + i // tps, 0, 0)),
                  pl.BlockSpec((D, 6 * WIDTH), lambda i: (0, 0)),
                  pl.BlockSpec((PROJ_ROWS, RES * LANES), lambda i: (i % tps, 0)),
                  pl.BlockSpec((PROJ_ROWS, RES * LANES), lambda i: (i % tps, 0)),
                  pl.BlockSpec((PERM, PERM), lambda i: (0, 0))],
        out_specs=[res_spec, res_spec, res_spec, nat_spec, nat_spec, nat_spec],
        out_shape=[res_shape, res_shape, res_shape, nat_shape, nat_shape, nat_shape],
        scratch_shapes=[pltpu.VMEM((PROJ_TOKENS, D), BF16), pltpu.VMEM((RES, PROJ_ROWS, D), BF16)],
        compiler_params=_cparams(("arbitrary",)),
        name="input_projection",
    )(x.reshape(N, D), mod3, w_in_bf, cos_v, sin_v, perm)


def _rope_tables(T):
    half = HEAD_DIM // 2
    inv_freq = 1.0 / (ROPE_THETA ** (jnp.arange(half, dtype=F32) / half))
    ang = jnp.arange(T, dtype=F32)[:, None] * inv_freq[None, :]
    cos, sin = jnp.cos(ang), jnp.sin(ang)
    cos_h = jnp.concatenate([cos, cos], axis=-1)
    sin_h = jnp.concatenate([-sin, sin], axis=-1)
    cos2 = jnp.concatenate([cos_h, cos_h], axis=-1)
    sin2 = jnp.concatenate([sin_h, sin_h], axis=-1)
    L = T // RES
    return cos2.reshape(L, RES * LANES), sin2.reshape(L, RES * LANES)


P2_ROWS = 32
P2_KROWS = 64
P1_ROWS = 16
P1_KROWS = 32
P1_SHIFT = 8
HALO = 64
P3_BATCH = 8
P2_BATCH = 2
P1_BATCH = 2


def _band_tables(has_halo):
    def mask(ok):
        return np.where(ok, 0.0, NEG).astype(np.float32)
    mq = np.arange(CHUNK)[:, None]
    koff3 = np.arange(2 * CHUNK) - HALO
    b3 = mask(np.abs(mq - koff3[None, :]) <= RADIUS)
    j = np.repeat(np.arange(4), P2_ROWS)[:, None]
    a = np.tile(np.arange(P2_ROWS), 4)[:, None]
    jk = np.repeat(np.arange(4), P2_KROWS)[None, :]
    bk = np.tile(np.arange(P2_KROWS), 4)[None, :]
    b2 = mask(np.abs(4 * (a - (bk - 16)) + (j - jk)) <= RADIUS)
    koff2 = (bk - 16).reshape(-1)
    r = np.repeat(np.arange(RES), P1_ROWS)[:, None]
    a = np.tile(np.arange(P1_ROWS), RES)[:, None]
    rk = np.repeat(np.arange(RES), P1_KROWS)[None, :]
    bk = np.tile(np.arange(P1_KROWS), RES)[None, :]
    b1 = mask(np.abs(RES * (a - (bk - P1_SHIFT)) + (r - rk)) <= RADIUS)
    koff1 = (bk - P1_SHIFT).reshape(-1)
    return (jnp.asarray(b3), jnp.asarray(koff3.astype(np.int32)[None, :]),
            jnp.asarray(b2), jnp.asarray(koff2.astype(np.int32)[None, :]),
            jnp.asarray(b1), jnp.asarray(koff1.astype(np.int32)[None, :]))


def _attend_pairs(problems, head0):
    scores = []
    for q, k, _, _ in problems:
        for h in range(2):
            sel = head0 if h == 0 else jnp.logical_not(head0)
            qh = jnp.where(sel, q, jnp.zeros_like(q))
            scores.append(lax.dot_general(qh, k, (((1,), (1,)), ((), ())), preferred_element_type=F32))
    probs, stats = [], []
    for i, s in enumerate(scores):
        s = s + problems[i // 2][3]
        m = jnp.max(s, axis=-1, keepdims=True)
        p = jnp.exp2(s - m)
        stats.append((m, jnp.sum(p, axis=-1, keepdims=True)))
        probs.append(p.astype(BF16))
    outs = [jnp.dot(p, problems[i // 2][2], preferred_element_type=F32) for i, p in enumerate(probs)]
    results = []
    for i in range(len(problems)):
        (m0, l0), (m1, l1) = stats[2 * i], stats[2 * i + 1]
        results.append((jnp.where(head0, outs[2 * i], outs[2 * i + 1]),
                        jnp.where(head0, m0, m1), jnp.where(head0, l0, l1)))
    return results


def _dilated_kernel(*refs, has_halo, seq_rows, chunks_per_seq):
    if has_halo:
        (q_ref, k_ref, v_ref, kp_ref, kn_ref, vp_ref, vn_ref,
         b3_ref, o3_ref, b2_ref, o2_ref, b1_ref, o1_ref,
         out_ref, kf, vf, kf8, vf8, acc, ms, ls) = refs
    else:
        (q_ref, k_ref, v_ref, b3_ref, o3_ref, b2_ref, o2_ref, b1_ref, o1_ref,
         out_ref, kf, vf, kf8, vf8, acc, ms, ls) = refs

    c = pl.program_id(0) % chunks_per_seq
    row0 = c * CHUNK
    lane = lax.broadcasted_iota(I32, (1, LANES), 1)
    head0 = lane < HEAD_DIM

    for r in range(RES):
        kf[r, HALO:HALO + CHUNK, :] = k_ref[r]
        vf[r, HALO:HALO + CHUNK, :] = v_ref[r]
        if has_halo:
            kf[r, 0:HALO, :] = kp_ref[r]
            vf[r, 0:HALO, :] = vp_ref[r]
            kf[r, HALO + CHUNK:, :] = kn_ref[r]
            vf[r, HALO + CHUNK:, :] = vn_ref[r]
        else:
            zeros = jnp.zeros((HALO, LANES), BF16)
            for ref in (kf, vf):
                ref[r, 0:HALO, :] = zeros
                ref[r, HALO + CHUNK:, :] = zeros
        for src, dst in ((kf, kf8), (vf, vf8)):
            w32 = src[r].astype(F32)
            dst[r] = jnp.concatenate([w32[P1_SHIFT:], w32[:P1_SHIFT]], axis=0).astype(BF16)

    def in_seq(base, off_ref):
        kv_row = base + off_ref[...]
        return jnp.where((kv_row >= 0) & (kv_row < seq_rows), 0.0, NEG).astype(F32)

    def attend(problems):
        return _attend_pairs(problems, head0)

    def merge(a_old, m_old, l_old, o, m, l):
        mn = jnp.maximum(m_old, m)
        wa = jnp.exp2(m_old - mn)
        wb = jnp.exp2(m - mn)
        return a_old * wa + o * wb, mn, l_old * wa + l * wb

    bias3 = b3_ref[...] + in_seq(row0, o3_ref)

    def body3(it, carry):
        rs = [it * P3_BATCH + u for u in range(P3_BATCH)]
        res = attend([(q_ref[r], kf[r], vf[r], bias3) for r in rs])
        for r, (o, m, l) in zip(rs, res):
            acc[r] = o
            ms[r] = m
            ls[r] = l
        return carry

    lax.fori_loop(0, RES // P3_BATCH, body3, 0)

    b2 = b2_ref[...]

    def body2(it, carry):
        def gather(ref, r4, start, rows):
            return jnp.concatenate([ref[4 * j + r4, pl.ds(start, rows), :] for j in range(4)], axis=0)

        where, problems = [], []
        for u in range(P2_BATCH):
            g = it * P2_BATCH + u
            qs = pl.multiple_of(g * P2_ROWS, P2_ROWS)
            ks = pl.multiple_of(HALO - 16 + g * P2_ROWS, 16)
            bias = b2 + in_seq(row0 + g * P2_ROWS, o2_ref)
            for r4 in range(4):
                where.append((r4, qs))
                problems.append((gather(q_ref, r4, qs, P2_ROWS), gather(kf, r4, ks, P2_KROWS),
                                 gather(vf, r4, ks, P2_KROWS), bias))
        for (r4, qs), (o, m, l) in zip(where, attend(problems)):
            a_new, m_new, l_new = merge(gather(acc, r4, qs, P2_ROWS), gather(ms, r4, qs, P2_ROWS),
                                        gather(ls, r4, qs, P2_ROWS), o, m, l)
            for j in range(4):
                acc[4 * j + r4, pl.ds(qs, P2_ROWS), :] = a_new[j * P2_ROWS:(j + 1) * P2_ROWS]
                ms[4 * j + r4, pl.ds(qs, P2_ROWS), :] = m_new[j * P2_ROWS:(j + 1) * P2_ROWS]
                ls[4 * j + r4, pl.ds(qs, P2_ROWS), :] = l_new[j * P2_ROWS:(j + 1) * P2_ROWS]
        return carry

    lax.fori_loop(0, CHUNK // P2_ROWS // P2_BATCH, body2, 0)

    b1 = b1_ref[...]

    def body1(it, carry):
        def gather(ref, start, rows):
            return jnp.concatenate([ref[r, pl.ds(start, rows), :] for r in range(RES)], axis=0)

        starts, problems = [], []
        for u in range(P1_BATCH):
            g = it * P1_BATCH + u
            qs = pl.multiple_of(g * P1_ROWS, P1_ROWS)
            ks = pl.multiple_of(HALO - 16 + g * P1_ROWS, 16)
            starts.append(qs)
            problems.append((gather(q_ref, qs, P1_ROWS), gather(kf8, ks, P1_KROWS), gather(vf8, ks, P1_KROWS),
                             b1 + in_seq(row0 + g * P1_ROWS, o1_ref)))
        for qs, (o, m, l) in zip(starts, attend(problems)):
            a_new, m_new, l_new = merge(gather(acc, qs, P1_ROWS), gather(ms, qs, P1_ROWS),
                                        gather(ls, qs, P1_ROWS), o, m, l)
            for r in range(RES):
                acc[r, pl.ds(qs, P1_ROWS), :] = a_new[r * P1_ROWS:(r + 1) * P1_ROWS]
                ms[r, pl.ds(qs, P1_ROWS), :] = m_new[r * P1_ROWS:(r + 1) * P1_ROWS]
                ls[r, pl.ds(qs, P1_ROWS), :] = l_new[r * P1_ROWS:(r + 1) * P1_ROWS]
        return carry

    lax.fori_loop(0, CHUNK // P1_ROWS // P1_BATCH, body1, 0)

    def body_out(r, carry):
        out_ref[r] = (acc[r] / ls[r]).astype(BF16)
        return carry

    lax.fori_loop(0, RES, body_out, 0)


def _dilated_attention(qa, ka, va, B, T):
    L = T // RES
    BL = B * L
    cps = L // CHUNK
    has_halo = cps > 1
    tables = _band_tables(has_halo)
    blk = (RES, None, CHUNK, LANES)
    center = pl.BlockSpec(blk, lambda i, hp: (0, hp, i, 0))
    in_specs = [center, center, center]
    args = [qa, ka, va]
    if has_halo:
        hblk = (RES, None, HALO, LANES)
        per = CHUNK // HALO

        def prev_map(i, hp):
            return (0, hp, jnp.maximum(per * i - 1, (i // cps) * cps * per), 0)

        def next_map(i, hp):
            return (0, hp, jnp.minimum(per * i + per, (i // cps + 1) * cps * per - 1), 0)

        in_specs += [pl.BlockSpec(hblk, prev_map), pl.BlockSpec(hblk, next_map),
                     pl.BlockSpec(hblk, prev_map), pl.BlockSpec(hblk, next_map)]
        args += [ka, ka, va, va]
    for t in tables:
        in_specs.append(pl.BlockSpec(t.shape, lambda i, hp: (0, 0)))
        args.append(t)
    kern = functools.partial(_dilated_kernel, has_halo=has_halo, seq_rows=L, chunks_per_seq=cps)
    return pl.pallas_call(
        kern,
        grid=(BL // CHUNK, N_PAIRS),
        in_specs=in_specs,
        out_specs=pl.BlockSpec(blk, lambda i, hp: (0, hp, i, 0)),
        out_shape=jax.ShapeDtypeStruct((RES, N_PAIRS, BL, LANES), BF16),
        scratch_shapes=[pltpu.VMEM((RES, 2 * CHUNK, LANES), BF16)] * 4 + [
                        pltpu.VMEM((RES, CHUNK, LANES), F32),
                        pltpu.VMEM((RES, CHUNK, LANES), F32),
                        pltpu.VMEM((RES, CHUNK, LANES), F32)],
        compiler_params=_cparams(("arbitrary", "arbitrary")),
        name="dilated_attention",
    )(*args)


NA_KEYS = NA_ROWS * GRID_W
NA_BLOCK_ROWS = 32
NA_BATCH = 32


def _na_bias_table(rpb):
    c = np.arange(GRID_W)
    col_start = np.clip(c - NA_COLS // 2, 0, GRID_W - NA_COLS)
    col_mask = (c[None, :] >= col_start[:, None]) & (c[None, :] < col_start[:, None] + NA_COLS)
    dc_idx = np.clip(c[None, :] - c[:, None], -(NA_COLS - 1), NA_COLS - 1) + NA_COLS - 1
    rel = rpb.astype(F32)[:, :, dc_idx] * LOG2E
    rel = jnp.where(col_mask[None, None], rel, NEG)
    per_off = [rel[:, d0:d0 + NA_ROWS].transpose(0, 2, 1, 3).reshape(N_HEADS, GRID_W, NA_KEYS)
               for d0 in range(NA_ROWS)]
    return jnp.stack(per_off, axis=0).reshape(NA_ROWS, N_PAIRS, 2, GRID_W, NA_KEYS)


def _na_kernel(q_ref, k_ref, v_ref, bias_ref, out_ref, *, grid_rows, block_rows):
    gb = pl.program_id(2)
    lane = lax.broadcasted_iota(I32, (1, LANES), 1)
    head0 = lane < HEAD_DIM

    def body(it, carry):
        rows, scores = [], []
        for u in range(NA_BATCH):
            i = it * NA_BATCH + u
            g = gb * block_rows + i
            rs = jnp.clip(g - NA_ROWS // 2, 0, grid_rows - NA_ROWS)
            d0 = rs - g + NA_ROWS - 1
            qs = pl.multiple_of(i * GRID_W, GRID_W)
            ks = pl.multiple_of(rs * GRID_W, GRID_W)
            q = q_ref[pl.ds(qs, GRID_W), :]
            k = k_ref[pl.ds(ks, NA_KEYS), :]
            rows.append((qs, ks, d0))
            for h in range(2):
                sel = head0 if h == 0 else jnp.logical_not(head0)
                qh = jnp.where(sel, q, jnp.zeros_like(q))
                scores.append(lax.dot_general(qh, k, (((1,), (1,)), ((), ())), preferred_element_type=F32))
        probs, sums = [], []
        for n, s in enumerate(scores):
            s = s + bias_ref[rows[n // 2][2], n % 2]
            p = jnp.exp2(s - jnp.max(s, axis=-1, keepdims=True))
            sums.append(jnp.sum(p, axis=-1, keepdims=True))
            probs.append(p.astype(BF16))
        outs = []
        for n, p in enumerate(probs):
            v = v_ref[pl.ds(rows[n // 2][1], NA_KEYS), :]
            outs.append(jnp.dot(p, v, preferred_element_type=F32) / sums[n])
        for u, (qs, _, _) in enumerate(rows):
            out_ref[pl.ds(qs, GRID_W), :] = jnp.where(head0, outs[2 * u], outs[2 * u + 1]).astype(BF16)
        return carry

    lax.fori_loop(0, block_rows // NA_BATCH, body, 0)


def _neighbourhood_attention(qb, kb, vb, bias, B, T):
    G = T // GRID_W
    rb = min(NA_BLOCK_ROWS, G)
    nb = G // rb
    kern = functools.partial(_na_kernel, grid_rows=G, block_rows=rb)
    seq = pl.BlockSpec((None, T, LANES), lambda hp, b, gb: (hp, b, 0))
    qblk = pl.BlockSpec((None, rb * GRID_W, LANES), lambda hp, b, gb: (hp, b * nb + gb, 0))
    return pl.pallas_call(
        kern,
        grid=(N_PAIRS, B, nb),
        in_specs=[qblk, seq, seq,
                  pl.BlockSpec((NA_ROWS, None, 2, GRID_W, NA_KEYS), lambda hp, b, gb: (0, hp, 0, 0, 0))],
        out_specs=qblk,
        out_shape=jax.ShapeDtypeStruct((N_PAIRS, B * T, LANES), BF16),
        compiler_params=_cparams(("arbitrary", "arbitrary", "arbitrary")),
        name="neighbourhood_attention",
    )(qb, kb, vb, bias)


def _layer_norm(h, g, b):
    mu = jnp.mean(h, axis=-1, keepdims=True)
    d = h - mu
    var = jnp.mean(d * d, axis=-1, keepdims=True)
    return d * lax.rsqrt(var + LN_EPS) * g + b


def _outproj_nat_kernel(oa_ref, ob_ref, x_ref, mod_ref, ga_ref, gb_ref, w_ref, lg_ref, lb_ref, wr_ref, br_ref,
                        permt_ref, x1_ref, u2_ref, logit_ref, mix_scr):
    D = D_MODEL
    gate_a = mod_ref[:, 2 * D:3 * D]
    shift_f = mod_ref[:, 3 * D:4 * D]
    scale_f = mod_ref[:, 4 * D:5 * D]

    def rms(o, g):
        return o * lax.rsqrt(jnp.mean(o * o, axis=-1, keepdims=True) + RMS_EPS) * g

    rows_g = PERM // RES
    for g in range(PROJ_TOKENS // PERM):
        grouped = jnp.concatenate(
            [jnp.concatenate([oa_ref[r, hp, g * rows_g:(g + 1) * rows_g, :] for hp in range(N_PAIRS)], axis=1)
             for r in range(RES)], axis=0)
        oa = jnp.dot(permt_ref[...], grouped, preferred_element_type=F32)
        ob = jnp.concatenate([ob_ref[hp, g * PERM:(g + 1) * PERM, :] for hp in range(N_PAIRS)], axis=1).astype(F32)
        mix = jnp.concatenate([rms(oa, ga_ref[...]), rms(ob, gb_ref[...])], axis=1)
        mix_scr[g * PERM:(g + 1) * PERM, :] = mix.astype(BF16)

    wr = wr_ref[...]
    wr_hi = wr.astype(BF16)
    wr_lo = (wr - wr_hi.astype(F32)).astype(BF16)
    sub = 512
    for s in range(PROJ_TOKENS // sub):
        rows = slice(s * sub, (s + 1) * sub)
        y = jnp.dot(mix_scr[rows, :], w_ref[...], preferred_element_type=F32)
        h = DEEPNORM_ALPHA * x_ref[rows, :] + (1.0 + gate_a) * y
        x1 = _layer_norm(h, lg_ref[...], lb_ref[...])
        x1_ref[rows, :] = x1
        u2 = x1 * (1.0 + scale_f) + shift_f
        u2_ref[rows, :] = u2.astype(BF16)
        u_hi = u2.astype(BF16)
        u_lo = (u2 - u_hi.astype(F32)).astype(BF16)
        logit_ref[rows, :] = (jnp.dot(u_hi, wr_hi, preferred_element_type=F32)
                              + jnp.dot(u_lo, wr_hi, preferred_element_type=F32)
                              + jnp.dot(u_hi, wr_lo, preferred_element_type=F32)) + br_ref[...]


def _output_projection_nat(oa, ob, x, mod3, mod_row0, g_a, g_b, w_out_bf, ln_g, ln_b, w_router_pad, b_router_pad,
                           perm_t):
    B, T, D = x.shape
    N = B * T
    tps = T // PROJ_TOKENS
    const = lambda shape: pl.BlockSpec(shape, lambda i: tuple(0 for _ in shape))
    rows = lambda width: pl.BlockSpec((PROJ_TOKENS, width), lambda i: (i, 0))
    return pl.pallas_call(
        _outproj_nat_kernel,
        grid=(N // PROJ_TOKENS,),
        in_specs=[pl.BlockSpec((RES, N_PAIRS, PROJ_ROWS, LANES), lambda i: (0, 0, i, 0)),
                  pl.BlockSpec((N_PAIRS, PROJ_TOKENS, LANES), lambda i: (0, i, 0)),
                  rows(D),
                  pl.BlockSpec((None, 1, 6 * D), lambda i: (mod_row0 + i // tps, 0, 0)),
                  const((1, WIDTH)), const((1, WIDTH)), const((2 * WIDTH, D)),
                  const((1, D)), const((1, D)), const((D, LANES)), const((1, LANES)), const((PERM, PERM))],
        out_specs=[rows(D), rows(D), rows(LANES)],
        out_shape=[jax.ShapeDtypeStruct((N, D), F32), jax.ShapeDtypeStruct((N, D), BF16),
                   jax.ShapeDtypeStruct((N, LANES), F32)],
        scratch_shapes=[pltpu.VMEM((PROJ_TOKENS, 2 * WIDTH), BF16)],
        compiler_params=_cparams(("arbitrary",)),
        name="output_projection",
    )(oa, ob, x.reshape(N, D), mod3, g_a, g_b, w_out_bf, ln_g, ln_b, w_router_pad, b_router_pad, perm_t)


def _route_kernel(logit_ref, idx_ref, gate_ref, cnt_ref):
    for t in range(ROUTE_TILE // SCATTER_TILE):
        rows = slice(t * SCATTER_TILE, (t + 1) * SCATTER_TILE)
        sel, gates, cnt = _route_tile(logit_ref[rows, :])
        idx_ref[rows, :] = sel
        gate_ref[rows, :] = gates
        cnt_ref[t] = cnt


def _route_tile(raw):
    tn = raw.shape[0]
    lane = lax.broadcasted_iota(I32, (tn, LANES), 1)
    logits = jnp.where(lane < N_EXPERTS, raw, -3.0e38)
    vals, idxs = [], []
    multi = jnp.zeros((tn, LANES), F32)
    for _ in range(TOP_K):
        m = jnp.max(logits, axis=-1, keepdims=True)
        idx = jnp.min(jnp.where(logits == m, lane, LANES), axis=-1, keepdims=True)
        hot = lane == idx
        vals.append(m)
        idxs.append(idx)
        multi = multi + hot.astype(F32)
        logits = jnp.where(hot, -3.0e38, logits)
    es = [jnp.exp(v - vals[0]) for v in vals]
    tot = es[0] + es[1] + es[2] + es[3]
    lane4 = lax.broadcasted_iota(I32, (tn, TOP_K), 1)
    sel = jnp.zeros((tn, TOP_K), I32)
    gates = jnp.zeros((tn, TOP_K), F32)
    for k in range(TOP_K):
        sel = jnp.where(lane4 == k, idxs[k], sel)
        gates = jnp.where(lane4 == k, es[k] / tot, gates)
    return sel, gates, jnp.sum(multi, axis=0, keepdims=True).astype(I32)


def _routing(logits):
    N = logits.shape[0]
    tn = ROUTE_TILE
    nt = N // SCATTER_TILE
    per_step = ROUTE_TILE // SCATTER_TILE
    return pl.pallas_call(
        _route_kernel,
        grid=(N // tn,),
        in_specs=[pl.BlockSpec((tn, LANES), lambda i: (i, 0))],
        out_specs=[pl.BlockSpec((tn, TOP_K), lambda i: (i, 0)),
                   pl.BlockSpec((tn, TOP_K), lambda i: (i, 0)),
                   pl.BlockSpec((per_step, 1, LANES), lambda i: (i, 0, 0))],
        out_shape=[jax.ShapeDtypeStruct((N, TOP_K), I32), jax.ShapeDtypeStruct((N, TOP_K), F32),
                   jax.ShapeDtypeStruct((nt, 1, LANES), I32)],
        compiler_params=_cparams(("arbitrary",)),
        name="moe_routing",
    )(logits)


SEG = 8
STAGE_ROWS = SCATTER_TILE * TOP_K + N_EXPERTS * SEG


def _lane_prefix_exclusive(v):
    lane = lax.broadcasted_iota(I32, v.shape, 1)
    incl = v
    s = 1
    while s < LANES:
        incl = incl + jnp.where(lane >= s, pltpu.roll(incl, s, 1), 0.0)
        s *= 2
    return incl - v


def _tile_ranks(idx, tn):
    lane = lax.broadcasted_iota(I32, (tn, LANES), 1)
    hots = [lane == idx[:, k:k + 1] for k in range(TOP_K)]
    multi = jnp.zeros((tn, LANES), F32)
    for h in hots:
        multi = multi + h.astype(F32)
    row = lax.broadcasted_iota(I32, (tn, tn), 0)
    col = lax.broadcasted_iota(I32, (tn, tn), 1)
    lower = (col < row).astype(BF16)
    before = jnp.dot(lower, multi.astype(BF16), preferred_element_type=F32)
    return hots, before


BIG_SEG = 4 * SEG


def _segment_dma_loops(cnt_s, rows_of, copy):
    def per_expert(e, carry):
        off, n_big, n_small = carry
        nseg = (cnt_s[0, e] + SEG - 1) // SEG
        big = nseg // (BIG_SEG // SEG)
        small = nseg - big * (BIG_SEG // SEG)

        def one_big(q, c):
            src, dst = rows_of(e, off, q * BIG_SEG)
            copy(src, dst, BIG_SEG).start()
            return c

        def one_small(q, c):
            src, dst = rows_of(e, off, big * BIG_SEG + q * SEG)
            copy(src, dst, SEG).start()
            return c

        lax.fori_loop(0, big, one_big, 0)
        lax.fori_loop(0, small, one_small, 0)
        return off + nseg * SEG, n_big + big, n_small + small

    _, n_big, n_small = lax.fori_loop(0, N_EXPERTS, per_expert, (0, 0, 0))
    return n_big, n_small


def _drain(copy, n_big, n_small):
    def wait_big(q, c):
        copy(0, 0, BIG_SEG).wait()
        return c

    def wait_small(q, c):
        copy(0, 0, SEG).wait()
        return c

    lax.fori_loop(0, n_big, wait_big, 0)
    lax.fori_loop(0, n_small, wait_small, 0)


def _dispatch_kernel(*refs, first, n_steps):
    if first:
        cnt_s, start_s, tail_s, idx_ref, u_ref, cntv_ref, xs_ref, stage, zbuf, pending, sem, zsem = refs
    else:
        cnt_s, start_s, tail_s, idx_ref, u_ref, cntv_ref, _, xs_ref, stage, zbuf, pending, sem, zsem = refs
    tn = idx_ref.shape[0]
    step = pl.program_id(0)
    slot = step % 2

    if first:
        @pl.when(pl.program_id(0) == 0)
        def _():
            zbuf[...] = jnp.zeros_like(zbuf)

            def tail_copy(row):
                return pltpu.make_async_copy(zbuf, xs_ref.at[pl.ds(pl.multiple_of(row, SEG), SEG)], zsem)

            def fill(e, n):
                lo = tail_s[0, e]
                nfull = (tail_s[1, e] - lo) // SEG

                def one(q, c):
                    tail_copy(lo + q * SEG).start()
                    return c

                lax.fori_loop(0, nfull, one, 0)
                return n + nfull

            n = lax.fori_loop(0, N_EXPERTS, fill, 0)

            def drain(q, c):
                tail_copy(0).wait()
                return c

            lax.fori_loop(0, n, drain, 0)

    hots, before = _tile_ranks(idx_ref[...], tn)
    cntf = cntv_ref[...].astype(F32)
    seg_len = jnp.ceil(cntf * (1.0 / SEG)) * SEG
    seg_off = _lane_prefix_exclusive(seg_len)
    lane_r = lax.broadcasted_iota(I32, (tn, STAGE_ROWS), 1)
    onehot = jnp.zeros((tn, STAGE_ROWS), F32)
    for k in range(TOP_K):
        stage_row = jnp.sum(jnp.where(hots[k], seg_off + before, 0.0), axis=-1, keepdims=True)
        onehot = jnp.where(lane_r == stage_row.astype(I32), 1.0, onehot)
    stage[slot] = lax.dot_general(onehot.astype(BF16), u_ref[...].astype(BF16), (((0,), (0,)), ((), ())),
                                  preferred_element_type=F32)

    def copier(buf):
        def seg_copy(src_row, dst_row, rows):
            return pltpu.make_async_copy(stage.at[buf, pl.ds(pl.multiple_of(src_row, SEG), rows)],
                                         xs_ref.at[pl.ds(pl.multiple_of(dst_row, SEG), rows)], sem.at[buf])
        return seg_copy

    @pl.when(step > 0)
    def _():
        _drain(copier(1 - slot), pending[0], pending[1])

    n_big, n_small = _segment_dma_loops(
        cnt_s, lambda e, off, moved: (off + moved, start_s[0, e] + moved), copier(slot))
    pending[0] = n_big
    pending[1] = n_small

    @pl.when(step == n_steps - 1)
    def _():
        _drain(copier(slot), n_big, n_small)


def _dispatch(idx, u2, tile_cnt, tile_start, tails, xs, xs_rows, t0):
    n, D = u2.shape
    tn = SCATTER_TILE
    nt = n // tn
    first = xs is None
    smem = lambda shape, imap: pl.BlockSpec(shape, imap, memory_space=pltpu.SMEM)
    kern = functools.partial(_dispatch_kernel, first=first, n_steps=nt)
    in_specs = [smem((None, 1, LANES), lambda i: (t0 + i, 0, 0)),
                smem((None, 1, LANES), lambda i: (t0 + i, 0, 0)),
                smem((2, LANES), lambda i: (0, 0)),
                pl.BlockSpec((tn, TOP_K), lambda i: (t0 + i, 0)),
                pl.BlockSpec((tn, D), lambda i: (i, 0)),
                pl.BlockSpec((None, 1, LANES), lambda i: (t0 + i, 0, 0))]
    args = [tile_cnt, tile_start, tails, idx, u2, tile_cnt]
    alias = {}
    if not first:
        alias = {len(args): 0}
        in_specs.append(pl.BlockSpec(memory_space=pl.ANY))
        args.append(xs)
    return pl.pallas_call(
        kern,
        grid=(nt,),
        in_specs=in_specs,
        out_specs=pl.BlockSpec(memory_space=pl.ANY),
        out_shape=jax.ShapeDtypeStruct((xs_rows, D), F32),
        scratch_shapes=[pltpu.VMEM((2, STAGE_ROWS, D), F32), pltpu.VMEM((SEG, D), F32), pltpu.SMEM((2,), I32),
                        pltpu.SemaphoreType.DMA((2,)), pltpu.SemaphoreType.DMA(())],
        input_output_aliases=alias,
        compiler_params=_cparams(("arbitrary",)),
        name="moe_dispatch",
    )(*args)


DEINT = 256


def _deinterleave_matrix():
    perm = np.zeros((DEINT, DEINT), np.float32)
    j = np.arange(DEINT // 2)
    perm[2 * j, j] = 1.0
    perm[2 * j + 1, DEINT // 2 + j] = 1.0
    return perm


def _expert_kernel(be_ref, nu_ref, x_ref, wu_ref, bu_ref, wd_ref, bd_ref, p_ref, y_ref, wu_bf, wd_bf):
    j = pl.program_id(0)

    @pl.when((j == 0) | (be_ref[j] != be_ref[jnp.maximum(j - 1, 0)]))
    def _():
        half = DEINT // 2
        for g in range(2 * D_FF // DEINT):
            wg = wu_ref[:, g * DEINT:(g + 1) * DEINT].astype(BF16)
            t = jnp.dot(wg, p_ref[...], preferred_element_type=F32)
            wu_bf[:, g * half:(g + 1) * half] = t[:, :half].astype(BF16)
            wu_bf[:, D_FF + g * half:D_FF + (g + 1) * half] = t[:, half:].astype(BF16)
        wd_bf[...] = wd_ref[...].astype(BF16)

    @pl.when(j < nu_ref[0])
    def _():
        x = x_ref[...].astype(BF16)
        hu = jnp.dot(x, wu_bf[...], preferred_element_type=F32) + bu_ref[...]
        glu = jnp.minimum(hu[:, :D_FF], SWIGLU_LIMIT)
        lin = jnp.clip(hu[:, D_FF:], -SWIGLU_LIMIT, SWIGLU_LIMIT)
        act = glu * (1.0 / (1.0 + jnp.exp(-SWIGLU_ALPHA * glu))) * (lin + 1.0)
        y_ref[...] = jnp.dot(act.astype(BF16), wd_bf[...], preferred_element_type=F32) + bd_ref[...]

    @pl.when(pl.program_id(0) >= nu_ref[0])
    def _():
        y_ref[...] = jnp.zeros_like(y_ref)


def _expert_blocks(xs, blk_expert, n_used, w_up, b_up_de, w_down, b_down):
    n_rows, D = xs.shape
    nblk = n_rows // MOE_ROWS

    def xmap(j, be, nu):
        return (jnp.minimum(j, nu[0] - 1), 0)

    grid_spec = pltpu.PrefetchScalarGridSpec(
        num_scalar_prefetch=2,
        grid=(nblk,),
        in_specs=[pl.BlockSpec((MOE_ROWS, D), xmap),
                  pl.BlockSpec((None, D, 2 * D_FF), lambda j, be, nu: (be[j], 0, 0)),
                  pl.BlockSpec((None, 1, 2 * D_FF), lambda j, be, nu: (be[j], 0, 0)),
                  pl.BlockSpec((None, D_FF, D), lambda j, be, nu: (be[j], 0, 0)),
                  pl.BlockSpec((None, 1, D), lambda j, be, nu: (be[j], 0, 0)),
                  pl.BlockSpec((DEINT, DEINT), lambda j, be, nu: (0, 0))],
        out_specs=pl.BlockSpec((MOE_ROWS, D), lambda j, be, nu: (j, 0)),
        scratch_shapes=[pltpu.VMEM((D, 2 * D_FF), BF16), pltpu.VMEM((D_FF, D), BF16)],
    )
    return pl.pallas_call(
        _expert_kernel,
        grid_spec=grid_spec,
        out_shape=jax.ShapeDtypeStruct((n_rows, D), F32),
        compiler_params=_cparams(("arbitrary",)),
        name="moe_experts",
    )(blk_expert, n_used, xs, w_up, b_up_de, w_down, b_down, jnp.asarray(_deinterleave_matrix(), BF16))


def _combine_kernel(cnt_s, start_s, idx_ref, gate_ref, cntv_ref, y_ref, x1_ref, mod_ref,
                    lg_ref, lb_ref, out_ref, stage, sem):
    D = D_MODEL
    tn = idx_ref.shape[0]

    def seg_copy(src_row, dst_row, rows):
        return pltpu.make_async_copy(y_ref.at[pl.ds(pl.multiple_of(src_row, SEG), rows)],
                                     stage.at[pl.ds(pl.multiple_of(dst_row, SEG), rows)], sem)

    @pl.when(pl.program_id(0) == 0)
    def _():
        stage[...] = jnp.zeros_like(stage)

    n_big, n_small = _segment_dma_loops(
        cnt_s, lambda e, off, moved: (start_s[0, e] + moved, off + moved), seg_copy)

    hots, before = _tile_ranks(idx_ref[...], tn)
    seg_len = jnp.ceil(cntv_ref[...].astype(F32) * (1.0 / SEG)) * SEG
    seg_off = _lane_prefix_exclusive(seg_len)
    gates = gate_ref[...]
    lane_r = lax.broadcasted_iota(I32, (tn, STAGE_ROWS), 1)
    weights = jnp.zeros((tn, STAGE_ROWS), F32)
    for k in range(TOP_K):
        stage_row = jnp.sum(jnp.where(hots[k], seg_off + before, 0.0), axis=-1, keepdims=True)
        weights = jnp.where(lane_r == stage_row.astype(I32), gates[:, k:k + 1], weights)

    _drain(seg_copy, n_big, n_small)

    y = jnp.dot(weights.astype(BF16), stage[...].astype(BF16), preferred_element_type=F32)
    gate_f = mod_ref[:, 5 * D:6 * D]
    h = DEEPNORM_ALPHA * x1_ref[...] + (1.0 + gate_f) * y
    out_ref[...] = _layer_norm(h, lg_ref[...], lb_ref[...])


def _combine(idx, gates, tile_cnt, tile_start, y_pad, x1, mod3, mod_row0, ln_g, ln_b, B, T, t0):
    D = D_MODEL
    tn = SCATTER_TILE
    n_rows = B * T
    nt = n_rows // tn
    tps = T // tn
    smem = lambda imap: pl.BlockSpec((None, 1, LANES), imap, memory_space=pltpu.SMEM)
    out = pl.pallas_call(
        _combine_kernel,
        grid=(nt,),
        in_specs=[smem(lambda i: (t0 + i, 0, 0)), smem(lambda i: (t0 + i, 0, 0)),
                  pl.BlockSpec((tn, TOP_K), lambda i: (t0 + i, 0)),
                  pl.BlockSpec((tn, TOP_K), lambda i: (t0 + i, 0)),
                  pl.BlockSpec((None, 1, LANES), lambda i: (t0 + i, 0, 0)),
                  pl.BlockSpec(memory_space=pl.ANY),
                  pl.BlockSpec((tn, D), lambda i: (i, 0)),
                  pl.BlockSpec((None, 1, 6 * D), lambda i: (mod_row0 + i // tps, 0, 0)),
                  pl.BlockSpec((1, D), lambda i: (0, 0)),
                  pl.BlockSpec((1, D), lambda i: (0, 0))],
        out_specs=pl.BlockSpec((tn, D), lambda i: (i, 0)),
        out_shape=jax.ShapeDtypeStruct((n_rows, D), F32),
        scratch_shapes=[pltpu.VMEM((STAGE_ROWS, D), F32), pltpu.SemaphoreType.DMA(())],
        compiler_params=_cparams(("arbitrary",)),
        name="moe_combine",
    )(tile_cnt, tile_start, idx, gates, tile_cnt, y_pad, x1, mod3, ln_g, ln_b)
    return out.reshape(B, T, D)


def kernel(x_prompt, x_sample, c_prompt, c_sample, w_mod, b_mod, w_in, rpb, g_out_a, g_out_b, w_out, ln1_g, ln1_b,
           w_router, b_router, w_up, b_up, w_down, b_down, ln2_g, ln2_b):
    D = D_MODEL
    groups = [x_prompt, x_sample]
    conds = [c_prompt, c_sample]
    for x in groups:
        assert x.shape[-1] == D and x.shape[1] % (CHUNK * RES) == 0, x.shape
        assert (x.shape[0] * x.shape[1]) % max(ROUTE_TILE, SCATTER_TILE, PROJ_TOKENS) == 0, x.shape
    assert w_up.shape == (1, N_EXPERTS, D, 2 * D_FF) and w_in.shape == (1, D, 6 * WIDTH), "single layer, fixed widths"
    n_cond = sum(c.shape[0] for c in conds)
    pad_rows = -n_cond % 8
    c_all = jnp.concatenate(conds + [jnp.zeros((pad_rows, D), F32)], axis=0)
    mod3 = _modulation(c_all, w_mod[0], b_mod[0]).reshape(n_cond + pad_rows, 1, 6 * D)

    w_in_bf = w_in[0].astype(BF16)
    w_out_bf = w_out[0].astype(BF16)
    na_bias = _na_bias_table(rpb[0])
    g_a = g_out_a[0].reshape(1, WIDTH)
    g_b = g_out_b[0].reshape(1, WIDTH)
    ln1g, ln1b = ln1_g[0].reshape(1, D), ln1_b[0].reshape(1, D)
    ln2g, ln2b = ln2_g[0].reshape(1, D), ln2_b[0].reshape(1, D)
    w_router_pad = jnp.pad(w_router[0], ((0, 0), (0, LANES - N_EXPERTS)))
    b_router_pad = jnp.pad(b_router[0], (0, LANES - N_EXPERTS)).reshape(1, LANES)

    perm_np = _residue_permutation()
    perm, perm_t = jnp.asarray(perm_np, BF16), jnp.asarray(perm_np.T, BF16)

    x1s, u2s, logit_list = [], [], []
    mod_row0 = 0
    mod_rows = []
    for x in groups:
        B, T, _ = x.shape
        cos_v, sin_v = _rope_tables(T)
        qa, ka, va, qb, kb, vb = _input_projection_nat(x, mod3, w_in_bf, cos_v, sin_v, perm, mod_row0)
        oa = _dilated_attention(qa, ka, va, B, T)
        ob = _neighbourhood_attention(qb, kb, vb, na_bias, B, T)
        x1, u2, logits = _output_projection_nat(oa, ob, x, mod3, mod_row0, g_a, g_b, w_out_bf, ln1g, ln1b,
                                                w_router_pad, b_router_pad, perm_t)
        x1s.append(x1)
        u2s.append(u2)
        logit_list.append(logits)
        mod_rows.append(mod_row0)
        mod_row0 += B

    logits_all = jnp.concatenate(logit_list, axis=0)
    N = logits_all.shape[0]
    idx, gates, tile_cnt = _routing(logits_all)

    tc = tile_cnt[:, 0, :N_EXPERTS]
    seg = (tc + SEG - 1) // SEG * SEG
    total = jnp.sum(seg, axis=0)
    padded = (total + MOE_ROWS - 1) // MOE_ROWS * MOE_ROWS
    pad_end = jnp.cumsum(padded)
    pad_start = pad_end - padded
    tile_start = pad_start[None, :] + jnp.cumsum(seg, axis=0) - seg
    n_tiles = N // SCATTER_TILE
    n_blocks = (N * TOP_K + n_tiles * N_EXPERTS * (SEG - 1)) // MOE_ROWS + N_EXPERTS
    n_rows = n_blocks * MOE_ROWS
    blk_end = pad_end // MOE_ROWS
    n_used = blk_end[-1:].astype(I32)
    blk_expert = jnp.minimum(
        jnp.sum(blk_end[None, :] <= jnp.arange(n_blocks, dtype=I32)[:, None], axis=1), N_EXPERTS - 1).astype(I32)
    lane_pad = ((0, 0), (0, LANES - N_EXPERTS))
    tile_start3 = jnp.pad(tile_start, lane_pad).astype(I32)[:, None, :]
    tails = jnp.pad(jnp.stack([pad_start + total, pad_end]), lane_pad).astype(I32)

    xs = None
    row0 = 0
    for u2 in u2s:
        n = u2.shape[0]
        xs = _dispatch(idx, u2, tile_cnt, tile_start3, tails, xs, n_rows, row0 // SCATTER_TILE)
        row0 += n

    b_up_de = jnp.concatenate([b_up[0][:, 0::2], b_up[0][:, 1::2]], axis=-1).reshape(N_EXPERTS, 1, 2 * D_FF)
    b_down3 = b_down[0].reshape(N_EXPERTS, 1, D)
    y_pad = _expert_blocks(xs, blk_expert, n_used, w_up[0], b_up_de, w_down[0], b_down3)

    outs = []
    row0 = 0
    for x, x1, mrow in zip(groups, x1s, mod_rows):
        B, T, _ = x.shape
        n = B * T
        outs.append(_combine(idx, gates, tile_cnt, tile_start3, y_pad, x1, mod3, mrow, ln2g, ln2b, B, T,
                             row0 // SCATTER_TILE))
        row0 += n
    return tuple(outs)
```

```python
import functools

import numpy as np
import jax
import jax.numpy as jnp
from jax import lax
from jax.experimental import pallas as pl
from jax.experimental.pallas import tpu as pltpu

F32 = jnp.float32
BF16 = jnp.bfloat16
I32 = jnp.int32

D_MODEL = 1024
HEAD_DIM = 64
N_HEADS = 8
WIDTH = N_HEADS * HEAD_DIM
N_PAIRS = WIDTH // 128
ROPE_THETA = 10000.0
RADIUS = 64
GRID_W = 64
NA_ROWS = 8
NA_COLS = 16
N_EXPERTS = 32
TOP_K = 4
D_FF = 1024
SWIGLU_ALPHA = 1.702
SWIGLU_LIMIT = 7.0
DEEPNORM_ALPHA = 2.0 ** 0.25
LN_EPS = 1e-5
RMS_EPS = 1e-6
NEG = -1e30
LOG2E = 1.4426950408889634

LANES = 128
RES = 16
CHUNK = 128
RES_PER_STEP = 4
MOE_ROWS = 512
ROUTE_TILE = 1024
SCATTER_TILE = 512
VMEM_LIMIT = 56 * 1024 * 1024


def _cparams(sem, vmem=VMEM_LIMIT):
    return pltpu.CompilerParams(dimension_semantics=sem, vmem_limit_bytes=vmem)


def _mod_kernel(c_ref, w_ref, b_ref, o_ref):
    c = c_ref[...]
    s = c * (1.0 / (1.0 + jnp.exp(-c)))
    o_ref[...] = jnp.dot(s.astype(BF16), w_ref[...].astype(BF16), preferred_element_type=F32) + b_ref[...]


def _modulation(c, w_mod, b_mod):
    rows = c.shape[0]
    n_out = w_mod.shape[1]
    tn = 1024
    return pl.pallas_call(
        _mod_kernel,
        grid=(n_out // tn,),
        in_specs=[pl.BlockSpec((rows, D_MODEL), lambda j: (0, 0)),
                  pl.BlockSpec((D_MODEL, tn), lambda j: (0, j)),
                  pl.BlockSpec((1, tn), lambda j: (0, j))],
        out_specs=pl.BlockSpec((rows, tn), lambda j: (0, j)),
        out_shape=jax.ShapeDtypeStruct((rows, n_out), F32),
        compiler_params=_cparams(("arbitrary",)),
        name="modulation",
    )(c, w_mod, b_mod.reshape(1, n_out))


PERM = 256
PROJ_TOKENS = 1024
PROJ_ROWS = PROJ_TOKENS // RES


def _residue_permutation():
    p = np.zeros((PERM, PERM), np.float32)
    m, r = np.meshgrid(np.arange(PERM // RES), np.arange(RES), indexing="ij")
    p[(r * (PERM // RES) + m).reshape(-1), (m * RES + r).reshape(-1)] = 1.0
    return p


def _inproj_nat_kernel(x_ref, mod_ref, w_ref, cos_ref, sin_ref, perm_ref,
                       qa_ref, ka_ref, va_ref, qb_ref, kb_ref, vb_ref, u_nat, u_view):
    D = D_MODEL
    shift = mod_ref[:, 0:D]
    scale = mod_ref[:, D:2 * D]
    qscale = HEAD_DIM ** -0.5 * LOG2E
    u_nat[...] = (x_ref[...] * (1.0 + scale) + shift).astype(BF16)

    sub = 512
    for s in range(PROJ_TOKENS // sub):
        u = u_nat[s * sub:(s + 1) * sub, :]
        for i, (ref, mul) in enumerate(((qb_ref, qscale), (kb_ref, 1.0), (vb_ref, 1.0))):
            p = jnp.dot(u, w_ref[:, (3 + i) * WIDTH:(4 + i) * WIDTH], preferred_element_type=F32) * mul
            for hp in range(N_PAIRS):
                ref[hp, s * sub:(s + 1) * sub, :] = p[:, hp * LANES:(hp + 1) * LANES].astype(BF16)

    rows_g = PERM // RES
    for g in range(PROJ_TOKENS // PERM):
        pv = jnp.dot(perm_ref[...], u_nat[g * PERM:(g + 1) * PERM, :], preferred_element_type=F32).astype(BF16)
        for r in range(RES):
            u_view[r, g * rows_g:(g + 1) * rows_g, :] = pv[r * rows_g:(r + 1) * rows_g]

    lane = lax.broadcasted_iota(I32, (1, WIDTH), 1)
    first_half = (lane % HEAD_DIM) < (HEAD_DIM // 2)
    for q in range(RES // RES_PER_STEP):
        u = jnp.concatenate([u_view[q * RES_PER_STEP + a] for a in range(RES_PER_STEP)], axis=0)
        lo = q * RES_PER_STEP * LANES
        cosf = jnp.concatenate([jnp.tile(cos_ref[:, lo + a * LANES:lo + (a + 1) * LANES], (1, N_PAIRS))
                                for a in range(RES_PER_STEP)], axis=0)
        sinf = jnp.concatenate([jnp.tile(sin_ref[:, lo + a * LANES:lo + (a + 1) * LANES], (1, N_PAIRS))
                                for a in range(RES_PER_STEP)], axis=0)

        def rope(p):
            rot = jnp.where(first_half, pltpu.roll(p, WIDTH - HEAD_DIM // 2, 1), pltpu.roll(p, HEAD_DIM // 2, 1))
            return p * cosf + rot * sinf

        for i, ref in enumerate((qa_ref, ka_ref, va_ref)):
            p = jnp.dot(u, w_ref[:, i * WIDTH:(i + 1) * WIDTH], preferred_element_type=F32)
            if i == 0:
                p = rope(p) * qscale
            elif i == 1:
                p = rope(p)
            for a in range(RES_PER_STEP):
                for hp in range(N_PAIRS):
                    ref[q * RES_PER_STEP + a, hp] = (
                        p[a * PROJ_ROWS:(a + 1) * PROJ_ROWS, hp * LANES:(hp + 1) * LANES].astype(BF16))


def _input_projection_nat(x, mod3, w_in_bf, cos_v, sin_v, perm, mod_row0):
    B, T, D = x.shape
    N = B * T
    BL = N // RES
    tps = T // PROJ_TOKENS
    res_shape = jax.ShapeDtypeStruct((RES, N_PAIRS, BL, LANES), BF16)
    nat_shape = jax.ShapeDtypeStruct((N_PAIRS, N, LANES), BF16)
    res_spec = pl.BlockSpec((RES, N_PAIRS, PROJ_ROWS, LANES), lambda i: (0, 0, i, 0))
    nat_spec = pl.BlockSpec((N_PAIRS, PROJ_TOKENS, LANES), lambda i: (0, i, 0))
    return pl.pallas_call(
        _inproj_nat_kernel,
        grid=(N // PROJ_TOKENS,),
        in_specs=[pl.BlockSpec((PROJ_TOKENS, D), lambda i: (i, 0)),
                  pl.BlockSpec((None, 1, 6 * D), lambda i: (mod_row0 + i // tps, 0, 0)),
                  pl.BlockSpec((D, 6 * WIDTH), lambda i: (0, 0)),
                  pl.BlockSpec((PROJ_ROWS, RES * LANES), lambda i: (i % tps, 0)),
                  pl.BlockSpec((PROJ_ROWS, RES * LANES), lambda i: (i % tps, 0)),
                  pl.BlockSpec((PERM, PERM), lambda i: (0, 0))],
        out_specs=[res_spec, res_spec, res_spec, nat_spec, nat_spec, nat_spec],
        out_shape=[res_shape, res_shape, res_shape, nat_shape, nat_shape, nat_shape],
        scratch_shapes=[pltpu.VMEM((PROJ_TOKENS, D), BF16), pltpu.VMEM((RES, PROJ_ROWS, D), BF16)],
        compiler_params=_cparams(("arbitrary",)),
        name="input_projection",
    )(x.reshape(N, D), mod3, w_in_bf, cos_v, sin_v, perm)


def _rope_tables(T):
    half = HEAD_DIM // 2
    inv_freq = 1.0 / (ROPE_THETA ** (jnp.arange(half, dtype=F32) / half))
    ang = jnp.arange(T, dtype=F32)[:, None] * inv_freq[None, :]
    cos, sin = jnp.cos(ang), jnp.sin(ang)
    cos_h = jnp.concatenate([cos, cos], axis=-1)
    sin_h = jnp.concatenate([-sin, sin], axis=-1)
    cos2 = jnp.concatenate([cos_h, cos_h], axis=-1)
    sin2 = jnp.concatenate([sin_h, sin_h], axis=-1)
    L = T // RES
    return cos2.reshape(L, RES * LANES), sin2.reshape(L, RES * LANES)


P2_ROWS = 32
P2_KROWS = 64
P1_ROWS = 16
P1_KROWS = 32
P1_SHIFT = 8
HALO = 64
P3_BATCH = 8
P2_BATCH = 2
P1_BATCH = 2


def _band_tables(has_halo):
    def mask(ok):
        return np.where(ok, 0.0, NEG).astype(np.float32)
    mq = np.arange(CHUNK)[:, None]
    koff3 = np.arange(2 * CHUNK) - HALO
    b3 = mask(np.abs(mq - koff3[None, :]) <= RADIUS)
    j = np.repeat(np.arange(4), P2_ROWS)[:, None]
    a = np.tile(np.arange(P2_ROWS), 4)[:, None]
    jk = np.repeat(np.arange(4), P2_KROWS)[None, :]
    bk = np.tile(np.arange(P2_KROWS), 4)[None, :]
    b2 = mask(np.abs(4 * (a - (bk - 16)) + (j - jk)) <= RADIUS)
    koff2 = (bk - 16).reshape(-1)
    r = np.repeat(np.arange(RES), P1_ROWS)[:, None]
    a = np.tile(np.arange(P1_ROWS), RES)[:, None]
    rk = np.repeat(np.arange(RES), P1_KROWS)[None, :]
    bk = np.tile(np.arange(P1_KROWS), RES)[None, :]
    b1 = mask(np.abs(RES * (a - (bk - P1_SHIFT)) + (r - rk)) <= RADIUS)
    koff1 = (bk - P1_SHIFT).reshape(-1)
    return (jnp.asarray(b3), jnp.asarray(koff3.astype(np.int32)[None, :]),
            jnp.asarray(b2), jnp.asarray(koff2.astype(np.int32)[None, :]),
            jnp.asarray(b1), jnp.asarray(koff1.astype(np.int32)[None, :]))


def _attend_pairs(problems, head0):
    scores = []
    for q, k, _, _ in problems:
        for h in range(2):
            sel = head0 if h == 0 else jnp.logical_not(head0)
            qh = jnp.where(sel, q, jnp.zeros_like(q))
            scores.append(lax.dot_general(qh, k, (((1,), (1,)), ((), ())), preferred_element_type=F32))
    probs, stats = [], []
    for i, s in enumerate(scores):
        s = s + problems[i // 2][3]
        m = jnp.max(s, axis=-1, keepdims=True)
        p = jnp.exp2(s - m)
        stats.append((m, jnp.sum(p, axis=-1, keepdims=True)))
        probs.append(p.astype(BF16))
    outs = [jnp.dot(p, problems[i // 2][2], preferred_element_type=F32) for i, p in enumerate(probs)]
    results = []
    for i in range(len(problems)):
        (m0, l0), (m1, l1) = stats[2 * i], stats[2 * i + 1]
        results.append((jnp.where(head0, outs[2 * i], outs[2 * i + 1]),
                        jnp.where(head0, m0, m1), jnp.where(head0, l0, l1)))
    return results


def _dilated_kernel(*refs, has_halo, seq_rows, chunks_per_seq):
    if has_halo:
        (q_ref, k_ref, v_ref, kp_ref, kn_ref, vp_ref, vn_ref,
         b3_ref, o3_ref, b2_ref, o2_ref, b1_ref, o1_ref,
         out_ref, kf, vf, kf8, vf8, acc, ms, ls) = refs
    else:
        (q_ref, k_ref, v_ref, b3_ref, o3_ref, b2_ref, o2_ref, b1_ref, o1_ref,
         out_ref, kf, vf, kf8, vf8, acc, ms, ls) = refs

    c = pl.program_id(0) % chunks_per_seq
    row0 = c * CHUNK
    lane = lax.broadcasted_iota(I32, (1, LANES), 1)
    head0 = lane < HEAD_DIM

    for r in range(RES):
        kf[r, HALO:HALO + CHUNK, :] = k_ref[r]
        vf[r, HALO:HALO + CHUNK, :] = v_ref[r]
        if has_halo:
            kf[r, 0:HALO, :] = kp_ref[r]
            vf[r, 0:HALO, :] = vp_ref[r]
            kf[r, HALO + CHUNK:, :] = kn_ref[r]
            vf[r, HALO + CHUNK:, :] = vn_ref[r]
        else:
            zeros = jnp.zeros((HALO, LANES), BF16)
            for ref in (kf, vf):
                ref[r, 0:HALO, :] = zeros
                ref[r, HALO + CHUNK:, :] = zeros
        for src, dst in ((kf, kf8), (vf, vf8)):
            w32 = src[r].astype(F32)
            dst[r] = jnp.concatenate([w32[P1_SHIFT:], w32[:P1_SHIFT]], axis=0).astype(BF16)

    def in_seq(base, off_ref):
        kv_row = base + off_ref[...]
        return jnp.where((kv_row >= 0) & (kv_row < seq_rows), 0.0, NEG).astype(F32)

    def attend(problems):
        return _attend_pairs(problems, head0)

    def merge(a_old, m_old, l_old, o, m, l):
        mn = jnp.maximum(m_old, m)
        wa = jnp.exp2(m_old - mn)
        wb = jnp.exp2(m - mn)
        return a_old * wa + o * wb, mn, l_old * wa + l * wb

    bias3 = b3_ref[...] + in_seq(row0, o3_ref)

    def body3(it, carry):
        rs = [it * P3_BATCH + u for u in range(P3_BATCH)]
        res = attend([(q_ref[r], kf[r], vf[r], bias3) for r in rs])
        for r, (o, m, l) in zip(rs, res):
            acc[r] = o
            ms[r] = m
            ls[r] = l
        return carry

    lax.fori_loop(0, RES // P3_BATCH, body3, 0)

    b2 = b2_ref[...]

    def body2(it, carry):
        def gather(ref, r4, start, rows):
            return jnp.concatenate([ref[4 * j + r4, pl.ds(start, rows), :] for j in range(4)], axis=0)

        where, problems = [], []
        for u in range(P2_BATCH):
            g = it * P2_BATCH + u
            qs = pl.multiple_of(g * P2_ROWS, P2_ROWS)
            ks = pl.multiple_of(HALO - 16 + g * P2_ROWS, 16)
            bias = b2 + in_seq(row0 + g * P2_ROWS, o2_ref)
            for r4 in range(4):
                where.append((r4, qs))
                problems.append((gather(q_ref, r4, qs, P2_ROWS), gather(kf, r4, ks, P2_KROWS),
                                 gather(vf, r4, ks, P2_KROWS), bias))
        for (r4, qs), (o, m, l) in zip(where, attend(problems)):
            a_new, m_new, l_new = merge(gather(acc, r4, qs, P2_ROWS), gather(ms, r4, qs, P2_ROWS),
                                        gather(ls, r4, qs, P2_ROWS), o, m, l)
            for j in range(4):
                acc[4 * j + r4, pl.ds(qs, P2_ROWS), :] = a_new[j * P2_ROWS:(j + 1) * P2_ROWS]
                ms[4 * j + r4, pl.ds(qs, P2_ROWS), :] = m_new[j * P2_ROWS:(j + 1) * P2_ROWS]
                ls[4 * j + r4, pl.ds(qs, P2_ROWS), :] = l_new[j * P2_ROWS:(j + 1) * P2_ROWS]
        return carry

    lax.fori_loop(0, CHUNK // P2_ROWS // P2_BATCH, body2, 0)

    b1 = b1_ref[...]

    def body1(it, carry):
        def gather(ref, start, rows):
            return jnp.concatenate([ref[r, pl.ds(start, rows), :] for r in range(RES)], axis=0)

        starts, problems = [], []
        for u in range(P1_BATCH):
            g = it * P1_BATCH + u
            qs = pl.multiple_of(g * P1_ROWS, P1_ROWS)
            ks = pl.multiple_of(HALO - 16 + g * P1_ROWS, 16)
            starts.append(qs)
            problems.append((gather(q_ref, qs, P1_ROWS), gather(kf8, ks, P1_KROWS), gather(vf8, ks, P1_KROWS),
                             b1 + in_seq(row0 + g * P1_ROWS, o1_ref)))
        for qs, (o, m, l) in zip(starts, attend(problems)):
            a_new, _, l_new = merge(gather(acc, qs, P1_ROWS), gather(ms, qs, P1_ROWS),
                                    gather(ls, qs, P1_ROWS), o, m, l)
            done = (a_new / l_new).astype(BF16)
            for r in range(RES):
                out_ref[r, pl.ds(qs, P1_ROWS), :] = done[r * P1_ROWS:(r + 1) * P1_ROWS]
        return carry

    lax.fori_loop(0, CHUNK // P1_ROWS // P1_BATCH, body1, 0)


def _dilated_attention(qa, ka, va, B, T):
    L = T // RES
    BL = B * L
    cps = L // CHUNK
    has_halo = cps > 1
    tables = _band_tables(has_halo)
    blk = (RES, None, CHUNK, LANES)
    center = pl.BlockSpec(blk, lambda i, hp: (0, hp, i, 0))
    in_specs = [center, center, center]
    args = [qa, ka, va]
    if has_halo:
        hblk = (RES, None, HALO, LANES)
        per = CHUNK // HALO

        def prev_map(i, hp):
            return (0, hp, jnp.maximum(per * i - 1, (i // cps) * cps * per), 0)

        def next_map(i, hp):
            return (0, hp, jnp.minimum(per * i + per, (i // cps + 1) * cps * per - 1), 0)

        in_specs += [pl.BlockSpec(hblk, prev_map), pl.BlockSpec(hblk, next_map),
                     pl.BlockSpec(hblk, prev_map), pl.BlockSpec(hblk, next_map)]
        args += [ka, ka, va, va]
    for t in tables:
        in_specs.append(pl.BlockSpec(t.shape, lambda i, hp: (0, 0)))
        args.append(t)
    kern = functools.partial(_dilated_kernel, has_halo=has_halo, seq_rows=L, chunks_per_seq=cps)
    return pl.pallas_call(
        kern,
        grid=(BL // CHUNK, N_PAIRS),
        in_specs=in_specs,
        out_specs=pl.BlockSpec(blk, lambda i, hp: (0, hp, i, 0)),
        out_shape=jax.ShapeDtypeStruct((RES, N_PAIRS, BL, LANES), BF16),
        scratch_shapes=[pltpu.VMEM((RES, 2 * CHUNK, LANES), BF16)] * 4 + [
                        pltpu.VMEM((RES, CHUNK, LANES), F32),
                        pltpu.VMEM((RES, CHUNK, LANES), F32),
                        pltpu.VMEM((RES, CHUNK, LANES), F32)],
        compiler_params=_cparams(("arbitrary", "arbitrary")),
        name="dilated_attention",
    )(*args)


NA_KEYS = NA_ROWS * GRID_W
NA_BLOCK_ROWS = 32
NA_BATCH = 32


def _na_bias_table(rpb):
    c = np.arange(GRID_W)
    col_start = np.clip(c - NA_COLS // 2, 0, GRID_W - NA_COLS)
    col_mask = (c[None, :] >= col_start[:, None]) & (c[None, :] < col_start[:, None] + NA_COLS)
    dc_idx = np.clip(c[None, :] - c[:, None], -(NA_COLS - 1), NA_COLS - 1) + NA_COLS - 1
    rel = rpb.astype(F32)[:, :, dc_idx] * LOG2E
    rel = jnp.where(col_mask[None, None], rel, NEG)
    per_off = [rel[:, d0:d0 + NA_ROWS].transpose(0, 2, 1, 3).reshape(N_HEADS, GRID_W, NA_KEYS)
               for d0 in range(NA_ROWS)]
    return jnp.stack(per_off, axis=0).reshape(NA_ROWS, N_PAIRS, 2, GRID_W, NA_KEYS)


def _na_kernel(q_ref, k_ref, v_ref, bias_ref, out_ref, *, grid_rows, block_rows):
    gb = pl.program_id(2)
    lane = lax.broadcasted_iota(I32, (1, LANES), 1)
    head0 = lane < HEAD_DIM

    def body(it, carry):
        rows, scores = [], []
        for u in range(NA_BATCH):
            i = it * NA_BATCH + u
            g = gb * block_rows + i
            rs = jnp.clip(g - NA_ROWS // 2, 0, grid_rows - NA_ROWS)
            d0 = rs - g + NA_ROWS - 1
            qs = pl.multiple_of(i * GRID_W, GRID_W)
            ks = pl.multiple_of(rs * GRID_W, GRID_W)
            q = q_ref[pl.ds(qs, GRID_W), :]
            k = k_ref[pl.ds(ks, NA_KEYS), :]
            rows.append((qs, ks, d0))
            for h in range(2):
                sel = head0 if h == 0 else jnp.logical_not(head0)
                qh = jnp.where(sel, q, jnp.zeros_like(q))
                scores.append(lax.dot_general(qh, k, (((1,), (1,)), ((), ())), preferred_element_type=F32))
        probs, sums = [], []
        for n, s in enumerate(scores):
            s = s + bias_ref[rows[n // 2][2], n % 2]
            p = jnp.exp2(s - jnp.max(s, axis=-1, keepdims=True))
            sums.append(jnp.sum(p, axis=-1, keepdims=True))
            probs.append(p.astype(BF16))
        outs = []
        for n, p in enumerate(probs):
            v = v_ref[pl.ds(rows[n // 2][1], NA_KEYS), :]
            outs.append(jnp.dot(p, v, preferred_element_type=F32) / sums[n])
        for u, (qs, _, _) in enumerate(rows):
            out_ref[pl.ds(qs, GRID_W), :] = jnp.where(head0, outs[2 * u], outs[2 * u + 1]).astype(BF16)
        return carry

    lax.fori_loop(0, block_rows // NA_BATCH, body, 0)


def _neighbourhood_attention(qb, kb, vb, bias, B, T):
    G = T // GRID_W
    rb = min(NA_BLOCK_ROWS, G)
    nb = G // rb
    kern = functools.partial(_na_kernel, grid_rows=G, block_rows=rb)
    seq = pl.BlockSpec((None, T, LANES), lambda hp, b, gb: (hp, b, 0))
    qblk = pl.BlockSpec((None, rb * GRID_W, LANES), lambda hp, b, gb: (hp, b * nb + gb, 0))
    return pl.pallas_call(
        kern,
        grid=(N_PAIRS, B, nb),
        in_specs=[qblk, seq, seq,
                  pl.BlockSpec((NA_ROWS, None, 2, GRID_W, NA_KEYS), lambda hp, b, gb: (0, hp, 0, 0, 0))],
        out_specs=qblk,
        out_shape=jax.ShapeDtypeStruct((N_PAIRS, B * T, LANES), BF16),
        compiler_params=_cparams(("arbitrary", "arbitrary", "arbitrary")),
        name="neighbourhood_attention",
    )(qb, kb, vb, bias)


def _layer_norm(h, g, b):
    mu = jnp.mean(h, axis=-1, keepdims=True)
    d = h - mu
    var = jnp.mean(d * d, axis=-1, keepdims=True)
    return d * lax.rsqrt(var + LN_EPS) * g + b


def _outproj_nat_kernel(oa_ref, ob_ref, x_ref, mod_ref, ga_ref, gb_ref, w_ref, lg_ref, lb_ref, wr_ref, br_ref,
                        permt_ref, x1_ref, u2_ref, logit_ref, mix_scr):
    D = D_MODEL
    gate_a = mod_ref[:, 2 * D:3 * D]
    shift_f = mod_ref[:, 3 * D:4 * D]
    scale_f = mod_ref[:, 4 * D:5 * D]

    def rms(o, g):
        return o * lax.rsqrt(jnp.mean(o * o, axis=-1, keepdims=True) + RMS_EPS) * g

    rows_g = PERM // RES
    for g in range(PROJ_TOKENS // PERM):
        grouped = jnp.concatenate(
            [jnp.concatenate([oa_ref[r, hp, g * rows_g:(g + 1) * rows_g, :] for hp in range(N_PAIRS)], axis=1)
             for r in range(RES)], axis=0)
        oa = jnp.dot(permt_ref[...], grouped, preferred_element_type=F32)
        ob = jnp.concatenate([ob_ref[hp, g * PERM:(g + 1) * PERM, :] for hp in range(N_PAIRS)], axis=1).astype(F32)
        mix = jnp.concatenate([rms(oa, ga_ref[...]), rms(ob, gb_ref[...])], axis=1)
        mix_scr[g * PERM:(g + 1) * PERM, :] = mix.astype(BF16)

    wr = wr_ref[...]
    wr_hi = wr.astype(BF16)
    wr_lo = (wr - wr_hi.astype(F32)).astype(BF16)
    sub = 512
    for s in range(PROJ_TOKENS // sub):
        rows = slice(s * sub, (s + 1) * sub)
        y = jnp.dot(mix_scr[rows, :], w_ref[...], preferred_element_type=F32)
        h = DEEPNORM_ALPHA * x_ref[rows, :] + (1.0 + gate_a) * y
        x1 = _layer_norm(h, lg_ref[...], lb_ref[...])
        x1_ref[rows, :] = x1
        u2 = x1 * (1.0 + scale_f) + shift_f
        u2_ref[rows, :] = u2.astype(BF16)
        u_hi = u2.astype(BF16)
        u_lo = (u2 - u_hi.astype(F32)).astype(BF16)
        logit_ref[rows, :] = (jnp.dot(u_hi, wr_hi, preferred_element_type=F32)
                              + jnp.dot(u_lo, wr_hi, preferred_element_type=F32)
                              + jnp.dot(u_hi, wr_lo, preferred_element_type=F32)) + br_ref[...]


def _output_projection_nat(oa, ob, x, mod3, mod_row0, g_a, g_b, w_out_bf, ln_g, ln_b, w_router_pad, b_router_pad,
                           perm_t):
    B, T, D = x.shape
    N = B * T
    tps = T // PROJ_TOKENS
    const = lambda shape: pl.BlockSpec(shape, lambda i: tuple(0 for _ in shape))
    rows = lambda width: pl.BlockSpec((PROJ_TOKENS, width), lambda i: (i, 0))
    return pl.pallas_call(
        _outproj_nat_kernel,
        grid=(N // PROJ_TOKENS,),
        in_specs=[pl.BlockSpec((RES, N_PAIRS, PROJ_ROWS, LANES), lambda i: (0, 0, i, 0)),
                  pl.BlockSpec((N_PAIRS, PROJ_TOKENS, LANES), lambda i: (0, i, 0)),
                  rows(D),
                  pl.BlockSpec((None, 1, 6 * D), lambda i: (mod_row0 + i // tps, 0, 0)),
                  const((1, WIDTH)), const((1, WIDTH)), const((2 * WIDTH, D)),
                  const((1, D)), const((1, D)), const((D, LANES)), const((1, LANES)), const((PERM, PERM))],
        out_specs=[rows(D), rows(D), rows(LANES)],
        out_shape=[jax.ShapeDtypeStruct((N, D), F32), jax.ShapeDtypeStruct((N, D), BF16),
                   jax.ShapeDtypeStruct((N, LANES), F32)],
        scratch_shapes=[pltpu.VMEM((PROJ_TOKENS, 2 * WIDTH), BF16)],
        compiler_params=_cparams(("arbitrary",)),
        name="output_projection",
    )(oa, ob, x.reshape(N, D), mod3, g_a, g_b, w_out_bf, ln_g, ln_b, w_router_pad, b_router_pad, perm_t)


def _route_kernel(logit_ref, idx_ref, gate_ref, cnt_ref):
    for t in range(ROUTE_TILE // SCATTER_TILE):
        rows = slice(t * SCATTER_TILE, (t + 1) * SCATTER_TILE)
        sel, gates, cnt = _route_tile(logit_ref[rows, :])
        idx_ref[rows, :] = sel
        gate_ref[rows, :] = gates
        cnt_ref[t] = cnt


def _route_tile(raw):
    tn = raw.shape[0]
    lane = lax.broadcasted_iota(I32, (tn, LANES), 1)
    logits = jnp.where(lane < N_EXPERTS, raw, -3.0e38)
    vals, idxs = [], []
    multi = jnp.zeros((tn, LANES), F32)
    for _ in range(TOP_K):
        m = jnp.max(logits, axis=-1, keepdims=True)
        idx = jnp.min(jnp.where(logits == m, lane, LANES), axis=-1, keepdims=True)
        hot = lane == idx
        vals.append(m)
        idxs.append(idx)
        multi = multi + hot.astype(F32)
        logits = jnp.where(hot, -3.0e38, logits)
    es = [jnp.exp(v - vals[0]) for v in vals]
    tot = es[0] + es[1] + es[2] + es[3]
    lane4 = lax.broadcasted_iota(I32, (tn, TOP_K), 1)
    sel = jnp.zeros((tn, TOP_K), I32)
    gates = jnp.zeros((tn, TOP_K), F32)
    for k in range(TOP_K):
        sel = jnp.where(lane4 == k, idxs[k], sel)
        gates = jnp.where(lane4 == k, es[k] / tot, gates)
    return sel, gates, jnp.sum(multi, axis=0, keepdims=True).astype(I32)


def _routing(logits):
    N = logits.shape[0]
    tn = ROUTE_TILE
    nt = N // SCATTER_TILE
    per_step = ROUTE_TILE // SCATTER_TILE
    return pl.pallas_call(
        _route_kernel,
        grid=(N // tn,),
        in_specs=[pl.BlockSpec((tn, LANES), lambda i: (i, 0))],
        out_specs=[pl.BlockSpec((tn, TOP_K), lambda i: (i, 0)),
                   pl.BlockSpec((tn, TOP_K), lambda i: (i, 0)),
                   pl.BlockSpec((per_step, 1, LANES), lambda i: (i, 0, 0))],
        out_shape=[jax.ShapeDtypeStruct((N, TOP_K), I32), jax.ShapeDtypeStruct((N, TOP_K), F32),
                   jax.ShapeDtypeStruct((nt, 1, LANES), I32)],
        compiler_params=_cparams(("arbitrary",)),
        name="moe_routing",
    )(logits)


SEG = 8
STAGE_ROWS = SCATTER_TILE * TOP_K + N_EXPERTS * SEG


def _lane_prefix_exclusive(v):
    lane = lax.broadcasted_iota(I32, v.shape, 1)
    incl = v
    s = 1
    while s < LANES:
        incl = incl + jnp.where(lane >= s, pltpu.roll(incl, s, 1), 0.0)
        s *= 2
    return incl - v


def _tile_ranks(idx, tn):
    lane = lax.broadcasted_iota(I32, (tn, LANES), 1)
    hots = [lane == idx[:, k:k + 1] for k in range(TOP_K)]
    multi = jnp.zeros((tn, LANES), F32)
    for h in hots:
        multi = multi + h.astype(F32)
    row = lax.broadcasted_iota(I32, (tn, tn), 0)
    col = lax.broadcasted_iota(I32, (tn, tn), 1)
    lower = (col < row).astype(BF16)
    before = jnp.dot(lower, multi.astype(BF16), preferred_element_type=F32)
    return hots, before


BIG_SEG = 4 * SEG


def _segment_dma_loops(cnt_s, rows_of, copy):
    def per_expert(e, carry):
        off, n_big, n_small = carry
        nseg = (cnt_s[0, e] + SEG - 1) // SEG
        big = nseg // (BIG_SEG // SEG)
        small = nseg - big * (BIG_SEG // SEG)

        def one_big(q, c):
            src, dst = rows_of(e, off, q * BIG_SEG)
            copy(src, dst, BIG_SEG).start()
            return c

        def one_small(q, c):
            src, dst = rows_of(e, off, big * BIG_SEG + q * SEG)
            copy(src, dst, SEG).start()
            return c

        lax.fori_loop(0, big, one_big, 0)
        lax.fori_loop(0, small, one_small, 0)
        return off + nseg * SEG, n_big + big, n_small + small

    _, n_big, n_small = lax.fori_loop(0, N_EXPERTS, per_expert, (0, 0, 0))
    return n_big, n_small


def _drain(copy, n_big, n_small):
    def wait_big(q, c):
        copy(0, 0, BIG_SEG).wait()
        return c

    def wait_small(q, c):
        copy(0, 0, SEG).wait()
        return c

    lax.fori_loop(0, n_big, wait_big, 0)
    lax.fori_loop(0, n_small, wait_small, 0)


def _dispatch_kernel(*refs, first, n_steps):
    if first:
        cnt_s, start_s, tail_s, idx_ref, u_ref, cntv_ref, xs_ref, stage, zbuf, pending, sem, zsem = refs
    else:
        cnt_s, start_s, tail_s, idx_ref, u_ref, cntv_ref, _, xs_ref, stage, zbuf, pending, sem, zsem = refs
    tn = idx_ref.shape[0]
    step = pl.program_id(0)
    slot = step % 2

    if first:
        @pl.when(pl.program_id(0) == 0)
        def _():
            zbuf[...] = jnp.zeros_like(zbuf)

            def tail_copy(row):
                return pltpu.make_async_copy(zbuf, xs_ref.at[pl.ds(pl.multiple_of(row, SEG), SEG)], zsem)

            def fill(e, n):
                lo = tail_s[0, e]
                nfull = (tail_s[1, e] - lo) // SEG

                def one(q, c):
                    tail_copy(lo + q * SEG).start()
                    return c

                lax.fori_loop(0, nfull, one, 0)
                return n + nfull

            n = lax.fori_loop(0, N_EXPERTS, fill, 0)

            def drain(q, c):
                tail_copy(0).wait()
                return c

            lax.fori_loop(0, n, drain, 0)

    hots, before = _tile_ranks(idx_ref[...], tn)
    cntf = cntv_ref[...].astype(F32)
    seg_len = jnp.ceil(cntf * (1.0 / SEG)) * SEG
    seg_off = _lane_prefix_exclusive(seg_len)
    lane_r = lax.broadcasted_iota(I32, (tn, STAGE_ROWS), 1)
    onehot = jnp.zeros((tn, STAGE_ROWS), F32)
    for k in range(TOP_K):
        stage_row = jnp.sum(jnp.where(hots[k], seg_off + before, 0.0), axis=-1, keepdims=True)
        onehot = jnp.where(lane_r == stage_row.astype(I32), 1.0, onehot)
    stage[slot] = lax.dot_general(onehot.astype(BF16), u_ref[...].astype(BF16), (((0,), (0,)), ((), ())),
                                  preferred_element_type=F32)

    def copier(buf):
        def seg_copy(src_row, dst_row, rows):
            return pltpu.make_async_copy(stage.at[buf, pl.ds(pl.multiple_of(src_row, SEG), rows)],
                                         xs_ref.at[pl.ds(pl.multiple_of(dst_row, SEG), rows)], sem.at[buf])
        return seg_copy

    @pl.when(step > 0)
    def _():
        _drain(copier(1 - slot), pending[0], pending[1])

    n_big, n_small = _segment_dma_loops(
        cnt_s, lambda e, off, moved: (off + moved, start_s[0, e] + moved), copier(slot))
    pending[0] = n_big
    pending[1] = n_small

    @pl.when(step == n_steps - 1)
    def _():
        _drain(copier(slot), n_big, n_small)


def _dispatch(idx, u2, tile_cnt, tile_start, tails, xs, xs_rows, t0):
    n, D = u2.shape
    tn = SCATTER_TILE
    nt = n // tn
    first = xs is None
    smem = lambda shape, imap: pl.BlockSpec(shape, imap, memory_space=pltpu.SMEM)
    kern = functools.partial(_dispatch_kernel, first=first, n_steps=nt)
    in_specs = [smem((None, 1, LANES), lambda i: (t0 + i, 0, 0)),
                smem((None, 1, LANES), lambda i: (t0 + i, 0, 0)),
                smem((2, LANES), lambda i: (0, 0)),
                pl.BlockSpec((tn, TOP_K), lambda i: (t0 + i, 0)),
                pl.BlockSpec((tn, D), lambda i: (i, 0)),
                pl.BlockSpec((None, 1, LANES), lambda i: (t0 + i, 0, 0))]
    args = [tile_cnt, tile_start, tails, idx, u2, tile_cnt]
    alias = {}
    if not first:
        alias = {len(args): 0}
        in_specs.append(pl.BlockSpec(memory_space=pl.ANY))
        args.append(xs)
    return pl.pallas_call(
        kern,
        grid=(nt,),
        in_specs=in_specs,
        out_specs=pl.BlockSpec(memory_space=pl.ANY),
        out_shape=jax.ShapeDtypeStruct((xs_rows, D), F32),
        scratch_shapes=[pltpu.VMEM((2, STAGE_ROWS, D), F32), pltpu.VMEM((SEG, D), F32), pltpu.SMEM((2,), I32),
                        pltpu.SemaphoreType.DMA((2,)), pltpu.SemaphoreType.DMA(())],
        input_output_aliases=alias,
        compiler_params=_cparams(("arbitrary",)),
        name="moe_dispatch",
    )(*args)


DEINT = 256


def _deinterleave_matrix():
    perm = np.zeros((DEINT, DEINT), np.float32)
    j = np.arange(DEINT // 2)
    perm[2 * j, j] = 1.0
    perm[2 * j + 1, DEINT // 2 + j] = 1.0
    return perm


def _expert_kernel(be_ref, nu_ref, x_ref, wu_ref, bu_ref, wd_ref, bd_ref, p_ref, y_ref, wu_bf, wd_bf):
    j = pl.program_id(0)

    @pl.when((j == 0) | (be_ref[j] != be_ref[jnp.maximum(j - 1, 0)]))
    def _():
        half = DEINT // 2
        for g in range(2 * D_FF // DEINT):
            wg = wu_ref[:, g * DEINT:(g + 1) * DEINT].astype(BF16)
            t = jnp.dot(wg, p_ref[...], preferred_element_type=F32)
            wu_bf[:, g * half:(g + 1) * half] = t[:, :half].astype(BF16)
            wu_bf[:, D_FF + g * half:D_FF + (g + 1) * half] = t[:, half:].astype(BF16)
        wd_bf[...] = wd_ref[...].astype(BF16)

    @pl.when(j < nu_ref[0])
    def _():
        x = x_ref[...].astype(BF16)
        hu = jnp.dot(x, wu_bf[...], preferred_element_type=F32) + bu_ref[...]
        glu = jnp.minimum(hu[:, :D_FF], SWIGLU_LIMIT)
        lin = jnp.clip(hu[:, D_FF:], -SWIGLU_LIMIT, SWIGLU_LIMIT)
        act = glu * (1.0 / (1.0 + jnp.exp(-SWIGLU_ALPHA * glu))) * (lin + 1.0)
        y_ref[...] = jnp.dot(act.astype(BF16), wd_bf[...], preferred_element_type=F32) + bd_ref[...]

    @pl.when(pl.program_id(0) >= nu_ref[0])
    def _():
        y_ref[...] = jnp.zeros_like(y_ref)


def _expert_blocks(xs, blk_expert, n_used, w_up, b_up_de, w_down, b_down):
    n_rows, D = xs.shape
    nblk = n_rows // MOE_ROWS

    def xmap(j, be, nu):
        return (jnp.minimum(j, nu[0] - 1), 0)

    grid_spec = pltpu.PrefetchScalarGridSpec(
        num_scalar_prefetch=2,
        grid=(nblk,),
        in_specs=[pl.BlockSpec((MOE_ROWS, D), xmap),
                  pl.BlockSpec((None, D, 2 * D_FF), lambda j, be, nu: (be[j], 0, 0)),
                  pl.BlockSpec((None, 1, 2 * D_FF), lambda j, be, nu: (be[j], 0, 0)),
                  pl.BlockSpec((None, D_FF, D), lambda j, be, nu: (be[j], 0, 0)),
                  pl.BlockSpec((None, 1, D), lambda j, be, nu: (be[j], 0, 0)),
                  pl.BlockSpec((DEINT, DEINT), lambda j, be, nu: (0, 0))],
        out_specs=pl.BlockSpec((MOE_ROWS, D), lambda j, be, nu: (j, 0)),
        scratch_shapes=[pltpu.VMEM((D, 2 * D_FF), BF16), pltpu.VMEM((D_FF, D), BF16)],
    )
    return pl.pallas_call(
        _expert_kernel,
        grid_spec=grid_spec,
        out_shape=jax.ShapeDtypeStruct((n_rows, D), F32),
        compiler_params=_cparams(("arbitrary",)),
        name="moe_experts",
    )(blk_expert, n_used, xs, w_up, b_up_de, w_down, b_down, jnp.asarray(_deinterleave_matrix(), BF16))


def _combine_kernel(cnt_s, start_s, idx_ref, gate_ref, cntv_ref, y_ref, x1_ref, mod_ref,
                    lg_ref, lb_ref, out_ref, stage, sem):
    D = D_MODEL
    tn = idx_ref.shape[0]

    def seg_copy(src_row, dst_row, rows):
        return pltpu.make_async_copy(y_ref.at[pl.ds(pl.multiple_of(src_row, SEG), rows)],
                                     stage.at[pl.ds(pl.multiple_of(dst_row, SEG), rows)], sem)

    @pl.when(pl.program_id(0) == 0)
    def _():
        stage[...] = jnp.zeros_like(stage)

    n_big, n_small = _segment_dma_loops(
        cnt_s, lambda e, off, moved: (start_s[0, e] + moved, off + moved), seg_copy)

    hots, before = _tile_ranks(idx_ref[...], tn)
    seg_len = jnp.ceil(cntv_ref[...].astype(F32) * (1.0 / SEG)) * SEG
    seg_off = _lane_prefix_exclusive(seg_len)
    gates = gate_ref[...]
    lane_r = lax.broadcasted_iota(I32, (tn, STAGE_ROWS), 1)
    weights = jnp.zeros((tn, STAGE_ROWS), F32)
    for k in range(TOP_K):
        stage_row = jnp.sum(jnp.where(hots[k], seg_off + before, 0.0), axis=-1, keepdims=True)
        weights = jnp.where(lane_r == stage_row.astype(I32), gates[:, k:k + 1], weights)

    _drain(seg_copy, n_big, n_small)

    y = jnp.dot(weights.astype(BF16), stage[...].astype(BF16), preferred_element_type=F32)
    gate_f = mod_ref[:, 5 * D:6 * D]
    h = DEEPNORM_ALPHA * x1_ref[...] + (1.0 + gate_f) * y
    out_ref[...] = _layer_norm(h, lg_ref[...], lb_ref[...])


def _combine(idx, gates, tile_cnt, tile_start, y_pad, x1, mod3, mod_row0, ln_g, ln_b, B, T, t0):
    D = D_MODEL
    tn = SCATTER_TILE
    n_rows = B * T
    nt = n_rows // tn
    tps = T // tn
    smem = lambda imap: pl.BlockSpec((None, 1, LANES), imap, memory_space=pltpu.SMEM)
    out = pl.pallas_call(
        _combine_kernel,
        grid=(nt,),
        in_specs=[smem(lambda i: (t0 + i, 0, 0)), smem(lambda i: (t0 + i, 0, 0)),
                  pl.BlockSpec((tn, TOP_K), lambda i: (t0 + i, 0)),
                  pl.BlockSpec((tn, TOP_K), lambda i: (t0 + i, 0)),
                  pl.BlockSpec((None, 1, LANES), lambda i: (t0 + i, 0, 0)),
                  pl.BlockSpec(memory_space=pl.ANY),
                  pl.BlockSpec((tn, D), lambda i: (i, 0)),
                  pl.BlockSpec((None, 1, 6 * D), lambda i: (mod_row0 + i // tps, 0, 0)),
                  pl.BlockSpec((1, D), lambda i: (0, 0)),
                  pl.BlockSpec((1, D), lambda i: (0, 0))],
        out_specs=pl.BlockSpec((tn, D), lambda i: (i, 0)),
        out_shape=jax.ShapeDtypeStruct((n_rows, D), F32),
        scratch_shapes=[pltpu.VMEM((STAGE_ROWS, D), F32), pltpu.SemaphoreType.DMA(())],
        compiler_params=_cparams(("arbitrary",)),
        name="moe_combine",
    )(tile_cnt, tile_start, idx, gates, tile_cnt, y_pad, x1, mod3, ln_g, ln_b)
    return out.reshape(B, T, D)


def kernel(x_prompt, x_sample, c_prompt, c_sample, w_mod, b_mod, w_in, rpb, g_out_a, g_out_b, w_out, ln1_g, ln1_b,
           w_router, b_router, w_up, b_up, w_down, b_down, ln2_g, ln2_b):
    D = D_MODEL
    groups = [x_prompt, x_sample]
    conds = [c_prompt, c_sample]
    for x in groups:
        assert x.shape[-1] == D and x.shape[1] % (CHUNK * RES) == 0, x.shape
        assert (x.shape[0] * x.shape[1]) % max(ROUTE_TILE, SCATTER_TILE, PROJ_TOKENS) == 0, x.shape
    assert w_up.shape == (1, N_EXPERTS, D, 2 * D_FF) and w_in.shape == (1, D, 6 * WIDTH), "single layer, fixed widths"
    n_cond = sum(c.shape[0] for c in conds)
    pad_rows = -n_cond % 8
    c_all = jnp.concatenate(conds + [jnp.zeros((pad_rows, D), F32)], axis=0)
    mod3 = _modulation(c_all, w_mod[0], b_mod[0]).reshape(n_cond + pad_rows, 1, 6 * D)

    w_in_bf = w_in[0].astype(BF16)
    w_out_bf = w_out[0].astype(BF16)
    na_bias = _na_bias_table(rpb[0])
    g_a = g_out_a[0].reshape(1, WIDTH)
    g_b = g_out_b[0].reshape(1, WIDTH)
    ln1g, ln1b = ln1_g[0].reshape(1, D), ln1_b[0].reshape(1, D)
    ln2g, ln2b = ln2_g[0].reshape(1, D), ln2_b[0].reshape(1, D)
    w_router_pad = jnp.pad(w_router[0], ((0, 0), (0, LANES - N_EXPERTS)))
    b_router_pad = jnp.pad(b_router[0], (0, LANES - N_EXPERTS)).reshape(1, LANES)

    perm_np = _residue_permutation()
    perm, perm_t = jnp.asarray(perm_np, BF16), jnp.asarray(perm_np.T, BF16)

    x1s, u2s, logit_list = [], [], []
    mod_row0 = 0
    mod_rows = []
    for x in groups:
        B, T, _ = x.shape
        cos_v, sin_v = _rope_tables(T)
        qa, ka, va, qb, kb, vb = _input_projection_nat(x, mod3, w_in_bf, cos_v, sin_v, perm, mod_row0)
        oa = _dilated_attention(qa, ka, va, B, T)
        ob = _neighbourhood_attention(qb, kb, vb, na_bias, B, T)
        x1, u2, logits = _output_projection_nat(oa, ob, x, mod3, mod_row0, g_a, g_b, w_out_bf, ln1g, ln1b,
                                                w_router_pad, b_router_pad, perm_t)
        x1s.append(x1)
        u2s.append(u2)
        logit_list.append(logits)
        mod_rows.append(mod_row0)
        mod_row0 += B

    logits_all = jnp.concatenate(logit_list, axis=0)
    N = logits_all.shape[0]
    idx, gates, tile_cnt = _routing(logits_all)

    tc = tile_cnt[:, 0, :N_EXPERTS]
    seg = (tc + SEG - 1) // SEG * SEG
    total = jnp.sum(seg, axis=0)
    padded = (total + MOE_ROWS - 1) // MOE_ROWS * MOE_ROWS
    pad_end = jnp.cumsum(padded)
    pad_start = pad_end - padded
    tile_start = pad_start[None, :] + jnp.cumsum(seg, axis=0) - seg
    n_tiles = N // SCATTER_TILE
    n_blocks = (N * TOP_K + n_tiles * N_EXPERTS * (SEG - 1)) // MOE_ROWS + N_EXPERTS
    n_rows = n_blocks * MOE_ROWS
    blk_end = pad_end // MOE_ROWS
    n_used = blk_end[-1:].astype(I32)
    blk_expert = jnp.minimum(
        jnp.sum(blk_end[None, :] <= jnp.arange(n_blocks, dtype=I32)[:, None], axis=1), N_EXPERTS - 1).astype(I32)
    lane_pad = ((0, 0), (0, LANES - N_EXPERTS))
    tile_start3 = jnp.pad(tile_start, lane_pad).astype(I32)[:, None, :]
    tails = jnp.pad(jnp.stack([pad_start + total, pad_end]), lane_pad).astype(I32)

    xs = None
    row0 = 0
    for u2 in u2s:
        n = u2.shape[0]
        xs = _dispatch(idx, u2, tile_cnt, tile_start3, tails, xs, n_rows, row0 // SCATTER_TILE)
        row0 += n

    b_up_de = jnp.concatenate([b_up[0][:, 0::2], b_up[0][:, 1::2]], axis=-1).reshape(N_EXPERTS, 1, 2 * D_FF)
    b_down3 = b_down[0].reshape(N_EXPERTS, 1, D)
    y_pad = _expert_blocks(xs, blk_expert, n_used, w_up[0], b_up_de, w_down[0], b_down3)

    outs = []
    row0 = 0
    for x, x1, mrow in zip(groups, x1s, mod_rows):
        B, T, _ = x.shape
        n = B * T
        outs.append(_combine(idx, gates, tile_cnt, tile_start3, y_pad, x1, mod3, mrow, ln2g, ln2b, B, T,
                             row0 // SCATTER_TILE))
        row0 += n
    return tuple(outs)
```

```python
import functools

import numpy as np
import jax
import jax.numpy as jnp
from jax import lax
from jax.experimental import pallas as pl
from jax.experimental.pallas import tpu as pltpu

F32 = jnp.float32
BF16 = jnp.bfloat16
I32 = jnp.int32

D_MODEL = 1024
HEAD_DIM = 64
N_HEADS = 8
WIDTH = N_HEADS * HEAD_DIM
N_PAIRS = WIDTH // 128
ROPE_THETA = 10000.0
RADIUS = 64
GRID_W = 64
NA_ROWS = 8
NA_COLS = 16
N_EXPERTS = 32
TOP_K = 4
D_FF = 1024
SWIGLU_ALPHA = 1.702
SWIGLU_LIMIT = 7.0
DEEPNORM_ALPHA = 2.0 ** 0.25
LN_EPS = 1e-5
RMS_EPS = 1e-6
NEG = -1e30
LOG2E = 1.4426950408889634

LANES = 128
RES = 16
CHUNK = 128
RES_PER_STEP = 4
MOE_ROWS = 512
ROUTE_TILE = 1024
SCATTER_TILE = 512
VMEM_LIMIT = 56 * 1024 * 1024


def _cparams(sem, vmem=VMEM_LIMIT):
    return pltpu.CompilerParams(dimension_semantics=sem, vmem_limit_bytes=vmem)


def _mod_kernel(c_ref, w_ref, b_ref, o_ref):
    c = c_ref[...]
    s = c * (1.0 / (1.0 + jnp.exp(-c)))
    o_ref[...] = jnp.dot(s.astype(BF16), w_ref[...].astype(BF16), preferred_element_type=F32) + b_ref[...]


def _modulation(c, w_mod, b_mod):
    rows = c.shape[0]
    n_out = w_mod.shape[1]
    tn = 1024
    return pl.pallas_call(
        _mod_kernel,
        grid=(n_out // tn,),
        in_specs=[pl.BlockSpec((rows, D_MODEL), lambda j: (0, 0)),
                  pl.BlockSpec((D_MODEL, tn), lambda j: (0, j)),
                  pl.BlockSpec((1, tn), lambda j: (0, j))],
        out_specs=pl.BlockSpec((rows, tn), lambda j: (0, j)),
        out_shape=jax.ShapeDtypeStruct((rows, n_out), F32),
        compiler_params=_cparams(("arbitrary",)),
        name="modulation",
    )(c, w_mod, b_mod.reshape(1, n_out))


PERM = 256
PROJ_TOKENS = 1024
PROJ_ROWS = PROJ_TOKENS // RES


def _residue_permutation():
    p = np.zeros((PERM, PERM), np.float32)
    m, r = np.meshgrid(np.arange(PERM // RES), np.arange(RES), indexing="ij")
    p[(r * (PERM // RES) + m).reshape(-1), (m * RES + r).reshape(-1)] = 1.0
    return p


def _inproj_nat_kernel(x_ref, mod_ref, w_ref, cos_ref, sin_ref, perm_ref,
                       qa_ref, ka_ref, va_ref, qb_ref, kb_ref, vb_ref, u_nat, u_view):
    D = D_MODEL
    shift = mod_ref[:, 0:D]
    scale = mod_ref[:, D:2 * D]
    qscale = HEAD_DIM ** -0.5 * LOG2E
    u_nat[...] = (x_ref[...] * (1.0 + scale) + shift).astype(BF16)

    sub = 512
    for s in range(PROJ_TOKENS // sub):
        u = u_nat[s * sub:(s + 1) * sub, :]
        for i, (ref, mul) in enumerate(((qb_ref, qscale), (kb_ref, 1.0), (vb_ref, 1.0))):
            p = jnp.dot(u, w_ref[:, (3 + i) * WIDTH:(4 + i) * WIDTH], preferred_element_type=F32) * mul
            for hp in range(N_PAIRS):
                ref[hp, s * sub:(s + 1) * sub, :] = p[:, hp * LANES:(hp + 1) * LANES].astype(BF16)

    rows_g = PERM // RES
    for g in range(PROJ_TOKENS // PERM):
        pv = jnp.dot(perm_ref[...], u_nat[g * PERM:(g + 1) * PERM, :], preferred_element_type=F32).astype(BF16)
        for r in range(RES):
            u_view[r, g * rows_g:(g + 1) * rows_g, :] = pv[r * rows_g:(r + 1) * rows_g]

    lane = lax.broadcasted_iota(I32, (1, WIDTH), 1)
    first_half = (lane % HEAD_DIM) < (HEAD_DIM // 2)
    for q in range(RES // RES_PER_STEP):
        u = jnp.concatenate([u_view[q * RES_PER_STEP + a] for a in range(RES_PER_STEP)], axis=0)
        lo = q * RES_PER_STEP * LANES
        cosf = jnp.concatenate([jnp.tile(cos_ref[:, lo + a * LANES:lo + (a + 1) * LANES], (1, N_PAIRS))
                                for a in range(RES_PER_STEP)], axis=0)
        sinf = jnp.concatenate([jnp.tile(sin_ref[:, lo + a * LANES:lo + (a + 1) * LANES], (1, N_PAIRS))
                                for a in range(RES_PER_STEP)], axis=0)

        def rope(p):
            rot = jnp.where(first_half, pltpu.roll(p, WIDTH - HEAD_DIM // 2, 1), pltpu.roll(p, HEAD_DIM // 2, 1))
            return p * cosf + rot * sinf

        for i, ref in enumerate((qa_ref, ka_ref, va_ref)):
            p = jnp.dot(u, w_ref[:, i * WIDTH:(i + 1) * WIDTH], preferred_element_type=F32)
            if i == 0:
                p = rope(p) * qscale
            elif i == 1:
                p = rope(p)
            for a in range(RES_PER_STEP):
                for hp in range(N_PAIRS):
                    ref[q * RES_PER_STEP + a, hp] = (
                        p[a * PROJ_ROWS:(a + 1) * PROJ_ROWS, hp * LANES:(hp + 1) * LANES].astype(BF16))


def _input_projection_nat(x, mod3, w_in_bf, cos_v, sin_v, perm, mod_row0):
    B, T, D = x.shape
    N = B * T
    BL = N // RES
    tps = T // PROJ_TOKENS
    res_shape = jax.ShapeDtypeStruct((RES, N_PAIRS, BL, LANES), BF16)
    nat_shape = jax.ShapeDtypeStruct((N_PAIRS, N, LANES), BF16)
    res_spec = pl.BlockSpec((RES, N_PAIRS, PROJ_ROWS, LANES), lambda i: (0, 0, i, 0))
    nat_spec = pl.BlockSpec((N_PAIRS, PROJ_TOKENS, LANES), lambda i: (0, i, 0))
    return pl.pallas_call(
        _inproj_nat_kernel,
        grid=(N // PROJ_TOKENS,),
        in_specs=[pl.BlockSpec((PROJ_TOKENS, D), lambda i: (i, 0)),
                  pl.BlockSpec((None, 1, 6 * D), lambda i: (mod_row0 + i // tps, 0, 0)),
                  pl.BlockSpec((D, 6 * WIDTH), lambda i: (0, 0)),
                  pl.BlockSpec((PROJ_ROWS, RES * LANES), lambda i: (i % tps, 0)),
                  pl.BlockSpec((PROJ_ROWS, RES * LANES), lambda i: (i % tps, 0)),
                  pl.BlockSpec((PERM, PERM), lambda i: (0, 0))],
        out_specs=[res_spec, res_spec, res_spec, nat_spec, nat_spec, nat_spec],
        out_shape=[res_shape, res_shape, res_shape, nat_shape, nat_shape, nat_shape],
        scratch_shapes=[pltpu.VMEM((PROJ_TOKENS, D), BF16), pltpu.VMEM((RES, PROJ_ROWS, D), BF16)],
        compiler_params=_cparams(("arbitrary",)),
        name="input_projection",
    )(x.reshape(N, D), mod3, w_in_bf, cos_v, sin_v, perm)


def _rope_tables(T):
    half = HEAD_DIM // 2
    inv_freq = 1.0 / (ROPE_THETA ** (jnp.arange(half, dtype=F32) / half))
    ang = jnp.arange(T, dtype=F32)[:, None] * inv_freq[None, :]
    cos, sin = jnp.cos(ang), jnp.sin(ang)
    cos_h = jnp.concatenate([cos, cos], axis=-1)
    sin_h = jnp.concatenate([-sin, sin], axis=-1)
    cos2 = jnp.concatenate([cos_h, cos_h], axis=-1)
    sin2 = jnp.concatenate([sin_h, sin_h], axis=-1)
    L = T // RES
    return cos2.reshape(L, RES * LANES), sin2.reshape(L, RES * LANES)


P2_ROWS = 32
P2_KROWS = 64
P1_ROWS = 16
P1_KROWS = 32
P1_SHIFT = 8
HALO = 64
P3_BATCH = 8
P2_BATCH = 2
P1_BATCH = 2


def _band_tables(has_halo):
    def mask(ok):
        return np.where(ok, 0.0, NEG).astype(np.float32)
    mq = np.arange(CHUNK)[:, None]
    koff3 = np.arange(2 * CHUNK) - HALO
    b3 = mask(np.abs(mq - koff3[None, :]) <= RADIUS)
    j = np.repeat(np.arange(4), P2_ROWS)[:, None]
    a = np.tile(np.arange(P2_ROWS), 4)[:, None]
    jk = np.repeat(np.arange(4), P2_KROWS)[None, :]
    bk = np.tile(np.arange(P2_KROWS), 4)[None, :]
    b2 = mask(np.abs(4 * (a - (bk - 16)) + (j - jk)) <= RADIUS)
    koff2 = (bk - 16).reshape(-1)
    r = np.repeat(np.arange(RES), P1_ROWS)[:, None]
    a = np.tile(np.arange(P1_ROWS), RES)[:, None]
    rk = np.repeat(np.arange(RES), P1_KROWS)[None, :]
    bk = np.tile(np.arange(P1_KROWS), RES)[None, :]
    b1 = mask(np.abs(RES * (a - (bk - P1_SHIFT)) + (r - rk)) <= RADIUS)
    koff1 = (bk - P1_SHIFT).reshape(-1)
    return (jnp.asarray(b3), jnp.asarray(koff3.astype(np.int32)[None, :]),
            jnp.asarray(b2), jnp.asarray(koff2.astype(np.int32)[None, :]),
            jnp.asarray(b1), jnp.asarray(koff1.astype(np.int32)[None, :]))


def _attend_pairs(problems, head0):
    scores = []
    for q, k, _, _ in problems:
        for h in range(2):
            sel = head0 if h == 0 else jnp.logical_not(head0)
            qh = jnp.where(sel, q, jnp.zeros_like(q))
            scores.append(lax.dot_general(qh, k, (((1,), (1,)), ((), ())), preferred_element_type=F32))
    probs, stats = [], []
    for i, s in enumerate(scores):
        s = s + problems[i // 2][3]
        m = jnp.max(s, axis=-1, keepdims=True)
        p = jnp.exp2(s - m)
        stats.append((m, jnp.sum(p, axis=-1, keepdims=True)))
        probs.append(p.astype(BF16))
    outs = [jnp.dot(p, problems[i // 2][2], preferred_element_type=F32) for i, p in enumerate(probs)]
    results = []
    for i in range(len(problems)):
        (m0, l0), (m1, l1) = stats[2 * i], stats[2 * i + 1]
        results.append((jnp.where(head0, outs[2 * i], outs[2 * i + 1]),
                        jnp.where(head0, m0, m1), jnp.where(head0, l0, l1)))
    return results


def _dilated_kernel(*refs, has_halo, seq_rows, chunks_per_seq):
    if has_halo:
        (q_ref, k_ref, v_ref, kp_ref, kn_ref, vp_ref, vn_ref,
         b3_ref, o3_ref, b2_ref, o2_ref, b1_ref, o1_ref,
         out_ref, kf, vf, kf8, vf8, acc, ms, ls) = refs
    else:
        (q_ref, k_ref, v_ref, b3_ref, o3_ref, b2_ref, o2_ref, b1_ref, o1_ref,
         out_ref, kf, vf, kf8, vf8, acc, ms, ls) = refs

    c = pl.program_id(0) % chunks_per_seq
    row0 = c * CHUNK
    lane = lax.broadcasted_iota(I32, (1, LANES), 1)
    head0 = lane < HEAD_DIM

    for r in range(RES):
        kf[r, HALO:HALO + CHUNK, :] = k_ref[r]
        vf[r, HALO:HALO + CHUNK, :] = v_ref[r]
        if has_halo:
            kf[r, 0:HALO, :] = kp_ref[r]
            vf[r, 0:HALO, :] = vp_ref[r]
            kf[r, HALO + CHUNK:, :] = kn_ref[r]
            vf[r, HALO + CHUNK:, :] = vn_ref[r]
        else:
            zeros = jnp.zeros((HALO, LANES), BF16)
            for ref in (kf, vf):
                ref[r, 0:HALO, :] = zeros
                ref[r, HALO + CHUNK:, :] = zeros
        for src, dst in ((kf, kf8), (vf, vf8)):
            w32 = src[r].astype(F32)
            dst[r] = jnp.concatenate([w32[P1_SHIFT:], w32[:P1_SHIFT]], axis=0).astype(BF16)

    def in_seq(base, off_ref):
        kv_row = base + off_ref[...]
        return jnp.where((kv_row >= 0) & (kv_row < seq_rows), 0.0, NEG).astype(F32)

    def attend(problems):
        return _attend_pairs(problems, head0)

    def merge(a_old, m_old, l_old, o, m, l):
        mn = jnp.maximum(m_old, m)
        wa = jnp.exp2(m_old - mn)
        wb = jnp.exp2(m - mn)
        return a_old * wa + o * wb, mn, l_old * wa + l * wb

    bias3 = b3_ref[...] + in_seq(row0, o3_ref)

    def body3(it, carry):
        rs = [it * P3_BATCH + u for u in range(P3_BATCH)]
        res = attend([(q_ref[r], kf[r], vf[r], bias3) for r in rs])
        for r, (o, m, l) in zip(rs, res):
            acc[r] = o
            ms[r] = m
            ls[r] = l
        return carry

    lax.fori_loop(0, RES // P3_BATCH, body3, 0)

    b2 = b2_ref[...]

    def body2(it, carry):
        def gather(ref, r4, start, rows):
            return jnp.concatenate([ref[4 * j + r4, pl.ds(start, rows), :] for j in range(4)], axis=0)

        where, problems = [], []
        for u in range(P2_BATCH):
            g = it * P2_BATCH + u
            qs = pl.multiple_of(g * P2_ROWS, P2_ROWS)
            ks = pl.multiple_of(HALO - 16 + g * P2_ROWS, 16)
            bias = b2 + in_seq(row0 + g * P2_ROWS, o2_ref)
            for r4 in range(4):
                where.append((r4, qs))
                problems.append((gather(q_ref, r4, qs, P2_ROWS), gather(kf, r4, ks, P2_KROWS),
                                 gather(vf, r4, ks, P2_KROWS), bias))
        for (r4, qs), (o, m, l) in zip(where, attend(problems)):
            a_new, m_new, l_new = merge(gather(acc, r4, qs, P2_ROWS), gather(ms, r4, qs, P2_ROWS),
                                        gather(ls, r4, qs, P2_ROWS), o, m, l)
            for j in range(4):
                acc[4 * j + r4, pl.ds(qs, P2_ROWS), :] = a_new[j * P2_ROWS:(j + 1) * P2_ROWS]
                ms[4 * j + r4, pl.ds(qs, P2_ROWS), :] = m_new[j * P2_ROWS:(j + 1) * P2_ROWS]
                ls[4 * j + r4, pl.ds(qs, P2_ROWS), :] = l_new[j * P2_ROWS:(j + 1) * P2_ROWS]
        return carry

    lax.fori_loop(0, CHUNK // P2_ROWS // P2_BATCH, body2, 0)

    b1 = b1_ref[...]

    def body1(it, carry):
        def gather(ref, start, rows):
            return jnp.concatenate([ref[r, pl.ds(start, rows), :] for r in range(RES)], axis=0)

        starts, problems = [], []
        for u in range(P1_BATCH):
            g = it * P1_BATCH + u
            qs = pl.multiple_of(g * P1_ROWS, P1_ROWS)
            ks = pl.multiple_of(HALO - 16 + g * P1_ROWS, 16)
            starts.append(qs)
            problems.append((gather(q_ref, qs, P1_ROWS), gather(kf8, ks, P1_KROWS), gather(vf8, ks, P1_KROWS),
                             b1 + in_seq(row0 + g * P1_ROWS, o1_ref)))
        for qs, (o, m, l) in zip(starts, attend(problems)):
            a_new, _, l_new = merge(gather(acc, qs, P1_ROWS), gather(ms, qs, P1_ROWS),
                                    gather(ls, qs, P1_ROWS), o, m, l)
            done = (a_new / l_new).astype(BF16)
            for r in range(RES):
                out_ref[r, pl.ds(qs, P1_ROWS), :] = done[r * P1_ROWS:(r + 1) * P1_ROWS]
        return carry

    lax.fori_loop(0, CHUNK // P1_ROWS // P1_BATCH, body1, 0)


def _dilated_attention(qa, ka, va, B, T):
    L = T // RES
    BL = B * L
    cps = L // CHUNK
    has_halo = cps > 1
    tables = _band_tables(has_halo)
    blk = (RES, None, CHUNK, LANES)
    center = pl.BlockSpec(blk, lambda i, hp: (0, hp, i, 0))
    in_specs = [center, center, center]
    args = [qa, ka, va]
    if has_halo:
        hblk = (RES, None, HALO, LANES)
        per = CHUNK // HALO

        def prev_map(i, hp):
            return (0, hp, jnp.maximum(per * i - 1, (i // cps) * cps * per), 0)

        def next_map(i, hp):
            return (0, hp, jnp.minimum(per * i + per, (i // cps + 1) * cps * per - 1), 0)

        in_specs += [pl.BlockSpec(hblk, prev_map), pl.BlockSpec(hblk, next_map),
                     pl.BlockSpec(hblk, prev_map), pl.BlockSpec(hblk, next_map)]
        args += [ka, ka, va, va]
    for t in tables:
        in_specs.append(pl.BlockSpec(t.shape, lambda i, hp: (0, 0)))
        args.append(t)
    kern = functools.partial(_dilated_kernel, has_halo=has_halo, seq_rows=L, chunks_per_seq=cps)
    return pl.pallas_call(
        kern,
        grid=(BL // CHUNK, N_PAIRS),
        in_specs=in_specs,
        out_specs=pl.BlockSpec(blk, lambda i, hp: (0, hp, i, 0)),
        out_shape=jax.ShapeDtypeStruct((RES, N_PAIRS, BL, LANES), BF16),
        scratch_shapes=[pltpu.VMEM((RES, 2 * CHUNK, LANES), BF16)] * 4 + [
                        pltpu.VMEM((RES, CHUNK, LANES), F32),
                        pltpu.VMEM((RES, CHUNK, LANES), F32),
                        pltpu.VMEM((RES, CHUNK, LANES), F32)],
        compiler_params=_cparams(("arbitrary", "arbitrary")),
        name="dilated_attention",
    )(*args)


NA_KEYS = NA_ROWS * GRID_W
NA_BLOCK_ROWS = 32
NA_BATCH = 32


def _na_bias_table(rpb):
    c = np.arange(GRID_W)
    col_start = np.clip(c - NA_COLS // 2, 0, GRID_W - NA_COLS)
    col_mask = (c[None, :] >= col_start[:, None]) & (c[None, :] < col_start[:, None] + NA_COLS)
    dc_idx = np.clip(c[None, :] - c[:, None], -(NA_COLS - 1), NA_COLS - 1) + NA_COLS - 1
    rel = rpb.astype(F32)[:, :, dc_idx] * LOG2E
    rel = jnp.where(col_mask[None, None], rel, NEG)
    per_off = [rel[:, d0:d0 + NA_ROWS].transpose(0, 2, 1, 3).reshape(N_HEADS, GRID_W, NA_KEYS)
               for d0 in range(NA_ROWS)]
    return jnp.stack(per_off, axis=0).reshape(NA_ROWS, N_PAIRS, 2, GRID_W, NA_KEYS)


def _na_kernel(q_ref, k_ref, v_ref, bias_ref, out_ref, *, grid_rows, block_rows):
    gb = pl.program_id(2)
    lane = lax.broadcasted_iota(I32, (1, LANES), 1)
    head0 = lane < HEAD_DIM

    def body(it, carry):
        rows, scores = [], []
        for u in range(NA_BATCH):
            i = it * NA_BATCH + u
            g = gb * block_rows + i
            rs = jnp.clip(g - NA_ROWS // 2, 0, grid_rows - NA_ROWS)
            d0 = rs - g + NA_ROWS - 1
            qs = pl.multiple_of(i * GRID_W, GRID_W)
            ks = pl.multiple_of(rs * GRID_W, GRID_W)
            q = q_ref[pl.ds(qs, GRID_W), :]
            k = k_ref[pl.ds(ks, NA_KEYS), :]
            rows.append((qs, ks, d0))
            for h in range(2):
                sel = head0 if h == 0 else jnp.logical_not(head0)
                qh = jnp.where(sel, q, jnp.zeros_like(q))
                scores.append(lax.dot_general(qh, k, (((1,), (1,)), ((), ())), preferred_element_type=F32))
        probs, sums = [], []
        for n, s in enumerate(scores):
            s = s + bias_ref[rows[n // 2][2], n % 2]
            p = jnp.exp2(s - jnp.max(s, axis=-1, keepdims=True))
            sums.append(jnp.sum(p, axis=-1, keepdims=True))
            probs.append(p.astype(BF16))
        outs = []
        for n, p in enumerate(probs):
            v = v_ref[pl.ds(rows[n // 2][1], NA_KEYS), :]
            outs.append(jnp.dot(p, v, preferred_element_type=F32) / sums[n])
        for u, (qs, _, _) in enumerate(rows):
            out_ref[pl.ds(qs, GRID_W), :] = jnp.where(head0, outs[2 * u], outs[2 * u + 1]).astype(BF16)
        return carry

    lax.fori_loop(0, block_rows // NA_BATCH, body, 0)


def _neighbourhood_attention(qb, kb, vb, bias, B, T):
    G = T // GRID_W
    rb = min(NA_BLOCK_ROWS, G)
    nb = G // rb
    kern = functools.partial(_na_kernel, grid_rows=G, block_rows=rb)
    seq = pl.BlockSpec((None, T, LANES), lambda hp, b, gb: (hp, b, 0))
    qblk = pl.BlockSpec((None, rb * GRID_W, LANES), lambda hp, b, gb: (hp, b * nb + gb, 0))
    return pl.pallas_call(
        kern,
        grid=(N_PAIRS, B, nb),
        in_specs=[qblk, seq, seq,
                  pl.BlockSpec((NA_ROWS, None, 2, GRID_W, NA_KEYS), lambda hp, b, gb: (0, hp, 0, 0, 0))],
        out_specs=qblk,
        out_shape=jax.ShapeDtypeStruct((N_PAIRS, B * T, LANES), BF16),
        compiler_params=_cparams(("arbitrary", "arbitrary", "arbitrary")),
        name="neighbourhood_attention",
    )(qb, kb, vb, bias)


def _layer_norm(h, g, b):
    mu = jnp.mean(h, axis=-1, keepdims=True)
    d = h - mu
    var = jnp.mean(d * d, axis=-1, keepdims=True)
    return d * lax.rsqrt(var + LN_EPS) * g + b


def _outproj_nat_kernel(oa_ref, ob_ref, x_ref, mod_ref, ga_ref, gb_ref, w_ref, lg_ref, lb_ref, wr_ref, br_ref,
                        permt_ref, x1_ref, u2_ref, logit_ref, mix_scr):
    D = D_MODEL
    gate_a = mod_ref[:, 2 * D:3 * D]
    shift_f = mod_ref[:, 3 * D:4 * D]
    scale_f = mod_ref[:, 4 * D:5 * D]

    def rms(o, g):
        return o * lax.rsqrt(jnp.mean(o * o, axis=-1, keepdims=True) + RMS_EPS) * g

    rows_g = PERM // RES
    for g in range(PROJ_TOKENS // PERM):
        grouped = jnp.concatenate(
            [jnp.concatenate([oa_ref[r, hp, g * rows_g:(g + 1) * rows_g, :] for hp in range(N_PAIRS)], axis=1)
             for r in range(RES)], axis=0)
        oa = jnp.dot(permt_ref[...], grouped, preferred_element_type=F32)
        ob = jnp.concatenate([ob_ref[hp, g * PERM:(g + 1) * PERM, :] for hp in range(N_PAIRS)], axis=1).astype(F32)
        mix = jnp.concatenate([rms(oa, ga_ref[...]), rms(ob, gb_ref[...])], axis=1)
        mix_scr[g * PERM:(g + 1) * PERM, :] = mix.astype(BF16)

    wr = wr_ref[...]
    wr_hi = wr.astype(BF16)
    wr_lo = (wr - wr_hi.astype(F32)).astype(BF16)
    sub = 512
    for s in range(PROJ_TOKENS // sub):
        rows = slice(s * sub, (s + 1) * sub)
        y = jnp.dot(mix_scr[rows, :], w_ref[...], preferred_element_type=F32)
        h = DEEPNORM_ALPHA * x_ref[rows, :] + (1.0 + gate_a) * y
        x1 = _layer_norm(h, lg_ref[...], lb_ref[...])
        x1_ref[rows, :] = x1
        u2 = x1 * (1.0 + scale_f) + shift_f
        u2_ref[rows, :] = u2.astype(BF16)
        u_hi = u2.astype(BF16)
        u_lo = (u2 - u_hi.astype(F32)).astype(BF16)
        logit_ref[rows, :] = (jnp.dot(u_hi, wr_hi, preferred_element_type=F32)
                              + jnp.dot(u_lo, wr_hi, preferred_element_type=F32)
                              + jnp.dot(u_hi, wr_lo, preferred_element_type=F32)) + br_ref[...]


def _output_projection_nat(oa, ob, x, mod3, mod_row0, g_a, g_b, w_out_bf, ln_g, ln_b, w_router_pad, b_router_pad,
                           perm_t):
    B, T, D = x.shape
    N = B * T
    tps = T // PROJ_TOKENS
    const = lambda shape: pl.BlockSpec(shape, lambda i: tuple(0 for _ in shape))
    rows = lambda width: pl.BlockSpec((PROJ_TOKENS, width), lambda i: (i, 0))
    return pl.pallas_call(
        _outproj_nat_kernel,
        grid=(N // PROJ_TOKENS,),
        in_specs=[pl.BlockSpec((RES, N_PAIRS, PROJ_ROWS, LANES), lambda i: (0, 0, i, 0)),
                  pl.BlockSpec((N_PAIRS, PROJ_TOKENS, LANES), lambda i: (0, i, 0)),
                  rows(D),
                  pl.BlockSpec((None, 1, 6 * D), lambda i: (mod_row0 + i // tps, 0, 0)),
                  const((1, WIDTH)), const((1, WIDTH)), const((2 * WIDTH, D)),
                  const((1, D)), const((1, D)), const((D, LANES)), const((1, LANES)), const((PERM, PERM))],
        out_specs=[rows(D), rows(D), rows(LANES)],
        out_shape=[jax.ShapeDtypeStruct((N, D), F32), jax.ShapeDtypeStruct((N, D), BF16),
                   jax.ShapeDtypeStruct((N, LANES), F32)],
        scratch_shapes=[pltpu.VMEM((PROJ_TOKENS, 2 * WIDTH), BF16)],
        compiler_params=_cparams(("arbitrary",)),
        name="output_projection",
    )(oa, ob, x.reshape(N, D), mod3, g_a, g_b, w_out_bf, ln_g, ln_b, w_router_pad, b_router_pad, perm_t)


def _route_kernel(logit_ref, idx_ref, gate_ref, cnt_ref):
    for t in range(ROUTE_TILE // SCATTER_TILE):
        rows = slice(t * SCATTER_TILE, (t + 1) * SCATTER_TILE)
        sel, gates, cnt = _route_tile(logit_ref[rows, :])
        idx_ref[rows, :] = sel
        gate_ref[rows, :] = gates
        cnt_ref[t] = cnt


def _route_tile(raw):
    tn = raw.shape[0]
    lane = lax.broadcasted_iota(I32, (tn, LANES), 1)
    logits = jnp.where(lane < N_EXPERTS, raw, -3.0e38)
    vals, idxs = [], []
    multi = jnp.zeros((tn, LANES), F32)
    for _ in range(TOP_K):
        m = jnp.max(logits, axis=-1, keepdims=True)
        idx = jnp.min(jnp.where(logits == m, lane, LANES), axis=-1, keepdims=True)
        hot = lane == idx
        vals.append(m)
        idxs.append(idx)
        multi = multi + hot.astype(F32)
        logits = jnp.where(hot, -3.0e38, logits)
    es = [jnp.exp(v - vals[0]) for v in vals]
    tot = es[0] + es[1] + es[2] + es[3]
    lane4 = lax.broadcasted_iota(I32, (tn, TOP_K), 1)
    sel = jnp.zeros((tn, TOP_K), I32)
    gates = jnp.zeros((tn, TOP_K), F32)
    for k in range(TOP_K):
        sel = jnp.where(lane4 == k, idxs[k], sel)
        gates = jnp.where(lane4 == k, es[k] / tot, gates)
    return sel, gates, jnp.sum(multi, axis=0, keepdims=True).astype(I32)


def _routing(logits):
    N = logits.shape[0]
    tn = ROUTE_TILE
    nt = N // SCATTER_TILE
    per_step = ROUTE_TILE // SCATTER_TILE
    return pl.pallas_call(
        _route_kernel,
        grid=(N // tn,),
        in_specs=[pl.BlockSpec((tn, LANES), lambda i: (i, 0))],
        out_specs=[pl.BlockSpec((tn, TOP_K), lambda i: (i, 0)),
                   pl.BlockSpec((tn, TOP_K), lambda i: (i, 0)),
                   pl.BlockSpec((per_step, 1, LANES), lambda i: (i, 0, 0))],
        out_shape=[jax.ShapeDtypeStruct((N, TOP_K), I32), jax.ShapeDtypeStruct((N, TOP_K), F32),
                   jax.ShapeDtypeStruct((nt, 1, LANES), I32)],
        compiler_params=_cparams(("arbitrary",)),
        name="moe_routing",
    )(logits)


SEG = 8
STAGE_ROWS = SCATTER_TILE * TOP_K + N_EXPERTS * SEG


def _lane_prefix_exclusive(v):
    lane = lax.broadcasted_iota(I32, v.shape, 1)
    incl = v
    s = 1
    while s < LANES:
        incl = incl + jnp.where(lane >= s, pltpu.roll(incl, s, 1), 0.0)
        s *= 2
    return incl - v


def _tile_ranks(idx, tn):
    lane = lax.broadcasted_iota(I32, (tn, LANES), 1)
    hots = [lane == idx[:, k:k + 1] for k in range(TOP_K)]
    multi = jnp.zeros((tn, LANES), F32)
    for h in hots:
        multi = multi + h.astype(F32)
    row = lax.broadcasted_iota(I32, (tn, tn), 0)
    col = lax.broadcasted_iota(I32, (tn, tn), 1)
    lower = (col < row).astype(BF16)
    before = jnp.dot(lower, multi.astype(BF16), preferred_element_type=F32)
    return hots, before


BIG_SEG = 4 * SEG


def _segment_dma_loops(cnt_s, rows_of, copy):
    def per_expert(e, carry):
        off, n_big, n_small = carry
        nseg = (cnt_s[0, e] + SEG - 1) // SEG
        big = nseg // (BIG_SEG // SEG)
        small = nseg - big * (BIG_SEG // SEG)

        def one_big(q, c):
            src, dst = rows_of(e, off, q * BIG_SEG)
            copy(src, dst, BIG_SEG).start()
            return c

        def one_small(q, c):
            src, dst = rows_of(e, off, big * BIG_SEG + q * SEG)
            copy(src, dst, SEG).start(priority=1)
            return c

        lax.fori_loop(0, big, one_big, 0)
        lax.fori_loop(0, small, one_small, 0)
        return off + nseg * SEG, n_big + big, n_small + small

    _, n_big, n_small = lax.fori_loop(0, N_EXPERTS, per_expert, (0, 0, 0))
    return n_big, n_small


def _drain(copy, n_big, n_small):
    def wait_big(q, c):
        copy(0, 0, BIG_SEG).wait()
        return c

    def wait_small(q, c):
        copy(0, 0, SEG).wait()
        return c

    lax.fori_loop(0, n_big, wait_big, 0)
    lax.fori_loop(0, n_small, wait_small, 0)


def _dispatch_kernel(*refs, first, n_steps):
    if first:
        cnt_s, start_s, tail_s, idx_ref, u_ref, cntv_ref, xs_ref, stage, zbuf, pending, sem, zsem = refs
    else:
        cnt_s, start_s, tail_s, idx_ref, u_ref, cntv_ref, _, xs_ref, stage, zbuf, pending, sem, zsem = refs
    tn = idx_ref.shape[0]
    step = pl.program_id(0)
    slot = step % 2

    if first:
        @pl.when(pl.program_id(0) == 0)
        def _():
            zbuf[...] = jnp.zeros_like(zbuf)

            def tail_copy(row):
                return pltpu.make_async_copy(zbuf, xs_ref.at[pl.ds(pl.multiple_of(row, SEG), SEG)], zsem)

            def fill(e, n):
                lo = tail_s[0, e]
                nfull = (tail_s[1, e] - lo) // SEG

                def one(q, c):
                    tail_copy(lo + q * SEG).start()
                    return c

                lax.fori_loop(0, nfull, one, 0)
                return n + nfull

            n = lax.fori_loop(0, N_EXPERTS, fill, 0)

            def drain(q, c):
                tail_copy(0).wait()
                return c

            lax.fori_loop(0, n, drain, 0)

    hots, before = _tile_ranks(idx_ref[...], tn)
    cntf = cntv_ref[...].astype(F32)
    seg_len = jnp.ceil(cntf * (1.0 / SEG)) * SEG
    seg_off = _lane_prefix_exclusive(seg_len)
    lane_r = lax.broadcasted_iota(I32, (tn, STAGE_ROWS), 1)
    onehot = jnp.zeros((tn, STAGE_ROWS), F32)
    for k in range(TOP_K):
        stage_row = jnp.sum(jnp.where(hots[k], seg_off + before, 0.0), axis=-1, keepdims=True)
        onehot = jnp.where(lane_r == stage_row.astype(I32), 1.0, onehot)
    stage[slot] = lax.dot_general(onehot.astype(BF16), u_ref[...].astype(BF16), (((0,), (0,)), ((), ())),
                                  preferred_element_type=F32)

    def copier(buf):
        def seg_copy(src_row, dst_row, rows):
            return pltpu.make_async_copy(stage.at[buf, pl.ds(pl.multiple_of(src_row, SEG), rows)],
                                         xs_ref.at[pl.ds(pl.multiple_of(dst_row, SEG), rows)], sem.at[buf])
        return seg_copy

    @pl.when(step > 0)
    def _():
        _drain(copier(1 - slot), pending[0], pending[1])

    n_big, n_small = _segment_dma_loops(
        cnt_s, lambda e, off, moved: (off + moved, start_s[0, e] + moved), copier(slot))
    pending[0] = n_big
    pending[1] = n_small

    @pl.when(step == n_steps - 1)
    def _():
        _drain(copier(slot), n_big, n_small)


def _dispatch(idx, u2, tile_cnt, tile_start, tails, xs, xs_rows, t0):
    n, D = u2.shape
    tn = SCATTER_TILE
    nt = n // tn
    first = xs is None
    smem = lambda shape, imap: pl.BlockSpec(shape, imap, memory_space=pltpu.SMEM)
    kern = functools.partial(_dispatch_kernel, first=first, n_steps=nt)
    in_specs = [smem((None, 1, LANES), lambda i: (t0 + i, 0, 0)),
                smem((None, 1, LANES), lambda i: (t0 + i, 0, 0)),
                smem((2, LANES), lambda i: (0, 0)),
                pl.BlockSpec((tn, TOP_K), lambda i: (t0 + i, 0)),
                pl.BlockSpec((tn, D), lambda i: (i, 0)),
                pl.BlockSpec((None, 1, LANES), lambda i: (t0 + i, 0, 0))]
    args = [tile_cnt, tile_start, tails, idx, u2, tile_cnt]
    alias = {}
    if not first:
        alias = {len(args): 0}
        in_specs.append(pl.BlockSpec(memory_space=pl.ANY))
        args.append(xs)
    return pl.pallas_call(
        kern,
        grid=(nt,),
        in_specs=in_specs,
        out_specs=pl.BlockSpec(memory_space=pl.ANY),
        out_shape=jax.ShapeDtypeStruct((xs_rows, D), F32),
        scratch_shapes=[pltpu.VMEM((2, STAGE_ROWS, D), F32), pltpu.VMEM((SEG, D), F32), pltpu.SMEM((2,), I32),
                        pltpu.SemaphoreType.DMA((2,)), pltpu.SemaphoreType.DMA(())],
        input_output_aliases=alias,
        compiler_params=_cparams(("arbitrary",)),
        name="moe_dispatch",
    )(*args)


DEINT = 256


def _deinterleave_matrix():
    perm = np.zeros((DEINT, DEINT), np.float32)
    j = np.arange(DEINT // 2)
    perm[2 * j, j] = 1.0
    perm[2 * j + 1, DEINT // 2 + j] = 1.0
    return perm


def _expert_kernel(be_ref, nu_ref, x_ref, wu_ref, bu_ref, wd_ref, bd_ref, p_ref, y_ref, wu_bf, wd_bf):
    j = pl.program_id(0)

    @pl.when((j == 0) | (be_ref[j] != be_ref[jnp.maximum(j - 1, 0)]))
    def _():
        half = DEINT // 2
        for g in range(2 * D_FF // DEINT):
            wg = wu_ref[:, g * DEINT:(g + 1) * DEINT].astype(BF16)
            t = jnp.dot(wg, p_ref[...], preferred_element_type=F32)
            wu_bf[:, g * half:(g + 1) * half] = t[:, :half].astype(BF16)
            wu_bf[:, D_FF + g * half:D_FF + (g + 1) * half] = t[:, half:].astype(BF16)
        wd_bf[...] = wd_ref[...].astype(BF16)

    @pl.when(j < nu_ref[0])
    def _():
        x = x_ref[...].astype(BF16)
        hu = jnp.dot(x, wu_bf[...], preferred_element_type=F32) + bu_ref[...]
        glu = jnp.minimum(hu[:, :D_FF], SWIGLU_LIMIT)
        lin = jnp.clip(hu[:, D_FF:], -SWIGLU_LIMIT, SWIGLU_LIMIT)
        act = glu * (1.0 / (1.0 + jnp.exp(-SWIGLU_ALPHA * glu))) * (lin + 1.0)
        y_ref[...] = jnp.dot(act.astype(BF16), wd_bf[...], preferred_element_type=F32) + bd_ref[...]

    @pl.when(pl.program_id(0) >= nu_ref[0])
    def _():
        y_ref[...] = jnp.zeros_like(y_ref)


def _expert_blocks(xs, blk_expert, n_used, w_up, b_up_de, w_down, b_down):
    n_rows, D = xs.shape
    nblk = n_rows // MOE_ROWS

    def xmap(j, be, nu):
        return (jnp.minimum(j, nu[0] - 1), 0)

    grid_spec = pltpu.PrefetchScalarGridSpec(
        num_scalar_prefetch=2,
        grid=(nblk,),
        in_specs=[pl.BlockSpec((MOE_ROWS, D), xmap),
                  pl.BlockSpec((None, D, 2 * D_FF), lambda j, be, nu: (be[j], 0, 0)),
                  pl.BlockSpec((None, 1, 2 * D_FF), lambda j, be, nu: (be[j], 0, 0)),
                  pl.BlockSpec((None, D_FF, D), lambda j, be, nu: (be[j], 0, 0)),
                  pl.BlockSpec((None, 1, D), lambda j, be, nu: (be[j], 0, 0)),
                  pl.BlockSpec((DEINT, DEINT), lambda j, be, nu: (0, 0))],
        out_specs=pl.BlockSpec((MOE_ROWS, D), lambda j, be, nu: (j, 0)),
        scratch_shapes=[pltpu.VMEM((D, 2 * D_FF), BF16), pltpu.VMEM((D_FF, D), BF16)],
    )
    return pl.pallas_call(
        _expert_kernel,
        grid_spec=grid_spec,
        out_shape=jax.ShapeDtypeStruct((n_rows, D), F32),
        compiler_params=_cparams(("arbitrary",)),
        name="moe_experts",
    )(blk_expert, n_used, xs, w_up, b_up_de, w_down, b_down, jnp.asarray(_deinterleave_matrix(), BF16))


def _combine_kernel(cnt_s, start_s, idx_ref, gate_ref, cntv_ref, y_ref, x1_ref, mod_ref,
                    lg_ref, lb_ref, out_ref, stage, sem):
    D = D_MODEL
    tn = idx_ref.shape[0]

    def seg_copy(src_row, dst_row, rows):
        return pltpu.make_async_copy(y_ref.at[pl.ds(pl.multiple_of(src_row, SEG), rows)],
                                     stage.at[pl.ds(pl.multiple_of(dst_row, SEG), rows)], sem)

    @pl.when(pl.program_id(0) == 0)
    def _():
        stage[...] = jnp.zeros_like(stage)

    n_big, n_small = _segment_dma_loops(
        cnt_s, lambda e, off, moved: (start_s[0, e] + moved, off + moved), seg_copy)

    hots, before = _tile_ranks(idx_ref[...], tn)
    seg_len = jnp.ceil(cntv_ref[...].astype(F32) * (1.0 / SEG)) * SEG
    seg_off = _lane_prefix_exclusive(seg_len)
    gates = gate_ref[...]
    lane_r = lax.broadcasted_iota(I32, (tn, STAGE_ROWS), 1)
    weights = jnp.zeros((tn, STAGE_ROWS), F32)
    for k in range(TOP_K):
        stage_row = jnp.sum(jnp.where(hots[k], seg_off + before, 0.0), axis=-1, keepdims=True)
        weights = jnp.where(lane_r == stage_row.astype(I32), gates[:, k:k + 1], weights)

    _drain(seg_copy, n_big, n_small)

    y = jnp.dot(weights.astype(BF16), stage[...].astype(BF16), preferred_element_type=F32)
    gate_f = mod_ref[:, 5 * D:6 * D]
    h = DEEPNORM_ALPHA * x1_ref[...] + (1.0 + gate_f) * y
    out_ref[...] = _layer_norm(h, lg_ref[...], lb_ref[...])


def _combine(idx, gates, tile_cnt, tile_start, y_pad, x1, mod3, mod_row0, ln_g, ln_b, B, T, t0):
    D = D_MODEL
    tn = SCATTER_TILE
    n_rows = B * T
    nt = n_rows // tn
    tps = T // tn
    smem = lambda imap: pl.BlockSpec((None, 1, LANES), imap, memory_space=pltpu.SMEM)
    out = pl.pallas_call(
        _combine_kernel,
        grid=(nt,),
        in_specs=[smem(lambda i: (t0 + i, 0, 0)), smem(lambda i: (t0 + i, 0, 0)),
                  pl.BlockSpec((tn, TOP_K), lambda i: (t0 + i, 0)),
                  pl.BlockSpec((tn, TOP_K), lambda i: (t0 + i, 0)),
                  pl.BlockSpec((None, 1, LANES), lambda i: (t0 + i, 0, 0)),
                  pl.BlockSpec(memory_space=pl.ANY),
                  pl.BlockSpec((tn, D), lambda i: (i, 0)),
                  pl.BlockSpec((None, 1, 6 * D), lambda i: (mod_row0 + i // tps, 0, 0)),
                  pl.BlockSpec((1, D), lambda i: (0, 0)),
                  pl.BlockSpec((1, D), lambda i: (0, 0))],
        out_specs=pl.BlockSpec((tn, D), lambda i: (i, 0)),
        out_shape=jax.ShapeDtypeStruct((n_rows, D), F32),
        scratch_shapes=[pltpu.VMEM((STAGE_ROWS, D), F32), pltpu.SemaphoreType.DMA(())],
        compiler_params=_cparams(("arbitrary",)),
        name="moe_combine",
    )(tile_cnt, tile_start, idx, gates, tile_cnt, y_pad, x1, mod3, ln_g, ln_b)
    return out.reshape(B, T, D)


def kernel(x_prompt, x_sample, c_prompt, c_sample, w_mod, b_mod, w_in, rpb, g_out_a, g_out_b, w_out, ln1_g, ln1_b,
           w_router, b_router, w_up, b_up, w_down, b_down, ln2_g, ln2_b):
    D = D_MODEL
    groups = [x_prompt, x_sample]
    conds = [c_prompt, c_sample]
    for x in groups:
        assert x.shape[-1] == D and x.shape[1] % (CHUNK * RES) == 0, x.shape
        assert (x.shape[0] * x.shape[1]) % max(ROUTE_TILE, SCATTER_TILE, PROJ_TOKENS) == 0, x.shape
    assert w_up.shape == (1, N_EXPERTS, D, 2 * D_FF) and w_in.shape == (1, D, 6 * WIDTH), "single layer, fixed widths"
    n_cond = sum(c.shape[0] for c in conds)
    pad_rows = -n_cond % 8
    c_all = jnp.concatenate(conds + [jnp.zeros((pad_rows, D), F32)], axis=0)
    mod3 = _modulation(c_all, w_mod[0], b_mod[0]).reshape(n_cond + pad_rows, 1, 6 * D)

    w_in_bf = w_in[0].astype(BF16)
    w_out_bf = w_out[0].astype(BF16)
    na_bias = _na_bias_table(rpb[0])
    g_a = g_out_a[0].reshape(1, WIDTH)
    g_b = g_out_b[0].reshape(1, WIDTH)
    ln1g, ln1b = ln1_g[0].reshape(1, D), ln1_b[0].reshape(1, D)
    ln2g, ln2b = ln2_g[0].reshape(1, D), ln2_b[0].reshape(1, D)
    w_router_pad = jnp.pad(w_router[0], ((0, 0), (0, LANES - N_EXPERTS)))
    b_router_pad = jnp.pad(b_router[0], (0, LANES - N_EXPERTS)).reshape(1, LANES)

    perm_np = _residue_permutation()
    perm, perm_t = jnp.asarray(perm_np, BF16), jnp.asarray(perm_np.T, BF16)

    x1s, u2s, logit_list = [], [], []
    mod_row0 = 0
    mod_rows = []
    for x in groups:
        B, T, _ = x.shape
        cos_v, sin_v = _rope_tables(T)
        qa, ka, va, qb, kb, vb = _input_projection_nat(x, mod3, w_in_bf, cos_v, sin_v, perm, mod_row0)
        oa = _dilated_attention(qa, ka, va, B, T)
        ob = _neighbourhood_attention(qb, kb, vb, na_bias, B, T)
        x1, u2, logits = _output_projection_nat(oa, ob, x, mod3, mod_row0, g_a, g_b, w_out_bf, ln1g, ln1b,
                                                w_router_pad, b_router_pad, perm_t)
        x1s.append(x1)
        u2s.append(u2)
        logit_list.append(logits)
        mod_rows.append(mod_row0)
        mod_row0 += B

    logits_all = jnp.concatenate(logit_list, axis=0)
    N = logits_all.shape[0]
    idx, gates, tile_cnt = _routing(logits_all)

    tc = tile_cnt[:, 0, :N_EXPERTS]
    seg = (tc + SEG - 1) // SEG * SEG
    total = jnp.sum(seg, axis=0)
    padded = (total + MOE_ROWS - 1) // MOE_ROWS * MOE_ROWS
    pad_end = jnp.cumsum(padded)
    pad_start = pad_end - padded
    tile_start = pad_start[None, :] + jnp.cumsum(seg, axis=0) - seg
    n_tiles = N // SCATTER_TILE
    n_blocks = (N * TOP_K + n_tiles * N_EXPERTS * (SEG - 1)) // MOE_ROWS + N_EXPERTS
    n_rows = n_blocks * MOE_ROWS
    blk_end = pad_end // MOE_ROWS
    n_used = blk_end[-1:].astype(I32)
    blk_expert = jnp.minimum(
        jnp.sum(blk_end[None, :] <= jnp.arange(n_blocks, dtype=I32)[:, None], axis=1), N_EXPERTS - 1).astype(I32)
    lane_pad = ((0, 0), (0, LANES - N_EXPERTS))
    tile_start3 = jnp.pad(tile_start, lane_pad).astype(I32)[:, None, :]
    tails = jnp.pad(jnp.stack([pad_start + total, pad_end]), lane_pad).astype(I32)

    xs = None
    row0 = 0
    for u2 in u2s:
        n = u2.shape[0]
        xs = _dispatch(idx, u2, tile_cnt, tile_start3, tails, xs, n_rows, row0 // SCATTER_TILE)
        row0 += n

    b_up_de = jnp.concatenate([b_up[0][:, 0::2], b_up[0][:, 1::2]], axis=-1).reshape(N_EXPERTS, 1, 2 * D_FF)
    b_down3 = b_down[0].reshape(N_EXPERTS, 1, D)
    y_pad = _expert_blocks(xs, blk_expert, n_used, w_up[0], b_up_de, w_down[0], b_down3)

    outs = []
    row0 = 0
    for x, x1, mrow in zip(groups, x1s, mod_rows):
        B, T, _ = x.shape
        n = B * T
        outs.append(_combine(idx, gates, tile_cnt, tile_start3, y_pad, x1, mod3, mrow, ln2g, ln2b, B, T,
                             row0 // SCATTER_TILE))
        row0 += n
    return tuple(outs)
```
